```python
import math
import jax, jax.numpy as jnp
from jax import lax
import numpy as np

D_MODEL = 1024
BATCH = 4
SEQ = 8192
DEPTH = 2

D_MIX = D_MODEL
GLA_WIDTH = 3 * D_MIX // 8
GLA_HEADS = 6
GLA_DV = GLA_WIDTH // GLA_HEADS
GLA_DK = GLA_DV // 2
GLA_LOWRANK = 16
GLA_TAU = 16.0
HG_WIDTH = 3 * D_MIX // 8
HG_HEADS = 6
HG_DV = HG_WIDTH // HG_HEADS
HG_EXPAND = 64
HY_WIDTH = D_MIX - GLA_WIDTH - HG_WIDTH
HY_ORDER = 2
HY_SHORT = 3
HY_EMB = 33
HY_BANDS = (HY_EMB - 1) // 2
HY_FFN = 64
HY_INNER = 2
HY_FAST_DECAY = 0.3
HY_SLOW_DECAY = 1.5
HY_TARGET = 1e-2
CHUNK = 64
N_GROUPS = 4
EXPERTS_PER_GROUP = 4
N_EXPERTS = N_GROUPS * EXPERTS_PER_GROUP
TOP_K = 2
D_EXPERT = D_MODEL // 2
MOE_BLOCK = 256
ALPHA = (2 * DEPTH) ** 0.25
BETA = (8 * DEPTH) ** -0.25
LN_EPS = 1e-5
RMS_EPS = 1e-6
LB_FLOOR = 1e-30
IN_SPLITS = (GLA_HEADS * GLA_DK, GLA_HEADS * GLA_DK, GLA_WIDTH, GLA_WIDTH, 2 * GLA_LOWRANK,
             HG_HEADS * HG_EXPAND, 2 * HG_HEADS * HG_EXPAND, HG_WIDTH, HG_WIDTH, 3 * HY_WIDTH)
D_IN = sum(IN_SPLITS)

kernel_name = 'hybrid_gla_hgrn2_hyena_hmoe_deepnorm'


def layer_norm(x, g, b):
    xf = x.astype(jnp.float32)
    xc = xf - jnp.mean(xf, axis=-1, keepdims=True)
    var = jnp.mean(xc * xc, axis=-1, keepdims=True)
    return (xc * lax.rsqrt(var + LN_EPS) * g.astype(jnp.float32) + b.astype(jnp.float32)).astype(x.dtype)


def head_rms_norm(o, g):
    of = o.astype(jnp.float32)
    of = of * lax.rsqrt(jnp.mean(of * of, axis=-1, keepdims=True) + RMS_EPS)
    return (of.reshape(o.shape[0], o.shape[1], -1) * g.astype(jnp.float32)).astype(o.dtype)


def chunked_gated_scan(q, k, v, log_g):
    out_dtype = v.dtype
    B, L, H, K = q.shape
    V = v.shape[-1]
    N = L // CHUNK

    def to_chunks(a):
        return a.astype(jnp.float32).reshape(B, N, CHUNK, H, a.shape[-1]).transpose(1, 0, 3, 2, 4)

    qc, kc, vc, gc = to_chunks(q), to_chunks(k), to_chunks(v), to_chunks(log_g)
    gc = jnp.cumsum(gc, axis=3)
    lower = jnp.tril(jnp.ones((CHUNK, CHUNK), dtype=bool))[:, :, None]

    def step(state, blk):
        qb, kb, vb, gb = blk
        g_end = gb[:, :, -1:, :]
        inter = jnp.einsum('bhik,bhkv->bhiv', qb * jnp.exp(gb), state)
        diff = gb[:, :, :, None, :] - gb[:, :, None, :, :]
        decay = jnp.where(lower, jnp.exp(jnp.where(lower, diff, 0.0)), 0.0)
        scores = jnp.einsum('bhik,bhjk,bhijk->bhij', qb, kb, decay)
        intra = jnp.einsum('bhij,bhjv->bhiv', scores, vb)
        state = (jnp.exp(g_end[:, :, 0, :])[..., None] * state
                 + jnp.einsum('bhjk,bhjv->bhkv', kb * jnp.exp(g_end - gb), vb))
        return state, inter + intra

    s0 = jnp.zeros((B, H, K, V), jnp.float32)
    _, out = lax.scan(step, s0, (qc, kc, vc, gc))
    return out.transpose(1, 0, 3, 2, 4).reshape(B, L, H, V).astype(out_dtype)


def bidirectional_scan(q, k_fwd, k_bwd, v, g_fwd, g_bwd):
    rev = lambda a: jnp.flip(a, axis=1)
    fwd = chunked_gated_scan(q, k_fwd, v, g_fwd)
    bwd = rev(chunked_gated_scan(rev(q), rev(k_bwd), rev(v), rev(g_bwd)))
    return fwd + bwd


def hyena_filter_spectrum(L, w1, b1, freq, w2, b2, w3):
    f32 = jnp.float32
    t = jnp.linspace(0.0, 1.0, L, dtype=f32)[:, None]
    w = 2.0 * math.pi * jnp.arange(L, dtype=f32)[:, None] / L
    bands = jnp.linspace(1e-4, HY_BANDS - 1, HY_BANDS, dtype=f32)[None, :]
    z = jnp.concatenate([t, jnp.cos(bands * w), -jnp.sin(bands * w)], axis=-1)
    fr = freq.astype(f32)
    h = jnp.sin(fr * (z @ w1.astype(f32) + b1.astype(f32)))
    for i in range(HY_INNER):
        h = jnp.sin(fr * (h @ w2[i].astype(f32) + b2[i].astype(f32)))
    h = (h @ w3.astype(f32)).reshape(L, HY_ORDER, 2, HY_WIDTH)
    max_decay = math.log(HY_TARGET) / HY_FAST_DECAY
    min_decay = math.log(HY_TARGET) / HY_SLOW_DECAY
    deltas = jnp.linspace(min_decay, max_decay, HY_WIDTH, dtype=f32)
    h = h * jnp.exp(-t * jnp.abs(deltas))[:, None, None, :]
    k = jnp.concatenate([h[:, :, 0], jnp.zeros((1, HY_ORDER, HY_WIDTH), f32),
                         jnp.flip(h[1:, :, 1], axis=0)], axis=0)
    k = k / jnp.maximum(jnp.sum(jnp.abs(k), axis=0, keepdims=True), 1e-12)
    return jnp.fft.rfft(k, axis=0)


def hyena_mixer(u, conv_w, conv_b, k_spec, bias):
    out_dtype = u.dtype
    C = u.shape[-1]
    L = u.shape[1]
    pad = (HY_SHORT - 1) // 2
    u = lax.conv_general_dilated(u, conv_w[:, None, :], (1,), ((pad, pad),),
                                 dimension_numbers=('NWC', 'WIO', 'NWC'),
                                 feature_group_count=C) + conv_b
    v, x1, x2 = jnp.split(u.astype(jnp.float32), 3, axis=-1)
    z = v
    for o, gate in enumerate((x1, x2)):
        zf = jnp.fft.rfft(z, n=2 * L, axis=1)
        conv = jnp.fft.irfft(zf * k_spec[:, o][None], n=2 * L, axis=1)[:, :L]
        z = gate * (conv + z * bias[o].astype(jnp.float32))
    return z.astype(out_dtype)


def hybrid_token_mixer(h, w_in, gla_wa2, gla_ba, gla_norm_g, lb, hg_norm_g,
                       hy_conv_w, hy_conv_b, hy_spec, hy_bias, w_out):
    B, L, _ = h.shape
    proj = h @ w_in
    cuts = np.cumsum(IN_SPLITS)[:-1].tolist()
    gq, gk, gv, gg, ga, hq, hf, hi, hgt, hyu = jnp.split(proj, cuts, axis=-1)

    q = gq.reshape(B, L, GLA_HEADS, GLA_DK) * (GLA_DK ** -0.5)
    k = gk.reshape(B, L, GLA_HEADS, GLA_DK)
    v = gv.reshape(B, L, GLA_HEADS, GLA_DV)
    a_logit = jnp.einsum('bldr,drk->bldk', ga.reshape(B, L, 2, GLA_LOWRANK), gla_wa2) + gla_ba
    log_a = (jax.nn.log_sigmoid(a_logit.astype(jnp.float32)) / GLA_TAU).reshape(B, L, 2, GLA_HEADS, GLA_DK)
    o_gla = bidirectional_scan(q, k, k, v, log_a[:, :, 0], log_a[:, :, 1])
    y_gla = head_rms_norm(o_gla, gla_norm_g) * jax.nn.silu(gg)

    qh = jax.nn.silu(hq).reshape(B, L, HG_HEADS, HG_EXPAND)
    zf = hf.reshape(B, L, 2, HG_HEADS * HG_EXPAND).astype(jnp.float32)
    lbf = lb.astype(jnp.float32)
    log_f = jnp.logaddexp(jnp.log1p(-lbf) + jax.nn.log_sigmoid(zf),
                          jnp.log(jnp.maximum(lbf, LB_FLOOR)))
    one_minus_f = (1.0 - lbf) * jax.nn.sigmoid(-zf)
    log_f = log_f.reshape(B, L, 2, HG_HEADS, HG_EXPAND)
    one_minus_f = one_minus_f.reshape(B, L, 2, HG_HEADS, HG_EXPAND)
    vi = hi.reshape(B, L, HG_HEADS, HG_DV)
    o_hg = bidirectional_scan(qh, one_minus_f[:, :, 0], one_minus_f[:, :, 1], vi,
                              log_f[:, :, 0], log_f[:, :, 1])
    y_hg = head_rms_norm(o_hg, hg_norm_g) * jax.nn.sigmoid(hgt)

    y_hy = hyena_mixer(hyu, hy_conv_w, hy_conv_b, hy_spec, hy_bias)

    mix = jnp.concatenate([y_gla.astype(h.dtype), y_hg.astype(h.dtype), y_hy], axis=-1)
    return mix @ w_out


def hierarchical_moe(x, wr_g, br_g, wr_e, br_e, w_gate, w_up, w_down):
    B, L, D = x.shape
    T = B * L
    xt = x.reshape(T, D)
    pg = jax.nn.softmax((xt @ wr_g).astype(jnp.float32) + br_g.astype(jnp.float32), axis=-1)
    g_val, g_idx = lax.top_k(pg, 1)
    le = ((xt @ wr_e).astype(jnp.float32) + br_e.astype(jnp.float32)).reshape(T, N_GROUPS, EXPERTS_PER_GROUP)
    le = le[jnp.arange(T), g_idx[:, 0]]
    pe = jax.nn.softmax(le, axis=-1)
    e_val, e_idx = lax.top_k(pe, TOP_K)
    e_val = e_val / jnp.sum(e_val, axis=-1, keepdims=True)
    weights = g_val * e_val
    experts = g_idx * EXPERTS_PER_GROUP + e_idx

    A = T * TOP_K
    flat_e = experts.reshape(A)
    flat_w = weights.reshape(A)
    flat_t = jnp.repeat(jnp.arange(T, dtype=jnp.int32), TOP_K)
    order = jnp.argsort(flat_e)
    se, sw, st = flat_e[order], flat_w[order], flat_t[order]
    counts = jnp.bincount(flat_e, length=N_EXPERTS)
    padded = ((counts + MOE_BLOCK - 1) // MOE_BLOCK) * MOE_BLOCK
    start = jnp.cumsum(counts) - counts
    pend = jnp.cumsum(padded)
    pstart = pend - padded
    dest = pstart[se] + (jnp.arange(A) - start[se])
    NB = -(-A // MOE_BLOCK) + N_EXPERTS
    slot_tok = jnp.full((NB * MOE_BLOCK,), T, jnp.int32).at[dest].set(st)
    slot_w = jnp.zeros((NB * MOE_BLOCK,), jnp.float32).at[dest].set(sw)
    block_e = jnp.minimum(jnp.searchsorted(pend, jnp.arange(NB) * MOE_BLOCK, side='right'), N_EXPERTS - 1)
    x_pad = jnp.concatenate([xt, jnp.zeros((1, D), xt.dtype)], axis=0)

    def run_block(args):
        tok, e = args
        xb = x_pad[tok]
        hid = jax.nn.silu(xb @ w_gate[e]) * (xb @ w_up[e])
        return hid @ w_down[e]

    yb = lax.map(run_block, (slot_tok.reshape(NB, MOE_BLOCK), block_e))
    y = jax.ops.segment_sum(yb.reshape(NB * MOE_BLOCK, D).astype(jnp.float32) * slot_w[:, None],
                            slot_tok, num_segments=T + 1)[:T]
    return y.reshape(B, L, D).astype(x.dtype)


def setup_inputs(seed: int = 0) -> dict:
    key = jax.random.key(seed)
    ks = iter(jax.random.split(key, 32))
    f32 = jnp.float32

    def nrm(shape, scale):
        return jax.random.normal(next(ks), shape, f32) * scale

    x = nrm((BATCH, SEQ, D_MODEL), 1.0)
    ln_in_g = 1.0 + nrm((D_MODEL,), 0.02)
    ln_in_b = nrm((D_MODEL,), 0.02)
    col_scale = np.ones((D_IN,), np.float32)
    offs = np.concatenate([[0], np.cumsum(IN_SPLITS)])
    col_scale[offs[2]:offs[3]] = BETA
    col_scale[offs[7]:offs[8]] = BETA
    col_scale[offs[9]:offs[9] + HY_WIDTH] = BETA
    w_in = nrm((DEPTH, D_MODEL, D_IN), D_MODEL ** -0.5) * jnp.asarray(col_scale)
    gla_wa2 = nrm((DEPTH, 2, GLA_LOWRANK, GLA_HEADS * GLA_DK), GLA_LOWRANK ** -0.5)
    gla_ba = nrm((DEPTH, 2, GLA_HEADS * GLA_DK), 0.1)
    gla_norm_g = 1.0 + nrm((DEPTH, GLA_WIDTH), 0.02)
    hg_lb_logits = nrm((DEPTH, 2, HG_HEADS * HG_EXPAND), 0.1)
    hg_norm_g = 1.0 + nrm((DEPTH, HG_WIDTH), 0.02)
    hy_conv_w = nrm((DEPTH, HY_SHORT, 3 * HY_WIDTH), HY_SHORT ** -0.5)
    hy_conv_b = nrm((DEPTH, 3 * HY_WIDTH), 0.02)
    hy_w1 = nrm((DEPTH, HY_EMB, HY_FFN), HY_EMB ** -0.5)
    hy_b1 = nrm((DEPTH, HY_FFN), 0.02)
    hy_freq = 1.0 + nrm((DEPTH, HY_FFN), 0.1)
    hy_w2 = nrm((DEPTH, HY_INNER, HY_FFN, HY_FFN), HY_FFN ** -0.5)
    hy_b2 = nrm((DEPTH, HY_INNER, HY_FFN), 0.02)
    hy_w3 = nrm((DEPTH, HY_FFN, HY_ORDER * 2 * HY_WIDTH), HY_FFN ** -0.5)
    hy_bias = nrm((DEPTH, HY_ORDER, HY_WIDTH), 1.0)
    w_out = nrm((DEPTH, D_MIX, D_MODEL), D_MIX ** -0.5 * BETA)
    ln1_g = 1.0 + nrm((DEPTH, D_MODEL), 0.02)
    ln1_b = nrm((DEPTH, D_MODEL), 0.02)
    moe_wr_g = nrm((DEPTH, D_MODEL, N_GROUPS), D_MODEL ** -0.5)
    moe_br_g = nrm((DEPTH, N_GROUPS), 0.01)
    moe_wr_e = nrm((DEPTH, D_MODEL, N_EXPERTS), D_MODEL ** -0.5)
    moe_br_e = nrm((DEPTH, N_EXPERTS), 0.01)
    moe_w_gate = nrm((DEPTH, N_EXPERTS, D_MODEL, D_EXPERT), D_MODEL ** -0.5 * BETA)
    moe_w_up = nrm((DEPTH, N_EXPERTS, D_MODEL, D_EXPERT), D_MODEL ** -0.5 * BETA)
    moe_w_down = nrm((DEPTH, N_EXPERTS, D_EXPERT, D_MODEL), D_EXPERT ** -0.5 * BETA)
    ln2_g = 1.0 + nrm((DEPTH, D_MODEL), 0.02)
    ln2_b = nrm((DEPTH, D_MODEL), 0.02)
    return {'x': x, 'ln_in_g': ln_in_g, 'ln_in_b': ln_in_b, 'w_in': w_in,
            'gla_wa2': gla_wa2, 'gla_ba': gla_ba, 'gla_norm_g': gla_norm_g,
            'hg_lb_logits': hg_lb_logits, 'hg_norm_g': hg_norm_g,
            'hy_conv_w': hy_conv_w, 'hy_conv_b': hy_conv_b, 'hy_w1': hy_w1, 'hy_b1': hy_b1,
            'hy_freq': hy_freq, 'hy_w2': hy_w2, 'hy_b2': hy_b2, 'hy_w3': hy_w3, 'hy_bias': hy_bias,
            'w_out': w_out, 'ln1_g': ln1_g, 'ln1_b': ln1_b,
            'moe_wr_g': moe_wr_g, 'moe_br_g': moe_br_g, 'moe_wr_e': moe_wr_e, 'moe_br_e': moe_br_e,
            'moe_w_gate': moe_w_gate, 'moe_w_up': moe_w_up, 'moe_w_down': moe_w_down,
            'ln2_g': ln2_g, 'ln2_b': ln2_b}


def reference(x, ln_in_g, ln_in_b, w_in, gla_wa2, gla_ba, gla_norm_g, hg_lb_logits, hg_norm_g,
              hy_conv_w, hy_conv_b, hy_w1, hy_b1, hy_freq, hy_w2, hy_b2, hy_w3, hy_bias,
              w_out, ln1_g, ln1_b, moe_wr_g, moe_br_g, moe_wr_e, moe_br_e,
              moe_w_gate, moe_w_up, moe_w_down, ln2_g, ln2_b):
    L = x.shape[1]
    p = jax.nn.softmax(hg_lb_logits.astype(jnp.float32), axis=0)
    lower_bounds = jnp.cumsum(p, axis=0) - p[0:1]
    h = layer_norm(x, ln_in_g, ln_in_b)
    for l in range(DEPTH):
        hy_spec = hyena_filter_spectrum(L, hy_w1[l], hy_b1[l], hy_freq[l], hy_w2[l], hy_b2[l], hy_w3[l])
        mix = hybrid_token_mixer(h, w_in[l], gla_wa2[l], gla_ba[l], gla_norm_g[l], lower_bounds[l],
                                 hg_norm_g[l], hy_conv_w[l], hy_conv_b[l], hy_spec, hy_bias[l], w_out[l])
        h = layer_norm(ALPHA * h + mix, ln1_g[l], ln1_b[l])
        ffn = hierarchical_moe(h, moe_wr_g[l], moe_br_g[l], moe_wr_e[l], moe_br_e[l],
                               moe_w_gate[l], moe_w_up[l], moe_w_down[l])
        h = layer_norm(ALPHA * h + ffn, ln2_g[l], ln2_b[l])
    return h
```

```python
import functools
import math

import numpy as np
import jax
import jax.numpy as jnp
from jax import lax
from jax.experimental import pallas as pl
from jax.experimental.pallas import tpu as pltpu

F32 = jnp.float32
BF16 = jnp.bfloat16
I32 = jnp.int32

GLA_HEADS, GLA_DK, GLA_DV, GLA_LOWRANK, GLA_TAU = 6, 32, 64, 16, 16.0
HG_HEADS, HG_DK, HG_DV = 6, 64, 64
HY_WIDTH, HY_ORDER, HY_EMB, HY_FFN, HY_INNER = 256, 2, 33, 64, 2
HY_FAST_DECAY, HY_SLOW_DECAY, HY_TARGET = 0.3, 1.5, 1e-2
N_GROUPS, EXPERTS_PER_GROUP = 4, 4
N_EXPERTS = N_GROUPS * EXPERTS_PER_GROUP
TOP_K = 2
LN_EPS, RMS_EPS, LB_FLOOR = 1e-5, 1e-6, 1e-30

LANES = 128
SCAN_CHUNK = LANES
VMEM_LIMIT = 56 * 1024 * 1024

GLA_W = GLA_HEADS * GLA_DV
GLA_K = GLA_HEADS * GLA_DK
HG_W = HG_HEADS * HG_DV
HG_K = HG_HEADS * HG_DK
OFF_HQ, OFF_HI, OFF_HGT, OFF_HF = 0, 384, 768, 1152
OFF_GV, OFF_GG, OFF_GQ, OFF_GK, OFF_HY, OFF_GA = 1920, 2304, 2688, 2880, 3072, 3840
D_IN = 3872


def _dot(a, b, dims=(((1,), (0,)), ((), ())), precision=None):
    return lax.dot_general(a, b, dims, preferred_element_type=F32, precision=precision)


_NT = (((1,), (1,)), ((), ()))
_TN = (((0,), (0,)), ((), ()))


def _layer_norm(x, g, b):
    mu = jnp.mean(x, axis=-1, keepdims=True)
    xc = x - mu
    var = jnp.mean(xc * xc, axis=-1, keepdims=True)
    return xc * lax.rsqrt(var + LN_EPS) * g + b


def _log_sigmoid(x):
    return jnp.minimum(x, 0.0) - jnp.log1p(jnp.exp(-jnp.abs(x)))


def _sigmoid(x):
    return 1.0 / (1.0 + jnp.exp(-x))


def _cparams(sem):
    return pltpu.CompilerParams(dimension_semantics=sem, vmem_limit_bytes=VMEM_LIMIT)


def _inproj_kernel(x_ref, g_ref, b_ref, w_ref, *outs, apply_ln):
    x = x_ref[...]
    if apply_ln:
        x = _layer_norm(x, g_ref[...], b_ref[...])
        outs[1][...] = x
    outs[0][...] = _dot(w_ref[...], x.astype(BF16), _NT)


def _inproj(x, g, b, w_t, apply_ln, tm=512):
    T, D = x.shape
    n_out = w_t.shape[0]
    out_shape = [jax.ShapeDtypeStruct((n_out, T), F32)]
    out_specs = [pl.BlockSpec((n_out, tm), lambda i: (0, i))]
    if apply_ln:
        out_shape.append(jax.ShapeDtypeStruct((T, D), F32))
        out_specs.append(pl.BlockSpec((tm, D), lambda i: (i, 0)))
    res = pl.pallas_call(
        functools.partial(_inproj_kernel, apply_ln=apply_ln),
        grid=(T // tm,),
        in_specs=[pl.BlockSpec((tm, D), lambda i: (i, 0)),
                  pl.BlockSpec((1, D), lambda i: (0, 0)),
                  pl.BlockSpec((1, D), lambda i: (0, 0)),
                  pl.BlockSpec((n_out, D), lambda i: (0, 0))],
        out_specs=out_specs,
        out_shape=out_shape,
        compiler_params=_cparams(("arbitrary",)),
        name="inproj",
    )(x, g.reshape(1, D), b.reshape(1, D), w_t)
    return res if apply_ln else (res[0], x)


def _scan_kernel(*refs, mode, H, K, V, TB, NCB):
    if mode == "gla":
        q_ref, k_ref, v_ref, ga_ref, wa_ref, ba_ref, gate_ref, ng_ref, y_ref, s_ref, oacc_ref = refs
    else:
        q_ref, z_ref, v_ref, lbc_ref, gate_ref, ng_ref, y_ref, s_ref, oacc_ref = refs
    C = SCAN_CHUNK
    half = C // 2
    nchunks = TB // C
    n = pl.program_id(1)

    @pl.when((n == 0) | (n == NCB))
    def _():
        s_ref[...] = jnp.zeros_like(s_ref)

    def gates(sl):
        if mode == "gla":
            a = _dot(wa_ref[0], ga_ref[:, sl], precision=lax.Precision.HIGHEST) + ba_ref[0]
            g = _log_sigmoid(a) * (1.0 / GLA_TAU)
            q = q_ref[:, sl] * (K ** -0.5)
            k = k_ref[:, sl]
        else:
            z = z_ref[:, sl]
            lbc = lbc_ref[0]
            c1, c2, c3 = lbc[:, 0:1], lbc[:, 1:2], lbc[:, 2:3]
            x1 = c1 + _log_sigmoid(z)
            g = jnp.maximum(x1, c2) + jnp.log1p(jnp.exp(-jnp.abs(x1 - c2)))
            k = c3 * _sigmoid(-z)
            hq = q_ref[:, sl]
            q = hq * _sigmoid(hq)
        return q, k, g

    def run(rev):
        r = lax.broadcasted_iota(I32, (C, C), 0)
        c = lax.broadcasted_iota(I32, (C, C), 1)
        same = (r >= half) == (c >= half)
        if not rev:
            cum = (r <= c)
            off1 = (r < half) & (c >= half)
            diag = same & (r <= c)
            c_end, c_mid, c_a, c_b = C - 1, half, half // 2, half + half // 2
        else:
            cum = (r >= c)
            off1 = (r >= half) & (c < half)
            diag = same & (r >= c)
            c_end, c_mid, c_a, c_b = 0, half - 1, half // 2 - 1, half + half // 2 - 1
        cum = cum.astype(BF16)
        lane_lo = lax.broadcasted_iota(I32, (1, C), 1) < half
        tb = jnp.where(n < NCB, n, 2 * NCB - 1 - n)

        def chunk(ci, carry):
            cj = (nchunks - 1 - ci) if rev else ci
            off = pl.multiple_of(cj * C, C)
            sl = pl.ds(off, C)
            q, k, g = gates(sl)
            v = v_ref[:, sl].astype(BF16)
            g1 = g.astype(BF16)
            r1 = g - g1.astype(F32)
            g2 = r1.astype(BF16)
            g3 = (r1 - g2.astype(F32)).astype(BF16)
            G = _dot(g1, cum) + _dot(g2, cum) + _dot(g3, cum)
            g_end = G[:, c_end:c_end + 1]
            g_mid = G[:, c_mid:c_mid + 1]
            g_ref2 = jnp.where(lane_lo, G[:, c_a:c_a + 1], G[:, c_b:c_b + 1])
            e2 = G - g_ref2
            q2 = (q * jnp.exp(e2)).astype(BF16)
            k2 = (k * jnp.exp(-e2)).astype(BF16)
            q1 = (q * jnp.exp(jnp.minimum(G - g_mid, 0.0))).astype(BF16)
            k1 = (k * jnp.exp(jnp.minimum(g_mid - G, 0.0))).astype(BF16)
            qg = (q * jnp.exp(G)).astype(BF16)
            kd = (k * jnp.exp(g_end - G)).astype(BF16)
            dec = jnp.exp(g_end)
            outs = []
            for h in range(H):
                rk = slice(h * K, (h + 1) * K)
                rv = slice(h * V, (h + 1) * V)
                p1 = _dot(k1[rk], q1[rk], _TN)
                p2 = _dot(k2[rk], q2[rk], _TN)
                p = jnp.where(off1, p1, jnp.where(diag, p2, 0.0)).astype(BF16)
                s_h = s_ref[h]
                o_h = _dot(v[rv], p) + _dot(s_h.astype(BF16), qg[rk], _TN)
                s_ref[h] = dec[rk] * s_h + _dot(kd[rk], v[rv], _NT)
                outs.append(o_h)
            o = jnp.concatenate(outs, axis=0)
            tsl = pl.ds(pl.multiple_of(tb * TB + off, C), C)
            if not rev:
                oacc_ref[:, tsl] = o
            else:
                o = o + oacc_ref[:, tsl]
                gate = gate_ref[:, sl]
                act = gate * _sigmoid(gate) if mode == "gla" else _sigmoid(gate)
                ng = ng_ref[...]
                ys = []
                for h in range(H):
                    rv = slice(h * V, (h + 1) * V)
                    oh = o[rv]
                    ms = jnp.mean(oh * oh, axis=0, keepdims=True)
                    ys.append(oh * lax.rsqrt(ms + RMS_EPS))
                y_ref[:, sl] = jnp.concatenate(ys, axis=0) * ng * act
            return carry

        lax.fori_loop(0, nchunks, chunk, 0)

    @pl.when(n < NCB)
    def _():
        run(False)

    @pl.when(n >= NCB)
    def _():
        run(True)


def _scan(proj_t, mode, B, L, extra, norm_g, TB=1024):
    NCB = L // TB
    if mode == "gla":
        H, K, V = GLA_HEADS, GLA_DK, GLA_DV
    else:
        H, K, V = HG_HEADS, HG_DK, HG_DV
    HK, HV = H * K, H * V

    def tb_of(n):
        return jnp.where(n < NCB, n, 2 * NCB - 1 - n)

    def tb_gate(n):
        return jnp.where(n < NCB, NCB - 1, 2 * NCB - 1 - n)

    def dirn(n):
        return (n >= NCB).astype(I32)

    col = lambda b, n: b * NCB + tb_of(n)
    colg = lambda b, n: b * NCB + tb_gate(n)
    if mode == "gla":
        wa_t, ba = extra
        in_specs = [
            pl.BlockSpec((HK, TB), lambda b, n: (OFF_GQ // HK, col(b, n))),
            pl.BlockSpec((HK, TB), lambda b, n: (OFF_GK // HK, col(b, n))),
            pl.BlockSpec((HV, TB), lambda b, n: (OFF_GV // HV, col(b, n))),
            pl.BlockSpec((GLA_LOWRANK, TB), lambda b, n: (OFF_GA // GLA_LOWRANK + dirn(n), col(b, n))),
            pl.BlockSpec((1, HK, GLA_LOWRANK), lambda b, n: (dirn(n), 0, 0)),
            pl.BlockSpec((1, HK, 1), lambda b, n: (dirn(n), 0, 0)),
            pl.BlockSpec((HV, TB), lambda b, n: (OFF_GG // HV, colg(b, n))),
            pl.BlockSpec((HV, 1), lambda b, n: (0, 0)),
        ]
        args = (proj_t, proj_t, proj_t, proj_t, wa_t, ba, proj_t, norm_g)
    else:
        (lbc,) = extra
        in_specs = [
            pl.BlockSpec((HK, TB), lambda b, n: (OFF_HQ // HK, col(b, n))),
            pl.BlockSpec((HK, TB), lambda b, n: (OFF_HF // HK + dirn(n), col(b, n))),
            pl.BlockSpec((HV, TB), lambda b, n: (OFF_HI // HV, col(b, n))),
            pl.BlockSpec((1, HK, 8), lambda b, n: (dirn(n), 0, 0)),
            pl.BlockSpec((HV, TB), lambda b, n: (OFF_HGT // HV, colg(b, n))),
            pl.BlockSpec((HV, 1), lambda b, n: (0, 0)),
        ]
        args = (proj_t, proj_t, proj_t, lbc, proj_t, norm_g)
    return pl.pallas_call(
        functools.partial(_scan_kernel, mode=mode, H=H, K=K, V=V, TB=TB, NCB=NCB),
        grid=(B, 2 * NCB),
        in_specs=in_specs,
        out_specs=pl.BlockSpec((HV, TB), lambda b, n: (0, colg(b, n))),
        out_shape=jax.ShapeDtypeStruct((HV, B * L), F32),
        scratch_shapes=[pltpu.VMEM((H, K, V), F32), pltpu.VMEM((HV, L), F32)],
        compiler_params=_cparams(("arbitrary", "arbitrary")),
        name="scan_" + mode,
    )(*args)


def _dft_consts(L):
    N = 2 * L
    NA = N // LANES
    a = np.arange(NA)[:, None] * np.arange(NA)[None, :]
    ca, sa = np.cos(2 * np.pi * a / NA), np.sin(2 * np.pi * a / NA)
    hh = NA // 2
    w1d = np.block([[ca[:, :hh], sa[:, :hh]], [-sa[:, :hh], ca[:, :hh]]])
    w1f = np.concatenate([ca, -sa], axis=0)
    w1i = np.block([[ca[:hh, :], -sa[:hh, :]], [sa[:hh, :], ca[:hh, :]]])
    bb = np.arange(LANES)[:, None] * np.arange(LANES)[None, :]
    cb, sb = np.cos(2 * np.pi * bb / LANES), np.sin(2 * np.pi * bb / LANES)
    w2 = np.block([[cb, -sb], [sb, cb]])
    w2i = np.block([[cb, sb], [-sb, cb]])
    tw = np.arange(NA)[:, None] * np.arange(LANES)[None, :]
    tc, ts = np.cos(2 * np.pi * tw / N), np.sin(2 * np.pi * tw / N)
    bf = lambda m: jnp.asarray(m, dtype=F32).astype(BF16)
    return dict(w1d=bf(w1d), w1f=bf(w1f), w1i=bf(w1i), w2=bf(w2), w2i=bf(w2i),
                tc=jnp.asarray(tc, F32), ts=jnp.asarray(ts, F32))


def _pos_features(L):
    t = np.linspace(0.0, 1.0, L)
    w = 2.0 * np.pi * np.arange(L) / L
    bands = np.linspace(1e-4, (HY_EMB - 1) // 2 - 1, (HY_EMB - 1) // 2)
    z = np.concatenate([t[None, :], np.cos(bands[:, None] * w[None, :]), -np.sin(bands[:, None] * w[None, :])], axis=0)
    kp = -(-HY_EMB // 8) * 8
    z = np.concatenate([z, np.zeros((kp - HY_EMB, L))], axis=0)
    idx = (L - np.arange(L)) % L
    z_rev, t_rev = z[:, idx], t[idx]
    mask = (np.arange(L) >= 1).astype(np.float64)
    return (jnp.asarray(z, F32), jnp.asarray(z_rev, F32), jnp.asarray(t[None, :], F32),
            jnp.asarray(t_rev[None, :], F32), jnp.asarray(mask[None, :], F32))


def _filter_kernel(z_ref, zr_ref, t_ref, tr_ref, m_ref, w1_ref, b1_ref, fr_ref, w2_ref, b2_ref,
                   w3f_ref, w3b_ref, dl_ref, out_ref, h_ref, *, L):
    first = (pl.program_id(0) == 0) & (pl.program_id(1) == 0)
    hi = lax.Precision.HIGHEST

    @pl.when(first)
    def _():
        fr = fr_ref[...]
        for idx, zz in enumerate((z_ref, zr_ref)):
            h = jnp.sin(fr * (_dot(w1_ref[...], zz[...], precision=hi) + b1_ref[...]))
            for i in range(HY_INNER):
                h = jnp.sin(fr * (_dot(w2_ref[i], h, precision=hi) + b2_ref[i]))
            h_ref[idx] = h

    ad = jnp.abs(dl_ref[...])
    kf = _dot(w3f_ref[...], h_ref[0], precision=hi) * jnp.exp(-t_ref[...] * ad)
    kb = _dot(w3b_ref[...], h_ref[1], precision=hi) * jnp.exp(-tr_ref[...] * ad) * m_ref[...]
    den = jnp.sum(jnp.abs(kf), axis=1, keepdims=True) + jnp.sum(jnp.abs(kb), axis=1, keepdims=True)
    scale = 1.0 / (jnp.maximum(den, 1e-12) * (2.0 * L))
    out_ref[0, :, 0:L] = kf * scale
    out_ref[0, :, L:2 * L] = kb * scale


def _hyena_filters(L, w1, b1, freq, w2, b2, w3, cg=64):
    z, z_rev, t, t_rev, mask = _pos_features(L)
    kp = z.shape[0]
    w1_t = jnp.zeros((HY_FFN, kp), F32).at[:, :HY_EMB].set(w1.T)
    w2_t = jnp.swapaxes(w2, 1, 2)
    w3_t = w3.T
    max_decay = math.log(HY_TARGET) / HY_FAST_DECAY
    min_decay = math.log(HY_TARGET) / HY_SLOW_DECAY
    deltas = jnp.asarray(np.linspace(min_decay, max_decay, HY_WIDTH).reshape(HY_WIDTH, 1), F32)
    ncg = HY_WIDTH // cg
    full = lambda shp: pl.BlockSpec(shp, lambda o, j: (0,) * len(shp))
    return pl.pallas_call(
        functools.partial(_filter_kernel, L=L),
        grid=(HY_ORDER, ncg),
        in_specs=[full((kp, L)), full((kp, L)), full((1, L)), full((1, L)), full((1, L)),
                  full((HY_FFN, kp)), full((HY_FFN, 1)), full((HY_FFN, 1)),
                  full((HY_INNER, HY_FFN, HY_FFN)), full((HY_INNER, HY_FFN, 1)),
                  pl.BlockSpec((cg, HY_FFN), lambda o, j: (o * 2 * ncg + j, 0)),
                  pl.BlockSpec((cg, HY_FFN), lambda o, j: (o * 2 * ncg + ncg + j, 0)),
                  pl.BlockSpec((cg, 1), lambda o, j: (j, 0))],
        out_specs=pl.BlockSpec((1, cg, 2 * L), lambda o, j: (o, j, 0)),
        out_shape=jax.ShapeDtypeStruct((HY_ORDER, HY_WIDTH, 2 * L), F32),
        scratch_shapes=[pltpu.VMEM((2, HY_FFN, L), F32)],
        compiler_params=_cparams(("arbitrary", "arbitrary")),
        name="hyena_filter",
    )(z, z_rev, t, t_rev, mask, w1_t, b1.reshape(HY_FFN, 1), freq.reshape(HY_FFN, 1), w2_t,
      b2.reshape(HY_INNER, HY_FFN, 1), w3_t, w3_t, deltas)


def _fft_fwd(xs, w1, tc, ts, w2, NA):
    a = _dot(w1, xs)
    a_re, a_im = a[:NA], a[NA:]
    b_re = a_re * tc + a_im * ts
    b_im = a_im * tc - a_re * ts
    return _dot(jnp.concatenate([b_re, b_im], axis=1).astype(BF16), w2)


def _spectrum_kernel(k_ref, w1_ref, tc_ref, ts_ref, w2_ref, out_ref, *, NA, cg):
    def body(ci, carry):
        out_ref[0, ci] = _fft_fwd(k_ref[0, ci].astype(BF16), w1_ref[...], tc_ref[...], ts_ref[...], w2_ref[...], NA)
        return carry
    lax.fori_loop(0, cg, body, 0)


def _hyena_spectrum(kt, consts, L, cg=32):
    NA = 2 * L // LANES
    k4 = kt.reshape(HY_ORDER, HY_WIDTH, NA, LANES)
    full = lambda shp: pl.BlockSpec(shp, lambda o, j: (0,) * len(shp))
    return pl.pallas_call(
        functools.partial(_spectrum_kernel, NA=NA, cg=cg),
        grid=(HY_ORDER, HY_WIDTH // cg),
        in_specs=[pl.BlockSpec((1, cg, NA, LANES), lambda o, j: (o, j, 0, 0)),
                  full((2 * NA, NA)), full((NA, LANES)), full((NA, LANES)), full((2 * LANES, 2 * LANES))],
        out_specs=pl.BlockSpec((1, cg, NA, 2 * LANES), lambda o, j: (o, j, 0, 0)),
        out_shape=jax.ShapeDtypeStruct((HY_ORDER, HY_WIDTH, NA, 2 * LANES), F32),
        compiler_params=_cparams(("arbitrary", "arbitrary")),
        name="hyena_spectrum",
    )(k4, consts["w1f"], consts["tc"], consts["ts"], consts["w2"])


def _hyena_kernel(cw_ref, cb_ref, hb_ref, v_ref, x1_ref, x2_ref, ks_ref, w1d_ref, w1i_ref, tc_ref, ts_ref,
                  w2_ref, w2i_ref, y_ref, *, NA, cg, B):
    hh = NA // 2
    j = pl.program_id(0)
    row = lax.broadcasted_iota(I32, (hh, LANES), 0)
    lane = lax.broadcasted_iota(I32, (hh, LANES), 1)
    first = (row == 0) & (lane == 0)
    last = (row == hh - 1) & (lane == LANES - 1)

    def short_conv(x, ch):
        r1 = pltpu.roll(x, 1, 1)
        prev = jnp.where(lane == 0, pltpu.roll(r1, 1, 0), r1)
        prev = jnp.where(first, 0.0, prev)
        r2 = pltpu.roll(x, LANES - 1, 1)
        nxt = jnp.where(lane == LANES - 1, pltpu.roll(r2, hh - 1, 0), r2)
        nxt = jnp.where(last, 0.0, nxt)
        return cw_ref[0, ch] * prev + cw_ref[1, ch] * x + cw_ref[2, ch] * nxt + cb_ref[ch]

    def body(ci, carry):
        c = j * cg + ci
        tc, ts = tc_ref[...], ts_ref[...]
        for p in range(B // 2):
            z = [short_conv(v_ref[ci, 2 * p + r], c) for r in range(2)]
            gates = ([short_conv(x1_ref[ci, 2 * p + r], HY_WIDTH + c) for r in range(2)],
                     [short_conv(x2_ref[ci, 2 * p + r], 2 * HY_WIDTH + c) for r in range(2)])
            for o in range(HY_ORDER):
                xs = jnp.concatenate(z, axis=0).astype(BF16)
                x = _fft_fwd(xs, w1d_ref[...], tc, ts, w2_ref[...], NA)
                ks = ks_ref[o, ci]
                x_re, x_im = x[:, :LANES], x[:, LANES:]
                k_re, k_im = ks[:, :LANES], ks[:, LANES:]
                y_re = x_re * k_re - x_im * k_im
                y_im = x_re * k_im + x_im * k_re
                bq = _dot(jnp.concatenate([y_re, y_im], axis=1).astype(BF16), w2i_ref[...])
                b_re, b_im = bq[:, :LANES], bq[:, LANES:]
                c_re = b_re * tc - b_im * ts
                c_im = b_re * ts + b_im * tc
                conv = _dot(w1i_ref[...], jnp.concatenate([c_re, c_im], axis=0).astype(BF16))
                bias = hb_ref[o, c]
                z = [gates[o][r] * (conv[r * hh:(r + 1) * hh] + z[r] * bias) for r in range(2)]
            for r in range(2):
                y_ref[ci, 2 * p + r] = z[r]
        return carry

    lax.fori_loop(0, cg, body, 0)


def _hyena(proj_t, kspec, consts, conv_w, conv_b, bias, B, L, cg=8):
    NA = 2 * L // LANES
    hh = NA // 2
    u = proj_t.reshape(proj_t.shape[0], B, hh, LANES)
    ncg = HY_WIDTH // cg
    base = OFF_HY // cg
    smem = pl.BlockSpec(memory_space=pltpu.SMEM)
    full = lambda shp: pl.BlockSpec(shp, lambda j: (0,) * len(shp))
    blk = lambda off: pl.BlockSpec((cg, B, hh, LANES), lambda j: (off + j, 0, 0, 0))
    y = pl.pallas_call(
        functools.partial(_hyena_kernel, NA=NA, cg=cg, B=B),
        grid=(ncg,),
        in_specs=[smem, smem, smem, blk(base), blk(base + ncg), blk(base + 2 * ncg),
                  pl.BlockSpec((HY_ORDER, cg, NA, 2 * LANES), lambda j: (0, j, 0, 0)),
                  full((2 * NA, NA)), full((NA, 2 * NA)), full((NA, LANES)), full((NA, LANES)),
                  full((2 * LANES, 2 * LANES)), full((2 * LANES, 2 * LANES))],
        out_specs=pl.BlockSpec((cg, B, hh, LANES), lambda j: (j, 0, 0, 0)),
        out_shape=jax.ShapeDtypeStruct((HY_WIDTH, B, hh, LANES), F32),
        compiler_params=_cparams(("arbitrary",)),
        name="hyena_conv",
    )(conv_w, conv_b, bias, u, u, u, kspec, consts["w1d"], consts["w1i"], consts["tc"], consts["ts"],
      consts["w2"], consts["w2i"])
    return y.reshape(HY_WIDTH, B * L)


def _outproj_kernel(yg_ref, yh_ref, yy_ref, h_ref, wo_ref, g_ref, b_ref, wrh_ref, wrl_ref, br_ref,
                    h1_ref, e_ref, w_ref, *, alpha):
    mix = _dot(yg_ref[...].astype(BF16), wo_ref[0:GLA_W], _TN)
    mix += _dot(yh_ref[...].astype(BF16), wo_ref[GLA_W:GLA_W + HG_W], _TN)
    mix += _dot(yy_ref[...].astype(BF16), wo_ref[GLA_W + HG_W:], _TN)
    h1 = _layer_norm(alpha * h_ref[...] + mix, g_ref[...], b_ref[...])
    h1_ref[...] = h1
    hi = h1.astype(BF16)
    lo = (h1 - hi.astype(F32)).astype(BF16)
    lg = _dot(wrh_ref[...], hi, _NT) + _dot(wrh_ref[...], lo, _NT) + _dot(wrl_ref[...], hi, _NT) + br_ref[...]
    tm = lg.shape[1]
    gl = [lg[g:g + 1] for g in range(N_GROUPS)]
    gmax = functools.reduce(jnp.maximum, gl)
    gidx = jnp.full((1, tm), N_GROUPS - 1, I32)
    for g in range(N_GROUPS - 2, -1, -1):
        gidx = jnp.where(gl[g] == gmax, g, gidx)
    gsum = functools.reduce(jnp.add, [jnp.exp(x - gmax) for x in gl])
    g_val = 1.0 / gsum
    el = []
    for r in range(EXPERTS_PER_GROUP):
        acc = jnp.zeros((1, tm), F32)
        for g in range(N_GROUPS):
            row = N_GROUPS + g * EXPERTS_PER_GROUP + r
            acc = jnp.where(gidx == g, lg[row:row + 1], acc)
        el.append(acc)
    emax = functools.reduce(jnp.maximum, el)
    pe = [jnp.exp(x - emax) for x in el]
    esum = functools.reduce(jnp.add, pe)
    pe = [x / esum for x in pe]
    v1 = functools.reduce(jnp.maximum, pe)
    i1 = jnp.full((1, tm), EXPERTS_PER_GROUP - 1, I32)
    for r in range(EXPERTS_PER_GROUP - 2, -1, -1):
        i1 = jnp.where(pe[r] == v1, r, i1)
    pe2 = [jnp.where(i1 == r, -1.0, pe[r]) for r in range(EXPERTS_PER_GROUP)]
    v2 = functools.reduce(jnp.maximum, pe2)
    i2 = jnp.full((1, tm), EXPERTS_PER_GROUP - 1, I32)
    for r in range(EXPERTS_PER_GROUP - 2, -1, -1):
        i2 = jnp.where(pe2[r] == v2, r, i2)
    den = v1 + v2
    e_ref[...] = jnp.concatenate([gidx * EXPERTS_PER_GROUP + i1, gidx * EXPERTS_PER_GROUP + i2], axis=0)
    w_ref[...] = jnp.concatenate([g_val * (v1 / den), g_val * (v2 / den)], axis=0)


def _outproj(yg, yh, yy, h, w_out, g, b, wr_t, br, alpha, tm=512):
    T, D = h.shape
    nr = wr_t.shape[0]
    wr_hi = wr_t.astype(BF16)
    wr_lo = (wr_t - wr_hi.astype(F32)).astype(BF16)
    full = lambda shp: pl.BlockSpec(shp, lambda i: (0,) * len(shp))
    return pl.pallas_call(
        functools.partial(_outproj_kernel, alpha=alpha),
        grid=(T // tm,),
        in_specs=[pl.BlockSpec((GLA_W, tm), lambda i: (0, i)),
                  pl.BlockSpec((HG_W, tm), lambda i: (0, i)),
                  pl.BlockSpec((HY_WIDTH, tm), lambda i: (0, i)),
                  pl.BlockSpec((tm, D), lambda i: (i, 0)),
                  full((D, D)), full((1, D)), full((1, D)), full((nr, D)), full((nr, D)), full((nr, 1))],
        out_specs=[pl.BlockSpec((tm, D), lambda i: (i, 0)),
                   pl.BlockSpec((TOP_K, tm), lambda i: (0, i)),
                   pl.BlockSpec((TOP_K, tm), lambda i: (0, i))],
        out_shape=[jax.ShapeDtypeStruct((T, D), F32),
                   jax.ShapeDtypeStruct((TOP_K, T), I32),
                   jax.ShapeDtypeStruct((TOP_K, T), F32)],
        compiler_params=_cparams(("arbitrary",)),
        name="outproj",
    )(yg, yh, yy, h, w_out.astype(BF16), g.reshape(1, D), b.reshape(1, D), wr_hi, wr_lo, br)


def _ffn_kernel(be_ref, tok_ref, h_hbm, wg_ref, wu_ref, wd_ref, y_ref, xbuf, sem, *, bm):
    j = pl.program_id(0)
    nb = pl.num_programs(0)

    def issue(jj, slot):
        base = jj * bm

        def body(r, carry):
            t = tok_ref[base + r]
            pltpu.make_async_copy(h_hbm.at[pl.ds(t, 1)], xbuf.at[slot, pl.ds(r, 1)], sem.at[slot]).start()
            return carry
        lax.fori_loop(0, bm, body, 0, unroll=8)

    @pl.when(j == 0)
    def _():
        issue(0, 0)

    @pl.when(j + 1 < nb)
    def _():
        issue(j + 1, (j + 1) % 2)

    slot = j % 2
    pltpu.make_async_copy(h_hbm.at[pl.ds(0, bm)], xbuf.at[slot], sem.at[slot]).wait()
    x = xbuf[slot].astype(BF16)
    a = _dot(x, wg_ref[0])
    hid = (a * _sigmoid(a)) * _dot(x, wu_ref[0])
    y_ref[...] = _dot(hid.astype(BF16), wd_ref[0])


def _moe_ffn(h1, block_e, slot_tok, wg, wu, wd, bm):
    T, D = h1.shape
    NB = block_e.shape[0]
    DE = wg.shape[-1]
    return pl.pallas_call(
        functools.partial(_ffn_kernel, bm=bm),
        grid_spec=pltpu.PrefetchScalarGridSpec(
            num_scalar_prefetch=2,
            grid=(NB,),
            in_specs=[pl.BlockSpec(memory_space=pl.ANY),
                      pl.BlockSpec((1, D, DE), lambda j, be, tok: (be[j], 0, 0)),
                      pl.BlockSpec((1, D, DE), lambda j, be, tok: (be[j], 0, 0)),
                      pl.BlockSpec((1, DE, D), lambda j, be, tok: (be[j], 0, 0))],
            out_specs=pl.BlockSpec((bm, D), lambda j, be, tok: (j, 0)),
            scratch_shapes=[pltpu.VMEM((2, bm, D), F32), pltpu.SemaphoreType.DMA((2,))]),
        out_shape=jax.ShapeDtypeStruct((NB * bm, D), F32),
        compiler_params=_cparams(("arbitrary",)),
        name="moe_ffn",
    )(block_e, slot_tok, h1, wg, wu, wd)


def _combine_kernel(d0_ref, d1_ref, yb_hbm, h_ref, w_ref, g_ref, b_ref, o_ref, rbuf, sem, *, tm, alpha):
    i = pl.program_id(0)
    n = pl.num_programs(0)

    def issue(ii, slot):
        base = ii * tm

        def body(r, carry):
            pltpu.make_async_copy(yb_hbm.at[pl.ds(d0_ref[base + r], 1)], rbuf.at[slot, 0, pl.ds(r, 1)],
                                  sem.at[slot]).start()
            pltpu.make_async_copy(yb_hbm.at[pl.ds(d1_ref[base + r], 1)], rbuf.at[slot, 1, pl.ds(r, 1)],
                                  sem.at[slot]).start()
            return carry
        lax.fori_loop(0, tm, body, 0, unroll=8)

    @pl.when(i == 0)
    def _():
        issue(0, 0)

    @pl.when(i + 1 < n)
    def _():
        issue(i + 1, (i + 1) % 2)

    slot = i % 2
    pltpu.make_async_copy(yb_hbm.at[pl.ds(0, tm)], rbuf.at[slot, 0], sem.at[slot]).wait()
    pltpu.make_async_copy(yb_hbm.at[pl.ds(0, tm)], rbuf.at[slot, 1], sem.at[slot]).wait()
    w = w_ref[...]
    ffn = w[:, 0:1] * rbuf[slot, 0] + w[:, 1:2] * rbuf[slot, 1]
    o_ref[...] = _layer_norm(alpha * h_ref[...] + ffn, g_ref[...], b_ref[...])


def _moe_combine(yb, h1, dest, w_tk, g, b, alpha, tm=256):
    T, D = h1.shape
    return pl.pallas_call(
        functools.partial(_combine_kernel, tm=tm, alpha=alpha),
        grid_spec=pltpu.PrefetchScalarGridSpec(
            num_scalar_prefetch=2,
            grid=(T // tm,),
            in_specs=[pl.BlockSpec(memory_space=pl.ANY),
                      pl.BlockSpec((tm, D), lambda i, d0, d1: (i, 0)),
                      pl.BlockSpec((tm, TOP_K), lambda i, d0, d1: (i, 0)),
                      pl.BlockSpec((1, D), lambda i, d0, d1: (0, 0)),
                      pl.BlockSpec((1, D), lambda i, d0, d1: (0, 0))],
            out_specs=pl.BlockSpec((tm, D), lambda i, d0, d1: (i, 0)),
            scratch_shapes=[pltpu.VMEM((2, TOP_K, tm, D), F32), pltpu.SemaphoreType.DMA((2,))]),
        out_shape=jax.ShapeDtypeStruct((T, D), F32),
        compiler_params=_cparams(("arbitrary",)),
        name="moe_combine",
    )(dest[0], dest[1], yb, h1, w_tk, g.reshape(1, D), b.reshape(1, D))


def _dispatch_plan(e_kt, T, bm):
    A = T * TOP_K
    flat_e = e_kt.T.reshape(A)
    flat_t = jnp.repeat(jnp.arange(T, dtype=I32), TOP_K)
    order = jnp.argsort(flat_e)
    se = flat_e[order]
    counts = jnp.bincount(flat_e, length=N_EXPERTS).astype(I32)
    padded = ((counts + bm - 1) // bm) * bm
    start = jnp.cumsum(counts) - counts
    pend = jnp.cumsum(padded)
    pstart = pend - padded
    dest_sorted = pstart[se] + (jnp.arange(A, dtype=I32) - start[se])
    NB = -(-A // bm) + N_EXPERTS
    dest_flat = jnp.zeros((A,), I32).at[order].set(dest_sorted)
    slot_tok = jnp.zeros((NB * bm,), I32).at[dest_flat].set(flat_t)
    block_e = jnp.minimum(jnp.searchsorted(pend, jnp.arange(NB, dtype=I32) * bm, side="right"),
                          N_EXPERTS - 1).astype(I32)
    return block_e, slot_tok, dest_flat.reshape(T, TOP_K).T


def _in_perm():
    splits = (192, 192, 384, 384, 32, 384, 768, 384, 384, 768)
    offs = np.concatenate([[0], np.cumsum(splits)])
    gq, gk, gv, gg, ga, hq, hf, hi, hgt, hyu = [np.arange(offs[i], offs[i + 1]) for i in range(10)]
    perm = np.concatenate([hq, hi, hgt, hf, gv, gg, gq, gk, hyu, ga])
    assert perm.shape[0] == D_IN
    return perm


def kernel(x, ln_in_g, ln_in_b, w_in, gla_wa2, gla_ba, gla_norm_g, hg_lb_logits, hg_norm_g, hy_conv_w, hy_conv_b, hy_w1, hy_b1, hy_freq, hy_w2, hy_b2, hy_w3, hy_bias, w_out, ln1_g, ln1_b, moe_wr_g, moe_br_g, moe_wr_e, moe_br_e, moe_w_gate, moe_w_up, moe_w_down, ln2_g, ln2_b):
    B, L, D = x.shape
    T = B * L
    depth = w_in.shape[0]
    alpha = (2 * depth) ** 0.25
    bm = 256
    perm = _in_perm()
    consts = _dft_consts(L)

    p = jax.nn.softmax(hg_lb_logits.astype(F32), axis=0)
    lbs = jnp.cumsum(p, axis=0) - p[0:1]
    lbc = jnp.stack([jnp.log1p(-lbs), jnp.log(jnp.maximum(lbs, LB_FLOOR)), 1.0 - lbs], axis=-1)
    lbc = jnp.concatenate([lbc, jnp.zeros(lbc.shape[:-1] + (5,), F32)], axis=-1)

    h = x.reshape(T, D)
    for l in range(depth):
        w_t = w_in[l][:, perm].T.astype(BF16)
        proj_t, h = _inproj(h, ln_in_g, ln_in_b, w_t, apply_ln=(l == 0))
        wa_t = jnp.swapaxes(gla_wa2[l], 1, 2)
        y_gla = _scan(proj_t, "gla", B, L, (wa_t, gla_ba[l].reshape(2, GLA_K, 1)), gla_norm_g[l].reshape(GLA_W, 1))
        y_hg = _scan(proj_t, "hg", B, L, (lbc[l],), hg_norm_g[l].reshape(HG_W, 1))
        kt = _hyena_filters(L, hy_w1[l], hy_b1[l], hy_freq[l], hy_w2[l], hy_b2[l], hy_w3[l])
        kspec = _hyena_spectrum(kt, consts, L)
        y_hy = _hyena(proj_t, kspec, consts, hy_conv_w[l], hy_conv_b[l], hy_bias[l], B, L)
        nr = N_GROUPS + N_EXPERTS
        nrp = -(-nr // 8) * 8
        wr_t = jnp.zeros((nrp, D), F32).at[:nr].set(jnp.concatenate([moe_wr_g[l], moe_wr_e[l]], axis=1).T)
        br = jnp.zeros((nrp, 1), F32).at[:nr, 0].set(jnp.concatenate([moe_br_g[l], moe_br_e[l]]))
        h1, e_kt, w_kt = _outproj(y_gla, y_hg, y_hy, h, w_out[l], ln1_g[l], ln1_b[l], wr_t, br, alpha)
        block_e, slot_tok, dest = _dispatch_plan(e_kt, T, bm)
        yb = _moe_ffn(h1, block_e, slot_tok, moe_w_gate[l].astype(BF16), moe_w_up[l].astype(BF16),
                      moe_w_down[l].astype(BF16), bm)
        h = _moe_combine(yb, h1, dest, w_kt.T, ln2_g[l], ln2_b[l], alpha)
    return h.reshape(B, L, D)
```

```python
import functools
import math

import numpy as np
import jax
import jax.numpy as jnp
from jax import lax
from jax.experimental import pallas as pl
from jax.experimental.pallas import tpu as pltpu

F32 = jnp.float32
BF16 = jnp.bfloat16
I32 = jnp.int32

GLA_HEADS, GLA_DK, GLA_DV, GLA_LOWRANK, GLA_TAU = 6, 32, 64, 16, 16.0
HG_HEADS, HG_DK, HG_DV = 6, 64, 64
HY_WIDTH, HY_ORDER, HY_EMB, HY_FFN, HY_INNER = 256, 2, 33, 64, 2
HY_FAST_DECAY, HY_SLOW_DECAY, HY_TARGET = 0.3, 1.5, 1e-2
N_GROUPS, EXPERTS_PER_GROUP = 4, 4
N_EXPERTS = N_GROUPS * EXPERTS_PER_GROUP
TOP_K = 2
LN_EPS, RMS_EPS, LB_FLOOR = 1e-5, 1e-6, 1e-30

LANES = 128
SCAN_CHUNK = LANES
ROW_CHUNK = 16
VMEM_LIMIT = 56 * 1024 * 1024

GLA_W = GLA_HEADS * GLA_DV
GLA_K = GLA_HEADS * GLA_DK
HG_W = HG_HEADS * HG_DV
HG_K = HG_HEADS * HG_DK
OFF_HQ, OFF_HI, OFF_HGT, OFF_HF = 0, 384, 768, 1152
OFF_GV, OFF_GG, OFF_GQ, OFF_GK, OFF_HY, OFF_GA = 1920, 2304, 2688, 2880, 3072, 3840
D_IN = 3872


def _dot(a, b, dims=(((1,), (0,)), ((), ())), precision=None):
    return lax.dot_general(a, b, dims, preferred_element_type=F32, precision=precision)


_NT = (((1,), (1,)), ((), ()))
_TN = (((0,), (0,)), ((), ()))


def _layer_norm(x, g, b):
    mu = jnp.mean(x, axis=-1, keepdims=True)
    xc = x - mu
    var = jnp.mean(xc * xc, axis=-1, keepdims=True)
    return xc * lax.rsqrt(var + LN_EPS) * g + b


def _log_sigmoid(x):
    return jnp.minimum(x, 0.0) - jnp.log1p(jnp.exp(-jnp.abs(x)))


def _sigmoid(x):
    return 1.0 / (1.0 + jnp.exp(-x))


def _cparams(sem):
    return pltpu.CompilerParams(dimension_semantics=sem, vmem_limit_bytes=VMEM_LIMIT)


def _inproj_kernel(x_ref, g_ref, b_ref, w_ref, *outs, apply_ln):
    x = x_ref[...]
    if apply_ln:
        x = _layer_norm(x, g_ref[...], b_ref[...])
        outs[1][...] = x
    outs[0][...] = _dot(w_ref[...], x.astype(BF16), _NT)


def _inproj(x, g, b, w_t, apply_ln, tm=512):
    T, D = x.shape
    n_out = w_t.shape[0]
    out_shape = [jax.ShapeDtypeStruct((n_out, T), F32)]
    out_specs = [pl.BlockSpec((n_out, tm), lambda i: (0, i))]
    if apply_ln:
        out_shape.append(jax.ShapeDtypeStruct((T, D), F32))
        out_specs.append(pl.BlockSpec((tm, D), lambda i: (i, 0)))
    res = pl.pallas_call(
        functools.partial(_inproj_kernel, apply_ln=apply_ln),
        grid=(T // tm,),
        in_specs=[pl.BlockSpec((tm, D), lambda i: (i, 0)),
                  pl.BlockSpec((1, D), lambda i: (0, 0)),
                  pl.BlockSpec((1, D), lambda i: (0, 0)),
                  pl.BlockSpec((n_out, D), lambda i: (0, 0))],
        out_specs=out_specs,
        out_shape=out_shape,
        compiler_params=_cparams(("arbitrary",)),
        name="inproj",
    )(x, g.reshape(1, D), b.reshape(1, D), w_t)
    return res if apply_ln else (res[0], x)


def _scan_kernel(*refs, mode, H, K, V, TB, NCB):
    if mode == "gla":
        q_ref, k_ref, v_ref, ga_ref, wa_ref, ba_ref, gate_ref, ng_ref, y_ref, s_ref, oacc_ref = refs
    else:
        q_ref, z_ref, v_ref, lbc_ref, gate_ref, ng_ref, y_ref, s_ref, oacc_ref = refs
    C = SCAN_CHUNK
    half = C // 2
    nchunks = TB // C
    n = pl.program_id(1)

    @pl.when((n == 0) | (n == NCB))
    def _():
        s_ref[...] = jnp.zeros_like(s_ref)

    def gates(sl):
        if mode == "gla":
            a = _dot(wa_ref[0], ga_ref[:, sl], precision=lax.Precision.HIGHEST) + ba_ref[0]
            g = _log_sigmoid(a) * (1.0 / GLA_TAU)
            q = q_ref[:, sl] * (K ** -0.5)
            k = k_ref[:, sl]
        else:
            z = z_ref[:, sl]
            lbc = lbc_ref[0]
            c1, c2, c3 = lbc[:, 0:1], lbc[:, 1:2], lbc[:, 2:3]
            x1 = c1 + _log_sigmoid(z)
            g = jnp.maximum(x1, c2) + jnp.log1p(jnp.exp(-jnp.abs(x1 - c2)))
            k = c3 * _sigmoid(-z)
            hq = q_ref[:, sl]
            q = hq * _sigmoid(hq)
        return q, k, g

    def run(rev):
        r = lax.broadcasted_iota(I32, (C, C), 0)
        c = lax.broadcasted_iota(I32, (C, C), 1)
        same = (r >= half) == (c >= half)
        if not rev:
            cum = (r <= c)
            off1 = (r < half) & (c >= half)
            diag = same & (r <= c)
            c_end, c_mid, c_a, c_b = C - 1, half, half // 2, half + half // 2
        else:
            cum = (r >= c)
            off1 = (r >= half) & (c < half)
            diag = same & (r >= c)
            c_end, c_mid, c_a, c_b = 0, half - 1, half // 2 - 1, half + half // 2 - 1
        cum = cum.astype(BF16)
        lane_lo = lax.broadcasted_iota(I32, (1, C), 1) < half
        tb = jnp.where(n < NCB, n, 2 * NCB - 1 - n)

        def chunk(ci, carry):
            cj = (nchunks - 1 - ci) if rev else ci
            off = pl.multiple_of(cj * C, C)
            sl = pl.ds(off, C)
            q, k, g = gates(sl)
            v = v_ref[:, sl].astype(BF16)
            g1 = g.astype(BF16)
            r1 = g - g1.astype(F32)
            g2 = r1.astype(BF16)
            g3 = (r1 - g2.astype(F32)).astype(BF16)
            G = _dot(g1, cum) + _dot(g2, cum) + _dot(g3, cum)
            g_end = G[:, c_end:c_end + 1]
            g_mid = G[:, c_mid:c_mid + 1]
            g_ref2 = jnp.where(lane_lo, G[:, c_a:c_a + 1], G[:, c_b:c_b + 1])
            e2 = G - g_ref2
            q2 = (q * jnp.exp(e2)).astype(BF16)
            k2 = (k * jnp.exp(-e2)).astype(BF16)
            q1 = (q * jnp.exp(jnp.minimum(G - g_mid, 0.0))).astype(BF16)
            k1 = (k * jnp.exp(jnp.minimum(g_mid - G, 0.0))).astype(BF16)
            qg = (q * jnp.exp(G)).astype(BF16)
            kd = (k * jnp.exp(g_end - G)).astype(BF16)
            dec = jnp.exp(g_end)
            outs = []
            for h in range(H):
                rk = slice(h * K, (h + 1) * K)
                rv = slice(h * V, (h + 1) * V)
                p1 = _dot(k1[rk], q1[rk], _TN)
                p2 = _dot(k2[rk], q2[rk], _TN)
                p = jnp.where(off1, p1, jnp.where(diag, p2, 0.0)).astype(BF16)
                s_h = s_ref[h]
                o_h = _dot(v[rv], p) + _dot(s_h.astype(BF16), qg[rk], _TN)
                s_ref[h] = dec[rk] * s_h + _dot(kd[rk], v[rv], _NT)
                outs.append(o_h)
            o = jnp.concatenate(outs, axis=0)
            tsl = pl.ds(pl.multiple_of(tb * TB + off, C), C)
            if not rev:
                oacc_ref[:, tsl] = o
            else:
                o = o + oacc_ref[:, tsl]
                gate = gate_ref[:, sl]
                act = gate * _sigmoid(gate) if mode == "gla" else _sigmoid(gate)
                ng = ng_ref[...]
                ys = []
                for h in range(H):
                    rv = slice(h * V, (h + 1) * V)
                    oh = o[rv]
                    ms = jnp.mean(oh * oh, axis=0, keepdims=True)
                    ys.append(oh * lax.rsqrt(ms + RMS_EPS))
                y_ref[:, sl] = jnp.concatenate(ys, axis=0) * ng * act
            return carry

        lax.fori_loop(0, nchunks, chunk, 0)

    @pl.when(n < NCB)
    def _():
        run(False)

    @pl.when(n >= NCB)
    def _():
        run(True)


def _scan(proj_t, mode, B, L, extra, norm_g, TB=1024):
    NCB = L // TB
    if mode == "gla":
        H, K, V = GLA_HEADS, GLA_DK, GLA_DV
    else:
        H, K, V = HG_HEADS, HG_DK, HG_DV
    HK, HV = H * K, H * V

    def tb_of(n):
        return jnp.where(n < NCB, n, 2 * NCB - 1 - n)

    def tb_gate(n):
        return jnp.where(n < NCB, NCB - 1, 2 * NCB - 1 - n)

    def dirn(n):
        return (n >= NCB).astype(I32)

    col = lambda b, n: b * NCB + tb_of(n)
    colg = lambda b, n: b * NCB + tb_gate(n)
    if mode == "gla":
        wa_t, ba = extra
        in_specs = [
            pl.BlockSpec((HK, TB), lambda b, n: (OFF_GQ // HK, col(b, n))),
            pl.BlockSpec((HK, TB), lambda b, n: (OFF_GK // HK, col(b, n))),
            pl.BlockSpec((HV, TB), lambda b, n: (OFF_GV // HV, col(b, n))),
            pl.BlockSpec((GLA_LOWRANK, TB), lambda b, n: (OFF_GA // GLA_LOWRANK + dirn(n), col(b, n))),
            pl.BlockSpec((1, HK, GLA_LOWRANK), lambda b, n: (dirn(n), 0, 0)),
            pl.BlockSpec((1, HK, 1), lambda b, n: (dirn(n), 0, 0)),
            pl.BlockSpec((HV, TB), lambda b, n: (OFF_GG // HV, colg(b, n))),
            pl.BlockSpec((HV, 1), lambda b, n: (0, 0)),
        ]
        args = (proj_t, proj_t, proj_t, proj_t, wa_t, ba, proj_t, norm_g)
    else:
        (lbc,) = extra
        in_specs = [
            pl.BlockSpec((HK, TB), lambda b, n: (OFF_HQ // HK, col(b, n))),
            pl.BlockSpec((HK, TB), lambda b, n: (OFF_HF // HK + dirn(n), col(b, n))),
            pl.BlockSpec((HV, TB), lambda b, n: (OFF_HI // HV, col(b, n))),
            pl.BlockSpec((1, HK, 8), lambda b, n: (dirn(n), 0, 0)),
            pl.BlockSpec((HV, TB), lambda b, n: (OFF_HGT // HV, colg(b, n))),
            pl.BlockSpec((HV, 1), lambda b, n: (0, 0)),
        ]
        args = (proj_t, proj_t, proj_t, lbc, proj_t, norm_g)
    return pl.pallas_call(
        functools.partial(_scan_kernel, mode=mode, H=H, K=K, V=V, TB=TB, NCB=NCB),
        grid=(B, 2 * NCB),
        in_specs=in_specs,
        out_specs=pl.BlockSpec((HV, TB), lambda b, n: (0, colg(b, n))),
        out_shape=jax.ShapeDtypeStruct((HV, B * L), F32),
        scratch_shapes=[pltpu.VMEM((H, K, V), F32), pltpu.VMEM((HV, L), F32)],
        compiler_params=_cparams(("arbitrary", "arbitrary")),
        name="scan_" + mode,
    )(*args)


def _dft_consts(L):
    N = 2 * L
    NA = N // LANES
    a = np.arange(NA)[:, None] * np.arange(NA)[None, :]
    ca, sa = np.cos(2 * np.pi * a / NA), np.sin(2 * np.pi * a / NA)
    hh = NA // 2
    w1d = np.block([[ca[:, :hh], sa[:, :hh]], [-sa[:, :hh], ca[:, :hh]]])
    w1f = np.concatenate([ca, -sa], axis=0)
    w1i = np.block([[ca[:hh, :], -sa[:hh, :]], [sa[:hh, :], ca[:hh, :]]])
    bb = np.arange(LANES)[:, None] * np.arange(LANES)[None, :]
    cb, sb = np.cos(2 * np.pi * bb / LANES), np.sin(2 * np.pi * bb / LANES)
    w2 = np.block([[cb, -sb], [sb, cb]])
    w2i = np.block([[cb, sb], [-sb, cb]])
    tw = np.arange(NA)[:, None] * np.arange(LANES)[None, :]
    tc, ts = np.cos(2 * np.pi * tw / N), np.sin(2 * np.pi * tw / N)
    bf = lambda m: jnp.asarray(m, dtype=F32).astype(BF16)
    return dict(w1d=bf(w1d), w1f=bf(w1f), w1i=bf(w1i), w2=bf(w2), w2i=bf(w2i),
                tc=jnp.asarray(tc, F32), ts=jnp.asarray(ts, F32))


def _pos_features(L):
    t = np.linspace(0.0, 1.0, L)
    w = 2.0 * np.pi * np.arange(L) / L
    bands = np.linspace(1e-4, (HY_EMB - 1) // 2 - 1, (HY_EMB - 1) // 2)
    z = np.concatenate([t[None, :], np.cos(bands[:, None] * w[None, :]), -np.sin(bands[:, None] * w[None, :])], axis=0)
    kp = -(-HY_EMB // 8) * 8
    z = np.concatenate([z, np.zeros((kp - HY_EMB, L))], axis=0)
    idx = (L - np.arange(L)) % L
    z_rev, t_rev = z[:, idx], t[idx]
    mask = (np.arange(L) >= 1).astype(np.float64)
    return (jnp.asarray(z, F32), jnp.asarray(z_rev, F32), jnp.asarray(t[None, :], F32),
            jnp.asarray(t_rev[None, :], F32), jnp.asarray(mask[None, :], F32))


def _filter_kernel(z_ref, zr_ref, t_ref, tr_ref, m_ref, w1_ref, b1_ref, fr_ref, w2_ref, b2_ref,
                   w3f_ref, w3b_ref, dl_ref, out_ref, h_ref, *, L):
    first = (pl.program_id(0) == 0) & (pl.program_id(1) == 0)
    hi = lax.Precision.HIGHEST

    @pl.when(first)
    def _():
        fr = fr_ref[...]
        for idx, zz in enumerate((z_ref, zr_ref)):
            h = jnp.sin(fr * (_dot(w1_ref[...], zz[...], precision=hi) + b1_ref[...]))
            for i in range(HY_INNER):
                h = jnp.sin(fr * (_dot(w2_ref[i], h, precision=hi) + b2_ref[i]))
            h_ref[idx] = h

    ad = jnp.abs(dl_ref[...])
    kf = _dot(w3f_ref[...], h_ref[0], precision=hi) * jnp.exp(-t_ref[...] * ad)
    kb = _dot(w3b_ref[...], h_ref[1], precision=hi) * jnp.exp(-tr_ref[...] * ad) * m_ref[...]
    den = jnp.sum(jnp.abs(kf), axis=1, keepdims=True) + jnp.sum(jnp.abs(kb), axis=1, keepdims=True)
    scale = 1.0 / (jnp.maximum(den, 1e-12) * (2.0 * L))
    out_ref[0, :, 0:L] = kf * scale
    out_ref[0, :, L:2 * L] = kb * scale


def _hyena_filters(L, w1, b1, freq, w2, b2, w3, cg=64):
    z, z_rev, t, t_rev, mask = _pos_features(L)
    kp = z.shape[0]
    w1_t = jnp.zeros((HY_FFN, kp), F32).at[:, :HY_EMB].set(w1.T)
    w2_t = jnp.swapaxes(w2, 1, 2)
    w3_t = w3.T
    max_decay = math.log(HY_TARGET) / HY_FAST_DECAY
    min_decay = math.log(HY_TARGET) / HY_SLOW_DECAY
    deltas = jnp.asarray(np.linspace(min_decay, max_decay, HY_WIDTH).reshape(HY_WIDTH, 1), F32)
    ncg = HY_WIDTH // cg
    full = lambda shp: pl.BlockSpec(shp, lambda o, j: (0,) * len(shp))
    return pl.pallas_call(
        functools.partial(_filter_kernel, L=L),
        grid=(HY_ORDER, ncg),
        in_specs=[full((kp, L)), full((kp, L)), full((1, L)), full((1, L)), full((1, L)),
                  full((HY_FFN, kp)), full((HY_FFN, 1)), full((HY_FFN, 1)),
                  full((HY_INNER, HY_FFN, HY_FFN)), full((HY_INNER, HY_FFN, 1)),
                  pl.BlockSpec((cg, HY_FFN), lambda o, j: (o * 2 * ncg + j, 0)),
                  pl.BlockSpec((cg, HY_FFN), lambda o, j: (o * 2 * ncg + ncg + j, 0)),
                  pl.BlockSpec((cg, 1), lambda o, j: (j, 0))],
        out_specs=pl.BlockSpec((1, cg, 2 * L), lambda o, j: (o, j, 0)),
        out_shape=jax.ShapeDtypeStruct((HY_ORDER, HY_WIDTH, 2 * L), F32),
        scratch_shapes=[pltpu.VMEM((2, HY_FFN, L), F32)],
        compiler_params=_cparams(("arbitrary", "arbitrary")),
        name="hyena_filter",
    )(z, z_rev, t, t_rev, mask, w1_t, b1.reshape(HY_FFN, 1), freq.reshape(HY_FFN, 1), w2_t,
      b2.reshape(HY_INNER, HY_FFN, 1), w3_t, w3_t, deltas)


def _fft_fwd(xs, w1, tc, ts, w2, NA):
    a = _dot(w1, xs)
    a_re, a_im = a[:NA], a[NA:]
    b_re = a_re * tc + a_im * ts
    b_im = a_im * tc - a_re * ts
    return _dot(jnp.concatenate([b_re, b_im], axis=1).astype(BF16), w2)


def _spectrum_kernel(k_ref, w1_ref, tc_ref, ts_ref, w2_ref, out_ref, *, NA, cg):
    def body(ci, carry):
        out_ref[0, ci] = _fft_fwd(k_ref[0, ci].astype(BF16), w1_ref[...], tc_ref[...], ts_ref[...], w2_ref[...], NA)
        return carry
    lax.fori_loop(0, cg, body, 0)


def _hyena_spectrum(kt, consts, L, cg=32):
    NA = 2 * L // LANES
    k4 = kt.reshape(HY_ORDER, HY_WIDTH, NA, LANES)
    full = lambda shp: pl.BlockSpec(shp, lambda o, j: (0,) * len(shp))
    return pl.pallas_call(
        functools.partial(_spectrum_kernel, NA=NA, cg=cg),
        grid=(HY_ORDER, HY_WIDTH // cg),
        in_specs=[pl.BlockSpec((1, cg, NA, LANES), lambda o, j: (o, j, 0, 0)),
                  full((2 * NA, NA)), full((NA, LANES)), full((NA, LANES)), full((2 * LANES, 2 * LANES))],
        out_specs=pl.BlockSpec((1, cg, NA, 2 * LANES), lambda o, j: (o, j, 0, 0)),
        out_shape=jax.ShapeDtypeStruct((HY_ORDER, HY_WIDTH, NA, 2 * LANES), F32),
        compiler_params=_cparams(("arbitrary", "arbitrary")),
        name="hyena_spectrum",
    )(k4, consts["w1f"], consts["tc"], consts["ts"], consts["w2"])


def _hyena_kernel(cw_ref, cb_ref, hb_ref, v_ref, x1_ref, x2_ref, ks_ref, w1d_ref, w1i_ref, tc_ref, ts_ref,
                  w2_ref, w2i_ref, y_ref, *, NA, cg, B):
    hh = NA // 2
    j = pl.program_id(0)
    row = lax.broadcasted_iota(I32, (hh, LANES), 0)
    lane = lax.broadcasted_iota(I32, (hh, LANES), 1)
    first = (row == 0) & (lane == 0)
    last = (row == hh - 1) & (lane == LANES - 1)

    def short_conv(x, ch):
        r1 = pltpu.roll(x, 1, 1)
        prev = jnp.where(lane == 0, pltpu.roll(r1, 1, 0), r1)
        prev = jnp.where(first, 0.0, prev)
        r2 = pltpu.roll(x, LANES - 1, 1)
        nxt = jnp.where(lane == LANES - 1, pltpu.roll(r2, hh - 1, 0), r2)
        nxt = jnp.where(last, 0.0, nxt)
        return cw_ref[0, ch] * prev + cw_ref[1, ch] * x + cw_ref[2, ch] * nxt + cb_ref[ch]

    def body(ci, carry):
        c = j * cg + ci
        tc, ts = tc_ref[...], ts_ref[...]
        for p in range(B // 2):
            z = [short_conv(v_ref[ci, 2 * p + r], c) for r in range(2)]
            gates = ([short_conv(x1_ref[ci, 2 * p + r], HY_WIDTH + c) for r in range(2)],
                     [short_conv(x2_ref[ci, 2 * p + r], 2 * HY_WIDTH + c) for r in range(2)])
            for o in range(HY_ORDER):
                xs = jnp.concatenate(z, axis=0).astype(BF16)
                x = _fft_fwd(xs, w1d_ref[...], tc, ts, w2_ref[...], NA)
                ks = ks_ref[o, ci]
                x_re, x_im = x[:, :LANES], x[:, LANES:]
                k_re, k_im = ks[:, :LANES], ks[:, LANES:]
                y_re = x_re * k_re - x_im * k_im
                y_im = x_re * k_im + x_im * k_re
                bq = _dot(jnp.concatenate([y_re, y_im], axis=1).astype(BF16), w2i_ref[...])
                b_re, b_im = bq[:, :LANES], bq[:, LANES:]
                c_re = b_re * tc - b_im * ts
                c_im = b_re * ts + b_im * tc
                conv = _dot(w1i_ref[...], jnp.concatenate([c_re, c_im], axis=0).astype(BF16))
                bias = hb_ref[o, c]
                z = [gates[o][r] * (conv[r * hh:(r + 1) * hh] + z[r] * bias) for r in range(2)]
            for r in range(2):
                y_ref[ci, 2 * p + r] = z[r]
        return carry

    lax.fori_loop(0, cg, body, 0, unroll=2)


def _hyena(proj_t, kspec, consts, conv_w, conv_b, bias, B, L, cg=8):
    NA = 2 * L // LANES
    hh = NA // 2
    u = proj_t.reshape(proj_t.shape[0], B, hh, LANES)
    ncg = HY_WIDTH // cg
    base = OFF_HY // cg
    smem = pl.BlockSpec(memory_space=pltpu.SMEM)
    full = lambda shp: pl.BlockSpec(shp, lambda j: (0,) * len(shp))
    blk = lambda off: pl.BlockSpec((cg, B, hh, LANES), lambda j: (off + j, 0, 0, 0))
    y = pl.pallas_call(
        functools.partial(_hyena_kernel, NA=NA, cg=cg, B=B),
        grid=(ncg,),
        in_specs=[smem, smem, smem, blk(base), blk(base + ncg), blk(base + 2 * ncg),
                  pl.BlockSpec((HY_ORDER, cg, NA, 2 * LANES), lambda j: (0, j, 0, 0)),
                  full((2 * NA, NA)), full((NA, 2 * NA)), full((NA, LANES)), full((NA, LANES)),
                  full((2 * LANES, 2 * LANES)), full((2 * LANES, 2 * LANES))],
        out_specs=pl.BlockSpec((cg, B, hh, LANES), lambda j: (j, 0, 0, 0)),
        out_shape=jax.ShapeDtypeStruct((HY_WIDTH, B, hh, LANES), F32),
        compiler_params=_cparams(("arbitrary",)),
        name="hyena_conv",
    )(conv_w, conv_b, bias, u, u, u, kspec, consts["w1d"], consts["w1i"], consts["tc"], consts["ts"],
      consts["w2"], consts["w2i"])
    return y.reshape(HY_WIDTH, B * L)


def _outproj_kernel(yg_ref, yh_ref, yy_ref, h_ref, wo_ref, g_ref, b_ref, wrh_ref, wrl_ref, br_ref,
                    h1_ref, e_ref, w_ref, cnt_ref, *, alpha):
    mix = _dot(yg_ref[...].astype(BF16), wo_ref[0:GLA_W], _TN)
    mix += _dot(yh_ref[...].astype(BF16), wo_ref[GLA_W:GLA_W + HG_W], _TN)
    mix += _dot(yy_ref[...].astype(BF16), wo_ref[GLA_W + HG_W:], _TN)
    h1 = _layer_norm(alpha * h_ref[...] + mix, g_ref[...], b_ref[...])
    h1_ref[...] = h1
    hi = h1.astype(BF16)
    lo = (h1 - hi.astype(F32)).astype(BF16)
    lg = _dot(wrh_ref[...], hi, _NT) + _dot(wrh_ref[...], lo, _NT) + _dot(wrl_ref[...], hi, _NT) + br_ref[...]
    tm = lg.shape[1]
    gl = [lg[g:g + 1] for g in range(N_GROUPS)]
    gmax = functools.reduce(jnp.maximum, gl)
    gidx = jnp.full((1, tm), N_GROUPS - 1, I32)
    for g in range(N_GROUPS - 2, -1, -1):
        gidx = jnp.where(gl[g] == gmax, g, gidx)
    gsum = functools.reduce(jnp.add, [jnp.exp(x - gmax) for x in gl])
    g_val = 1.0 / gsum
    el = []
    for r in range(EXPERTS_PER_GROUP):
        acc = jnp.zeros((1, tm), F32)
        for g in range(N_GROUPS):
            row = N_GROUPS + g * EXPERTS_PER_GROUP + r
            acc = jnp.where(gidx == g, lg[row:row + 1], acc)
        el.append(acc)
    emax = functools.reduce(jnp.maximum, el)
    pe = [jnp.exp(x - emax) for x in el]
    esum = functools.reduce(jnp.add, pe)
    pe = [x / esum for x in pe]
    v1 = functools.reduce(jnp.maximum, pe)
    i1 = jnp.full((1, tm), EXPERTS_PER_GROUP - 1, I32)
    for r in range(EXPERTS_PER_GROUP - 2, -1, -1):
        i1 = jnp.where(pe[r] == v1, r, i1)
    pe2 = [jnp.where(i1 == r, -1.0, pe[r]) for r in range(EXPERTS_PER_GROUP)]
    v2 = functools.reduce(jnp.maximum, pe2)
    i2 = jnp.full((1, tm), EXPERTS_PER_GROUP - 1, I32)
    for r in range(EXPERTS_PER_GROUP - 2, -1, -1):
        i2 = jnp.where(pe2[r] == v2, r, i2)
    den = v1 + v2
    e0 = gidx * EXPERTS_PER_GROUP + i1
    e1 = gidx * EXPERTS_PER_GROUP + i2
    e_ref[...] = jnp.concatenate([e0, e1], axis=0)
    w_ref[...] = jnp.concatenate([g_val * (v1 / den), g_val * (v2 / den)], axis=0)
    eio = lax.broadcasted_iota(I32, (N_EXPERTS, tm), 0)
    hit = jnp.where((eio == e0) | (eio == e1), 1.0, 0.0)
    cnt_ref[0] = jnp.sum(hit, axis=1, keepdims=True)


def _outproj(yg, yh, yy, h, w_out, g, b, wr_t, br, alpha, tm=512):
    T, D = h.shape
    nr = wr_t.shape[0]
    wr_hi = wr_t.astype(BF16)
    wr_lo = (wr_t - wr_hi.astype(F32)).astype(BF16)
    full = lambda shp: pl.BlockSpec(shp, lambda i: (0,) * len(shp))
    return pl.pallas_call(
        functools.partial(_outproj_kernel, alpha=alpha),
        grid=(T // tm,),
        in_specs=[pl.BlockSpec((GLA_W, tm), lambda i: (0, i)),
                  pl.BlockSpec((HG_W, tm), lambda i: (0, i)),
                  pl.BlockSpec((HY_WIDTH, tm), lambda i: (0, i)),
                  pl.BlockSpec((tm, D), lambda i: (i, 0)),
                  full((D, D)), full((1, D)), full((1, D)), full((nr, D)), full((nr, D)), full((nr, 1))],
        out_specs=[pl.BlockSpec((tm, D), lambda i: (i, 0)),
                   pl.BlockSpec((TOP_K, tm), lambda i: (0, i)),
                   pl.BlockSpec((TOP_K, tm), lambda i: (0, i)),
                   pl.BlockSpec((1, N_EXPERTS, 1), lambda i: (i, 0, 0))],
        out_shape=[jax.ShapeDtypeStruct((T, D), F32),
                   jax.ShapeDtypeStruct((TOP_K, T), I32),
                   jax.ShapeDtypeStruct((TOP_K, T), F32),
                   jax.ShapeDtypeStruct((T // tm, N_EXPERTS, 1), F32)],
        compiler_params=_cparams(("arbitrary",)),
        name="outproj",
    )(yg, yh, yy, h, w_out.astype(BF16), g.reshape(1, D), b.reshape(1, D), wr_hi, wr_lo, br)


def _chunk_loop(n, fn):
    def body(c, carry):
        fn(pl.multiple_of(c * ROW_CHUNK, ROW_CHUNK))
        return carry
    lax.fori_loop(0, n, body, 0)


def _tile_chunks(nch_s, tile):
    return functools.reduce(lambda a, b: a + b, [nch_s[tile * N_EXPERTS + e] for e in range(N_EXPERTS)])


def _dispatch_kernel(offs_s, gst_s, nch_s, tst_s, tn_s, nu_s, h_ref, e_ref, base_ref, tri_ref,
                     pos_ref, xg_hbm, xs_ref, zbuf, sem, zsem, *, tm, LP, bm, nblocks):
    i = pl.program_id(0)
    nt = pl.num_programs(0)
    slot = i % 2

    def chunk_copy(s, src_row, dst_row):
        return pltpu.make_async_copy(xs_ref.at[s, pl.ds(src_row, ROW_CHUNK)],
                                     xg_hbm.at[pl.ds(dst_row, ROW_CHUNK)], sem.at[s])

    def zero_copy(dst_row):
        return pltpu.make_async_copy(zbuf.at[pl.ds(0, ROW_CHUNK)], xg_hbm.at[pl.ds(dst_row, ROW_CHUNK)], zsem.at[0])

    def zero_block(blk):
        return pltpu.make_async_copy(zbuf, xg_hbm.at[pl.ds(pl.multiple_of(blk * bm, bm), bm)], zsem.at[0])

    def wait_tile(tile, s):
        _chunk_loop(_tile_chunks(nch_s, tile), lambda off: chunk_copy(s, 0, 0).wait())

    @pl.when(i == 0)
    def _():
        zbuf[...] = jnp.zeros_like(zbuf)
        for e in range(N_EXPERTS):
            _chunk_loop(tn_s[e], lambda off, e=e: zero_copy(pl.multiple_of(tst_s[e] + off, ROW_CHUNK)).start())
        lax.fori_loop(nu_s[0], nblocks, lambda blk, c: (zero_block(blk).start(), c)[1], 0)
        for e in range(N_EXPERTS):
            _chunk_loop(tn_s[e], lambda off: zero_copy(0).wait())
        lax.fori_loop(nu_s[0], nblocks, lambda blk, c: (zero_block(0).wait(), c)[1], 0)

    e0, e1 = e_ref[0:1, :], e_ref[1:2, :]
    eio = lax.broadcasted_iota(I32, (N_EXPERTS, tm), 0)
    oh0, oh1 = eio == e0, eio == e1
    hit = jnp.where(oh0 | oh1, 1.0, 0.0).astype(BF16)
    posm = base_ref[0] + _dot(hit, tri_ref[...])
    pos0 = jnp.sum(jnp.where(oh0, posm, 0.0), axis=0, keepdims=True).astype(I32)
    pos1 = jnp.sum(jnp.where(oh1, posm, 0.0), axis=0, keepdims=True).astype(I32)
    pos_ref[...] = jnp.concatenate([pos0, pos1], axis=0)
    rio = lax.broadcasted_iota(I32, (LP, tm), 0)
    perm = jnp.where((rio == pos0) | (rio == pos1), 1.0, 0.0).astype(BF16)
    xs_ref[slot] = _dot(perm, h_ref[...].astype(BF16)).astype(BF16)

    @pl.when(i > 0)
    def _():
        wait_tile(i - 1, 1 - slot)

    for e in range(N_EXPERTS):
        k = i * N_EXPERTS + e
        src0, dst0 = offs_s[k], gst_s[k]
        _chunk_loop(nch_s[k], lambda off, src0=src0, dst0=dst0: chunk_copy(
            slot, pl.multiple_of(src0 + off, ROW_CHUNK), pl.multiple_of(dst0 + off, ROW_CHUNK)).start())

    @pl.when(i == nt - 1)
    def _():
        wait_tile(i, slot)


def _moe_dispatch(h1, e_kt, tables, nblocks, bm, tm):
    T, D = h1.shape
    NT = T // tm
    nrows = nblocks * bm
    LP = TOP_K * tm + N_EXPERTS * ROW_CHUNK
    r = np.arange(tm)
    tri = jnp.asarray(r[:, None] < r[None, :], F32).astype(BF16)
    base = tables["offs"].astype(F32).reshape(NT, N_EXPERTS, 1)
    flat = lambda a: a.reshape(-1).astype(I32)
    pos, xg = pl.pallas_call(
        functools.partial(_dispatch_kernel, tm=tm, LP=LP, bm=bm, nblocks=nblocks),
        grid_spec=pltpu.PrefetchScalarGridSpec(
            num_scalar_prefetch=6,
            grid=(NT,),
            in_specs=[pl.BlockSpec((tm, D), lambda i, *_: (i, 0)),
                      pl.BlockSpec((TOP_K, tm), lambda i, *_: (0, i)),
                      pl.BlockSpec((1, N_EXPERTS, 1), lambda i, *_: (i, 0, 0)),
                      pl.BlockSpec((tm, tm), lambda i, *_: (0, 0))],
            out_specs=[pl.BlockSpec((TOP_K, tm), lambda i, *_: (0, i)),
                       pl.BlockSpec(memory_space=pl.ANY)],
            scratch_shapes=[pltpu.VMEM((2, LP, D), BF16), pltpu.VMEM((bm, D), BF16),
                            pltpu.SemaphoreType.DMA((2,)), pltpu.SemaphoreType.DMA((1,))]),
        out_shape=[jax.ShapeDtypeStruct((TOP_K, T), I32), jax.ShapeDtypeStruct((nrows, D), BF16)],
        compiler_params=_cparams(("arbitrary",)),
        name="moe_dispatch",
    )(flat(tables["offs"]), flat(tables["gstart"]), flat(tables["nch"]), flat(tables["tail_start"]),
      flat(tables["tail_n"]), tables["nused"], h1, e_kt, base, tri)
    return pos, xg


def _ffn_kernel(be_ref, nu_ref, x_ref, wg_ref, wu_ref, wd_ref, y_ref):
    used = pl.program_id(0) < nu_ref[0]

    @pl.when(used)
    def _():
        x = x_ref[...]
        a = _dot(x, wg_ref[0])
        hid = (a * _sigmoid(a)) * _dot(x, wu_ref[0])
        y_ref[...] = _dot(hid.astype(BF16), wd_ref[0]).astype(BF16)

    @pl.when(jnp.logical_not(used))
    def _():
        y_ref[...] = jnp.zeros_like(y_ref)


def _moe_ffn(xg, block_e, nused, wg, wu, wd, bm):
    nrows, D = xg.shape
    NB = nrows // bm
    DE = wg.shape[-1]
    row = lambda j, be, nu: (jnp.minimum(j, nu[0] - 1), 0)
    wsel = lambda j, be, nu: (be[j], 0, 0)
    return pl.pallas_call(
        _ffn_kernel,
        grid_spec=pltpu.PrefetchScalarGridSpec(
            num_scalar_prefetch=2,
            grid=(NB,),
            in_specs=[pl.BlockSpec((bm, D), row),
                      pl.BlockSpec((1, D, DE), wsel),
                      pl.BlockSpec((1, D, DE), wsel),
                      pl.BlockSpec((1, DE, D), wsel)],
            out_specs=pl.BlockSpec((bm, D), lambda j, be, nu: (j, 0))),
        out_shape=jax.ShapeDtypeStruct((nrows, D), BF16),
        compiler_params=_cparams(("arbitrary",)),
        name="moe_ffn",
    )(block_e, nused, xg, wg, wu, wd)


def _combine_kernel(offs_s, gst_s, nch_s, yb_hbm, pos_ref, w_ref, h_ref, g_ref, b_ref, o_ref, ybl, sem,
                    *, tm, LP, alpha):
    i = pl.program_id(0)
    nt = pl.num_programs(0)
    slot = i % 2

    def chunk_copy(s, src_row, dst_row):
        return pltpu.make_async_copy(yb_hbm.at[pl.ds(src_row, ROW_CHUNK)],
                                     ybl.at[s, pl.ds(dst_row, ROW_CHUNK)], sem.at[s])

    def issue(tile, s):
        for e in range(N_EXPERTS):
            k = tile * N_EXPERTS + e
            src0, dst0 = gst_s[k], offs_s[k]
            _chunk_loop(nch_s[k], lambda off, src0=src0, dst0=dst0: chunk_copy(
                s, pl.multiple_of(src0 + off, ROW_CHUNK), pl.multiple_of(dst0 + off, ROW_CHUNK)).start())

    @pl.when(i == 0)
    def _():
        ybl[...] = jnp.zeros_like(ybl)
        issue(0, 0)

    @pl.when(i + 1 < nt)
    def _():
        issue(i + 1, 1 - slot)

    _chunk_loop(_tile_chunks(nch_s, i), lambda off: chunk_copy(slot, 0, 0).wait())
    pos0, pos1 = pos_ref[0:1, :], pos_ref[1:2, :]
    w0, w1 = w_ref[0:1, :], w_ref[1:2, :]
    rio = lax.broadcasted_iota(I32, (LP, tm), 0)
    pw = (jnp.where(rio == pos0, w0, 0.0) + jnp.where(rio == pos1, w1, 0.0)).astype(BF16)
    ffn = _dot(pw, ybl[slot], _TN)
    o_ref[...] = _layer_norm(alpha * h_ref[...] + ffn, g_ref[...], b_ref[...])


def _moe_combine(yb, h1, pos, w_kt, tables, g, b, alpha, tm):
    T, D = h1.shape
    LP = TOP_K * tm + N_EXPERTS * ROW_CHUNK
    flat = lambda a: a.reshape(-1).astype(I32)
    return pl.pallas_call(
        functools.partial(_combine_kernel, tm=tm, LP=LP, alpha=alpha),
        grid_spec=pltpu.PrefetchScalarGridSpec(
            num_scalar_prefetch=3,
            grid=(T // tm,),
            in_specs=[pl.BlockSpec(memory_space=pl.ANY),
                      pl.BlockSpec((TOP_K, tm), lambda i, *_: (0, i)),
                      pl.BlockSpec((TOP_K, tm), lambda i, *_: (0, i)),
                      pl.BlockSpec((tm, D), lambda i, *_: (i, 0)),
                      pl.BlockSpec((1, D), lambda i, *_: (0, 0)),
                      pl.BlockSpec((1, D), lambda i, *_: (0, 0))],
            out_specs=pl.BlockSpec((tm, D), lambda i, *_: (i, 0)),
            scratch_shapes=[pltpu.VMEM((2, LP, D), BF16), pltpu.SemaphoreType.DMA((2,))]),
        out_shape=jax.ShapeDtypeStruct((T, D), F32),
        compiler_params=_cparams(("arbitrary",)),
        name="moe_combine",
    )(flat(tables["offs"]), flat(tables["gstart"]), flat(tables["nch"]), yb, pos, w_kt, h1,
      g.reshape(1, D), b.reshape(1, D))


def _dispatch_tables(cnt, bm, nblocks):
    padlen = ((cnt + ROW_CHUNK - 1) // ROW_CHUNK) * ROW_CHUNK
    offs = jnp.cumsum(padlen, axis=1) - padlen
    tot = jnp.sum(padlen, axis=0)
    region = ((tot + bm - 1) // bm) * bm
    rend = jnp.cumsum(region)
    rstart = rend - region
    gstart = rstart[None, :] + jnp.cumsum(padlen, axis=0) - padlen
    blk_row = jnp.arange(nblocks, dtype=I32)[:, None] * bm
    block_e = jnp.minimum(jnp.sum((blk_row >= rend[None, :]).astype(I32), axis=1), N_EXPERTS - 1)
    return dict(offs=offs, gstart=gstart, nch=padlen // ROW_CHUNK, tail_start=rstart + tot,
                tail_n=(region - tot) // ROW_CHUNK, block_e=block_e.astype(I32),
                nused=(rend[-1:] // bm).astype(I32))


def _in_perm():
    splits = (192, 192, 384, 384, 32, 384, 768, 384, 384, 768)
    offs = np.concatenate([[0], np.cumsum(splits)])
    gq, gk, gv, gg, ga, hq, hf, hi, hgt, hyu = [np.arange(offs[i], offs[i + 1]) for i in range(10)]
    perm = np.concatenate([hq, hi, hgt, hf, gv, gg, gq, gk, hyu, ga])
    assert perm.shape[0] == D_IN
    return perm


def kernel(x, ln_in_g, ln_in_b, w_in, gla_wa2, gla_ba, gla_norm_g, hg_lb_logits, hg_norm_g, hy_conv_w, hy_conv_b, hy_w1, hy_b1, hy_freq, hy_w2, hy_b2, hy_w3, hy_bias, w_out, ln1_g, ln1_b, moe_wr_g, moe_br_g, moe_wr_e, moe_br_e, moe_w_gate, moe_w_up, moe_w_down, ln2_g, ln2_b):
    B, L, D = x.shape
    T = B * L
    depth = w_in.shape[0]
    alpha = (2 * depth) ** 0.25
    bm = 256
    tmr = 512
    nblocks = -(-(T * TOP_K + (T // tmr) * N_EXPERTS * (ROW_CHUNK - 1) + N_EXPERTS * (bm - 1)) // bm)
    perm = _in_perm()
    consts = _dft_consts(L)

    p = jax.nn.softmax(hg_lb_logits.astype(F32), axis=0)
    lbs = jnp.cumsum(p, axis=0) - p[0:1]
    lbc = jnp.stack([jnp.log1p(-lbs), jnp.log(jnp.maximum(lbs, LB_FLOOR)), 1.0 - lbs], axis=-1)
    lbc = jnp.concatenate([lbc, jnp.zeros(lbc.shape[:-1] + (5,), F32)], axis=-1)

    h = x.reshape(T, D)
    for l in range(depth):
        w_t = w_in[l][:, perm].T.astype(BF16)
        proj_t, h = _inproj(h, ln_in_g, ln_in_b, w_t, apply_ln=(l == 0))
        wa_t = jnp.swapaxes(gla_wa2[l], 1, 2)
        y_gla = _scan(proj_t, "gla", B, L, (wa_t, gla_ba[l].reshape(2, GLA_K, 1)), gla_norm_g[l].reshape(GLA_W, 1))
        y_hg = _scan(proj_t, "hg", B, L, (lbc[l],), hg_norm_g[l].reshape(HG_W, 1))
        kt = _hyena_filters(L, hy_w1[l], hy_b1[l], hy_freq[l], hy_w2[l], hy_b2[l], hy_w3[l])
        kspec = _hyena_spectrum(kt, consts, L)
        y_hy = _hyena(proj_t, kspec, consts, hy_conv_w[l], hy_conv_b[l], hy_bias[l], B, L)
        nr = N_GROUPS + N_EXPERTS
        nrp = -(-nr // 8) * 8
        wr_t = jnp.zeros((nrp, D), F32).at[:nr].set(jnp.concatenate([moe_wr_g[l], moe_wr_e[l]], axis=1).T)
        br = jnp.zeros((nrp, 1), F32).at[:nr, 0].set(jnp.concatenate([moe_br_g[l], moe_br_e[l]]))
        h1, e_kt, w_kt, cnt = _outproj(y_gla, y_hg, y_hy, h, w_out[l], ln1_g[l], ln1_b[l], wr_t, br, alpha, tm=tmr)
        tables = _dispatch_tables(cnt.reshape(T // tmr, N_EXPERTS).astype(I32), bm, nblocks)
        pos, xg = _moe_dispatch(h1, e_kt, tables, nblocks, bm, tmr)
        yb = _moe_ffn(xg, tables["block_e"], tables["nused"], moe_w_gate[l].astype(BF16),
                      moe_w_up[l].astype(BF16), moe_w_down[l].astype(BF16), bm)
        h = _moe_combine(yb, h1, pos, w_kt, tables, ln2_g[l], ln2_b[l], alpha, tmr)
    return h.reshape(B, L, D)
```

```python
import functools
import math

import numpy as np
import jax
import jax.numpy as jnp
from jax import lax
from jax.experimental import pallas as pl
from jax.experimental.pallas import tpu as pltpu

F32 = jnp.float32
BF16 = jnp.bfloat16
I32 = jnp.int32

GLA_HEADS, GLA_DK, GLA_DV, GLA_LOWRANK, GLA_TAU = 6, 32, 64, 16, 16.0
HG_HEADS, HG_DK, HG_DV = 6, 64, 64
HY_WIDTH, HY_ORDER, HY_EMB, HY_FFN, HY_INNER = 256, 2, 33, 64, 2
HY_FAST_DECAY, HY_SLOW_DECAY, HY_TARGET = 0.3, 1.5, 1e-2
N_GROUPS, EXPERTS_PER_GROUP = 4, 4
N_EXPERTS = N_GROUPS * EXPERTS_PER_GROUP
TOP_K = 2
LN_EPS, RMS_EPS, LB_FLOOR = 1e-5, 1e-6, 1e-30

LANES = 128
SCAN_CHUNK = LANES
ROW_CHUNK = 16
VMEM_LIMIT = 56 * 1024 * 1024

GLA_W = GLA_HEADS * GLA_DV
GLA_K = GLA_HEADS * GLA_DK
HG_W = HG_HEADS * HG_DV
HG_K = HG_HEADS * HG_DK
OFF_HQ, OFF_HI, OFF_HGT, OFF_HF = 0, 384, 768, 1152
OFF_GV, OFF_GG, OFF_GQ, OFF_GK, OFF_HY, OFF_GA = 1920, 2304, 2688, 2880, 3072, 3840
D_IN = 3872


def _dot(a, b, dims=(((1,), (0,)), ((), ())), precision=None):
    return lax.dot_general(a, b, dims, preferred_element_type=F32, precision=precision)


_NT = (((1,), (1,)), ((), ()))
_TN = (((0,), (0,)), ((), ()))


def _layer_norm(x, g, b):
    mu = jnp.mean(x, axis=-1, keepdims=True)
    xc = x - mu
    var = jnp.mean(xc * xc, axis=-1, keepdims=True)
    return xc * lax.rsqrt(var + LN_EPS) * g + b


def _log_sigmoid(x):
    return jnp.minimum(x, 0.0) - jnp.log(1.0 + jnp.exp(-jnp.abs(x)))


def _sigmoid(x):
    return 0.5 + 0.5 * jnp.tanh(0.5 * x)


def _cparams(sem):
    return pltpu.CompilerParams(dimension_semantics=sem, vmem_limit_bytes=VMEM_LIMIT)


def _inproj_kernel(x_ref, g_ref, b_ref, w_ref, *outs, apply_ln):
    x = x_ref[...]
    if apply_ln:
        x = _layer_norm(x, g_ref[...], b_ref[...])
        outs[1][...] = x
    outs[0][...] = _dot(w_ref[...], x.astype(BF16), _NT)


def _inproj(x, g, b, w_t, apply_ln, tm=512):
    T, D = x.shape
    n_out = w_t.shape[0]
    out_shape = [jax.ShapeDtypeStruct((n_out, T), F32)]
    out_specs = [pl.BlockSpec((n_out, tm), lambda i: (0, i))]
    if apply_ln:
        out_shape.append(jax.ShapeDtypeStruct((T, D), F32))
        out_specs.append(pl.BlockSpec((tm, D), lambda i: (i, 0)))
    res = pl.pallas_call(
        functools.partial(_inproj_kernel, apply_ln=apply_ln),
        grid=(T // tm,),
        in_specs=[pl.BlockSpec((tm, D), lambda i: (i, 0)),
                  pl.BlockSpec((1, D), lambda i: (0, 0)),
                  pl.BlockSpec((1, D), lambda i: (0, 0)),
                  pl.BlockSpec((n_out, D), lambda i: (0, 0))],
        out_specs=out_specs,
        out_shape=out_shape,
        compiler_params=_cparams(("arbitrary",)),
        name="inproj",
    )(x, g.reshape(1, D), b.reshape(1, D), w_t)
    return res if apply_ln else (res[0], x)


def _scan_kernel(*refs, mode, H, K, V, TB, NCB):
    if mode == "gla":
        (qf_ref, qb_ref, kf_ref, kb_ref, vf_ref, vb_ref, gaf_ref, gab_ref, wa_ref, ba_ref,
         gtf_ref, gtb_ref, ng_ref, y_ref, s_ref, oacc_ref) = refs
        q_refs, k_refs, ga_refs = (qf_ref, qb_ref), (kf_ref, kb_ref), (gaf_ref, gab_ref)
    else:
        (qf_ref, qb_ref, zf_ref, zb_ref, vf_ref, vb_ref, lbc_ref,
         gtf_ref, gtb_ref, ng_ref, y_ref, s_ref, oacc_ref) = refs
        q_refs, z_refs = (qf_ref, qb_ref), (zf_ref, zb_ref)
    v_refs, gate_refs = (vf_ref, vb_ref), (gtf_ref, gtb_ref)
    C = SCAN_CHUNK
    half = C // 2
    nchunks = TB // C
    n = pl.program_id(1)
    blocks = (n, NCB - 1 - n)

    @pl.when(n == 0)
    def _():
        s_ref[...] = jnp.zeros_like(s_ref)

    def gates(d, sl):
        if mode == "gla":
            a = _dot(wa_ref[d], ga_refs[d][:, sl], precision=lax.Precision.HIGHEST) + ba_ref[d]
            g = _log_sigmoid(a) * (1.0 / GLA_TAU)
            q = q_refs[d][:, sl] * (K ** -0.5)
            k = k_refs[d][:, sl]
        else:
            z = z_refs[d][:, sl]
            lbc = lbc_ref[d]
            c1, c2, c3 = lbc[:, 0:1], lbc[:, 1:2], lbc[:, 2:3]
            x1 = c1 + _log_sigmoid(z)
            g = jnp.maximum(x1, c2) + jnp.log(1.0 + jnp.exp(-jnp.abs(x1 - c2)))
            k = c3 * _sigmoid(-z)
            hq = q_refs[d][:, sl]
            q = hq * _sigmoid(hq)
        return q, k, g

    r = lax.broadcasted_iota(I32, (C, C), 0)
    c = lax.broadcasted_iota(I32, (C, C), 1)
    same = (r >= half) == (c >= half)
    lane_lo = lax.broadcasted_iota(I32, (1, C), 1) < half
    consts = (((r <= c).astype(BF16), (r < half) & (c >= half), same & (r <= c),
               (C - 1, half, half // 2, half + half // 2)),
              ((r >= c).astype(BF16), (r >= half) & (c < half), same & (r >= c),
               (0, half - 1, half // 2 - 1, half + half // 2 - 1)))

    def one_dir(d, sl):
        cum, off1, diag, (c_end, c_mid, c_a, c_b) = consts[d]
        q, k, g = gates(d, sl)
        v = v_refs[d][:, sl].astype(BF16)
        g1 = g.astype(BF16)
        r1 = g - g1.astype(F32)
        g2 = r1.astype(BF16)
        g3 = (r1 - g2.astype(F32)).astype(BF16)
        G = _dot(g1, cum) + _dot(g2, cum) + _dot(g3, cum)
        g_end = G[:, c_end:c_end + 1]
        g_mid = G[:, c_mid:c_mid + 1]
        e2 = G - jnp.where(lane_lo, G[:, c_a:c_a + 1], G[:, c_b:c_b + 1])
        q2 = (q * jnp.exp(e2)).astype(BF16)
        k2 = (k * jnp.exp(-e2)).astype(BF16)
        dm = G - g_mid
        x1 = jnp.exp(jnp.where(lane_lo, dm, -dm) if d else jnp.where(lane_lo, -dm, dm))
        q1 = (q * x1).astype(BF16)
        k1 = (k * x1).astype(BF16)
        qg = (q * jnp.exp(G)).astype(BF16)
        kd = (k * jnp.exp(g_end - G)).astype(BF16)
        dec = jnp.exp(g_end)
        outs = []
        for h in range(H):
            rk = slice(h * K, (h + 1) * K)
            rv = slice(h * V, (h + 1) * V)
            p1 = _dot(k1[rk], q1[rk], _TN)
            p2 = _dot(k2[rk], q2[rk], _TN)
            p = jnp.where(off1, p1, jnp.where(diag, p2, 0.0)).astype(BF16)
            s_h = s_ref[d, h]
            outs.append(_dot(v[rv], p) + _dot(s_h.astype(BF16), qg[rk], _TN))
            s_ref[d, h] = dec[rk] * s_h + _dot(kd[rk], v[rv], _NT)
        return jnp.concatenate(outs, axis=0)

    def finish(o, gate):
        act = gate * _sigmoid(gate) if mode == "gla" else _sigmoid(gate)
        ys = []
        for h in range(H):
            oh = o[h * V:(h + 1) * V]
            ms = jnp.mean(oh * oh, axis=0, keepdims=True)
            ys.append(oh * lax.rsqrt(ms + RMS_EPS))
        return (jnp.concatenate(ys, axis=0) * ng_ref[...] * act).astype(y_ref.dtype)

    def chunk(ci, carry):
        offs = (pl.multiple_of(ci * C, C), pl.multiple_of((nchunks - 1 - ci) * C, C))
        sls = [pl.ds(offs[d], C) for d in range(2)]
        tsls = [pl.ds(pl.multiple_of(blocks[d] * TB + offs[d], C), C) for d in range(2)]
        o = [one_dir(d, sls[d]) for d in range(2)]

        @pl.when(2 * n < NCB)
        def _():
            for d in range(2):
                oacc_ref[:, tsls[d]] = o[d]

        @pl.when(2 * n >= NCB)
        def _():
            tot = [o[d] + oacc_ref[:, tsls[d]] for d in range(2)]
            for d in range(2):
                y_ref[:, tsls[d]] = finish(tot[d], gate_refs[d][:, sls[d]])
        return carry

    lax.fori_loop(0, nchunks, chunk, 0)


def _scan(proj_t, mode, B, L, extra, norm_g, TB=512):
    NCB = L // TB
    assert NCB % 2 == 0
    if mode == "gla":
        H, K, V = GLA_HEADS, GLA_DK, GLA_DV
    else:
        H, K, V = HG_HEADS, HG_DK, HG_DV
    HK, HV = H * K, H * V
    cf = lambda b, n: b * NCB + n
    cb = lambda b, n: b * NCB + NCB - 1 - n
    gf = lambda b, n: b * NCB + jnp.maximum(n, NCB // 2)
    gb = lambda b, n: b * NCB + jnp.minimum(NCB - 1 - n, NCB // 2 - 1)

    def pair(rows, off, fwd=cf, bwd=cb, dir_step=0):
        return [pl.BlockSpec((rows, TB), lambda b, n: (off // rows, fwd(b, n))),
                pl.BlockSpec((rows, TB), lambda b, n: (off // rows + dir_step, bwd(b, n)))]

    whole = lambda shp: pl.BlockSpec(shp, lambda b, n: (0,) * len(shp))
    if mode == "gla":
        wa_t, ba = extra
        in_specs = (pair(HK, OFF_GQ) + pair(HK, OFF_GK) + pair(HV, OFF_GV)
                    + pair(GLA_LOWRANK, OFF_GA, dir_step=1)
                    + [whole((2, HK, GLA_LOWRANK)), whole((2, HK, 1))]
                    + pair(HV, OFF_GG, gf, gb) + [whole((HV, 1))])
        args = (proj_t,) * 8 + (wa_t, ba, proj_t, proj_t, norm_g)
    else:
        (lbc,) = extra
        in_specs = (pair(HK, OFF_HQ) + pair(HK, OFF_HF, dir_step=1) + pair(HV, OFF_HI)
                    + [whole((2, HK, 8))] + pair(HV, OFF_HGT, gf, gb) + [whole((HV, 1))])
        args = (proj_t,) * 6 + (lbc, proj_t, proj_t, norm_g)
    return pl.pallas_call(
        functools.partial(_scan_kernel, mode=mode, H=H, K=K, V=V, TB=TB, NCB=NCB),
        grid=(B, NCB),
        in_specs=in_specs,
        out_specs=pl.BlockSpec((HV, L), lambda b, n: (0, b)),
        out_shape=jax.ShapeDtypeStruct((HV, B * L), BF16),
        scratch_shapes=[pltpu.VMEM((2, H, K, V), F32), pltpu.VMEM((HV, L), F32)],
        compiler_params=_cparams(("arbitrary", "arbitrary")),
        name="scan_" + mode,
    )(*args)


def _dft_consts(L):
    N = 2 * L
    NA = N // LANES
    a = np.arange(NA)[:, None] * np.arange(NA)[None, :]
    ca, sa = np.cos(2 * np.pi * a / NA), np.sin(2 * np.pi * a / NA)
    hh = NA // 2
    w1d = np.block([[ca[:, :hh], sa[:, :hh]], [-sa[:, :hh], ca[:, :hh]]])
    w1f = np.concatenate([ca, -sa], axis=0)
    w1i = np.block([[ca[:hh, :], -sa[:hh, :]], [sa[:hh, :], ca[:hh, :]]])
    bb = np.arange(LANES)[:, None] * np.arange(LANES)[None, :]
    cb, sb = np.cos(2 * np.pi * bb / LANES), np.sin(2 * np.pi * bb / LANES)
    w2 = np.block([[cb, -sb], [sb, cb]])
    w2i = np.block([[cb, sb], [-sb, cb]])
    tw = np.arange(NA)[:, None] * np.arange(LANES)[None, :]
    tc, ts = np.cos(2 * np.pi * tw / N), np.sin(2 * np.pi * tw / N)
    bf = lambda m: jnp.asarray(m, dtype=F32).astype(BF16)
    return dict(w1d=bf(w1d), w1f=bf(w1f), w1i=bf(w1i), w2=bf(w2), w2i=bf(w2i),
                tc=jnp.asarray(tc, F32), ts=jnp.asarray(ts, F32))


def _pos_features(L):
    t = np.linspace(0.0, 1.0, L)
    w = 2.0 * np.pi * np.arange(L) / L
    bands = np.linspace(1e-4, (HY_EMB - 1) // 2 - 1, (HY_EMB - 1) // 2)
    z = np.concatenate([t[None, :], np.cos(bands[:, None] * w[None, :]), -np.sin(bands[:, None] * w[None, :])], axis=0)
    kp = -(-HY_EMB // 8) * 8
    z = np.concatenate([z, np.zeros((kp - HY_EMB, L))], axis=0)
    idx = (L - np.arange(L)) % L
    z_rev, t_rev = z[:, idx], t[idx]
    mask = (np.arange(L) >= 1).astype(np.float64)
    return (jnp.asarray(z, F32), jnp.asarray(z_rev, F32), jnp.asarray(t[None, :], F32),
            jnp.asarray(t_rev[None, :], F32), jnp.asarray(mask[None, :], F32))


def _filter_kernel(z_ref, zr_ref, t_ref, tr_ref, m_ref, w1_ref, b1_ref, fr_ref, w2_ref, b2_ref,
                   w3f_ref, w3b_ref, dl_ref, out_ref, h_ref, *, L):
    first = (pl.program_id(0) == 0) & (pl.program_id(1) == 0)
    hi = lax.Precision.HIGHEST

    @pl.when(first)
    def _():
        fr = fr_ref[...]
        for idx, zz in enumerate((z_ref, zr_ref)):
            h = jnp.sin(fr * (_dot(w1_ref[...], zz[...], precision=hi) + b1_ref[...]))
            for i in range(HY_INNER):
                h = jnp.sin(fr * (_dot(w2_ref[i], h, precision=hi) + b2_ref[i]))
            h_ref[idx] = h

    ad = jnp.abs(dl_ref[...])
    kf = _dot(w3f_ref[...], h_ref[0], precision=hi) * jnp.exp(-t_ref[...] * ad)
    kb = _dot(w3b_ref[...], h_ref[1], precision=hi) * jnp.exp(-tr_ref[...] * ad) * m_ref[...]
    den = jnp.sum(jnp.abs(kf), axis=1, keepdims=True) + jnp.sum(jnp.abs(kb), axis=1, keepdims=True)
    scale = 1.0 / (jnp.maximum(den, 1e-12) * (2.0 * L))
    out_ref[0, :, 0:L] = kf * scale
    out_ref[0, :, L:2 * L] = kb * scale


def _hyena_filters(L, w1, b1, freq, w2, b2, w3, cg=64):
    z, z_rev, t, t_rev, mask = _pos_features(L)
    kp = z.shape[0]
    w1_t = jnp.zeros((HY_FFN, kp), F32).at[:, :HY_EMB].set(w1.T)
    w2_t = jnp.swapaxes(w2, 1, 2)
    w3_t = w3.T
    max_decay = math.log(HY_TARGET) / HY_FAST_DECAY
    min_decay = math.log(HY_TARGET) / HY_SLOW_DECAY
    deltas = jnp.asarray(np.linspace(min_decay, max_decay, HY_WIDTH).reshape(HY_WIDTH, 1), F32)
    ncg = HY_WIDTH // cg
    full = lambda shp: pl.BlockSpec(shp, lambda o, j: (0,) * len(shp))
    return pl.pallas_call(
        functools.partial(_filter_kernel, L=L),
        grid=(HY_ORDER, ncg),
        in_specs=[full((kp, L)), full((kp, L)), full((1, L)), full((1, L)), full((1, L)),
                  full((HY_FFN, kp)), full((HY_FFN, 1)), full((HY_FFN, 1)),
                  full((HY_INNER, HY_FFN, HY_FFN)), full((HY_INNER, HY_FFN, 1)),
                  pl.BlockSpec((cg, HY_FFN), lambda o, j: (o * 2 * ncg + j, 0)),
                  pl.BlockSpec((cg, HY_FFN), lambda o, j: (o * 2 * ncg + ncg + j, 0)),
                  pl.BlockSpec((cg, 1), lambda o, j: (j, 0))],
        out_specs=pl.BlockSpec((1, cg, 2 * L), lambda o, j: (o, j, 0)),
        out_shape=jax.ShapeDtypeStruct((HY_ORDER, HY_WIDTH, 2 * L), F32),
        scratch_shapes=[pltpu.VMEM((2, HY_FFN, L), F32)],
        compiler_params=_cparams(("arbitrary", "arbitrary")),
        name="hyena_filter",
    )(z, z_rev, t, t_rev, mask, w1_t, b1.reshape(HY_FFN, 1), freq.reshape(HY_FFN, 1), w2_t,
      b2.reshape(HY_INNER, HY_FFN, 1), w3_t, w3_t, deltas)


def _fft_fwd(xs, w1, tc, ts, w2, NA):
    a = _dot(w1, xs)
    a_re, a_im = a[:NA], a[NA:]
    b_re = a_re * tc + a_im * ts
    b_im = a_im * tc - a_re * ts
    return _dot(jnp.concatenate([b_re, b_im], axis=1).astype(BF16), w2)


def _spectrum_kernel(k_ref, w1_ref, tc_ref, ts_ref, w2_ref, out_ref, *, NA, cg):
    def body(ci, carry):
        out_ref[0, ci] = _fft_fwd(k_ref[0, ci].astype(BF16), w1_ref[...], tc_ref[...], ts_ref[...], w2_ref[...], NA)
        return carry
    lax.fori_loop(0, cg, body, 0)


def _hyena_spectrum(kt, consts, L, cg=32):
    NA = 2 * L // LANES
    k4 = kt.reshape(HY_ORDER, HY_WIDTH, NA, LANES)
    full = lambda shp: pl.BlockSpec(shp, lambda o, j: (0,) * len(shp))
    return pl.pallas_call(
        functools.partial(_spectrum_kernel, NA=NA, cg=cg),
        grid=(HY_ORDER, HY_WIDTH // cg),
        in_specs=[pl.BlockSpec((1, cg, NA, LANES), lambda o, j: (o, j, 0, 0)),
                  full((2 * NA, NA)), full((NA, LANES)), full((NA, LANES)), full((2 * LANES, 2 * LANES))],
        out_specs=pl.BlockSpec((1, cg, NA, 2 * LANES), lambda o, j: (o, j, 0, 0)),
        out_shape=jax.ShapeDtypeStruct((HY_ORDER, HY_WIDTH, NA, 2 * LANES), F32),
        compiler_params=_cparams(("arbitrary", "arbitrary")),
        name="hyena_spectrum",
    )(k4, consts["w1f"], consts["tc"], consts["ts"], consts["w2"])


def _hyena_kernel(cw_ref, cb_ref, hb_ref, v_ref, x1_ref, x2_ref, ks_ref, w1d_ref, w1i_ref, tc_ref, ts_ref,
                  w2_ref, w2i_ref, y_ref, *, NA, cg, B):
    hh = NA // 2
    j = pl.program_id(0)
    row = lax.broadcasted_iota(I32, (hh, LANES), 0)
    lane = lax.broadcasted_iota(I32, (hh, LANES), 1)
    first = (row == 0) & (lane == 0)
    last = (row == hh - 1) & (lane == LANES - 1)

    def short_conv(x, ch):
        r1 = pltpu.roll(x, 1, 1)
        prev = jnp.where(lane == 0, pltpu.roll(r1, 1, 0), r1)
        prev = jnp.where(first, 0.0, prev)
        r2 = pltpu.roll(x, LANES - 1, 1)
        nxt = jnp.where(lane == LANES - 1, pltpu.roll(r2, hh - 1, 0), r2)
        nxt = jnp.where(last, 0.0, nxt)
        return cw_ref[0, ch] * prev + cw_ref[1, ch] * x + cw_ref[2, ch] * nxt + cb_ref[ch]

    P = B // 2
    tc, ts = tc_ref[...], ts_ref[...]
    tc2 = jnp.concatenate([tc, tc], axis=1)
    ts2 = jnp.concatenate([ts, ts], axis=1)
    lane2 = lambda x, cc: x[:, cc * LANES:(cc + 1) * LANES]

    def body(it, carry):
        cis = [2 * it + cc for cc in range(2)]
        chs = [j * cg + ci for ci in cis]
        seqs = [(p, cc) for p in range(P) for cc in range(2)]
        z = {(p, cc): [short_conv(v_ref[cis[cc], 2 * p + r], chs[cc]) for r in range(2)] for p, cc in seqs}
        gate_refs = (x1_ref, x2_ref)
        gates = [{(p, cc): [short_conv(gate_refs[o][cis[cc], 2 * p + r], (o + 1) * HY_WIDTH + chs[cc])
                            for r in range(2)] for p, cc in seqs} for o in range(HY_ORDER)]
        for o in range(HY_ORDER):
            lhs = []
            for p in range(P):
                rhs = jnp.concatenate([jnp.concatenate(z[(p, cc)], axis=0) for cc in range(2)], axis=1)
                a = _dot(w1d_ref[...], rhs.astype(BF16))
                a_re, a_im = a[:NA], a[NA:]
                b_re = a_re * tc2 + a_im * ts2
                b_im = a_im * tc2 - a_re * ts2
                lhs += [jnp.concatenate([lane2(b_re, cc), lane2(b_im, cc)], axis=1) for cc in range(2)]
            x = _dot(jnp.concatenate(lhs, axis=0).astype(BF16), w2_ref[...])
            ys = []
            for idx, (p, cc) in enumerate(seqs):
                xb = x[idx * NA:(idx + 1) * NA]
                ks = ks_ref[o, cis[cc]]
                x_re, x_im = lane2(xb, 0), lane2(xb, 1)
                k_re, k_im = lane2(ks, 0), lane2(ks, 1)
                ys.append(jnp.concatenate([x_re * k_re - x_im * k_im, x_re * k_im + x_im * k_re], axis=1))
            bq = _dot(jnp.concatenate(ys, axis=0).astype(BF16), w2i_ref[...])
            for p in range(P):
                cr, cim = [], []
                for cc in range(2):
                    blk = bq[(2 * p + cc) * NA:(2 * p + cc + 1) * NA]
                    b_re, b_im = lane2(blk, 0), lane2(blk, 1)
                    cr.append(b_re * tc - b_im * ts)
                    cim.append(b_re * ts + b_im * tc)
                rhs = jnp.concatenate([jnp.concatenate(cr, axis=1), jnp.concatenate(cim, axis=1)], axis=0)
                conv = _dot(w1i_ref[...], rhs.astype(BF16))
                for cc in range(2):
                    bias = hb_ref[o, chs[cc]]
                    z[(p, cc)] = [gates[o][(p, cc)][r] * (lane2(conv, cc)[r * hh:(r + 1) * hh] + z[(p, cc)][r] * bias)
                                  for r in range(2)]
        for p, cc in seqs:
            for r in range(2):
                y_ref[cis[cc], 2 * p + r] = z[(p, cc)][r]
        return carry

    lax.fori_loop(0, cg // 2, body, 0)


def _hyena(proj_t, kspec, consts, conv_w, conv_b, bias, B, L, cg=8):
    NA = 2 * L // LANES
    hh = NA // 2
    u = proj_t.reshape(proj_t.shape[0], B, hh, LANES)
    ncg = HY_WIDTH // cg
    base = OFF_HY // cg
    smem = pl.BlockSpec(memory_space=pltpu.SMEM)
    full = lambda shp: pl.BlockSpec(shp, lambda j: (0,) * len(shp))
    blk = lambda off: pl.BlockSpec((cg, B, hh, LANES), lambda j: (off + j, 0, 0, 0))
    y = pl.pallas_call(
        functools.partial(_hyena_kernel, NA=NA, cg=cg, B=B),
        grid=(ncg,),
        in_specs=[smem, smem, smem, blk(base), blk(base + ncg), blk(base + 2 * ncg),
                  pl.BlockSpec((HY_ORDER, cg, NA, 2 * LANES), lambda j: (0, j, 0, 0)),
                  full((2 * NA, NA)), full((NA, 2 * NA)), full((NA, LANES)), full((NA, LANES)),
                  full((2 * LANES, 2 * LANES)), full((2 * LANES, 2 * LANES))],
        out_specs=pl.BlockSpec((cg, B, hh, LANES), lambda j: (j, 0, 0, 0)),
        out_shape=jax.ShapeDtypeStruct((HY_WIDTH, B, hh, LANES), F32),
        compiler_params=_cparams(("arbitrary",)),
        name="hyena_conv",
    )(conv_w, conv_b, bias, u, u, u, kspec, consts["w1d"], consts["w1i"], consts["tc"], consts["ts"],
      consts["w2"], consts["w2i"])
    return y.reshape(HY_WIDTH, B * L)


def _outproj_kernel(yg_ref, yh_ref, yy_ref, h_ref, wo_ref, g_ref, b_ref, wrh_ref, wrl_ref, br_ref,
                    h1_ref, e_ref, w_ref, cnt_ref, *, alpha):
    mix = _dot(yg_ref[...].astype(BF16), wo_ref[0:GLA_W], _TN)
    mix += _dot(yh_ref[...].astype(BF16), wo_ref[GLA_W:GLA_W + HG_W], _TN)
    mix += _dot(yy_ref[...].astype(BF16), wo_ref[GLA_W + HG_W:], _TN)
    h1 = _layer_norm(alpha * h_ref[...] + mix, g_ref[...], b_ref[...])
    h1_ref[...] = h1
    hi = h1.astype(BF16)
    lo = (h1 - hi.astype(F32)).astype(BF16)
    lg = _dot(wrh_ref[...], hi, _NT) + _dot(wrh_ref[...], lo, _NT) + _dot(wrl_ref[...], hi, _NT) + br_ref[...]
    tm = lg.shape[1]
    gl = [lg[g:g + 1] for g in range(N_GROUPS)]
    gmax = functools.reduce(jnp.maximum, gl)
    gidx = jnp.full((1, tm), N_GROUPS - 1, I32)
    for g in range(N_GROUPS - 2, -1, -1):
        gidx = jnp.where(gl[g] == gmax, g, gidx)
    gsum = functools.reduce(jnp.add, [jnp.exp(x - gmax) for x in gl])
    g_val = 1.0 / gsum
    el = []
    for r in range(EXPERTS_PER_GROUP):
        acc = jnp.zeros((1, tm), F32)
        for g in range(N_GROUPS):
            row = N_GROUPS + g * EXPERTS_PER_GROUP + r
            acc = jnp.where(gidx == g, lg[row:row + 1], acc)
        el.append(acc)
    emax = functools.reduce(jnp.maximum, el)
    pe = [jnp.exp(x - emax) for x in el]
    esum = functools.reduce(jnp.add, pe)
    pe = [x / esum for x in pe]
    v1 = functools.reduce(jnp.maximum, pe)
    i1 = jnp.full((1, tm), EXPERTS_PER_GROUP - 1, I32)
    for r in range(EXPERTS_PER_GROUP - 2, -1, -1):
        i1 = jnp.where(pe[r] == v1, r, i1)
    pe2 = [jnp.where(i1 == r, -1.0, pe[r]) for r in range(EXPERTS_PER_GROUP)]
    v2 = functools.reduce(jnp.maximum, pe2)
    i2 = jnp.full((1, tm), EXPERTS_PER_GROUP - 1, I32)
    for r in range(EXPERTS_PER_GROUP - 2, -1, -1):
        i2 = jnp.where(pe2[r] == v2, r, i2)
    den = v1 + v2
    e0 = gidx * EXPERTS_PER_GROUP + i1
    e1 = gidx * EXPERTS_PER_GROUP + i2
    e_ref[...] = jnp.concatenate([e0, e1], axis=0)
    w_ref[...] = jnp.concatenate([g_val * (v1 / den), g_val * (v2 / den)], axis=0)
    eio = lax.broadcasted_iota(I32, (N_EXPERTS, tm), 0)
    hit = jnp.where((eio == e0) | (eio == e1), 1.0, 0.0)
    cnt_ref[0] = jnp.sum(hit, axis=1, keepdims=True)


def _outproj(yg, yh, yy, h, w_out, g, b, wr_t, br, alpha, tm=512):
    T, D = h.shape
    nr = wr_t.shape[0]
    wr_hi = wr_t.astype(BF16)
    wr_lo = (wr_t - wr_hi.astype(F32)).astype(BF16)
    full = lambda shp: pl.BlockSpec(shp, lambda i: (0,) * len(shp))
    return pl.pallas_call(
        functools.partial(_outproj_kernel, alpha=alpha),
        grid=(T // tm,),
        in_specs=[pl.BlockSpec((GLA_W, tm), lambda i: (0, i)),
                  pl.BlockSpec((HG_W, tm), lambda i: (0, i)),
                  pl.BlockSpec((HY_WIDTH, tm), lambda i: (0, i)),
                  pl.BlockSpec((tm, D), lambda i: (i, 0)),
                  full((D, D)), full((1, D)), full((1, D)), full((nr, D)), full((nr, D)), full((nr, 1))],
        out_specs=[pl.BlockSpec((tm, D), lambda i: (i, 0)),
                   pl.BlockSpec((TOP_K, tm), lambda i: (0, i)),
                   pl.BlockSpec((TOP_K, tm), lambda i: (0, i)),
                   pl.BlockSpec((1, N_EXPERTS, 1), lambda i: (i, 0, 0))],
        out_shape=[jax.ShapeDtypeStruct((T, D), F32),
                   jax.ShapeDtypeStruct((TOP_K, T), I32),
                   jax.ShapeDtypeStruct((TOP_K, T), F32),
                   jax.ShapeDtypeStruct((T // tm, N_EXPERTS, 1), F32)],
        compiler_params=_cparams(("arbitrary",)),
        name="outproj",
    )(yg, yh, yy, h, w_out.astype(BF16), g.reshape(1, D), b.reshape(1, D), wr_hi, wr_lo, br)


def _chunk_loop(n, fn):
    def body(c, carry):
        fn(pl.multiple_of(c * ROW_CHUNK, ROW_CHUNK))
        return carry
    lax.fori_loop(0, n, body, 0)


def _tile_chunks(nch_s, tile):
    return functools.reduce(lambda a, b: a + b, [nch_s[tile * N_EXPERTS + e] for e in range(N_EXPERTS)])


def _dispatch_kernel(offs_s, gst_s, nch_s, tst_s, tn_s, nu_s, h_ref, e_ref, base_ref, tri_ref,
                     pos_ref, xg_hbm, xs_ref, zbuf, sem, zsem, *, tm, LP, bm, nblocks):
    i = pl.program_id(0)
    nt = pl.num_programs(0)
    slot = i % 2

    def chunk_copy(s, src_row, dst_row):
        return pltpu.make_async_copy(xs_ref.at[s, pl.ds(src_row, ROW_CHUNK)],
                                     xg_hbm.at[pl.ds(dst_row, ROW_CHUNK)], sem.at[s])

    def zero_copy(dst_row):
        return pltpu.make_async_copy(zbuf.at[pl.ds(0, ROW_CHUNK)], xg_hbm.at[pl.ds(dst_row, ROW_CHUNK)], zsem.at[0])

    def zero_block(blk):
        return pltpu.make_async_copy(zbuf, xg_hbm.at[pl.ds(pl.multiple_of(blk * bm, bm), bm)], zsem.at[0])

    def wait_tile(tile, s):
        _chunk_loop(_tile_chunks(nch_s, tile), lambda off: chunk_copy(s, 0, 0).wait())

    @pl.when(i == 0)
    def _():
        zbuf[...] = jnp.zeros_like(zbuf)
        for e in range(N_EXPERTS):
            _chunk_loop(tn_s[e], lambda off, e=e: zero_copy(pl.multiple_of(tst_s[e] + off, ROW_CHUNK)).start())
        lax.fori_loop(nu_s[0], nblocks, lambda blk, c: (zero_block(blk).start(), c)[1], 0)
        for e in range(N_EXPERTS):
            _chunk_loop(tn_s[e], lambda off: zero_copy(0).wait())
        lax.fori_loop(nu_s[0], nblocks, lambda blk, c: (zero_block(0).wait(), c)[1], 0)

    e0, e1 = e_ref[0:1, :], e_ref[1:2, :]
    eio = lax.broadcasted_iota(I32, (N_EXPERTS, tm), 0)
    oh0, oh1 = eio == e0, eio == e1
    hit = jnp.where(oh0 | oh1, 1.0, 0.0).astype(BF16)
    posm = base_ref[0] + _dot(hit, tri_ref[...])
    pos0 = jnp.sum(jnp.where(oh0, posm, 0.0), axis=0, keepdims=True).astype(I32)
    pos1 = jnp.sum(jnp.where(oh1, posm, 0.0), axis=0, keepdims=True).astype(I32)
    pos_ref[...] = jnp.concatenate([pos0, pos1], axis=0)
    rio = lax.broadcasted_iota(I32, (LP, tm), 0)
    perm = jnp.where((rio == pos0) | (rio == pos1), 1.0, 0.0).astype(BF16)
    xs_ref[slot] = _dot(perm, h_ref[...].astype(BF16)).astype(BF16)

    @pl.when(i > 0)
    def _():
        wait_tile(i - 1, 1 - slot)

    for e in range(N_EXPERTS):
        k = i * N_EXPERTS + e
        src0, dst0 = offs_s[k], gst_s[k]
        _chunk_loop(nch_s[k], lambda off, src0=src0, dst0=dst0: chunk_copy(
            slot, pl.multiple_of(src0 + off, ROW_CHUNK), pl.multiple_of(dst0 + off, ROW_CHUNK)).start())

    @pl.when(i == nt - 1)
    def _():
        wait_tile(i, slot)


def _moe_dispatch(h1, e_kt, tables, nblocks, bm, tm):
    T, D = h1.shape
    NT = T // tm
    nrows = nblocks * bm
    LP = TOP_K * tm + N_EXPERTS * ROW_CHUNK
    r = np.arange(tm)
    tri = jnp.asarray(r[:, None] < r[None, :], F32).astype(BF16)
    base = tables["offs"].astype(F32).reshape(NT, N_EXPERTS, 1)
    flat = lambda a: a.reshape(-1).astype(I32)
    pos, xg = pl.pallas_call(
        functools.partial(_dispatch_kernel, tm=tm, LP=LP, bm=bm, nblocks=nblocks),
        grid_spec=pltpu.PrefetchScalarGridSpec(
            num_scalar_prefetch=6,
            grid=(NT,),
            in_specs=[pl.BlockSpec((tm, D), lambda i, *_: (i, 0)),
                      pl.BlockSpec((TOP_K, tm), lambda i, *_: (0, i)),
                      pl.BlockSpec((1, N_EXPERTS, 1), lambda i, *_: (i, 0, 0)),
                      pl.BlockSpec((tm, tm), lambda i, *_: (0, 0))],
            out_specs=[pl.BlockSpec((TOP_K, tm), lambda i, *_: (0, i)),
                       pl.BlockSpec(memory_space=pl.ANY)],
            scratch_shapes=[pltpu.VMEM((2, LP, D), BF16), pltpu.VMEM((bm, D), BF16),
                            pltpu.SemaphoreType.DMA((2,)), pltpu.SemaphoreType.DMA((1,))]),
        out_shape=[jax.ShapeDtypeStruct((TOP_K, T), I32), jax.ShapeDtypeStruct((nrows, D), BF16)],
        compiler_params=_cparams(("arbitrary",)),
        name="moe_dispatch",
    )(flat(tables["offs"]), flat(tables["gstart"]), flat(tables["nch"]), flat(tables["tail_start"]),
      flat(tables["tail_n"]), tables["nused"], h1, e_kt, base, tri)
    return pos, xg


def _ffn_kernel(be_ref, nu_ref, x_ref, wg_ref, wu_ref, wd_ref, y_ref, wgb, wub, wdb):
    j = pl.program_id(0)
    used = j < nu_ref[0]

    @pl.when((j == 0) | (be_ref[j] != be_ref[jnp.maximum(j - 1, 0)]))
    def _():
        wgb[...] = wg_ref[0].astype(BF16)
        wub[...] = wu_ref[0].astype(BF16)
        wdb[...] = wd_ref[0].astype(BF16)

    @pl.when(used)
    def _():
        x = x_ref[...]
        a = _dot(x, wgb[...])
        hid = (a * _sigmoid(a)) * _dot(x, wub[...])
        y_ref[...] = _dot(hid.astype(BF16), wdb[...]).astype(BF16)

    @pl.when(jnp.logical_not(used))
    def _():
        y_ref[...] = jnp.zeros_like(y_ref)


def _moe_ffn(xg, block_e, nused, wg, wu, wd, bm, first_expert=0):
    nrows, D = xg.shape
    NB = nrows // bm
    DE = wg.shape[-1]
    row = lambda j, be, nu: (jnp.minimum(j, nu[0] - 1), 0)
    wsel = lambda j, be, nu: (first_expert + be[j], 0, 0)
    return pl.pallas_call(
        _ffn_kernel,
        grid_spec=pltpu.PrefetchScalarGridSpec(
            num_scalar_prefetch=2,
            grid=(NB,),
            in_specs=[pl.BlockSpec((bm, D), row),
                      pl.BlockSpec((1, D, DE), wsel),
                      pl.BlockSpec((1, D, DE), wsel),
                      pl.BlockSpec((1, DE, D), wsel)],
            out_specs=pl.BlockSpec((bm, D), lambda j, be, nu: (j, 0)),
            scratch_shapes=[pltpu.VMEM((D, DE), BF16), pltpu.VMEM((D, DE), BF16), pltpu.VMEM((DE, D), BF16)]),
        out_shape=jax.ShapeDtypeStruct((nrows, D), BF16),
        compiler_params=_cparams(("arbitrary",)),
        name="moe_ffn",
    )(block_e, nused, xg, wg, wu, wd)


def _combine_kernel(offs_s, gst_s, nch_s, yb_hbm, pos_ref, w_ref, h_ref, g_ref, b_ref, o_ref, ybl, sem,
                    *, tm, LP, alpha):
    i = pl.program_id(0)
    nt = pl.num_programs(0)
    slot = i % 2

    def chunk_copy(s, src_row, dst_row):
        return pltpu.make_async_copy(yb_hbm.at[pl.ds(src_row, ROW_CHUNK)],
                                     ybl.at[s, pl.ds(dst_row, ROW_CHUNK)], sem.at[s])

    def issue(tile, s):
        for e in range(N_EXPERTS):
            k = tile * N_EXPERTS + e
            src0, dst0 = gst_s[k], offs_s[k]
            _chunk_loop(nch_s[k], lambda off, src0=src0, dst0=dst0: chunk_copy(
                s, pl.multiple_of(src0 + off, ROW_CHUNK), pl.multiple_of(dst0 + off, ROW_CHUNK)).start())

    @pl.when(i == 0)
    def _():
        ybl[...] = jnp.zeros_like(ybl)
        issue(0, 0)

    @pl.when(i + 1 < nt)
    def _():
        issue(i + 1, 1 - slot)

    _chunk_loop(_tile_chunks(nch_s, i), lambda off: chunk_copy(slot, 0, 0).wait())
    pos0, pos1 = pos_ref[0:1, :], pos_ref[1:2, :]
    w0, w1 = w_ref[0:1, :], w_ref[1:2, :]
    rio = lax.broadcasted_iota(I32, (LP, tm), 0)
    pw = (jnp.where(rio == pos0, w0, 0.0) + jnp.where(rio == pos1, w1, 0.0)).astype(BF16)
    ffn = _dot(pw, ybl[slot], _TN)
    o_ref[...] = _layer_norm(alpha * h_ref[...] + ffn, g_ref[...], b_ref[...])


def _moe_combine(yb, h1, pos, w_kt, tables, g, b, alpha, tm):
    T, D = h1.shape
    LP = TOP_K * tm + N_EXPERTS * ROW_CHUNK
    flat = lambda a: a.reshape(-1).astype(I32)
    return pl.pallas_call(
        functools.partial(_combine_kernel, tm=tm, LP=LP, alpha=alpha),
        grid_spec=pltpu.PrefetchScalarGridSpec(
            num_scalar_prefetch=3,
            grid=(T // tm,),
            in_specs=[pl.BlockSpec(memory_space=pl.ANY),
                      pl.BlockSpec((TOP_K, tm), lambda i, *_: (0, i)),
                      pl.BlockSpec((TOP_K, tm), lambda i, *_: (0, i)),
                      pl.BlockSpec((tm, D), lambda i, *_: (i, 0)),
                      pl.BlockSpec((1, D), lambda i, *_: (0, 0)),
                      pl.BlockSpec((1, D), lambda i, *_: (0, 0))],
            out_specs=pl.BlockSpec((tm, D), lambda i, *_: (i, 0)),
            scratch_shapes=[pltpu.VMEM((2, LP, D), BF16), pltpu.SemaphoreType.DMA((2,))]),
        out_shape=jax.ShapeDtypeStruct((T, D), F32),
        compiler_params=_cparams(("arbitrary",)),
        name="moe_combine",
    )(flat(tables["offs"]), flat(tables["gstart"]), flat(tables["nch"]), yb, pos, w_kt, h1,
      g.reshape(1, D), b.reshape(1, D))


def _dispatch_tables(cnt, bm, nblocks):
    padlen = ((cnt + ROW_CHUNK - 1) // ROW_CHUNK) * ROW_CHUNK
    offs = jnp.cumsum(padlen, axis=1) - padlen
    tot = jnp.sum(padlen, axis=0)
    region = ((tot + bm - 1) // bm) * bm
    rend = jnp.cumsum(region)
    rstart = rend - region
    gstart = rstart[None, :] + jnp.cumsum(padlen, axis=0) - padlen
    blk_row = jnp.arange(nblocks, dtype=I32)[:, None] * bm
    block_e = jnp.minimum(jnp.sum((blk_row >= rend[None, :]).astype(I32), axis=1), N_EXPERTS - 1)
    return dict(offs=offs, gstart=gstart, nch=padlen // ROW_CHUNK, tail_start=rstart + tot,
                tail_n=(region - tot) // ROW_CHUNK, block_e=block_e.astype(I32),
                nused=(rend[-1:] // bm).astype(I32))


def _in_perm():
    splits = (192, 192, 384, 384, 32, 384, 768, 384, 384, 768)
    offs = np.concatenate([[0], np.cumsum(splits)])
    gq, gk, gv, gg, ga, hq, hf, hi, hgt, hyu = [np.arange(offs[i], offs[i + 1]) for i in range(10)]
    perm = np.concatenate([hq, hi, hgt, hf, gv, gg, gq, gk, hyu, ga])
    assert perm.shape[0] == D_IN
    return perm


def kernel(x, ln_in_g, ln_in_b, w_in, gla_wa2, gla_ba, gla_norm_g, hg_lb_logits, hg_norm_g, hy_conv_w, hy_conv_b, hy_w1, hy_b1, hy_freq, hy_w2, hy_b2, hy_w3, hy_bias, w_out, ln1_g, ln1_b, moe_wr_g, moe_br_g, moe_wr_e, moe_br_e, moe_w_gate, moe_w_up, moe_w_down, ln2_g, ln2_b):
    B, L, D = x.shape
    T = B * L
    depth = w_in.shape[0]
    alpha = (2 * depth) ** 0.25
    bm = 256
    tmr = 512
    nblocks = -(-(T * TOP_K + (T // tmr) * N_EXPERTS * (ROW_CHUNK - 1) + N_EXPERTS * (bm - 1)) // bm)
    perm = _in_perm()
    consts = _dft_consts(L)

    p = jax.nn.softmax(hg_lb_logits.astype(F32), axis=0)
    lbs = jnp.cumsum(p, axis=0) - p[0:1]
    lbc = jnp.stack([jnp.log1p(-lbs), jnp.log(jnp.maximum(lbs, LB_FLOOR)), 1.0 - lbs], axis=-1)
    lbc = jnp.concatenate([lbc, jnp.zeros(lbc.shape[:-1] + (5,), F32)], axis=-1)

    wg_all = moe_w_gate.reshape((depth * N_EXPERTS,) + moe_w_gate.shape[2:])
    wu_all = moe_w_up.reshape((depth * N_EXPERTS,) + moe_w_up.shape[2:])
    wd_all = moe_w_down.reshape((depth * N_EXPERTS,) + moe_w_down.shape[2:])
    h = x.reshape(T, D)
    for l in range(depth):
        w_t = w_in[l][:, perm].T.astype(BF16)
        proj_t, h = _inproj(h, ln_in_g, ln_in_b, w_t, apply_ln=(l == 0))
        wa_t = jnp.swapaxes(gla_wa2[l], 1, 2)
        y_gla = _scan(proj_t, "gla", B, L, (wa_t, gla_ba[l].reshape(2, GLA_K, 1)), gla_norm_g[l].reshape(GLA_W, 1))
        y_hg = _scan(proj_t, "hg", B, L, (lbc[l],), hg_norm_g[l].reshape(HG_W, 1))
        kt = _hyena_filters(L, hy_w1[l], hy_b1[l], hy_freq[l], hy_w2[l], hy_b2[l], hy_w3[l])
        kspec = _hyena_spectrum(kt, consts, L)
        y_hy = _hyena(proj_t, kspec, consts, hy_conv_w[l], hy_conv_b[l], hy_bias[l], B, L)
        nr = N_GROUPS + N_EXPERTS
        nrp = -(-nr // 8) * 8
        wr_t = jnp.zeros((nrp, D), F32).at[:nr].set(jnp.concatenate([moe_wr_g[l], moe_wr_e[l]], axis=1).T)
        br = jnp.zeros((nrp, 1), F32).at[:nr, 0].set(jnp.concatenate([moe_br_g[l], moe_br_e[l]]))
        h1, e_kt, w_kt, cnt = _outproj(y_gla, y_hg, y_hy, h, w_out[l], ln1_g[l], ln1_b[l], wr_t, br, alpha, tm=tmr)
        tables = _dispatch_tables(cnt.reshape(T // tmr, N_EXPERTS).astype(I32), bm, nblocks)
        pos, xg = _moe_dispatch(h1, e_kt, tables, nblocks, bm, tmr)
        yb = _moe_ffn(xg, tables["block_e"], tables["nused"], wg_all, wu_all, wd_all, bm, first_expert=l * N_EXPERTS)
        h = _moe_combine(yb, h1, pos, w_kt, tables, ln2_g[l], ln2_b[l], alpha, tmr)
    return h.reshape(B, L, D)
```

```python
import functools
import math

import numpy as np
import jax
import jax.numpy as jnp
from jax import lax
from jax.experimental import pallas as pl
from jax.experimental.pallas import tpu as pltpu

F32 = jnp.float32
BF16 = jnp.bfloat16
I32 = jnp.int32

GLA_HEADS, GLA_DK, GLA_DV, GLA_LOWRANK, GLA_TAU = 6, 32, 64, 16, 16.0
HG_HEADS, HG_DK, HG_DV = 6, 64, 64
HY_WIDTH, HY_ORDER, HY_EMB, HY_FFN, HY_INNER = 256, 2, 33, 64, 2
HY_FAST_DECAY, HY_SLOW_DECAY, HY_TARGET = 0.3, 1.5, 1e-2
N_GROUPS, EXPERTS_PER_GROUP = 4, 4
N_EXPERTS = N_GROUPS * EXPERTS_PER_GROUP
TOP_K = 2
LN_EPS, RMS_EPS, LB_FLOOR = 1e-5, 1e-6, 1e-30

LANES = 128
SCAN_CHUNK = LANES
ROW_CHUNK = 16
VMEM_LIMIT = 56 * 1024 * 1024

GLA_W = GLA_HEADS * GLA_DV
GLA_K = GLA_HEADS * GLA_DK
HG_W = HG_HEADS * HG_DV
HG_K = HG_HEADS * HG_DK
OFF_HQ, OFF_HI, OFF_HGT, OFF_HF = 0, 384, 768, 1152
OFF_GV, OFF_GG, OFF_GQ, OFF_GK, OFF_HY, OFF_GA = 1920, 2304, 2688, 2880, 3072, 3840
D_IN = 3872


def _dot(a, b, dims=(((1,), (0,)), ((), ())), precision=None):
    return lax.dot_general(a, b, dims, preferred_element_type=F32, precision=precision)


_NT = (((1,), (1,)), ((), ()))
_TN = (((0,), (0,)), ((), ()))


def _layer_norm(x, g, b):
    mu = jnp.mean(x, axis=-1, keepdims=True)
    xc = x - mu
    var = jnp.mean(xc * xc, axis=-1, keepdims=True)
    return xc * lax.rsqrt(var + LN_EPS) * g + b


def _log_sigmoid(x):
    return jnp.minimum(x, 0.0) - jnp.log(1.0 + jnp.exp(-jnp.abs(x)))


def _sigmoid(x):
    return 0.5 + 0.5 * jnp.tanh(0.5 * x)


def _cparams(sem):
    return pltpu.CompilerParams(dimension_semantics=sem, vmem_limit_bytes=VMEM_LIMIT)


def _inproj_kernel(x_ref, g_ref, b_ref, w_ref, *outs, apply_ln):
    x = x_ref[...]
    if apply_ln:
        x = _layer_norm(x, g_ref[...], b_ref[...])
        outs[1][...] = x
    outs[0][...] = _dot(w_ref[...], x.astype(BF16), _NT)


def _inproj(x, g, b, w_t, apply_ln, tm=512):
    T, D = x.shape
    n_out = w_t.shape[0]
    out_shape = [jax.ShapeDtypeStruct((n_out, T), F32)]
    out_specs = [pl.BlockSpec((n_out, tm), lambda i: (0, i))]
    if apply_ln:
        out_shape.append(jax.ShapeDtypeStruct((T, D), F32))
        out_specs.append(pl.BlockSpec((tm, D), lambda i: (i, 0)))
    res = pl.pallas_call(
        functools.partial(_inproj_kernel, apply_ln=apply_ln),
        grid=(T // tm,),
        in_specs=[pl.BlockSpec((tm, D), lambda i: (i, 0)),
                  pl.BlockSpec((1, D), lambda i: (0, 0)),
                  pl.BlockSpec((1, D), lambda i: (0, 0)),
                  pl.BlockSpec((n_out, D), lambda i: (0, 0))],
        out_specs=out_specs,
        out_shape=out_shape,
        compiler_params=_cparams(("arbitrary",)),
        name="inproj",
    )(x, g.reshape(1, D), b.reshape(1, D), w_t)
    return res if apply_ln else (res[0], x)


def _scan_kernel(*refs, mode, H, K, V, TB, NCB):
    if mode == "gla":
        (qf_ref, qb_ref, kf_ref, kb_ref, vf_ref, vb_ref, gaf_ref, gab_ref, wa_ref, ba_ref,
         gtf_ref, gtb_ref, ng_ref, y_ref, s_ref, oacc_ref) = refs
        q_refs, k_refs, ga_refs = (qf_ref, qb_ref), (kf_ref, kb_ref), (gaf_ref, gab_ref)
    else:
        (qf_ref, qb_ref, zf_ref, zb_ref, vf_ref, vb_ref, lbc_ref,
         gtf_ref, gtb_ref, ng_ref, y_ref, s_ref, oacc_ref) = refs
        q_refs, z_refs = (qf_ref, qb_ref), (zf_ref, zb_ref)
    v_refs, gate_refs = (vf_ref, vb_ref), (gtf_ref, gtb_ref)
    C = SCAN_CHUNK
    half = C // 2
    nchunks = TB // C
    n = pl.program_id(1)
    blocks = (n, NCB - 1 - n)

    @pl.when(n == 0)
    def _():
        s_ref[...] = jnp.zeros_like(s_ref)

    def gates(d, sl):
        if mode == "gla":
            a = _dot(wa_ref[d], ga_refs[d][:, sl], precision=lax.Precision.HIGHEST) + ba_ref[d]
            g = _log_sigmoid(a) * (1.0 / GLA_TAU)
            q = q_refs[d][:, sl] * (K ** -0.5)
            k = k_refs[d][:, sl]
        else:
            one_m_lb, lb_floor = lbc_ref[d, :, 0:C], lbc_ref[d, :, C:2 * C]
            s = one_m_lb * _sigmoid(z_refs[d][:, sl])
            g = jnp.log(s + lb_floor)
            k = one_m_lb - s
            hq = q_refs[d][:, sl]
            q = hq * _sigmoid(hq)
        return q, k, g

    r = lax.broadcasted_iota(I32, (C, C), 0)
    c = lax.broadcasted_iota(I32, (C, C), 1)
    same = (r >= half) == (c >= half)
    lane_lo = lax.broadcasted_iota(I32, (1, C), 1) < half
    consts = (((r <= c).astype(BF16), (r < half) & (c >= half), same & (r <= c),
               (C - 1, half, half // 2, half + half // 2)),
              ((r >= c).astype(BF16), (r >= half) & (c < half), same & (r >= c),
               (0, half - 1, half // 2 - 1, half + half // 2 - 1)))

    heads = [(slice(h * K, (h + 1) * K), slice(h * V, (h + 1) * V)) for h in range(H)]

    def prep(d, sl):
        cum, _, _, (c_end, c_mid, c_a, c_b) = consts[d]
        q, k, g = gates(d, sl)
        v = v_refs[d][:, sl].astype(BF16)
        g1 = g.astype(BF16)
        r1 = g - g1.astype(F32)
        g2 = r1.astype(BF16)
        g3 = (r1 - g2.astype(F32)).astype(BF16)
        G = _dot(g1, cum) + _dot(g2, cum) + _dot(g3, cum)
        g_end = G[:, c_end:c_end + 1]
        g_mid = G[:, c_mid:c_mid + 1]
        e2 = G - jnp.where(lane_lo, G[:, c_a:c_a + 1], G[:, c_b:c_b + 1])
        q2 = (q * jnp.exp(e2)).astype(BF16)
        k2 = (k * jnp.exp(-e2)).astype(BF16)
        dm = G - g_mid
        x1 = jnp.exp(jnp.where(lane_lo, dm, -dm) if d else jnp.where(lane_lo, -dm, dm))
        q1 = (q * x1).astype(BF16)
        k1 = (k * x1).astype(BF16)
        qg = (q * jnp.exp(G)).astype(BF16)
        kd = (k * jnp.exp(g_end - G)).astype(BF16)
        return dict(q1=q1, k1=k1, q2=q2, k2=k2, qg=qg, kd=kd, v=v, dec=jnp.exp(g_end))

    def scores(x):
        return [(_dot(x["k1"][rk], x["q1"][rk], _TN), _dot(x["k2"][rk], x["q2"][rk], _TN)) for rk, _ in heads]

    def masked(d, ps):
        _, off1, diag, _ = consts[d]
        return [jnp.where(off1, p1, jnp.where(diag, p2, 0.0)).astype(BF16) for p1, p2 in ps]

    def outputs(d, x, pm):
        s_old = [s_ref[d, h] for h in range(H)]
        o = [_dot(x["v"][rv], pm[h]) + _dot(s_old[h].astype(BF16), x["qg"][rk], _TN)
             for h, (rk, rv) in enumerate(heads)]
        return jnp.concatenate(o, axis=0), s_old

    def update(d, x, s_old):
        for h, (rk, rv) in enumerate(heads):
            s_ref[d, h] = x["dec"][rk] * s_old[h] + _dot(x["kd"][rk], x["v"][rv], _NT)

    def finish(o, gate):
        act = gate * _sigmoid(gate) if mode == "gla" else _sigmoid(gate)
        ys = []
        for h in range(H):
            oh = o[h * V:(h + 1) * V]
            ms = jnp.mean(oh * oh, axis=0, keepdims=True)
            ys.append(oh * lax.rsqrt(ms + RMS_EPS))
        return (jnp.concatenate(ys, axis=0) * ng_ref[...] * act).astype(y_ref.dtype)

    def chunk(ci, carry):
        offs = (pl.multiple_of(ci * C, C), pl.multiple_of((nchunks - 1 - ci) * C, C))
        sls = [pl.ds(offs[d], C) for d in range(2)]
        tsls = [pl.ds(pl.multiple_of(blocks[d] * TB + offs[d], C), C) for d in range(2)]
        x0 = prep(0, sls[0])
        sc0 = scores(x0)
        x1 = prep(1, sls[1])
        sc1 = scores(x1)
        o0, s0 = outputs(0, x0, masked(0, sc0))
        o1, s1 = outputs(1, x1, masked(1, sc1))
        update(0, x0, s0)
        update(1, x1, s1)
        o = (o0, o1)

        @pl.when(2 * n < NCB)
        def _():
            for d in range(2):
                oacc_ref[:, tsls[d]] = o[d]

        @pl.when(2 * n >= NCB)
        def _():
            tot = [o[d] + oacc_ref[:, tsls[d]] for d in range(2)]
            for d in range(2):
                y_ref[:, tsls[d]] = finish(tot[d], gate_refs[d][:, sls[d]])
        return carry

    lax.fori_loop(0, nchunks, chunk, 0)


def _scan(proj_t, mode, B, L, extra, norm_g, TB=512):
    NCB = L // TB
    assert NCB % 2 == 0
    if mode == "gla":
        H, K, V = GLA_HEADS, GLA_DK, GLA_DV
    else:
        H, K, V = HG_HEADS, HG_DK, HG_DV
    HK, HV = H * K, H * V
    cf = lambda b, n: b * NCB + n
    cb = lambda b, n: b * NCB + NCB - 1 - n
    gf = lambda b, n: b * NCB + jnp.maximum(n, NCB // 2)
    gb = lambda b, n: b * NCB + jnp.minimum(NCB - 1 - n, NCB // 2 - 1)

    def pair(rows, off, fwd=cf, bwd=cb, dir_step=0):
        return [pl.BlockSpec((rows, TB), lambda b, n: (off // rows, fwd(b, n))),
                pl.BlockSpec((rows, TB), lambda b, n: (off // rows + dir_step, bwd(b, n)))]

    whole = lambda shp: pl.BlockSpec(shp, lambda b, n: (0,) * len(shp))
    if mode == "gla":
        wa_t, ba = extra
        in_specs = (pair(HK, OFF_GQ) + pair(HK, OFF_GK) + pair(HV, OFF_GV)
                    + pair(GLA_LOWRANK, OFF_GA, dir_step=1)
                    + [whole((2, HK, GLA_LOWRANK)), whole((2, HK, 1))]
                    + pair(HV, OFF_GG, gf, gb) + [whole((HV, 1))])
        args = (proj_t,) * 8 + (wa_t, ba, proj_t, proj_t, norm_g)
    else:
        (lbc,) = extra
        in_specs = (pair(HK, OFF_HQ) + pair(HK, OFF_HF, dir_step=1) + pair(HV, OFF_HI)
                    + [whole((2, HK, 2 * LANES))] + pair(HV, OFF_HGT, gf, gb) + [whole((HV, 1))])
        args = (proj_t,) * 6 + (lbc, proj_t, proj_t, norm_g)
    return pl.pallas_call(
        functools.partial(_scan_kernel, mode=mode, H=H, K=K, V=V, TB=TB, NCB=NCB),
        grid=(B, NCB),
        in_specs=in_specs,
        out_specs=pl.BlockSpec((HV, L), lambda b, n: (0, b)),
        out_shape=jax.ShapeDtypeStruct((HV, B * L), BF16),
        scratch_shapes=[pltpu.VMEM((2, H, K, V), F32), pltpu.VMEM((HV, L), F32)],
        compiler_params=_cparams(("arbitrary", "arbitrary")),
        name="scan_" + mode,
    )(*args)


def _dft_consts(L):
    N = 2 * L
    NA = N // LANES
    a = np.arange(NA)[:, None] * np.arange(NA)[None, :]
    ca, sa = np.cos(2 * np.pi * a / NA), np.sin(2 * np.pi * a / NA)
    hh = NA // 2
    w1d = np.block([[ca[:, :hh], sa[:, :hh]], [-sa[:, :hh], ca[:, :hh]]])
    w1f = np.concatenate([ca, -sa], axis=0)
    w1i = np.block([[ca[:hh, :], -sa[:hh, :]], [sa[:hh, :], ca[:hh, :]]])
    bb = np.arange(LANES)[:, None] * np.arange(LANES)[None, :]
    cb, sb = np.cos(2 * np.pi * bb / LANES), np.sin(2 * np.pi * bb / LANES)
    w2 = np.block([[cb, -sb], [sb, cb]])
    w2i = np.block([[cb, sb], [-sb, cb]])
    tw = np.arange(NA)[:, None] * np.arange(LANES)[None, :]
    tc, ts = np.cos(2 * np.pi * tw / N), np.sin(2 * np.pi * tw / N)
    bf = lambda m: jnp.asarray(m, dtype=F32).astype(BF16)
    return dict(w1d=bf(w1d), w1f=bf(w1f), w1i=bf(w1i), w2=bf(w2), w2i=bf(w2i),
                tc=jnp.asarray(tc, F32), ts=jnp.asarray(ts, F32))


def _pos_features(L):
    t = np.linspace(0.0, 1.0, L)
    w = 2.0 * np.pi * np.arange(L) / L
    bands = np.linspace(1e-4, (HY_EMB - 1) // 2 - 1, (HY_EMB - 1) // 2)
    z = np.concatenate([t[None, :], np.cos(bands[:, None] * w[None, :]), -np.sin(bands[:, None] * w[None, :])], axis=0)
    kp = -(-HY_EMB // 8) * 8
    z = np.concatenate([z, np.zeros((kp - HY_EMB, L))], axis=0)
    idx = (L - np.arange(L)) % L
    z_rev, t_rev = z[:, idx], t[idx]
    mask = (np.arange(L) >= 1).astype(np.float64)
    return (jnp.asarray(z, F32), jnp.asarray(z_rev, F32), jnp.asarray(t[None, :], F32),
            jnp.asarray(t_rev[None, :], F32), jnp.asarray(mask[None, :], F32))


def _filter_kernel(z_ref, zr_ref, t_ref, tr_ref, m_ref, w1_ref, b1_ref, fr_ref, w2_ref, b2_ref,
                   w3f_ref, w3b_ref, dl_ref, out_ref, h_ref, *, L):
    first = (pl.program_id(0) == 0) & (pl.program_id(1) == 0)
    hi = lax.Precision.HIGHEST

    @pl.when(first)
    def _():
        fr = fr_ref[...]
        for idx, zz in enumerate((z_ref, zr_ref)):
            h = jnp.sin(fr * (_dot(w1_ref[...], zz[...], precision=hi) + b1_ref[...]))
            for i in range(HY_INNER):
                h = jnp.sin(fr * (_dot(w2_ref[i], h, precision=hi) + b2_ref[i]))
            h_ref[idx] = h.astype(BF16)

    ad = jnp.abs(dl_ref[...])
    kf = _dot(w3f_ref[...].astype(BF16), h_ref[0]) * jnp.exp(-t_ref[...] * ad)
    kb = _dot(w3b_ref[...].astype(BF16), h_ref[1]) * jnp.exp(-tr_ref[...] * ad) * m_ref[...]
    den = jnp.sum(jnp.abs(kf), axis=1, keepdims=True) + jnp.sum(jnp.abs(kb), axis=1, keepdims=True)
    scale = 1.0 / (jnp.maximum(den, 1e-12) * (2.0 * L))
    out_ref[0, :, 0:L] = kf * scale
    out_ref[0, :, L:2 * L] = kb * scale


def _hyena_filters(L, w1, b1, freq, w2, b2, w3, cg=64):
    z, z_rev, t, t_rev, mask = _pos_features(L)
    kp = z.shape[0]
    w1_t = jnp.zeros((HY_FFN, kp), F32).at[:, :HY_EMB].set(w1.T)
    w2_t = jnp.swapaxes(w2, 1, 2)
    w3_t = w3.T
    max_decay = math.log(HY_TARGET) / HY_FAST_DECAY
    min_decay = math.log(HY_TARGET) / HY_SLOW_DECAY
    deltas = jnp.asarray(np.linspace(min_decay, max_decay, HY_WIDTH).reshape(HY_WIDTH, 1), F32)
    ncg = HY_WIDTH // cg
    full = lambda shp: pl.BlockSpec(shp, lambda o, j: (0,) * len(shp))
    return pl.pallas_call(
        functools.partial(_filter_kernel, L=L),
        grid=(HY_ORDER, ncg),
        in_specs=[full((kp, L)), full((kp, L)), full((1, L)), full((1, L)), full((1, L)),
                  full((HY_FFN, kp)), full((HY_FFN, 1)), full((HY_FFN, 1)),
                  full((HY_INNER, HY_FFN, HY_FFN)), full((HY_INNER, HY_FFN, 1)),
                  pl.BlockSpec((cg, HY_FFN), lambda o, j: (o * 2 * ncg + j, 0)),
                  pl.BlockSpec((cg, HY_FFN), lambda o, j: (o * 2 * ncg + ncg + j, 0)),
                  pl.BlockSpec((cg, 1), lambda o, j: (j, 0))],
        out_specs=pl.BlockSpec((1, cg, 2 * L), lambda o, j: (o, j, 0)),
        out_shape=jax.ShapeDtypeStruct((HY_ORDER, HY_WIDTH, 2 * L), F32),
        scratch_shapes=[pltpu.VMEM((2, HY_FFN, L), BF16)],
        compiler_params=_cparams(("arbitrary", "arbitrary")),
        name="hyena_filter",
    )(z, z_rev, t, t_rev, mask, w1_t, b1.reshape(HY_FFN, 1), freq.reshape(HY_FFN, 1), w2_t,
      b2.reshape(HY_INNER, HY_FFN, 1), w3_t, w3_t, deltas)


def _spectrum_kernel(k_ref, w1_ref, tc_ref, ts_ref, w2_ref, out_ref, *, NA, cg):
    tc2 = jnp.concatenate([tc_ref[...]] * 2, axis=1)
    ts2 = jnp.concatenate([ts_ref[...]] * 2, axis=1)
    G = 4

    def body(it, carry):
        lhs = []
        for pr in range(G // 2):
            rhs = jnp.concatenate([k_ref[0, G * it + 2 * pr + cc] for cc in range(2)], axis=1)
            a = _dot(w1_ref[...], rhs.astype(BF16))
            a_re, a_im = a[:NA], a[NA:]
            b_re = a_re * tc2 + a_im * ts2
            b_im = a_im * tc2 - a_re * ts2
            lhs += [jnp.concatenate([b_re[:, cc * LANES:(cc + 1) * LANES], b_im[:, cc * LANES:(cc + 1) * LANES]], axis=1)
                    for cc in range(2)]
        x = _dot(jnp.concatenate(lhs, axis=0).astype(BF16), w2_ref[...])
        for i in range(G):
            out_ref[0, G * it + i] = x[i * NA:(i + 1) * NA]
        return carry
    lax.fori_loop(0, cg // G, body, 0)


def _hyena_spectrum(kt, consts, L, cg=32):
    NA = 2 * L // LANES
    k4 = kt.reshape(HY_ORDER, HY_WIDTH, NA, LANES)
    full = lambda shp: pl.BlockSpec(shp, lambda o, j: (0,) * len(shp))
    return pl.pallas_call(
        functools.partial(_spectrum_kernel, NA=NA, cg=cg),
        grid=(HY_ORDER, HY_WIDTH // cg),
        in_specs=[pl.BlockSpec((1, cg, NA, LANES), lambda o, j: (o, j, 0, 0)),
                  full((2 * NA, NA)), full((NA, LANES)), full((NA, LANES)), full((2 * LANES, 2 * LANES))],
        out_specs=pl.BlockSpec((1, cg, NA, 2 * LANES), lambda o, j: (o, j, 0, 0)),
        out_shape=jax.ShapeDtypeStruct((HY_ORDER, HY_WIDTH, NA, 2 * LANES), F32),
        compiler_params=_cparams(("arbitrary", "arbitrary")),
        name="hyena_spectrum",
    )(k4, consts["w1f"], consts["tc"], consts["ts"], consts["w2"])


def _hyena_kernel(cw_ref, cb_ref, hb_ref, v_ref, x1_ref, x2_ref, ks_ref, w1d_ref, w1i_ref, tc_ref, ts_ref,
                  w2_ref, w2i_ref, y_ref, *, NA, cg, B):
    hh = NA // 2
    j = pl.program_id(0)
    row = lax.broadcasted_iota(I32, (hh, LANES), 0)
    lane = lax.broadcasted_iota(I32, (hh, LANES), 1)
    first = (row == 0) & (lane == 0)
    last = (row == hh - 1) & (lane == LANES - 1)

    def short_conv(x, ch):
        r1 = pltpu.roll(x, 1, 1)
        prev = jnp.where(lane == 0, pltpu.roll(r1, 1, 0), r1)
        prev = jnp.where(first, 0.0, prev)
        r2 = pltpu.roll(x, LANES - 1, 1)
        nxt = jnp.where(lane == LANES - 1, pltpu.roll(r2, hh - 1, 0), r2)
        nxt = jnp.where(last, 0.0, nxt)
        return cw_ref[0, ch] * prev + cw_ref[1, ch] * x + cw_ref[2, ch] * nxt + cb_ref[ch]

    P = B // 2
    tc, ts = tc_ref[...], ts_ref[...]
    tc2 = jnp.concatenate([tc, tc], axis=1)
    ts2 = jnp.concatenate([ts, ts], axis=1)
    lane2 = lambda x, cc: x[:, cc * LANES:(cc + 1) * LANES]

    def body(it, carry):
        cis = [2 * it + cc for cc in range(2)]
        chs = [j * cg + ci for ci in cis]
        seqs = [(p, cc) for p in range(P) for cc in range(2)]
        z = {(p, cc): [short_conv(v_ref[cis[cc], 2 * p + r], chs[cc]) for r in range(2)] for p, cc in seqs}
        gate_refs = (x1_ref, x2_ref)
        gates = [{(p, cc): [short_conv(gate_refs[o][cis[cc], 2 * p + r], (o + 1) * HY_WIDTH + chs[cc])
                            for r in range(2)] for p, cc in seqs} for o in range(HY_ORDER)]
        for o in range(HY_ORDER):
            lhs = []
            for p in range(P):
                rhs = jnp.concatenate([jnp.concatenate(z[(p, cc)], axis=0) for cc in range(2)], axis=1)
                a = _dot(w1d_ref[...], rhs.astype(BF16))
                a_re, a_im = a[:NA], a[NA:]
                b_re = a_re * tc2 + a_im * ts2
                b_im = a_im * tc2 - a_re * ts2
                lhs += [jnp.concatenate([lane2(b_re, cc), lane2(b_im, cc)], axis=1) for cc in range(2)]
            x = _dot(jnp.concatenate(lhs, axis=0).astype(BF16), w2_ref[...])
            ys = []
            for idx, (p, cc) in enumerate(seqs):
                xb = x[idx * NA:(idx + 1) * NA]
                ks = ks_ref[o, cis[cc]]
                x_re, x_im = lane2(xb, 0), lane2(xb, 1)
                k_re, k_im = lane2(ks, 0), lane2(ks, 1)
                ys.append(jnp.concatenate([x_re * k_re - x_im * k_im, x_re * k_im + x_im * k_re], axis=1))
            bq = _dot(jnp.concatenate(ys, axis=0).astype(BF16), w2i_ref[...])
            for p in range(P):
                cr, cim = [], []
                for cc in range(2):
                    blk = bq[(2 * p + cc) * NA:(2 * p + cc + 1) * NA]
                    b_re, b_im = lane2(blk, 0), lane2(blk, 1)
                    cr.append(b_re * tc - b_im * ts)
                    cim.append(b_re * ts + b_im * tc)
                rhs = jnp.concatenate([jnp.concatenate(cr, axis=1), jnp.concatenate(cim, axis=1)], axis=0)
                conv = _dot(w1i_ref[...], rhs.astype(BF16))
                for cc in range(2):
                    bias = hb_ref[o, chs[cc]]
                    z[(p, cc)] = [gates[o][(p, cc)][r] * (lane2(conv, cc)[r * hh:(r + 1) * hh] + z[(p, cc)][r] * bias)
                                  for r in range(2)]
        for p, cc in seqs:
            for r in range(2):
                y_ref[cis[cc], 2 * p + r] = z[(p, cc)][r]
        return carry

    lax.fori_loop(0, cg // 2, body, 0)


def _hyena(proj_t, kspec, consts, conv_w, conv_b, bias, B, L, cg=8):
    NA = 2 * L // LANES
    hh = NA // 2
    u = proj_t.reshape(proj_t.shape[0], B, hh, LANES)
    ncg = HY_WIDTH // cg
    base = OFF_HY // cg
    smem = pl.BlockSpec(memory_space=pltpu.SMEM)
    full = lambda shp: pl.BlockSpec(shp, lambda j: (0,) * len(shp))
    blk = lambda off: pl.BlockSpec((cg, B, hh, LANES), lambda j: (off + j, 0, 0, 0))
    y = pl.pallas_call(
        functools.partial(_hyena_kernel, NA=NA, cg=cg, B=B),
        grid=(ncg,),
        in_specs=[smem, smem, smem, blk(base), blk(base + ncg), blk(base + 2 * ncg),
                  pl.BlockSpec((HY_ORDER, cg, NA, 2 * LANES), lambda j: (0, j, 0, 0)),
                  full((2 * NA, NA)), full((NA, 2 * NA)), full((NA, LANES)), full((NA, LANES)),
                  full((2 * LANES, 2 * LANES)), full((2 * LANES, 2 * LANES))],
        out_specs=pl.BlockSpec((cg, B, hh, LANES), lambda j: (j, 0, 0, 0)),
        out_shape=jax.ShapeDtypeStruct((HY_WIDTH, B, hh, LANES), F32),
        compiler_params=_cparams(("arbitrary",)),
        name="hyena_conv",
    )(conv_w, conv_b, bias, u, u, u, kspec, consts["w1d"], consts["w1i"], consts["tc"], consts["ts"],
      consts["w2"], consts["w2i"])
    return y.reshape(HY_WIDTH, B * L)


def _outproj_kernel(yg_ref, yh_ref, yy_ref, h_ref, wo_ref, g_ref, b_ref, wrh_ref, wrl_ref, br_ref,
                    h1_ref, e_ref, w_ref, cnt_ref, *, alpha):
    mix = _dot(yg_ref[...].astype(BF16), wo_ref[0:GLA_W], _TN)
    mix += _dot(yh_ref[...].astype(BF16), wo_ref[GLA_W:GLA_W + HG_W], _TN)
    mix += _dot(yy_ref[...].astype(BF16), wo_ref[GLA_W + HG_W:], _TN)
    h1 = _layer_norm(alpha * h_ref[...] + mix, g_ref[...], b_ref[...])
    h1_ref[...] = h1
    hi = h1.astype(BF16)
    lo = (h1 - hi.astype(F32)).astype(BF16)
    lg = _dot(wrh_ref[...], hi, _NT) + _dot(wrh_ref[...], lo, _NT) + _dot(wrl_ref[...], hi, _NT) + br_ref[...]
    tm = lg.shape[1]
    gl = [lg[g:g + 1] for g in range(N_GROUPS)]
    gmax = functools.reduce(jnp.maximum, gl)
    gidx = jnp.full((1, tm), N_GROUPS - 1, I32)
    for g in range(N_GROUPS - 2, -1, -1):
        gidx = jnp.where(gl[g] == gmax, g, gidx)
    gsum = functools.reduce(jnp.add, [jnp.exp(x - gmax) for x in gl])
    g_val = 1.0 / gsum
    el = []
    for r in range(EXPERTS_PER_GROUP):
        acc = jnp.zeros((1, tm), F32)
        for g in range(N_GROUPS):
            row = N_GROUPS + g * EXPERTS_PER_GROUP + r
            acc = jnp.where(gidx == g, lg[row:row + 1], acc)
        el.append(acc)
    emax = functools.reduce(jnp.maximum, el)
    pe = [jnp.exp(x - emax) for x in el]
    esum = functools.reduce(jnp.add, pe)
    pe = [x / esum for x in pe]
    v1 = functools.reduce(jnp.maximum, pe)
    i1 = jnp.full((1, tm), EXPERTS_PER_GROUP - 1, I32)
    for r in range(EXPERTS_PER_GROUP - 2, -1, -1):
        i1 = jnp.where(pe[r] == v1, r, i1)
    pe2 = [jnp.where(i1 == r, -1.0, pe[r]) for r in range(EXPERTS_PER_GROUP)]
    v2 = functools.reduce(jnp.maximum, pe2)
    i2 = jnp.full((1, tm), EXPERTS_PER_GROUP - 1, I32)
    for r in range(EXPERTS_PER_GROUP - 2, -1, -1):
        i2 = jnp.where(pe2[r] == v2, r, i2)
    den = v1 + v2
    e0 = gidx * EXPERTS_PER_GROUP + i1
    e1 = gidx * EXPERTS_PER_GROUP + i2
    e_ref[...] = jnp.concatenate([e0, e1], axis=0)
    w_ref[...] = jnp.concatenate([g_val * (v1 / den), g_val * (v2 / den)], axis=0)
    eio = lax.broadcasted_iota(I32, (N_EXPERTS, tm), 0)
    hit = jnp.where((eio == e0) | (eio == e1), 1.0, 0.0)
    cnt_ref[0] = jnp.sum(hit, axis=1, keepdims=True)


def _outproj(yg, yh, yy, h, w_out, g, b, wr_t, br, alpha, tm=512):
    T, D = h.shape
    nr = wr_t.shape[0]
    wr_hi = wr_t.astype(BF16)
    wr_lo = (wr_t - wr_hi.astype(F32)).astype(BF16)
    full = lambda shp: pl.BlockSpec(shp, lambda i: (0,) * len(shp))
    return pl.pallas_call(
        functools.partial(_outproj_kernel, alpha=alpha),
        grid=(T // tm,),
        in_specs=[pl.BlockSpec((GLA_W, tm), lambda i: (0, i)),
                  pl.BlockSpec((HG_W, tm), lambda i: (0, i)),
                  pl.BlockSpec((HY_WIDTH, tm), lambda i: (0, i)),
                  pl.BlockSpec((tm, D), lambda i: (i, 0)),
                  full((D, D)), full((1, D)), full((1, D)), full((nr, D)), full((nr, D)), full((nr, 1))],
        out_specs=[pl.BlockSpec((tm, D), lambda i: (i, 0)),
                   pl.BlockSpec((TOP_K, tm), lambda i: (0, i)),
                   pl.BlockSpec((TOP_K, tm), lambda i: (0, i)),
                   pl.BlockSpec((1, N_EXPERTS, 1), lambda i: (i, 0, 0))],
        out_shape=[jax.ShapeDtypeStruct((T, D), F32),
                   jax.ShapeDtypeStruct((TOP_K, T), I32),
                   jax.ShapeDtypeStruct((TOP_K, T), F32),
                   jax.ShapeDtypeStruct((T // tm, N_EXPERTS, 1), F32)],
        compiler_params=_cparams(("arbitrary",)),
        name="outproj",
    )(yg, yh, yy, h, w_out.astype(BF16), g.reshape(1, D), b.reshape(1, D), wr_hi, wr_lo, br)


def _chunk_loop(n, fn):
    def body(c, carry):
        fn(pl.multiple_of(c * ROW_CHUNK, ROW_CHUNK))
        return carry
    lax.fori_loop(0, n, body, 0)


def _tile_chunks(nch_s, tile):
    return functools.reduce(lambda a, b: a + b, [nch_s[tile * N_EXPERTS + e] for e in range(N_EXPERTS)])


def _dispatch_kernel(offs_s, gst_s, nch_s, tst_s, tn_s, nu_s, h_ref, e_ref, base_ref, tri_ref,
                     pos_ref, xg_hbm, xs_ref, zbuf, sem, zsem, *, tm, LP, bm, nblocks):
    i = pl.program_id(0)
    nt = pl.num_programs(0)
    slot = i % 2

    def chunk_copy(s, src_row, dst_row):
        return pltpu.make_async_copy(xs_ref.at[s, pl.ds(src_row, ROW_CHUNK)],
                                     xg_hbm.at[pl.ds(dst_row, ROW_CHUNK)], sem.at[s])

    def zero_copy(dst_row):
        return pltpu.make_async_copy(zbuf.at[pl.ds(0, ROW_CHUNK)], xg_hbm.at[pl.ds(dst_row, ROW_CHUNK)], zsem.at[0])

    def zero_block(blk):
        return pltpu.make_async_copy(zbuf, xg_hbm.at[pl.ds(pl.multiple_of(blk * bm, bm), bm)], zsem.at[0])

    def wait_tile(tile, s):
        _chunk_loop(_tile_chunks(nch_s, tile), lambda off: chunk_copy(s, 0, 0).wait())

    @pl.when(i == 0)
    def _():
        zbuf[...] = jnp.zeros_like(zbuf)
        for e in range(N_EXPERTS):
            _chunk_loop(tn_s[e], lambda off, e=e: zero_copy(pl.multiple_of(tst_s[e] + off, ROW_CHUNK)).start())
        lax.fori_loop(nu_s[0], nblocks, lambda blk, c: (zero_block(blk).start(), c)[1], 0)
        for e in range(N_EXPERTS):
            _chunk_loop(tn_s[e], lambda off: zero_copy(0).wait())
        lax.fori_loop(nu_s[0], nblocks, lambda blk, c: (zero_block(0).wait(), c)[1], 0)

    e0, e1 = e_ref[0:1, :], e_ref[1:2, :]
    eio = lax.broadcasted_iota(I32, (N_EXPERTS, tm), 0)
    oh0, oh1 = eio == e0, eio == e1
    hit = jnp.where(oh0 | oh1, 1.0, 0.0).astype(BF16)
    posm = base_ref[0] + _dot(hit, tri_ref[...])
    pos0 = jnp.sum(jnp.where(oh0, posm, 0.0), axis=0, keepdims=True).astype(I32)
    pos1 = jnp.sum(jnp.where(oh1, posm, 0.0), axis=0, keepdims=True).astype(I32)
    pos_ref[...] = jnp.concatenate([pos0, pos1], axis=0)
    rio = lax.broadcasted_iota(I32, (LP, tm), 0)
    perm = jnp.where((rio == pos0) | (rio == pos1), 1.0, 0.0).astype(BF16)
    xs_ref[slot] = _dot(perm, h_ref[...].astype(BF16)).astype(BF16)

    @pl.when(i > 0)
    def _():
        wait_tile(i - 1, 1 - slot)

    for e in range(N_EXPERTS):
        k = i * N_EXPERTS + e
        src0, dst0 = offs_s[k], gst_s[k]
        _chunk_loop(nch_s[k], lambda off, src0=src0, dst0=dst0: chunk_copy(
            slot, pl.multiple_of(src0 + off, ROW_CHUNK), pl.multiple_of(dst0 + off, ROW_CHUNK)).start())

    @pl.when(i == nt - 1)
    def _():
        wait_tile(i, slot)


def _moe_dispatch(h1, e_kt, tables, nblocks, bm, tm):
    T, D = h1.shape
    NT = T // tm
    nrows = nblocks * bm
    LP = TOP_K * tm + N_EXPERTS * ROW_CHUNK
    r = np.arange(tm)
    tri = jnp.asarray(r[:, None] < r[None, :], F32).astype(BF16)
    base = tables["offs"].astype(F32).reshape(NT, N_EXPERTS, 1)
    flat = lambda a: a.reshape(-1).astype(I32)
    pos, xg = pl.pallas_call(
        functools.partial(_dispatch_kernel, tm=tm, LP=LP, bm=bm, nblocks=nblocks),
        grid_spec=pltpu.PrefetchScalarGridSpec(
            num_scalar_prefetch=6,
            grid=(NT,),
            in_specs=[pl.BlockSpec((tm, D), lambda i, *_: (i, 0)),
                      pl.BlockSpec((TOP_K, tm), lambda i, *_: (0, i)),
                      pl.BlockSpec((1, N_EXPERTS, 1), lambda i, *_: (i, 0, 0)),
                      pl.BlockSpec((tm, tm), lambda i, *_: (0, 0))],
            out_specs=[pl.BlockSpec((TOP_K, tm), lambda i, *_: (0, i)),
                       pl.BlockSpec(memory_space=pl.ANY)],
            scratch_shapes=[pltpu.VMEM((2, LP, D), BF16), pltpu.VMEM((bm, D), BF16),
                            pltpu.SemaphoreType.DMA((2,)), pltpu.SemaphoreType.DMA((1,))]),
        out_shape=[jax.ShapeDtypeStruct((TOP_K, T), I32), jax.ShapeDtypeStruct((nrows, D), BF16)],
        compiler_params=_cparams(("arbitrary",)),
        name="moe_dispatch",
    )(flat(tables["offs"]), flat(tables["gstart"]), flat(tables["nch"]), flat(tables["tail_start"]),
      flat(tables["tail_n"]), tables["nused"], h1, e_kt, base, tri)
    return pos, xg


def _ffn_kernel(be_ref, nu_ref, x_ref, wg_ref, wu_ref, wd_ref, y_ref, wgb, wub, wdb):
    j = pl.program_id(0)
    used = j < nu_ref[0]

    @pl.when((j == 0) | (be_ref[j] != be_ref[jnp.maximum(j - 1, 0)]))
    def _():
        wgb[...] = wg_ref[0].astype(BF16)
        wub[...] = wu_ref[0].astype(BF16)
        wdb[...] = wd_ref[0].astype(BF16)

    @pl.when(used)
    def _():
        x = x_ref[...]
        a = _dot(x, wgb[...])
        hid = (a * _sigmoid(a)) * _dot(x, wub[...])
        y_ref[...] = _dot(hid.astype(BF16), wdb[...]).astype(BF16)

    @pl.when(jnp.logical_not(used))
    def _():
        y_ref[...] = jnp.zeros_like(y_ref)


def _moe_ffn(xg, block_e, nused, wg, wu, wd, bm, first_expert=0):
    nrows, D = xg.shape
    NB = nrows // bm
    DE = wg.shape[-1]
    row = lambda j, be, nu: (jnp.minimum(j, nu[0] - 1), 0)
    wsel = lambda j, be, nu: (first_expert + be[j], 0, 0)
    return pl.pallas_call(
        _ffn_kernel,
        grid_spec=pltpu.PrefetchScalarGridSpec(
            num_scalar_prefetch=2,
            grid=(NB,),
            in_specs=[pl.BlockSpec((bm, D), row),
                      pl.BlockSpec((1, D, DE), wsel),
                      pl.BlockSpec((1, D, DE), wsel),
                      pl.BlockSpec((1, DE, D), wsel)],
            out_specs=pl.BlockSpec((bm, D), lambda j, be, nu: (j, 0)),
            scratch_shapes=[pltpu.VMEM((D, DE), BF16), pltpu.VMEM((D, DE), BF16), pltpu.VMEM((DE, D), BF16)]),
        out_shape=jax.ShapeDtypeStruct((nrows, D), BF16),
        compiler_params=_cparams(("arbitrary",)),
        name="moe_ffn",
    )(block_e, nused, xg, wg, wu, wd)


def _combine_kernel(offs_s, gst_s, nch_s, yb_hbm, pos_ref, w_ref, h_ref, g_ref, b_ref, o_ref, ybl, sem,
                    *, tm, LP, alpha):
    i = pl.program_id(0)
    nt = pl.num_programs(0)
    slot = i % 2

    def chunk_copy(s, src_row, dst_row):
        return pltpu.make_async_copy(yb_hbm.at[pl.ds(src_row, ROW_CHUNK)],
                                     ybl.at[s, pl.ds(dst_row, ROW_CHUNK)], sem.at[s])

    def issue(tile, s):
        for e in range(N_EXPERTS):
            k = tile * N_EXPERTS + e
            src0, dst0 = gst_s[k], offs_s[k]
            _chunk_loop(nch_s[k], lambda off, src0=src0, dst0=dst0: chunk_copy(
                s, pl.multiple_of(src0 + off, ROW_CHUNK), pl.multiple_of(dst0 + off, ROW_CHUNK)).start())

    @pl.when(i == 0)
    def _():
        ybl[...] = jnp.zeros_like(ybl)
        issue(0, 0)

    @pl.when(i + 1 < nt)
    def _():
        issue(i + 1, 1 - slot)

    _chunk_loop(_tile_chunks(nch_s, i), lambda off: chunk_copy(slot, 0, 0).wait())
    pos0, pos1 = pos_ref[0:1, :], pos_ref[1:2, :]
    w0, w1 = w_ref[0:1, :], w_ref[1:2, :]
    rio = lax.broadcasted_iota(I32, (LP, tm), 0)
    pw = (jnp.where(rio == pos0, w0, 0.0) + jnp.where(rio == pos1, w1, 0.0)).astype(BF16)
    ffn = _dot(pw, ybl[slot], _TN)
    o_ref[...] = _layer_norm(alpha * h_ref[...] + ffn, g_ref[...], b_ref[...])


def _moe_combine(yb, h1, pos, w_kt, tables, g, b, alpha, tm):
    T, D = h1.shape
    LP = TOP_K * tm + N_EXPERTS * ROW_CHUNK
    flat = lambda a: a.reshape(-1).astype(I32)
    return pl.pallas_call(
        functools.partial(_combine_kernel, tm=tm, LP=LP, alpha=alpha),
        grid_spec=pltpu.PrefetchScalarGridSpec(
            num_scalar_prefetch=3,
            grid=(T // tm,),
            in_specs=[pl.BlockSpec(memory_space=pl.ANY),
                      pl.BlockSpec((TOP_K, tm), lambda i, *_: (0, i)),
                      pl.BlockSpec((TOP_K, tm), lambda i, *_: (0, i)),
                      pl.BlockSpec((tm, D), lambda i, *_: (i, 0)),
                      pl.BlockSpec((1, D), lambda i, *_: (0, 0)),
                      pl.BlockSpec((1, D), lambda i, *_: (0, 0))],
            out_specs=pl.BlockSpec((tm, D), lambda i, *_: (i, 0)),
            scratch_shapes=[pltpu.VMEM((2, LP, D), BF16), pltpu.SemaphoreType.DMA((2,))]),
        out_shape=jax.ShapeDtypeStruct((T, D), F32),
        compiler_params=_cparams(("arbitrary",)),
        name="moe_combine",
    )(flat(tables["offs"]), flat(tables["gstart"]), flat(tables["nch"]), yb, pos, w_kt, h1,
      g.reshape(1, D), b.reshape(1, D))


def _dispatch_tables(cnt, bm, nblocks):
    padlen = ((cnt + ROW_CHUNK - 1) // ROW_CHUNK) * ROW_CHUNK
    offs = jnp.cumsum(padlen, axis=1) - padlen
    tot = jnp.sum(padlen, axis=0)
    region = ((tot + bm - 1) // bm) * bm
    rend = jnp.cumsum(region)
    rstart = rend - region
    gstart = rstart[None, :] + jnp.cumsum(padlen, axis=0) - padlen
    blk_row = jnp.arange(nblocks, dtype=I32)[:, None] * bm
    block_e = jnp.minimum(jnp.sum((blk_row >= rend[None, :]).astype(I32), axis=1), N_EXPERTS - 1)
    return dict(offs=offs, gstart=gstart, nch=padlen // ROW_CHUNK, tail_start=rstart + tot,
                tail_n=(region - tot) // ROW_CHUNK, block_e=block_e.astype(I32),
                nused=(rend[-1:] // bm).astype(I32))


def _in_perm():
    splits = (192, 192, 384, 384, 32, 384, 768, 384, 384, 768)
    offs = np.concatenate([[0], np.cumsum(splits)])
    gq, gk, gv, gg, ga, hq, hf, hi, hgt, hyu = [np.arange(offs[i], offs[i + 1]) for i in range(10)]
    perm = np.concatenate([hq, hi, hgt, hf, gv, gg, gq, gk, hyu, ga])
    assert perm.shape[0] == D_IN
    return perm


def kernel(x, ln_in_g, ln_in_b, w_in, gla_wa2, gla_ba, gla_norm_g, hg_lb_logits, hg_norm_g, hy_conv_w, hy_conv_b, hy_w1, hy_b1, hy_freq, hy_w2, hy_b2, hy_w3, hy_bias, w_out, ln1_g, ln1_b, moe_wr_g, moe_br_g, moe_wr_e, moe_br_e, moe_w_gate, moe_w_up, moe_w_down, ln2_g, ln2_b):
    B, L, D = x.shape
    T = B * L
    depth = w_in.shape[0]
    alpha = (2 * depth) ** 0.25
    bm = 256
    tmr = 512
    nblocks = -(-(T * TOP_K + (T // tmr) * N_EXPERTS * (ROW_CHUNK - 1) + N_EXPERTS * (bm - 1)) // bm)
    perm = _in_perm()
    consts = _dft_consts(L)

    p = jax.nn.softmax(hg_lb_logits.astype(F32), axis=0)
    lbs = jnp.cumsum(p, axis=0) - p[0:1]
    lbc = jnp.concatenate([jnp.broadcast_to((1.0 - lbs)[..., None], lbs.shape + (LANES,)),
                           jnp.broadcast_to(jnp.maximum(lbs, LB_FLOOR)[..., None], lbs.shape + (LANES,))], axis=-1)

    wg_all = moe_w_gate.reshape((depth * N_EXPERTS,) + moe_w_gate.shape[2:])
    wu_all = moe_w_up.reshape((depth * N_EXPERTS,) + moe_w_up.shape[2:])
    wd_all = moe_w_down.reshape((depth * N_EXPERTS,) + moe_w_down.shape[2:])
    h = x.reshape(T, D)
    for l in range(depth):
        w_t = w_in[l][:, perm].T.astype(BF16)
        proj_t, h = _inproj(h, ln_in_g, ln_in_b, w_t, apply_ln=(l == 0))
        wa_t = jnp.swapaxes(gla_wa2[l], 1, 2)
        y_gla = _scan(proj_t, "gla", B, L, (wa_t, gla_ba[l].reshape(2, GLA_K, 1)), gla_norm_g[l].reshape(GLA_W, 1))
        y_hg = _scan(proj_t, "hg", B, L, (lbc[l],), hg_norm_g[l].reshape(HG_W, 1))
        kt = _hyena_filters(L, hy_w1[l], hy_b1[l], hy_freq[l], hy_w2[l], hy_b2[l], hy_w3[l])
        kspec = _hyena_spectrum(kt, consts, L)
        y_hy = _hyena(proj_t, kspec, consts, hy_conv_w[l], hy_conv_b[l], hy_bias[l], B, L)
        nr = N_GROUPS + N_EXPERTS
        nrp = -(-nr // 8) * 8
        wr_t = jnp.zeros((nrp, D), F32).at[:nr].set(jnp.concatenate([moe_wr_g[l], moe_wr_e[l]], axis=1).T)
        br = jnp.zeros((nrp, 1), F32).at[:nr, 0].set(jnp.concatenate([moe_br_g[l], moe_br_e[l]]))
        h1, e_kt, w_kt, cnt = _outproj(y_gla, y_hg, y_hy, h, w_out[l], ln1_g[l], ln1_b[l], wr_t, br, alpha, tm=tmr)
        tables = _dispatch_tables(cnt.reshape(T // tmr, N_EXPERTS).astype(I32), bm, nblocks)
        pos, xg = _moe_dispatch(h1, e_kt, tables, nblocks, bm, tmr)
        yb = _moe_ffn(xg, tables["block_e"], tables["nused"], wg_all, wu_all, wd_all, bm, first_expert=l * N_EXPERTS)
        h = _moe_combine(yb, h1, pos, w_kt, tables, ln2_g[l], ln2_b[l], alpha, tmr)
    return h.reshape(B, L, D)
```

```python
import functools
import math

import numpy as np
import jax
import jax.numpy as jnp
from jax import lax
from jax.experimental import pallas as pl
from jax.experimental.pallas import tpu as pltpu

F32 = jnp.float32
BF16 = jnp.bfloat16
I32 = jnp.int32

GLA_HEADS, GLA_DK, GLA_DV, GLA_LOWRANK, GLA_TAU = 6, 32, 64, 16, 16.0
HG_HEADS, HG_DK, HG_DV = 6, 64, 64
HY_WIDTH, HY_ORDER, HY_EMB, HY_FFN, HY_INNER = 256, 2, 33, 64, 2
HY_FAST_DECAY, HY_SLOW_DECAY, HY_TARGET = 0.3, 1.5, 1e-2
N_GROUPS, EXPERTS_PER_GROUP = 4, 4
N_EXPERTS = N_GROUPS * EXPERTS_PER_GROUP
TOP_K = 2
LN_EPS, RMS_EPS, LB_FLOOR = 1e-5, 1e-6, 1e-30

LANES = 128
SCAN_CHUNK = LANES
ROW_CHUNK = 16
VMEM_LIMIT = 56 * 1024 * 1024

GLA_W = GLA_HEADS * GLA_DV
GLA_K = GLA_HEADS * GLA_DK
HG_W = HG_HEADS * HG_DV
HG_K = HG_HEADS * HG_DK
OFF_HQ, OFF_HI, OFF_HGT, OFF_HF = 0, 384, 768, 1152
OFF_GV, OFF_GG, OFF_GQ, OFF_GK, OFF_HY, OFF_GA = 1920, 2304, 2688, 2880, 3072, 3840
D_IN = 3872


def _dot(a, b, dims=(((1,), (0,)), ((), ())), precision=None):
    return lax.dot_general(a, b, dims, preferred_element_type=F32, precision=precision)


_NT = (((1,), (1,)), ((), ()))
_TN = (((0,), (0,)), ((), ()))


def _layer_norm(x, g, b):
    mu = jnp.mean(x, axis=-1, keepdims=True)
    xc = x - mu
    var = jnp.mean(xc * xc, axis=-1, keepdims=True)
    return xc * lax.rsqrt(var + LN_EPS) * g + b


def _log_sigmoid(x):
    return jnp.minimum(x, 0.0) - jnp.log(1.0 + jnp.exp(-jnp.abs(x)))


def _sigmoid(x):
    return 0.5 + 0.5 * jnp.tanh(0.5 * x)


def _cparams(sem):
    return pltpu.CompilerParams(dimension_semantics=sem, vmem_limit_bytes=VMEM_LIMIT)


def _inproj_kernel(x_ref, g_ref, b_ref, w_ref, *outs, apply_ln):
    x = x_ref[...]
    if apply_ln:
        x = _layer_norm(x, g_ref[...], b_ref[...])
        outs[1][...] = x
    outs[0][...] = _dot(w_ref[...], x.astype(BF16), _NT)


def _inproj(x, g, b, w_t, apply_ln, tm=512):
    T, D = x.shape
    n_out = w_t.shape[0]
    out_shape = [jax.ShapeDtypeStruct((n_out, T), F32)]
    out_specs = [pl.BlockSpec((n_out, tm), lambda i: (0, i))]
    if apply_ln:
        out_shape.append(jax.ShapeDtypeStruct((T, D), F32))
        out_specs.append(pl.BlockSpec((tm, D), lambda i: (i, 0)))
    res = pl.pallas_call(
        functools.partial(_inproj_kernel, apply_ln=apply_ln),
        grid=(T // tm,),
        in_specs=[pl.BlockSpec((tm, D), lambda i: (i, 0)),
                  pl.BlockSpec((1, D), lambda i: (0, 0)),
                  pl.BlockSpec((1, D), lambda i: (0, 0)),
                  pl.BlockSpec((n_out, D), lambda i: (0, 0))],
        out_specs=out_specs,
        out_shape=out_shape,
        compiler_params=_cparams(("arbitrary",)),
        name="inproj",
    )(x, g.reshape(1, D), b.reshape(1, D), w_t)
    return res if apply_ln else (res[0], x)


def _scan_kernel(*refs, mode, H, K, V, TB, NCB):
    if mode == "gla":
        (qf_ref, qb_ref, kf_ref, kb_ref, vf_ref, vb_ref, gaf_ref, gab_ref, wa_ref, ba_ref,
         gtf_ref, gtb_ref, ng_ref, y_ref, s_ref, oacc_ref, oi_ref, qg_ref, ut_ref, dr_ref) = refs
        q_refs, k_refs, ga_refs = (qf_ref, qb_ref), (kf_ref, kb_ref), (gaf_ref, gab_ref)
    else:
        (qf_ref, qb_ref, zf_ref, zb_ref, vf_ref, vb_ref, lbc_ref,
         gtf_ref, gtb_ref, ng_ref, y_ref, s_ref, oacc_ref, oi_ref, qg_ref, ut_ref, dr_ref) = refs
        q_refs, z_refs = (qf_ref, qb_ref), (zf_ref, zb_ref)
    v_refs, gate_refs = (vf_ref, vb_ref), (gtf_ref, gtb_ref)
    C = SCAN_CHUNK
    half = C // 2
    nchunks = TB // C
    n = pl.program_id(1)
    blocks = (n, NCB - 1 - n)

    @pl.when(n == 0)
    def _():
        s_ref[...] = jnp.zeros_like(s_ref)

    def gates(d, sl):
        if mode == "gla":
            a = _dot(wa_ref[d], ga_refs[d][:, sl], precision=lax.Precision.HIGHEST) + ba_ref[d]
            g = _log_sigmoid(a) * (1.0 / GLA_TAU)
            q = q_refs[d][:, sl] * (K ** -0.5)
            k = k_refs[d][:, sl]
        else:
            one_m_lb, lb_floor = lbc_ref[d, :, 0:C], lbc_ref[d, :, C:2 * C]
            s = one_m_lb * _sigmoid(z_refs[d][:, sl])
            g = jnp.log(s + lb_floor)
            k = one_m_lb - s
            hq = q_refs[d][:, sl]
            q = hq * _sigmoid(hq)
        return q, k, g

    r = lax.broadcasted_iota(I32, (C, C), 0)
    c = lax.broadcasted_iota(I32, (C, C), 1)
    same = (r >= half) == (c >= half)
    lane_lo = lax.broadcasted_iota(I32, (1, C), 1) < half
    sign_lo = jnp.where(lane_lo, 1.0, -1.0)
    consts = (((r <= c).astype(BF16), (r < half) & (c >= half), same & (r <= c),
               (C - 1, half, half // 2, half + half // 2)),
              ((r >= c).astype(BF16), (r >= half) & (c < half), same & (r >= c),
               (0, half - 1, half // 2 - 1, half + half // 2 - 1)))

    heads = [(slice(h * K, (h + 1) * K), slice(h * V, (h + 1) * V)) for h in range(H)]

    def cumulate(d, g):
        cum = consts[d][0]
        g1 = g.astype(BF16)
        g2 = (g - g1.astype(F32)).astype(BF16)
        G = _dot(g1, cum) + _dot(g2, cum)
        ge_row = _dot(ones8, g1, _NT) + _dot(ones8, g2, _NT)
        return G, ge_row

    def scale(d, sl, q, k, G, ge_row):
        _, _, _, (c_end, c_mid, c_a, c_b) = consts[d]
        v = v_refs[d][:, sl].astype(BF16)
        g_end = G[:, c_end:c_end + 1]
        g_mid = G[:, c_mid:c_mid + 1]
        e2 = G - jnp.where(lane_lo, G[:, c_a:c_a + 1], G[:, c_b:c_b + 1])
        q2 = (q * jnp.exp(e2)).astype(BF16)
        k2 = (k * jnp.exp(-e2)).astype(BF16)
        dm = G - g_mid
        x1 = jnp.exp(dm * (sign_lo if d else -sign_lo))
        q1 = (q * x1).astype(BF16)
        k1 = (k * x1).astype(BF16)
        qg = (q * jnp.exp(G)).astype(BF16)
        kd = (k * jnp.exp(g_end - G)).astype(BF16)
        return dict(q1=q1, k1=k1, q2=q2, k2=k2, qg=qg, kd=kd, v=v, dec_row=jnp.exp(ge_row))

    ones8 = jnp.ones((8, C), BF16)

    def scores(x):
        return [(_dot(x["k1"][rk], x["q1"][rk], _TN), _dot(x["k2"][rk], x["q2"][rk], _TN)) for rk, _ in heads]

    def masked(d, ps):
        _, off1, diag, _ = consts[d]
        return [jnp.where(off1, p1, jnp.where(diag, p2, 0.0)).astype(BF16) for p1, p2 in ps]

    def park(d, ci, x, pm):
        oi_ref[d, ci] = jnp.concatenate([_dot(x["v"][rv], pm[h]) for h, (_, rv) in enumerate(heads)], axis=0)
        qg_ref[d, ci] = x["qg"]
        dr_ref[d, ci] = x["dec_row"]
        for h, (rk, rv) in enumerate(heads):
            ut_ref[d, ci, h] = _dot(x["v"][rv], x["kd"][rk], _NT)

    def chunk_slices(ci):
        offs = (pl.multiple_of(ci * C, C), pl.multiple_of((nchunks - 1 - ci) * C, C))
        return offs, [pl.ds(offs[d], C) for d in range(2)]

    def independent(ci, carry):
        _, sls = chunk_slices(ci)
        dirs = range(2)
        qkg = [gates(d, sls[d]) for d in dirs]
        cums = [cumulate(d, qkg[d][2]) for d in dirs]
        xs = [scale(d, sls[d], qkg[d][0], qkg[d][1], *cums[d]) for d in dirs]
        scs = [scores(xs[d]) for d in dirs]
        pms = [masked(d, scs[d]) for d in dirs]
        for d in dirs:
            park(d, ci, xs[d], pms[d])
        return carry

    lax.fori_loop(0, nchunks, independent, 0)

    def carried(d, ci):
        o = []
        for h, (rk, rv) in enumerate(heads):
            st = s_ref[d, h]
            o.append(oi_ref[d, ci, rv, :] + _dot(st.astype(BF16), qg_ref[d, ci, rk, :]))
            s_ref[d, h] = st * dr_ref[d, ci, 0:1, rk] + ut_ref[d, ci, h]
        return jnp.concatenate(o, axis=0)

    def finish(o, gate):
        act = gate * _sigmoid(gate) if mode == "gla" else _sigmoid(gate)
        ys = []
        for h in range(H):
            oh = o[h * V:(h + 1) * V]
            ms = jnp.mean(oh * oh, axis=0, keepdims=True)
            ys.append(oh * lax.rsqrt(ms + RMS_EPS))
        return (jnp.concatenate(ys, axis=0) * ng_ref[...] * act).astype(y_ref.dtype)

    def chunk(ci, carry):
        offs, sls = chunk_slices(ci)
        tsls = [pl.ds(pl.multiple_of(blocks[d] * TB + offs[d], C), C) for d in range(2)]
        o = (carried(0, ci), carried(1, ci))

        @pl.when(2 * n < NCB)
        def _():
            for d in range(2):
                oacc_ref[:, tsls[d]] = o[d]

        @pl.when(2 * n >= NCB)
        def _():
            tot = [o[d] + oacc_ref[:, tsls[d]] for d in range(2)]
            for d in range(2):
                y_ref[:, tsls[d]] = finish(tot[d], gate_refs[d][:, sls[d]])
        return carry

    lax.fori_loop(0, nchunks, chunk, 0)


def _scan(proj_t, mode, B, L, extra, norm_g, TB=512):
    NCB = L // TB
    nch = TB // SCAN_CHUNK
    assert NCB % 2 == 0
    if mode == "gla":
        H, K, V = GLA_HEADS, GLA_DK, GLA_DV
    else:
        H, K, V = HG_HEADS, HG_DK, HG_DV
    HK, HV = H * K, H * V
    cf = lambda b, n: b * NCB + n
    cb = lambda b, n: b * NCB + NCB - 1 - n
    gf = lambda b, n: b * NCB + jnp.maximum(n, NCB // 2)
    gb = lambda b, n: b * NCB + jnp.minimum(NCB - 1 - n, NCB // 2 - 1)

    def pair(rows, off, fwd=cf, bwd=cb, dir_step=0):
        return [pl.BlockSpec((rows, TB), lambda b, n: (off // rows, fwd(b, n))),
                pl.BlockSpec((rows, TB), lambda b, n: (off // rows + dir_step, bwd(b, n)))]

    whole = lambda shp: pl.BlockSpec(shp, lambda b, n: (0,) * len(shp))
    if mode == "gla":
        wa_t, ba = extra
        in_specs = (pair(HK, OFF_GQ) + pair(HK, OFF_GK) + pair(HV, OFF_GV)
                    + pair(GLA_LOWRANK, OFF_GA, dir_step=1)
                    + [whole((2, HK, GLA_LOWRANK)), whole((2, HK, 1))]
                    + pair(HV, OFF_GG, gf, gb) + [whole((HV, 1))])
        args = (proj_t,) * 8 + (wa_t, ba, proj_t, proj_t, norm_g)
    else:
        (lbc,) = extra
        in_specs = (pair(HK, OFF_HQ) + pair(HK, OFF_HF, dir_step=1) + pair(HV, OFF_HI)
                    + [whole((2, HK, 2 * LANES))] + pair(HV, OFF_HGT, gf, gb) + [whole((HV, 1))])
        args = (proj_t,) * 6 + (lbc, proj_t, proj_t, norm_g)
    return pl.pallas_call(
        functools.partial(_scan_kernel, mode=mode, H=H, K=K, V=V, TB=TB, NCB=NCB),
        grid=(B, NCB),
        in_specs=in_specs,
        out_specs=pl.BlockSpec((HV, L), lambda b, n: (0, b)),
        out_shape=jax.ShapeDtypeStruct((HV, B * L), BF16),
        scratch_shapes=[pltpu.VMEM((2, H, V, K), F32), pltpu.VMEM((HV, L), F32),
                        pltpu.VMEM((2, nch, HV, SCAN_CHUNK), F32), pltpu.VMEM((2, nch, HK, SCAN_CHUNK), BF16),
                        pltpu.VMEM((2, nch, H, V, K), F32), pltpu.VMEM((2, nch, 8, HK), F32)],
        compiler_params=_cparams(("arbitrary", "arbitrary")),
        name="scan_" + mode,
    )(*args)


def _dft_consts(L):
    N = 2 * L
    NA = N // LANES
    a = np.arange(NA)[:, None] * np.arange(NA)[None, :]
    ca, sa = np.cos(2 * np.pi * a / NA), np.sin(2 * np.pi * a / NA)
    hh = NA // 2
    w1d = np.block([[ca[:, :hh], sa[:, :hh]], [-sa[:, :hh], ca[:, :hh]]])
    w1f = np.concatenate([ca, -sa], axis=0)
    w1i = np.block([[ca[:hh, :], -sa[:hh, :]], [sa[:hh, :], ca[:hh, :]]])
    bb = np.arange(LANES)[:, None] * np.arange(LANES)[None, :]
    cb, sb = np.cos(2 * np.pi * bb / LANES), np.sin(2 * np.pi * bb / LANES)
    w2 = np.block([[cb, -sb], [sb, cb]])
    w2i = np.block([[cb, sb], [-sb, cb]])
    tw = np.arange(NA)[:, None] * np.arange(LANES)[None, :]
    tc, ts = np.cos(2 * np.pi * tw / N), np.sin(2 * np.pi * tw / N)
    bf = lambda m: jnp.asarray(m, dtype=F32).astype(BF16)
    return dict(w1d=bf(w1d), w1f=bf(w1f), w1i=bf(w1i), w2=bf(w2), w2i=bf(w2i),
                tc=jnp.asarray(tc, F32), ts=jnp.asarray(ts, F32))


def _pos_features(L):
    t = np.linspace(0.0, 1.0, L)
    w = 2.0 * np.pi * np.arange(L) / L
    bands = np.linspace(1e-4, (HY_EMB - 1) // 2 - 1, (HY_EMB - 1) // 2)
    z = np.concatenate([t[None, :], np.cos(bands[:, None] * w[None, :]), -np.sin(bands[:, None] * w[None, :])], axis=0)
    kp = -(-HY_EMB // 8) * 8
    z = np.concatenate([z, np.zeros((kp - HY_EMB, L))], axis=0)
    idx = (L - np.arange(L)) % L
    z_rev, t_rev = z[:, idx], t[idx]
    mask = (np.arange(L) >= 1).astype(np.float64)
    return (jnp.asarray(z, F32), jnp.asarray(z_rev, F32), jnp.asarray(t[None, :], F32),
            jnp.asarray(t_rev[None, :], F32), jnp.asarray(mask[None, :], F32))


def _filter_kernel(z_ref, zr_ref, t_ref, tr_ref, m_ref, w1_ref, b1_ref, fr_ref, w2_ref, b2_ref,
                   w3f_ref, w3b_ref, dl_ref, out_ref, h_ref, *, L):
    first = (pl.program_id(0) == 0) & (pl.program_id(1) == 0)
    hi = lax.Precision.HIGHEST

    @pl.when(first)
    def _():
        fr = fr_ref[...]
        for idx, zz in enumerate((z_ref, zr_ref)):
            h = jnp.sin(fr * (_dot(w1_ref[...], zz[...], precision=hi) + b1_ref[...]))
            for i in range(HY_INNER):
                h = jnp.sin(fr * (_dot(w2_ref[i], h, precision=hi) + b2_ref[i]))
            h_ref[idx] = h.astype(BF16)

    ad = jnp.abs(dl_ref[...])
    kf = _dot(w3f_ref[...].astype(BF16), h_ref[0]) * jnp.exp(-t_ref[...] * ad)
    kb = _dot(w3b_ref[...].astype(BF16), h_ref[1]) * jnp.exp(-tr_ref[...] * ad) * m_ref[...]
    den = jnp.sum(jnp.abs(kf), axis=1, keepdims=True) + jnp.sum(jnp.abs(kb), axis=1, keepdims=True)
    scale = 1.0 / (jnp.maximum(den, 1e-12) * (2.0 * L))
    out_ref[0, :, 0:L] = kf * scale
    out_ref[0, :, L:2 * L] = kb * scale


def _hyena_filters(L, w1, b1, freq, w2, b2, w3, cg=64):
    z, z_rev, t, t_rev, mask = _pos_features(L)
    kp = z.shape[0]
    w1_t = jnp.zeros((HY_FFN, kp), F32).at[:, :HY_EMB].set(w1.T)
    w2_t = jnp.swapaxes(w2, 1, 2)
    w3_t = w3.T
    max_decay = math.log(HY_TARGET) / HY_FAST_DECAY
    min_decay = math.log(HY_TARGET) / HY_SLOW_DECAY
    deltas = jnp.asarray(np.linspace(min_decay, max_decay, HY_WIDTH).reshape(HY_WIDTH, 1), F32)
    ncg = HY_WIDTH // cg
    full = lambda shp: pl.BlockSpec(shp, lambda o, j: (0,) * len(shp))
    return pl.pallas_call(
        functools.partial(_filter_kernel, L=L),
        grid=(HY_ORDER, ncg),
        in_specs=[full((kp, L)), full((kp, L)), full((1, L)), full((1, L)), full((1, L)),
                  full((HY_FFN, kp)), full((HY_FFN, 1)), full((HY_FFN, 1)),
                  full((HY_INNER, HY_FFN, HY_FFN)), full((HY_INNER, HY_FFN, 1)),
                  pl.BlockSpec((cg, HY_FFN), lambda o, j: (o * 2 * ncg + j, 0)),
                  pl.BlockSpec((cg, HY_FFN), lambda o, j: (o * 2 * ncg + ncg + j, 0)),
                  pl.BlockSpec((cg, 1), lambda o, j: (j, 0))],
        out_specs=pl.BlockSpec((1, cg, 2 * L), lambda o, j: (o, j, 0)),
        out_shape=jax.ShapeDtypeStruct((HY_ORDER, HY_WIDTH, 2 * L), F32),
        scratch_shapes=[pltpu.VMEM((2, HY_FFN, L), BF16)],
        compiler_params=_cparams(("arbitrary", "arbitrary")),
        name="hyena_filter",
    )(z, z_rev, t, t_rev, mask, w1_t, b1.reshape(HY_FFN, 1), freq.reshape(HY_FFN, 1), w2_t,
      b2.reshape(HY_INNER, HY_FFN, 1), w3_t, w3_t, deltas)


def _spectrum_kernel(k_ref, w1_ref, tc_ref, ts_ref, w2_ref, out_ref, *, NA, cg):
    tc2 = jnp.concatenate([tc_ref[...]] * 2, axis=1)
    ts2 = jnp.concatenate([ts_ref[...]] * 2, axis=1)
    G = 4

    def body(it, carry):
        lhs = []
        for pr in range(G // 2):
            rhs = jnp.concatenate([k_ref[0, G * it + 2 * pr + cc] for cc in range(2)], axis=1)
            a = _dot(w1_ref[...], rhs.astype(BF16))
            a_re, a_im = a[:NA], a[NA:]
            b_re = a_re * tc2 + a_im * ts2
            b_im = a_im * tc2 - a_re * ts2
            lhs += [jnp.concatenate([b_re[:, cc * LANES:(cc + 1) * LANES], b_im[:, cc * LANES:(cc + 1) * LANES]], axis=1)
                    for cc in range(2)]
        x = _dot(jnp.concatenate(lhs, axis=0).astype(BF16), w2_ref[...])
        for i in range(G):
            out_ref[0, G * it + i] = x[i * NA:(i + 1) * NA]
        return carry
    lax.fori_loop(0, cg // G, body, 0)


def _hyena_spectrum(kt, consts, L, cg=32):
    NA = 2 * L // LANES
    k4 = kt.reshape(HY_ORDER, HY_WIDTH, NA, LANES)
    full = lambda shp: pl.BlockSpec(shp, lambda o, j: (0,) * len(shp))
    return pl.pallas_call(
        functools.partial(_spectrum_kernel, NA=NA, cg=cg),
        grid=(HY_ORDER, HY_WIDTH // cg),
        in_specs=[pl.BlockSpec((1, cg, NA, LANES), lambda o, j: (o, j, 0, 0)),
                  full((2 * NA, NA)), full((NA, LANES)), full((NA, LANES)), full((2 * LANES, 2 * LANES))],
        out_specs=pl.BlockSpec((1, cg, NA, 2 * LANES), lambda o, j: (o, j, 0, 0)),
        out_shape=jax.ShapeDtypeStruct((HY_ORDER, HY_WIDTH, NA, 2 * LANES), F32),
        compiler_params=_cparams(("arbitrary", "arbitrary")),
        name="hyena_spectrum",
    )(k4, consts["w1f"], consts["tc"], consts["ts"], consts["w2"])


def _hyena_kernel(cw_ref, cb_ref, hb_ref, v_ref, x1_ref, x2_ref, ks_ref, w1d_ref, w1i_ref, tc_ref, ts_ref,
                  w2_ref, w2i_ref, y_ref, *, NA, cg, B):
    hh = NA // 2
    j = pl.program_id(0)
    row = lax.broadcasted_iota(I32, (hh, LANES), 0)
    lane = lax.broadcasted_iota(I32, (hh, LANES), 1)
    first = (row == 0) & (lane == 0)
    last = (row == hh - 1) & (lane == LANES - 1)

    def short_conv(x, ch):
        r1 = pltpu.roll(x, 1, 1)
        prev = jnp.where(lane == 0, pltpu.roll(r1, 1, 0), r1)
        prev = jnp.where(first, 0.0, prev)
        r2 = pltpu.roll(x, LANES - 1, 1)
        nxt = jnp.where(lane == LANES - 1, pltpu.roll(r2, hh - 1, 0), r2)
        nxt = jnp.where(last, 0.0, nxt)
        return cw_ref[0, ch] * prev + cw_ref[1, ch] * x + cw_ref[2, ch] * nxt + cb_ref[ch]

    P = B // 2
    tc, ts = tc_ref[...], ts_ref[...]
    tc2 = jnp.concatenate([tc, tc], axis=1)
    ts2 = jnp.concatenate([ts, ts], axis=1)
    lane2 = lambda x, cc: x[:, cc * LANES:(cc + 1) * LANES]

    def body(it, carry):
        cis = [2 * it + cc for cc in range(2)]
        chs = [j * cg + ci for ci in cis]
        seqs = [(p, cc) for p in range(P) for cc in range(2)]
        z = {(p, cc): [short_conv(v_ref[cis[cc], 2 * p + r], chs[cc]) for r in range(2)] for p, cc in seqs}
        gate_refs = (x1_ref, x2_ref)
        gates = [{(p, cc): [short_conv(gate_refs[o][cis[cc], 2 * p + r], (o + 1) * HY_WIDTH + chs[cc])
                            for r in range(2)] for p, cc in seqs} for o in range(HY_ORDER)]
        for o in range(HY_ORDER):
            lhs = []
            for p in range(P):
                rhs = jnp.concatenate([jnp.concatenate(z[(p, cc)], axis=0) for cc in range(2)], axis=1)
                a = _dot(w1d_ref[...], rhs.astype(BF16))
                a_re, a_im = a[:NA], a[NA:]
                b_re = a_re * tc2 + a_im * ts2
                b_im = a_im * tc2 - a_re * ts2
                lhs += [jnp.concatenate([lane2(b_re, cc), lane2(b_im, cc)], axis=1) for cc in range(2)]
            x = _dot(jnp.concatenate(lhs, axis=0).astype(BF16), w2_ref[...])
            ys = []
            for idx, (p, cc) in enumerate(seqs):
                xb = x[idx * NA:(idx + 1) * NA]
                ks = ks_ref[o, cis[cc]]
                x_re, x_im = lane2(xb, 0), lane2(xb, 1)
                k_re, k_im = lane2(ks, 0), lane2(ks, 1)
                ys.append(jnp.concatenate([x_re * k_re - x_im * k_im, x_re * k_im + x_im * k_re], axis=1))
            bq = _dot(jnp.concatenate(ys, axis=0).astype(BF16), w2i_ref[...])
            for p in range(P):
                cr, cim = [], []
                for cc in range(2):
                    blk = bq[(2 * p + cc) * NA:(2 * p + cc + 1) * NA]
                    b_re, b_im = lane2(blk, 0), lane2(blk, 1)
                    cr.append(b_re * tc - b_im * ts)
                    cim.append(b_re * ts + b_im * tc)
                rhs = jnp.concatenate([jnp.concatenate(cr, axis=1), jnp.concatenate(cim, axis=1)], axis=0)
                conv = _dot(w1i_ref[...], rhs.astype(BF16))
                for cc in range(2):
                    bias = hb_ref[o, chs[cc]]
                    z[(p, cc)] = [gates[o][(p, cc)][r] * (lane2(conv, cc)[r * hh:(r + 1) * hh] + z[(p, cc)][r] * bias)
                                  for r in range(2)]
        for p, cc in seqs:
            for r in range(2):
                y_ref[cis[cc], 2 * p + r] = z[(p, cc)][r]
        return carry

    lax.fori_loop(0, cg // 2, body, 0)


def _hyena(proj_t, kspec, consts, conv_w, conv_b, bias, B, L, cg=8):
    NA = 2 * L // LANES
    hh = NA // 2
    u = proj_t.reshape(proj_t.shape[0], B, hh, LANES)
    ncg = HY_WIDTH // cg
    base = OFF_HY // cg
    smem = pl.BlockSpec(memory_space=pltpu.SMEM)
    full = lambda shp: pl.BlockSpec(shp, lambda j: (0,) * len(shp))
    blk = lambda off: pl.BlockSpec((cg, B, hh, LANES), lambda j: (off + j, 0, 0, 0))
    y = pl.pallas_call(
        functools.partial(_hyena_kernel, NA=NA, cg=cg, B=B),
        grid=(ncg,),
        in_specs=[smem, smem, smem, blk(base), blk(base + ncg), blk(base + 2 * ncg),
                  pl.BlockSpec((HY_ORDER, cg, NA, 2 * LANES), lambda j: (0, j, 0, 0)),
                  full((2 * NA, NA)), full((NA, 2 * NA)), full((NA, LANES)), full((NA, LANES)),
                  full((2 * LANES, 2 * LANES)), full((2 * LANES, 2 * LANES))],
        out_specs=pl.BlockSpec((cg, B, hh, LANES), lambda j: (j, 0, 0, 0)),
        out_shape=jax.ShapeDtypeStruct((HY_WIDTH, B, hh, LANES), F32),
        compiler_params=_cparams(("arbitrary",)),
        name="hyena_conv",
    )(conv_w, conv_b, bias, u, u, u, kspec, consts["w1d"], consts["w1i"], consts["tc"], consts["ts"],
      consts["w2"], consts["w2i"])
    return y.reshape(HY_WIDTH, B * L)


def _outproj_kernel(yg_ref, yh_ref, yy_ref, h_ref, wo_ref, g_ref, b_ref, wrh_ref, wrl_ref, br_ref,
                    h1_ref, e_ref, w_ref, cnt_ref, *, alpha):
    mix = _dot(yg_ref[...].astype(BF16), wo_ref[0:GLA_W], _TN)
    mix += _dot(yh_ref[...].astype(BF16), wo_ref[GLA_W:GLA_W + HG_W], _TN)
    mix += _dot(yy_ref[...].astype(BF16), wo_ref[GLA_W + HG_W:], _TN)
    h1 = _layer_norm(alpha * h_ref[...] + mix, g_ref[...], b_ref[...])
    h1_ref[...] = h1
    hi = h1.astype(BF16)
    lo = (h1 - hi.astype(F32)).astype(BF16)
    lg = _dot(wrh_ref[...], hi, _NT) + _dot(wrh_ref[...], lo, _NT) + _dot(wrl_ref[...], hi, _NT) + br_ref[...]
    tm = lg.shape[1]
    gl = [lg[g:g + 1] for g in range(N_GROUPS)]
    gmax = functools.reduce(jnp.maximum, gl)
    gidx = jnp.full((1, tm), N_GROUPS - 1, I32)
    for g in range(N_GROUPS - 2, -1, -1):
        gidx = jnp.where(gl[g] == gmax, g, gidx)
    gsum = functools.reduce(jnp.add, [jnp.exp(x - gmax) for x in gl])
    g_val = 1.0 / gsum
    el = []
    for r in range(EXPERTS_PER_GROUP):
        acc = jnp.zeros((1, tm), F32)
        for g in range(N_GROUPS):
            row = N_GROUPS + g * EXPERTS_PER_GROUP + r
            acc = jnp.where(gidx == g, lg[row:row + 1], acc)
        el.append(acc)
    emax = functools.reduce(jnp.maximum, el)
    pe = [jnp.exp(x - emax) for x in el]
    esum = functools.reduce(jnp.add, pe)
    pe = [x / esum for x in pe]
    v1 = functools.reduce(jnp.maximum, pe)
    i1 = jnp.full((1, tm), EXPERTS_PER_GROUP - 1, I32)
    for r in range(EXPERTS_PER_GROUP - 2, -1, -1):
        i1 = jnp.where(pe[r] == v1, r, i1)
    pe2 = [jnp.where(i1 == r, -1.0, pe[r]) for r in range(EXPERTS_PER_GROUP)]
    v2 = functools.reduce(jnp.maximum, pe2)
    i2 = jnp.full((1, tm), EXPERTS_PER_GROUP - 1, I32)
    for r in range(EXPERTS_PER_GROUP - 2, -1, -1):
        i2 = jnp.where(pe2[r] == v2, r, i2)
    den = v1 + v2
    e0 = gidx * EXPERTS_PER_GROUP + i1
    e1 = gidx * EXPERTS_PER_GROUP + i2
    e_ref[...] = jnp.concatenate([e0, e1], axis=0)
    w_ref[...] = jnp.concatenate([g_val * (v1 / den), g_val * (v2 / den)], axis=0)
    eio = lax.broadcasted_iota(I32, (N_EXPERTS, tm), 0)
    hit = jnp.where((eio == e0) | (eio == e1), 1.0, 0.0)
    cnt_ref[0] = jnp.sum(hit, axis=1, keepdims=True)


def _outproj(yg, yh, yy, h, w_out, g, b, wr_t, br, alpha, tm=512):
    T, D = h.shape
    nr = wr_t.shape[0]
    wr_hi = wr_t.astype(BF16)
    wr_lo = (wr_t - wr_hi.astype(F32)).astype(BF16)
    full = lambda shp: pl.BlockSpec(shp, lambda i: (0,) * len(shp))
    return pl.pallas_call(
        functools.partial(_outproj_kernel, alpha=alpha),
        grid=(T // tm,),
        in_specs=[pl.BlockSpec((GLA_W, tm), lambda i: (0, i)),
                  pl.BlockSpec((HG_W, tm), lambda i: (0, i)),
                  pl.BlockSpec((HY_WIDTH, tm), lambda i: (0, i)),
                  pl.BlockSpec((tm, D), lambda i: (i, 0)),
                  full((D, D)), full((1, D)), full((1, D)), full((nr, D)), full((nr, D)), full((nr, 1))],
        out_specs=[pl.BlockSpec((tm, D), lambda i: (i, 0)),
                   pl.BlockSpec((TOP_K, tm), lambda i: (0, i)),
                   pl.BlockSpec((TOP_K, tm), lambda i: (0, i)),
                   pl.BlockSpec((1, N_EXPERTS, 1), lambda i: (i, 0, 0))],
        out_shape=[jax.ShapeDtypeStruct((T, D), F32),
                   jax.ShapeDtypeStruct((TOP_K, T), I32),
                   jax.ShapeDtypeStruct((TOP_K, T), F32),
                   jax.ShapeDtypeStruct((T // tm, N_EXPERTS, 1), F32)],
        compiler_params=_cparams(("arbitrary",)),
        name="outproj",
    )(yg, yh, yy, h, w_out.astype(BF16), g.reshape(1, D), b.reshape(1, D), wr_hi, wr_lo, br)


def _chunk_loop(n, fn):
    def body(c, carry):
        fn(pl.multiple_of(c * ROW_CHUNK, ROW_CHUNK))
        return carry
    lax.fori_loop(0, n, body, 0)


def _tile_chunks(nch_s, tile):
    return functools.reduce(lambda a, b: a + b, [nch_s[tile * N_EXPERTS + e] for e in range(N_EXPERTS)])


def _dispatch_kernel(offs_s, gst_s, nch_s, tst_s, tn_s, nu_s, h_ref, e_ref, base_ref, tri_ref,
                     pos_ref, xg_hbm, xs_ref, zbuf, sem, zsem, *, tm, LP, bm, nblocks):
    i = pl.program_id(0)
    nt = pl.num_programs(0)
    slot = i % 2

    def chunk_copy(s, src_row, dst_row):
        return pltpu.make_async_copy(xs_ref.at[s, pl.ds(src_row, ROW_CHUNK)],
                                     xg_hbm.at[pl.ds(dst_row, ROW_CHUNK)], sem.at[s])

    def zero_copy(dst_row):
        return pltpu.make_async_copy(zbuf.at[pl.ds(0, ROW_CHUNK)], xg_hbm.at[pl.ds(dst_row, ROW_CHUNK)], zsem.at[0])

    def zero_block(blk):
        return pltpu.make_async_copy(zbuf, xg_hbm.at[pl.ds(pl.multiple_of(blk * bm, bm), bm)], zsem.at[0])

    def wait_tile(tile, s):
        _chunk_loop(_tile_chunks(nch_s, tile), lambda off: chunk_copy(s, 0, 0).wait())

    @pl.when(i == 0)
    def _():
        zbuf[...] = jnp.zeros_like(zbuf)
        for e in range(N_EXPERTS):
            _chunk_loop(tn_s[e], lambda off, e=e: zero_copy(pl.multiple_of(tst_s[e] + off, ROW_CHUNK)).start())
        lax.fori_loop(nu_s[0], nblocks, lambda blk, c: (zero_block(blk).start(), c)[1], 0)
        for e in range(N_EXPERTS):
            _chunk_loop(tn_s[e], lambda off: zero_copy(0).wait())
        lax.fori_loop(nu_s[0], nblocks, lambda blk, c: (zero_block(0).wait(), c)[1], 0)

    e0, e1 = e_ref[0:1, :], e_ref[1:2, :]
    eio = lax.broadcasted_iota(I32, (N_EXPERTS, tm), 0)
    oh0, oh1 = eio == e0, eio == e1
    hit = jnp.where(oh0 | oh1, 1.0, 0.0).astype(BF16)
    posm = base_ref[0] + _dot(hit, tri_ref[...])
    pos0 = jnp.sum(jnp.where(oh0, posm, 0.0), axis=0, keepdims=True).astype(I32)
    pos1 = jnp.sum(jnp.where(oh1, posm, 0.0), axis=0, keepdims=True).astype(I32)
    pos_ref[...] = jnp.concatenate([pos0, pos1], axis=0)
    rio = lax.broadcasted_iota(I32, (LP, tm), 0)
    perm = jnp.where((rio == pos0) | (rio == pos1), 1.0, 0.0).astype(BF16)
    xs_ref[slot] = _dot(perm, h_ref[...].astype(BF16)).astype(BF16)

    @pl.when(i > 0)
    def _():
        wait_tile(i - 1, 1 - slot)

    for e in range(N_EXPERTS):
        k = i * N_EXPERTS + e
        src0, dst0 = offs_s[k], gst_s[k]
        _chunk_loop(nch_s[k], lambda off, src0=src0, dst0=dst0: chunk_copy(
            slot, pl.multiple_of(src0 + off, ROW_CHUNK), pl.multiple_of(dst0 + off, ROW_CHUNK)).start())

    @pl.when(i == nt - 1)
    def _():
        wait_tile(i, slot)


def _moe_dispatch(h1, e_kt, tables, nblocks, bm, tm):
    T, D = h1.shape
    NT = T // tm
    nrows = nblocks * bm
    LP = TOP_K * tm + N_EXPERTS * ROW_CHUNK
    r = np.arange(tm)
    tri = jnp.asarray(r[:, None] < r[None, :], F32).astype(BF16)
    base = tables["offs"].astype(F32).reshape(NT, N_EXPERTS, 1)
    flat = lambda a: a.reshape(-1).astype(I32)
    pos, xg = pl.pallas_call(
        functools.partial(_dispatch_kernel, tm=tm, LP=LP, bm=bm, nblocks=nblocks),
        grid_spec=pltpu.PrefetchScalarGridSpec(
            num_scalar_prefetch=6,
            grid=(NT,),
            in_specs=[pl.BlockSpec((tm, D), lambda i, *_: (i, 0)),
                      pl.BlockSpec((TOP_K, tm), lambda i, *_: (0, i)),
                      pl.BlockSpec((1, N_EXPERTS, 1), lambda i, *_: (i, 0, 0)),
                      pl.BlockSpec((tm, tm), lambda i, *_: (0, 0))],
            out_specs=[pl.BlockSpec((TOP_K, tm), lambda i, *_: (0, i)),
                       pl.BlockSpec(memory_space=pl.ANY)],
            scratch_shapes=[pltpu.VMEM((2, LP, D), BF16), pltpu.VMEM((bm, D), BF16),
                            pltpu.SemaphoreType.DMA((2,)), pltpu.SemaphoreType.DMA((1,))]),
        out_shape=[jax.ShapeDtypeStruct((TOP_K, T), I32), jax.ShapeDtypeStruct((nrows, D), BF16)],
        compiler_params=_cparams(("arbitrary",)),
        name="moe_dispatch",
    )(flat(tables["offs"]), flat(tables["gstart"]), flat(tables["nch"]), flat(tables["tail_start"]),
      flat(tables["tail_n"]), tables["nused"], h1, e_kt, base, tri)
    return pos, xg


def _ffn_kernel(be_ref, nu_ref, x_ref, wg_ref, wu_ref, wd_ref, y_ref, wgb, wub, wdb):
    j = pl.program_id(0)
    used = j < nu_ref[0]

    @pl.when((j == 0) | (be_ref[j] != be_ref[jnp.maximum(j - 1, 0)]))
    def _():
        wgb[...] = wg_ref[0].astype(BF16)
        wub[...] = wu_ref[0].astype(BF16)
        wdb[...] = wd_ref[0].astype(BF16)

    @pl.when(used)
    def _():
        x = x_ref[...]
        a = _dot(x, wgb[...])
        hid = (a * _sigmoid(a)) * _dot(x, wub[...])
        y_ref[...] = _dot(hid.astype(BF16), wdb[...]).astype(BF16)

    @pl.when(jnp.logical_not(used))
    def _():
        y_ref[...] = jnp.zeros_like(y_ref)


def _moe_ffn(xg, block_e, nused, wg, wu, wd, bm, first_expert=0):
    nrows, D = xg.shape
    NB = nrows // bm
    DE = wg.shape[-1]
    row = lambda j, be, nu: (jnp.minimum(j, nu[0] - 1), 0)
    wsel = lambda j, be, nu: (first_expert + be[j], 0, 0)
    return pl.pallas_call(
        _ffn_kernel,
        grid_spec=pltpu.PrefetchScalarGridSpec(
            num_scalar_prefetch=2,
            grid=(NB,),
            in_specs=[pl.BlockSpec((bm, D), row),
                      pl.BlockSpec((1, D, DE), wsel),
                      pl.BlockSpec((1, D, DE), wsel),
                      pl.BlockSpec((1, DE, D), wsel)],
            out_specs=pl.BlockSpec((bm, D), lambda j, be, nu: (j, 0)),
            scratch_shapes=[pltpu.VMEM((D, DE), BF16), pltpu.VMEM((D, DE), BF16), pltpu.VMEM((DE, D), BF16)]),
        out_shape=jax.ShapeDtypeStruct((nrows, D), BF16),
        compiler_params=_cparams(("arbitrary",)),
        name="moe_ffn",
    )(block_e, nused, xg, wg, wu, wd)


def _combine_kernel(offs_s, gst_s, nch_s, yb_hbm, pos_ref, w_ref, h_ref, g_ref, b_ref, o_ref, ybl, sem,
                    *, tm, LP, alpha):
    i = pl.program_id(0)
    nt = pl.num_programs(0)
    slot = i % 2

    def chunk_copy(s, src_row, dst_row):
        return pltpu.make_async_copy(yb_hbm.at[pl.ds(src_row, ROW_CHUNK)],
                                     ybl.at[s, pl.ds(dst_row, ROW_CHUNK)], sem.at[s])

    def issue(tile, s):
        for e in range(N_EXPERTS):
            k = tile * N_EXPERTS + e
            src0, dst0 = gst_s[k], offs_s[k]
            _chunk_loop(nch_s[k], lambda off, src0=src0, dst0=dst0: chunk_copy(
                s, pl.multiple_of(src0 + off, ROW_CHUNK), pl.multiple_of(dst0 + off, ROW_CHUNK)).start())

    @pl.when(i == 0)
    def _():
        ybl[...] = jnp.zeros_like(ybl)
        issue(0, 0)

    @pl.when(i + 1 < nt)
    def _():
        issue(i + 1, 1 - slot)

    _chunk_loop(_tile_chunks(nch_s, i), lambda off: chunk_copy(slot, 0, 0).wait())
    pos0, pos1 = pos_ref[0:1, :], pos_ref[1:2, :]
    w0, w1 = w_ref[0:1, :], w_ref[1:2, :]
    rio = lax.broadcasted_iota(I32, (LP, tm), 0)
    pw = (jnp.where(rio == pos0, w0, 0.0) + jnp.where(rio == pos1, w1, 0.0)).astype(BF16)
    ffn = _dot(pw, ybl[slot], _TN)
    o_ref[...] = _layer_norm(alpha * h_ref[...] + ffn, g_ref[...], b_ref[...])


def _moe_combine(yb, h1, pos, w_kt, tables, g, b, alpha, tm):
    T, D = h1.shape
    LP = TOP_K * tm + N_EXPERTS * ROW_CHUNK
    flat = lambda a: a.reshape(-1).astype(I32)
    return pl.pallas_call(
        functools.partial(_combine_kernel, tm=tm, LP=LP, alpha=alpha),
        grid_spec=pltpu.PrefetchScalarGridSpec(
            num_scalar_prefetch=3,
            grid=(T // tm,),
            in_specs=[pl.BlockSpec(memory_space=pl.ANY),
                      pl.BlockSpec((TOP_K, tm), lambda i, *_: (0, i)),
                      pl.BlockSpec((TOP_K, tm), lambda i, *_: (0, i)),
                      pl.BlockSpec((tm, D), lambda i, *_: (i, 0)),
                      pl.BlockSpec((1, D), lambda i, *_: (0, 0)),
                      pl.BlockSpec((1, D), lambda i, *_: (0, 0))],
            out_specs=pl.BlockSpec((tm, D), lambda i, *_: (i, 0)),
            scratch_shapes=[pltpu.VMEM((2, LP, D), BF16), pltpu.SemaphoreType.DMA((2,))]),
        out_shape=jax.ShapeDtypeStruct((T, D), F32),
        compiler_params=_cparams(("arbitrary",)),
        name="moe_combine",
    )(flat(tables["offs"]), flat(tables["gstart"]), flat(tables["nch"]), yb, pos, w_kt, h1,
      g.reshape(1, D), b.reshape(1, D))


def _dispatch_tables(cnt, bm, nblocks):
    padlen = ((cnt + ROW_CHUNK - 1) // ROW_CHUNK) * ROW_CHUNK
    offs = jnp.cumsum(padlen, axis=1) - padlen
    tot = jnp.sum(padlen, axis=0)
    region = ((tot + bm - 1) // bm) * bm
    rend = jnp.cumsum(region)
    rstart = rend - region
    gstart = rstart[None, :] + jnp.cumsum(padlen, axis=0) - padlen
    blk_row = jnp.arange(nblocks, dtype=I32)[:, None] * bm
    block_e = jnp.minimum(jnp.sum((blk_row >= rend[None, :]).astype(I32), axis=1), N_EXPERTS - 1)
    return dict(offs=offs, gstart=gstart, nch=padlen // ROW_CHUNK, tail_start=rstart + tot,
                tail_n=(region - tot) // ROW_CHUNK, block_e=block_e.astype(I32),
                nused=(rend[-1:] // bm).astype(I32))


def _in_perm():
    splits = (192, 192, 384, 384, 32, 384, 768, 384, 384, 768)
    offs = np.concatenate([[0], np.cumsum(splits)])
    gq, gk, gv, gg, ga, hq, hf, hi, hgt, hyu = [np.arange(offs[i], offs[i + 1]) for i in range(10)]
    perm = np.concatenate([hq, hi, hgt, hf, gv, gg, gq, gk, hyu, ga])
    assert perm.shape[0] == D_IN
    return perm


def kernel(x, ln_in_g, ln_in_b, w_in, gla_wa2, gla_ba, gla_norm_g, hg_lb_logits, hg_norm_g, hy_conv_w, hy_conv_b, hy_w1, hy_b1, hy_freq, hy_w2, hy_b2, hy_w3, hy_bias, w_out, ln1_g, ln1_b, moe_wr_g, moe_br_g, moe_wr_e, moe_br_e, moe_w_gate, moe_w_up, moe_w_down, ln2_g, ln2_b):
    B, L, D = x.shape
    T = B * L
    depth = w_in.shape[0]
    alpha = (2 * depth) ** 0.25
    bm = 512
    tmr = 512
    nblocks = -(-(T * TOP_K + (T // tmr) * N_EXPERTS * (ROW_CHUNK - 1) + N_EXPERTS * (bm - 1)) // bm)
    perm = _in_perm()
    consts = _dft_consts(L)

    p = jax.nn.softmax(hg_lb_logits.astype(F32), axis=0)
    lbs = jnp.cumsum(p, axis=0) - p[0:1]
    lbc = jnp.concatenate([jnp.broadcast_to((1.0 - lbs)[..., None], lbs.shape + (LANES,)),
                           jnp.broadcast_to(jnp.maximum(lbs, LB_FLOOR)[..., None], lbs.shape + (LANES,))], axis=-1)

    wg_all = moe_w_gate.reshape((depth * N_EXPERTS,) + moe_w_gate.shape[2:])
    wu_all = moe_w_up.reshape((depth * N_EXPERTS,) + moe_w_up.shape[2:])
    wd_all = moe_w_down.reshape((depth * N_EXPERTS,) + moe_w_down.shape[2:])
    h = x.reshape(T, D)
    for l in range(depth):
        w_t = w_in[l][:, perm].T.astype(BF16)
        proj_t, h = _inproj(h, ln_in_g, ln_in_b, w_t, apply_ln=(l == 0))
        wa_t = jnp.swapaxes(gla_wa2[l], 1, 2)
        y_gla = _scan(proj_t, "gla", B, L, (wa_t, gla_ba[l].reshape(2, GLA_K, 1)), gla_norm_g[l].reshape(GLA_W, 1))
        y_hg = _scan(proj_t, "hg", B, L, (lbc[l],), hg_norm_g[l].reshape(HG_W, 1))
        kt = _hyena_filters(L, hy_w1[l], hy_b1[l], hy_freq[l], hy_w2[l], hy_b2[l], hy_w3[l])
        kspec = _hyena_spectrum(kt, consts, L)
        y_hy = _hyena(proj_t, kspec, consts, hy_conv_w[l], hy_conv_b[l], hy_bias[l], B, L)
        nr = N_GROUPS + N_EXPERTS
        nrp = -(-nr // 8) * 8
        wr_t = jnp.zeros((nrp, D), F32).at[:nr].set(jnp.concatenate([moe_wr_g[l], moe_wr_e[l]], axis=1).T)
        br = jnp.zeros((nrp, 1), F32).at[:nr, 0].set(jnp.concatenate([moe_br_g[l], moe_br_e[l]]))
        h1, e_kt, w_kt, cnt = _outproj(y_gla, y_hg, y_hy, h, w_out[l], ln1_g[l], ln1_b[l], wr_t, br, alpha, tm=tmr)
        tables = _dispatch_tables(cnt.reshape(T // tmr, N_EXPERTS).astype(I32), bm, nblocks)
        pos, xg = _moe_dispatch(h1, e_kt, tables, nblocks, bm, tmr)
        yb = _moe_ffn(xg, tables["block_e"], tables["nused"], wg_all, wu_all, wd_all, bm, first_expert=l * N_EXPERTS)
        h = _moe_combine(yb, h1, pos, w_kt, tables, ln2_g[l], ln2_b[l], alpha, tmr)
    return h.reshape(B, L, D)
```

```python
import functools
import math

import numpy as np
import jax
import jax.numpy as jnp
from jax import lax
from jax.experimental import pallas as pl
from jax.experimental.pallas import tpu as pltpu

F32 = jnp.float32
BF16 = jnp.bfloat16
I32 = jnp.int32

GLA_HEADS, GLA_DK, GLA_DV, GLA_LOWRANK, GLA_TAU = 6, 32, 64, 16, 16.0
HG_HEADS, HG_DK, HG_DV = 6, 64, 64
HY_WIDTH, HY_ORDER, HY_EMB, HY_FFN, HY_INNER = 256, 2, 33, 64, 2
HY_FAST_DECAY, HY_SLOW_DECAY, HY_TARGET = 0.3, 1.5, 1e-2
N_GROUPS, EXPERTS_PER_GROUP = 4, 4
N_EXPERTS = N_GROUPS * EXPERTS_PER_GROUP
TOP_K = 2
LN_EPS, RMS_EPS, LB_FLOOR = 1e-5, 1e-6, 1e-30

LANES = 128
SCAN_CHUNK = LANES
ROW_CHUNK = 16
SEG_SIZES = (64, 32, ROW_CHUNK)
VMEM_LIMIT = 56 * 1024 * 1024

GLA_W = GLA_HEADS * GLA_DV
GLA_K = GLA_HEADS * GLA_DK
HG_W = HG_HEADS * HG_DV
HG_K = HG_HEADS * HG_DK
OFF_HQ, OFF_HI, OFF_HGT, OFF_HF = 0, 384, 768, 1152
OFF_GV, OFF_GG, OFF_GQ, OFF_GK, OFF_HY, OFF_GA = 1920, 2304, 2688, 2880, 3072, 3840
D_IN = 3872


def _dot(a, b, dims=(((1,), (0,)), ((), ())), precision=None):
    return lax.dot_general(a, b, dims, preferred_element_type=F32, precision=precision)


_NT = (((1,), (1,)), ((), ()))
_TN = (((0,), (0,)), ((), ()))


def _layer_norm(x, g, b):
    mu = jnp.mean(x, axis=-1, keepdims=True)
    xc = x - mu
    var = jnp.mean(xc * xc, axis=-1, keepdims=True)
    return xc * lax.rsqrt(var + LN_EPS) * g + b


def _log_sigmoid(x):
    return jnp.minimum(x, 0.0) - jnp.log(1.0 + jnp.exp(-jnp.abs(x)))


def _sigmoid(x):
    return 0.5 + 0.5 * jnp.tanh(0.5 * x)


def _cparams(sem):
    return pltpu.CompilerParams(dimension_semantics=sem, vmem_limit_bytes=VMEM_LIMIT)


def _inproj_kernel(x_ref, g_ref, b_ref, w_ref, *outs, apply_ln):
    x = x_ref[...]
    if apply_ln:
        x = _layer_norm(x, g_ref[...], b_ref[...])
        outs[1][...] = x
    outs[0][...] = _dot(w_ref[...], x.astype(BF16), _NT)


def _inproj(x, g, b, w_t, apply_ln, tm=512):
    T, D = x.shape
    n_out = w_t.shape[0]
    out_shape = [jax.ShapeDtypeStruct((n_out, T), F32)]
    out_specs = [pl.BlockSpec((n_out, tm), lambda i: (0, i))]
    if apply_ln:
        out_shape.append(jax.ShapeDtypeStruct((T, D), F32))
        out_specs.append(pl.BlockSpec((tm, D), lambda i: (i, 0)))
    res = pl.pallas_call(
        functools.partial(_inproj_kernel, apply_ln=apply_ln),
        grid=(T // tm,),
        in_specs=[pl.BlockSpec((tm, D), lambda i: (i, 0)),
                  pl.BlockSpec((1, D), lambda i: (0, 0)),
                  pl.BlockSpec((1, D), lambda i: (0, 0)),
                  pl.BlockSpec((n_out, D), lambda i: (0, 0))],
        out_specs=out_specs,
        out_shape=out_shape,
        compiler_params=_cparams(("arbitrary",)),
        name="inproj",
    )(x, g.reshape(1, D), b.reshape(1, D), w_t)
    return res if apply_ln else (res[0], x)


def _scan_kernel(*refs, mode, H, K, V, TB, NCB):
    if mode == "gla":
        (qf_ref, qb_ref, kf_ref, kb_ref, vf_ref, vb_ref, gaf_ref, gab_ref, wa_ref, ba_ref,
         gtf_ref, gtb_ref, ng_ref, y_ref, s_ref, oacc_ref, oi_ref, qg_ref, ut_ref, dr_ref) = refs
        q_refs, k_refs, ga_refs = (qf_ref, qb_ref), (kf_ref, kb_ref), (gaf_ref, gab_ref)
    else:
        (qf_ref, qb_ref, zf_ref, zb_ref, vf_ref, vb_ref, lbc_ref,
         gtf_ref, gtb_ref, ng_ref, y_ref, s_ref, oacc_ref, oi_ref, qg_ref, ut_ref, dr_ref) = refs
        q_refs, z_refs = (qf_ref, qb_ref), (zf_ref, zb_ref)
    v_refs, gate_refs = (vf_ref, vb_ref), (gtf_ref, gtb_ref)
    C = SCAN_CHUNK
    half = C // 2
    nchunks = TB // C
    n = pl.program_id(1)
    blocks = (n, NCB - 1 - n)

    @pl.when(n == 0)
    def _():
        s_ref[...] = jnp.zeros_like(s_ref)

    def gates(d, sl):
        if mode == "gla":
            a = _dot(wa_ref[d], ga_refs[d][:, sl], precision=lax.Precision.HIGHEST) + ba_ref[d]
            g = _log_sigmoid(a) * (1.0 / GLA_TAU)
            q = q_refs[d][:, sl] * (K ** -0.5)
            k = k_refs[d][:, sl]
        else:
            one_m_lb, lb_floor = lbc_ref[d, :, 0:C], lbc_ref[d, :, C:2 * C]
            s = one_m_lb * _sigmoid(z_refs[d][:, sl])
            g = jnp.log(s + lb_floor)
            k = one_m_lb - s
            hq = q_refs[d][:, sl]
            q = hq * _sigmoid(hq)
        return q, k, g

    r = lax.broadcasted_iota(I32, (C, C), 0)
    c = lax.broadcasted_iota(I32, (C, C), 1)
    same = (r >= half) == (c >= half)
    lane_lo = lax.broadcasted_iota(I32, (1, C), 1) < half
    sign_lo = jnp.where(lane_lo, 1.0, -1.0)
    consts = (((r <= c).astype(BF16), (r < half) & (c >= half), same & (r <= c),
               (C - 1, half, half // 2, half + half // 2)),
              ((r >= c).astype(BF16), (r >= half) & (c < half), same & (r >= c),
               (0, half - 1, half // 2 - 1, half + half // 2 - 1)))

    heads = [(slice(h * K, (h + 1) * K), slice(h * V, (h + 1) * V)) for h in range(H)]

    def cumulate(d, g):
        cum = consts[d][0]
        g1 = g.astype(BF16)
        g2 = (g - g1.astype(F32)).astype(BF16)
        G = _dot(g1, cum) + _dot(g2, cum)
        ge_row = _dot(ones8, g1, _NT) + _dot(ones8, g2, _NT)
        return G, ge_row

    def scale(d, sl, q, k, G, ge_row):
        _, _, _, (c_end, c_mid, c_a, c_b) = consts[d]
        v = v_refs[d][:, sl].astype(BF16)
        g_end = G[:, c_end:c_end + 1]
        g_mid = G[:, c_mid:c_mid + 1]
        e2 = G - jnp.where(lane_lo, G[:, c_a:c_a + 1], G[:, c_b:c_b + 1])
        q2 = (q * jnp.exp(e2)).astype(BF16)
        k2 = (k * jnp.exp(-e2)).astype(BF16)
        dm = G - g_mid
        x1 = jnp.exp(dm * (sign_lo if d else -sign_lo))
        q1 = (q * x1).astype(BF16)
        k1 = (k * x1).astype(BF16)
        qg = (q * jnp.exp(G)).astype(BF16)
        kd = (k * jnp.exp(g_end - G)).astype(BF16)
        return dict(q1=q1, k1=k1, q2=q2, k2=k2, qg=qg, kd=kd, v=v, dec_row=jnp.exp(ge_row))

    ones8 = jnp.ones((8, C), BF16)

    def scores(x):
        return [(_dot(x["k1"][rk], x["q1"][rk], _TN), _dot(x["k2"][rk], x["q2"][rk], _TN)) for rk, _ in heads]

    def masked(d, ps):
        _, off1, diag, _ = consts[d]
        return [jnp.where(off1, p1, jnp.where(diag, p2, 0.0)).astype(BF16) for p1, p2 in ps]

    def park(d, ci, x, pm):
        oi_ref[d, ci] = jnp.concatenate([_dot(x["v"][rv], pm[h]) for h, (_, rv) in enumerate(heads)], axis=0)
        qg_ref[d, ci] = x["qg"]
        dr_ref[d, ci] = x["dec_row"]
        for h, (rk, rv) in enumerate(heads):
            ut_ref[d, ci, h] = _dot(x["v"][rv], x["kd"][rk], _NT)

    def chunk_slices(ci):
        offs = (pl.multiple_of(ci * C, C), pl.multiple_of((nchunks - 1 - ci) * C, C))
        return offs, [pl.ds(offs[d], C) for d in range(2)]

    def independent(ci, carry):
        _, sls = chunk_slices(ci)
        dirs = range(2)
        qkg = [gates(d, sls[d]) for d in dirs]
        cums = [cumulate(d, qkg[d][2]) for d in dirs]
        xs = [scale(d, sls[d], qkg[d][0], qkg[d][1], *cums[d]) for d in dirs]
        scs = [scores(xs[d]) for d in dirs]
        pms = [masked(d, scs[d]) for d in dirs]
        for d in dirs:
            park(d, ci, xs[d], pms[d])
        return carry

    lax.fori_loop(0, nchunks, independent, 0)

    def carried(d, ci):
        o = []
        for h, (rk, rv) in enumerate(heads):
            st = s_ref[d, h]
            o.append(oi_ref[d, ci, rv, :] + _dot(st.astype(BF16), qg_ref[d, ci, rk, :]))
            s_ref[d, h] = st * dr_ref[d, ci, 0:1, rk] + ut_ref[d, ci, h]
        return jnp.concatenate(o, axis=0)

    def finish(o, gate):
        act = gate * _sigmoid(gate) if mode == "gla" else _sigmoid(gate)
        ys = []
        for h in range(H):
            oh = o[h * V:(h + 1) * V]
            ms = jnp.mean(oh * oh, axis=0, keepdims=True)
            ys.append(oh * lax.rsqrt(ms + RMS_EPS))
        return (jnp.concatenate(ys, axis=0) * ng_ref[...] * act).astype(y_ref.dtype)

    def chunk(ci, carry):
        offs, sls = chunk_slices(ci)
        tsls = [pl.ds(pl.multiple_of(blocks[d] * TB + offs[d], C), C) for d in range(2)]
        o = (carried(0, ci), carried(1, ci))

        @pl.when(2 * n < NCB)
        def _():
            for d in range(2):
                oacc_ref[:, tsls[d]] = o[d]

        @pl.when(2 * n >= NCB)
        def _():
            tot = [o[d] + oacc_ref[:, tsls[d]] for d in range(2)]
            for d in range(2):
                y_ref[:, tsls[d]] = finish(tot[d], gate_refs[d][:, sls[d]])
        return carry

    lax.fori_loop(0, nchunks, chunk, 0)


def _scan(proj_t, mode, B, L, extra, norm_g, TB=512):
    NCB = L // TB
    nch = TB // SCAN_CHUNK
    assert NCB % 2 == 0
    if mode == "gla":
        H, K, V = GLA_HEADS, GLA_DK, GLA_DV
    else:
        H, K, V = HG_HEADS, HG_DK, HG_DV
    HK, HV = H * K, H * V
    cf = lambda b, n: b * NCB + n
    cb = lambda b, n: b * NCB + NCB - 1 - n
    gf = lambda b, n: b * NCB + jnp.maximum(n, NCB // 2)
    gb = lambda b, n: b * NCB + jnp.minimum(NCB - 1 - n, NCB // 2 - 1)

    def pair(rows, off, fwd=cf, bwd=cb, dir_step=0):
        return [pl.BlockSpec((rows, TB), lambda b, n: (off // rows, fwd(b, n))),
                pl.BlockSpec((rows, TB), lambda b, n: (off // rows + dir_step, bwd(b, n)))]

    whole = lambda shp: pl.BlockSpec(shp, lambda b, n: (0,) * len(shp))
    if mode == "gla":
        wa_t, ba = extra
        in_specs = (pair(HK, OFF_GQ) + pair(HK, OFF_GK) + pair(HV, OFF_GV)
                    + pair(GLA_LOWRANK, OFF_GA, dir_step=1)
                    + [whole((2, HK, GLA_LOWRANK)), whole((2, HK, 1))]
                    + pair(HV, OFF_GG, gf, gb) + [whole((HV, 1))])
        args = (proj_t,) * 8 + (wa_t, ba, proj_t, proj_t, norm_g)
    else:
        (lbc,) = extra
        in_specs = (pair(HK, OFF_HQ) + pair(HK, OFF_HF, dir_step=1) + pair(HV, OFF_HI)
                    + [whole((2, HK, 2 * LANES))] + pair(HV, OFF_HGT, gf, gb) + [whole((HV, 1))])
        args = (proj_t,) * 6 + (lbc, proj_t, proj_t, norm_g)
    return pl.pallas_call(
        functools.partial(_scan_kernel, mode=mode, H=H, K=K, V=V, TB=TB, NCB=NCB),
        grid=(B, NCB),
        in_specs=in_specs,
        out_specs=pl.BlockSpec((HV, L), lambda b, n: (0, b)),
        out_shape=jax.ShapeDtypeStruct((HV, B * L), BF16),
        scratch_shapes=[pltpu.VMEM((2, H, V, K), F32), pltpu.VMEM((HV, L), F32),
                        pltpu.VMEM((2, nch, HV, SCAN_CHUNK), F32), pltpu.VMEM((2, nch, HK, SCAN_CHUNK), BF16),
                        pltpu.VMEM((2, nch, H, V, K), F32), pltpu.VMEM((2, nch, 8, HK), F32)],
        compiler_params=_cparams(("arbitrary", "arbitrary")),
        name="scan_" + mode,
    )(*args)


def _dft_consts(L):
    N = 2 * L
    NA = N // LANES
    a = np.arange(NA)[:, None] * np.arange(NA)[None, :]
    ca, sa = np.cos(2 * np.pi * a / NA), np.sin(2 * np.pi * a / NA)
    hh = NA // 2
    w1d = np.block([[ca[:, :hh], sa[:, :hh]], [-sa[:, :hh], ca[:, :hh]]])
    w1f = np.concatenate([ca, -sa], axis=0)
    w1i = np.block([[ca[:hh, :], -sa[:hh, :]], [sa[:hh, :], ca[:hh, :]]])
    bb = np.arange(LANES)[:, None] * np.arange(LANES)[None, :]
    cb, sb = np.cos(2 * np.pi * bb / LANES), np.sin(2 * np.pi * bb / LANES)
    w2 = np.block([[cb, -sb], [sb, cb]])
    w2i = np.block([[cb, sb], [-sb, cb]])
    tw = np.arange(NA)[:, None] * np.arange(LANES)[None, :]
    tc, ts = np.cos(2 * np.pi * tw / N), np.sin(2 * np.pi * tw / N)
    bf = lambda m: jnp.asarray(m, dtype=F32).astype(BF16)
    return dict(w1d=bf(w1d), w1f=bf(w1f), w1i=bf(w1i), w2=bf(w2), w2i=bf(w2i),
                tc=jnp.asarray(tc, F32), ts=jnp.asarray(ts, F32))


def _pos_features(L):
    t = np.linspace(0.0, 1.0, L)
    w = 2.0 * np.pi * np.arange(L) / L
    bands = np.linspace(1e-4, (HY_EMB - 1) // 2 - 1, (HY_EMB - 1) // 2)
    z = np.concatenate([t[None, :], np.cos(bands[:, None] * w[None, :]), -np.sin(bands[:, None] * w[None, :])], axis=0)
    kp = -(-HY_EMB // 8) * 8
    z = np.concatenate([z, np.zeros((kp - HY_EMB, L))], axis=0)
    idx = (L - np.arange(L)) % L
    z_rev, t_rev = z[:, idx], t[idx]
    mask = (np.arange(L) >= 1).astype(np.float64)
    return (jnp.asarray(z, F32), jnp.asarray(z_rev, F32), jnp.asarray(t[None, :], F32),
            jnp.asarray(t_rev[None, :], F32), jnp.asarray(mask[None, :], F32))


def _filter_kernel(z_ref, zr_ref, t_ref, tr_ref, m_ref, w1_ref, b1_ref, fr_ref, w2_ref, b2_ref,
                   w3f_ref, w3b_ref, dl_ref, out_ref, h_ref, *, L):
    first = (pl.program_id(0) == 0) & (pl.program_id(1) == 0)
    hi = lax.Precision.HIGHEST

    @pl.when(first)
    def _():
        fr = fr_ref[...]
        for idx, zz in enumerate((z_ref, zr_ref)):
            h = jnp.sin(fr * (_dot(w1_ref[...], zz[...], precision=hi) + b1_ref[...]))
            for i in range(HY_INNER):
                h = jnp.sin(fr * (_dot(w2_ref[i], h, precision=hi) + b2_ref[i]))
            h_ref[idx] = h.astype(BF16)

    ad = jnp.abs(dl_ref[...])
    kf = _dot(w3f_ref[...].astype(BF16), h_ref[0]) * jnp.exp(-t_ref[...] * ad)
    kb = _dot(w3b_ref[...].astype(BF16), h_ref[1]) * jnp.exp(-tr_ref[...] * ad) * m_ref[...]
    den = jnp.sum(jnp.abs(kf), axis=1, keepdims=True) + jnp.sum(jnp.abs(kb), axis=1, keepdims=True)
    scale = 1.0 / (jnp.maximum(den, 1e-12) * (2.0 * L))
    out_ref[0, :, 0:L] = kf * scale
    out_ref[0, :, L:2 * L] = kb * scale


def _hyena_filters(L, w1, b1, freq, w2, b2, w3, cg=64):
    z, z_rev, t, t_rev, mask = _pos_features(L)
    kp = z.shape[0]
    w1_t = jnp.zeros((HY_FFN, kp), F32).at[:, :HY_EMB].set(w1.T)
    w2_t = jnp.swapaxes(w2, 1, 2)
    w3_t = w3.T
    max_decay = math.log(HY_TARGET) / HY_FAST_DECAY
    min_decay = math.log(HY_TARGET) / HY_SLOW_DECAY
    deltas = jnp.asarray(np.linspace(min_decay, max_decay, HY_WIDTH).reshape(HY_WIDTH, 1), F32)
    ncg = HY_WIDTH // cg
    full = lambda shp: pl.BlockSpec(shp, lambda o, j: (0,) * len(shp))
    return pl.pallas_call(
        functools.partial(_filter_kernel, L=L),
        grid=(HY_ORDER, ncg),
        in_specs=[full((kp, L)), full((kp, L)), full((1, L)), full((1, L)), full((1, L)),
                  full((HY_FFN, kp)), full((HY_FFN, 1)), full((HY_FFN, 1)),
                  full((HY_INNER, HY_FFN, HY_FFN)), full((HY_INNER, HY_FFN, 1)),
                  pl.BlockSpec((cg, HY_FFN), lambda o, j: (o * 2 * ncg + j, 0)),
                  pl.BlockSpec((cg, HY_FFN), lambda o, j: (o * 2 * ncg + ncg + j, 0)),
                  pl.BlockSpec((cg, 1), lambda o, j: (j, 0))],
        out_specs=pl.BlockSpec((1, cg, 2 * L), lambda o, j: (o, j, 0)),
        out_shape=jax.ShapeDtypeStruct((HY_ORDER, HY_WIDTH, 2 * L), F32),
        scratch_shapes=[pltpu.VMEM((2, HY_FFN, L), BF16)],
        compiler_params=_cparams(("arbitrary", "arbitrary")),
        name="hyena_filter",
    )(z, z_rev, t, t_rev, mask, w1_t, b1.reshape(HY_FFN, 1), freq.reshape(HY_FFN, 1), w2_t,
      b2.reshape(HY_INNER, HY_FFN, 1), w3_t, w3_t, deltas)


def _spectrum_kernel(k2_ref, w1_ref, tc_ref, ts_ref, w2_ref, out_ref, k_ref, *, NA, cg):
    k_ref[0] = k2_ref[0].reshape(cg, NA, LANES)
    tc2 = jnp.concatenate([tc_ref[...]] * 2, axis=1)
    ts2 = jnp.concatenate([ts_ref[...]] * 2, axis=1)
    G = 4

    def body(it, carry):
        lhs = []
        for pr in range(G // 2):
            rhs = jnp.concatenate([k_ref[0, G * it + 2 * pr + cc] for cc in range(2)], axis=1)
            a = _dot(w1_ref[...], rhs.astype(BF16))
            a_re, a_im = a[:NA], a[NA:]
            b_re = a_re * tc2 + a_im * ts2
            b_im = a_im * tc2 - a_re * ts2
            lhs += [jnp.concatenate([b_re[:, cc * LANES:(cc + 1) * LANES], b_im[:, cc * LANES:(cc + 1) * LANES]], axis=1)
                    for cc in range(2)]
        x = _dot(jnp.concatenate(lhs, axis=0).astype(BF16), w2_ref[...])
        for i in range(G):
            out_ref[0, G * it + i] = x[i * NA:(i + 1) * NA]
        return carry
    lax.fori_loop(0, cg // G, body, 0)


def _hyena_spectrum(kt, consts, L, cg=32):
    NA = 2 * L // LANES
    full = lambda shp: pl.BlockSpec(shp, lambda o, j: (0,) * len(shp))
    return pl.pallas_call(
        functools.partial(_spectrum_kernel, NA=NA, cg=cg),
        grid=(HY_ORDER, HY_WIDTH // cg),
        in_specs=[pl.BlockSpec((1, cg, 2 * L), lambda o, j: (o, j, 0)),
                  full((2 * NA, NA)), full((NA, LANES)), full((NA, LANES)), full((2 * LANES, 2 * LANES))],
        out_specs=pl.BlockSpec((1, cg, NA, 2 * LANES), lambda o, j: (o, j, 0, 0)),
        out_shape=jax.ShapeDtypeStruct((HY_ORDER, HY_WIDTH, NA, 2 * LANES), F32),
        scratch_shapes=[pltpu.VMEM((1, cg, NA, LANES), F32)],
        compiler_params=_cparams(("arbitrary", "arbitrary")),
        name="hyena_spectrum",
    )(kt, consts["w1f"], consts["tc"], consts["ts"], consts["w2"])


def _hyena_kernel(cw_ref, cb_ref, hb_ref, v2_ref, x12_ref, x22_ref, ks_ref, w1d_ref, w1i_ref, tc_ref, ts_ref,
                  w2_ref, w2i_ref, y2_ref, v_ref, x1_ref, x2_ref, y_ref, *, NA, cg, B):
    hh = NA // 2
    L = hh * LANES
    j = pl.program_id(0)
    for src, dst in ((v2_ref, v_ref), (x12_ref, x1_ref), (x22_ref, x2_ref)):
        for b in range(B):
            dst[:, b] = src[:, b * L:(b + 1) * L].reshape(cg, hh, LANES)
    row = lax.broadcasted_iota(I32, (hh, LANES), 0)
    lane = lax.broadcasted_iota(I32, (hh, LANES), 1)
    first = (row == 0) & (lane == 0)
    last = (row == hh - 1) & (lane == LANES - 1)

    def short_conv(x, ch):
        r1 = pltpu.roll(x, 1, 1)
        prev = jnp.where(lane == 0, pltpu.roll(r1, 1, 0), r1)
        prev = jnp.where(first, 0.0, prev)
        r2 = pltpu.roll(x, LANES - 1, 1)
        nxt = jnp.where(lane == LANES - 1, pltpu.roll(r2, hh - 1, 0), r2)
        nxt = jnp.where(last, 0.0, nxt)
        return cw_ref[0, ch] * prev + cw_ref[1, ch] * x + cw_ref[2, ch] * nxt + cb_ref[ch]

    P = B // 2
    tc, ts = tc_ref[...], ts_ref[...]
    tc2 = jnp.concatenate([tc, tc], axis=1)
    ts2 = jnp.concatenate([ts, ts], axis=1)
    lane2 = lambda x, cc: x[:, cc * LANES:(cc + 1) * LANES]

    def body(it, carry):
        cis = [2 * it + cc for cc in range(2)]
        chs = [j * cg + ci for ci in cis]
        seqs = [(p, cc) for p in range(P) for cc in range(2)]
        z = {(p, cc): [short_conv(v_ref[cis[cc], 2 * p + r], chs[cc]) for r in range(2)] for p, cc in seqs}
        gate_refs = (x1_ref, x2_ref)
        gates = [{(p, cc): [short_conv(gate_refs[o][cis[cc], 2 * p + r], (o + 1) * HY_WIDTH + chs[cc])
                            for r in range(2)] for p, cc in seqs} for o in range(HY_ORDER)]
        for o in range(HY_ORDER):
            lhs = []
            for p in range(P):
                rhs = jnp.concatenate([jnp.concatenate(z[(p, cc)], axis=0) for cc in range(2)], axis=1)
                a = _dot(w1d_ref[...], rhs.astype(BF16))
                a_re, a_im = a[:NA], a[NA:]
                b_re = a_re * tc2 + a_im * ts2
                b_im = a_im * tc2 - a_re * ts2
                lhs += [jnp.concatenate([lane2(b_re, cc), lane2(b_im, cc)], axis=1) for cc in range(2)]
            x = _dot(jnp.concatenate(lhs, axis=0).astype(BF16), w2_ref[...])
            ys = []
            for idx, (p, cc) in enumerate(seqs):
                xb = x[idx * NA:(idx + 1) * NA]
                ks = ks_ref[o, cis[cc]]
                x_re, x_im = lane2(xb, 0), lane2(xb, 1)
                k_re, k_im = lane2(ks, 0), lane2(ks, 1)
                ys.append(jnp.concatenate([x_re * k_re - x_im * k_im, x_re * k_im + x_im * k_re], axis=1))
            bq = _dot(jnp.concatenate(ys, axis=0).astype(BF16), w2i_ref[...])
            for p in range(P):
                cr, cim = [], []
                for cc in range(2):
                    blk = bq[(2 * p + cc) * NA:(2 * p + cc + 1) * NA]
                    b_re, b_im = lane2(blk, 0), lane2(blk, 1)
                    cr.append(b_re * tc - b_im * ts)
                    cim.append(b_re * ts + b_im * tc)
                rhs = jnp.concatenate([jnp.concatenate(cr, axis=1), jnp.concatenate(cim, axis=1)], axis=0)
                conv = _dot(w1i_ref[...], rhs.astype(BF16))
                for cc in range(2):
                    bias = hb_ref[o, chs[cc]]
                    z[(p, cc)] = [gates[o][(p, cc)][r] * (lane2(conv, cc)[r * hh:(r + 1) * hh] + z[(p, cc)][r] * bias)
                                  for r in range(2)]
        for p, cc in seqs:
            for r in range(2):
                y_ref[cis[cc], 2 * p + r] = z[(p, cc)][r]
        return carry

    lax.fori_loop(0, cg // 2, body, 0)
    for b in range(B):
        y2_ref[:, b * L:(b + 1) * L] = y_ref[:, b].reshape(cg, L)


def _hyena(proj_t, kspec, consts, conv_w, conv_b, bias, B, L, cg=8):
    NA = 2 * L // LANES
    hh = NA // 2
    ncg = HY_WIDTH // cg
    base = OFF_HY // cg
    smem = pl.BlockSpec(memory_space=pltpu.SMEM)
    full = lambda shp: pl.BlockSpec(shp, lambda j: (0,) * len(shp))
    blk = lambda off: pl.BlockSpec((cg, B * L), lambda j: (off + j, 0))
    tiles = pltpu.VMEM((cg, B, hh, LANES), F32)
    return pl.pallas_call(
        functools.partial(_hyena_kernel, NA=NA, cg=cg, B=B),
        grid=(ncg,),
        in_specs=[smem, smem, smem, blk(base), blk(base + ncg), blk(base + 2 * ncg),
                  pl.BlockSpec((HY_ORDER, cg, NA, 2 * LANES), lambda j: (0, j, 0, 0)),
                  full((2 * NA, NA)), full((NA, 2 * NA)), full((NA, LANES)), full((NA, LANES)),
                  full((2 * LANES, 2 * LANES)), full((2 * LANES, 2 * LANES))],
        out_specs=pl.BlockSpec((cg, B * L), lambda j: (j, 0)),
        out_shape=jax.ShapeDtypeStruct((HY_WIDTH, B * L), F32),
        scratch_shapes=[tiles, tiles, tiles, tiles],
        compiler_params=_cparams(("arbitrary",)),
        name="hyena_conv",
    )(conv_w, conv_b, bias, proj_t, proj_t, proj_t, kspec, consts["w1d"], consts["w1i"], consts["tc"], consts["ts"],
      consts["w2"], consts["w2i"])


def _outproj_kernel(yg_ref, yh_ref, yy_ref, h_ref, wo_ref, g_ref, b_ref, wrh_ref, wrl_ref, br_ref,
                    h1_ref, e_ref, w_ref, cnt_ref, *, alpha):
    mix = _dot(yg_ref[...].astype(BF16), wo_ref[0:GLA_W], _TN)
    mix += _dot(yh_ref[...].astype(BF16), wo_ref[GLA_W:GLA_W + HG_W], _TN)
    mix += _dot(yy_ref[...].astype(BF16), wo_ref[GLA_W + HG_W:], _TN)
    h1 = _layer_norm(alpha * h_ref[...] + mix, g_ref[...], b_ref[...])
    h1_ref[...] = h1
    hi = h1.astype(BF16)
    lo = (h1 - hi.astype(F32)).astype(BF16)
    lg = _dot(wrh_ref[...], hi, _NT) + _dot(wrh_ref[...], lo, _NT) + _dot(wrl_ref[...], hi, _NT) + br_ref[...]
    tm = lg.shape[1]
    gl = [lg[g:g + 1] for g in range(N_GROUPS)]
    gmax = functools.reduce(jnp.maximum, gl)
    gidx = jnp.full((1, tm), N_GROUPS - 1, I32)
    for g in range(N_GROUPS - 2, -1, -1):
        gidx = jnp.where(gl[g] == gmax, g, gidx)
    gsum = functools.reduce(jnp.add, [jnp.exp(x - gmax) for x in gl])
    g_val = 1.0 / gsum
    el = []
    for r in range(EXPERTS_PER_GROUP):
        acc = jnp.zeros((1, tm), F32)
        for g in range(N_GROUPS):
            row = N_GROUPS + g * EXPERTS_PER_GROUP + r
            acc = jnp.where(gidx == g, lg[row:row + 1], acc)
        el.append(acc)
    emax = functools.reduce(jnp.maximum, el)
    pe = [jnp.exp(x - emax) for x in el]
    esum = functools.reduce(jnp.add, pe)
    pe = [x / esum for x in pe]
    v1 = functools.reduce(jnp.maximum, pe)
    i1 = jnp.full((1, tm), EXPERTS_PER_GROUP - 1, I32)
    for r in range(EXPERTS_PER_GROUP - 2, -1, -1):
        i1 = jnp.where(pe[r] == v1, r, i1)
    pe2 = [jnp.where(i1 == r, -1.0, pe[r]) for r in range(EXPERTS_PER_GROUP)]
    v2 = functools.reduce(jnp.maximum, pe2)
    i2 = jnp.full((1, tm), EXPERTS_PER_GROUP - 1, I32)
    for r in range(EXPERTS_PER_GROUP - 2, -1, -1):
        i2 = jnp.where(pe2[r] == v2, r, i2)
    den = v1 + v2
    e0 = gidx * EXPERTS_PER_GROUP + i1
    e1 = gidx * EXPERTS_PER_GROUP + i2
    e_ref[...] = jnp.concatenate([e0, e1], axis=0)
    w_ref[...] = jnp.concatenate([g_val * (v1 / den), g_val * (v2 / den)], axis=0)
    eio = lax.broadcasted_iota(I32, (N_EXPERTS, tm), 0)
    hit = jnp.where((eio == e0) | (eio == e1), 1.0, 0.0)
    cnt_ref[0] = jnp.sum(hit, axis=1, keepdims=True)


def _outproj(yg, yh, yy, h, w_out, g, b, wr_t, br, alpha, tm=512):
    T, D = h.shape
    nr = wr_t.shape[0]
    wr_hi = wr_t.astype(BF16)
    wr_lo = (wr_t - wr_hi.astype(F32)).astype(BF16)
    full = lambda shp: pl.BlockSpec(shp, lambda i: (0,) * len(shp))
    return pl.pallas_call(
        functools.partial(_outproj_kernel, alpha=alpha),
        grid=(T // tm,),
        in_specs=[pl.BlockSpec((GLA_W, tm), lambda i: (0, i)),
                  pl.BlockSpec((HG_W, tm), lambda i: (0, i)),
                  pl.BlockSpec((HY_WIDTH, tm), lambda i: (0, i)),
                  pl.BlockSpec((tm, D), lambda i: (i, 0)),
                  full((D, D)), full((1, D)), full((1, D)), full((nr, D)), full((nr, D)), full((nr, 1))],
        out_specs=[pl.BlockSpec((tm, D), lambda i: (i, 0)),
                   pl.BlockSpec((TOP_K, tm), lambda i: (0, i)),
                   pl.BlockSpec((TOP_K, tm), lambda i: (0, i)),
                   pl.BlockSpec((1, N_EXPERTS, 1), lambda i: (i, 0, 0))],
        out_shape=[jax.ShapeDtypeStruct((T, D), F32),
                   jax.ShapeDtypeStruct((TOP_K, T), I32),
                   jax.ShapeDtypeStruct((TOP_K, T), F32),
                   jax.ShapeDtypeStruct((T // tm, N_EXPERTS, 1), F32)],
        compiler_params=_cparams(("arbitrary",)),
        name="outproj",
    )(yg, yh, yy, h, w_out.astype(BF16), g.reshape(1, D), b.reshape(1, D), wr_hi, wr_lo, br)


def _chunk_loop(n, fn):
    def body(c, carry):
        fn(pl.multiple_of(c * ROW_CHUNK, ROW_CHUNK))
        return carry
    lax.fori_loop(0, n, body, 0)


def _segment_copies(nch_s, k, fn):
    off = 0
    for si, rows in enumerate(SEG_SIZES):
        n = nch_s[k * len(SEG_SIZES) + si]

        def body(c, carry, off=off, rows=rows):
            fn(rows, pl.multiple_of(off + c * rows, ROW_CHUNK))
            return carry
        lax.fori_loop(0, n, body, 0)
        off = off + n * rows


def _wait_tile_copies(nch_s, tile, wait_one):
    for si, rows in enumerate(SEG_SIZES):
        tot = functools.reduce(lambda a, b: a + b, [nch_s[(tile * N_EXPERTS + e) * len(SEG_SIZES) + si]
                                                    for e in range(N_EXPERTS)])
        lax.fori_loop(0, tot, lambda c, carry, rows=rows: (wait_one(rows), carry)[1], 0)


def _dispatch_kernel(offs_s, gst_s, nch_s, tst_s, tn_s, nu_s, h_ref, e_ref, base_ref, tri_ref,
                     pos_ref, xg_hbm, xs_ref, zbuf, sem, zsem, *, tm, LP, bm, nblocks):
    i = pl.program_id(0)
    nt = pl.num_programs(0)
    slot = i % 2

    def seg_copy(s, rows, src_row, dst_row):
        return pltpu.make_async_copy(xs_ref.at[s, pl.ds(src_row, rows)],
                                     xg_hbm.at[pl.ds(dst_row, rows)], sem.at[s])

    def zero_copy(dst_row):
        return pltpu.make_async_copy(zbuf.at[pl.ds(0, ROW_CHUNK)], xg_hbm.at[pl.ds(dst_row, ROW_CHUNK)], zsem.at[0])

    def zero_block(blk):
        return pltpu.make_async_copy(zbuf, xg_hbm.at[pl.ds(pl.multiple_of(blk * bm, bm), bm)], zsem.at[0])

    def wait_tile(tile, s):
        _wait_tile_copies(nch_s, tile, lambda rows: seg_copy(s, rows, 0, 0).wait())

    @pl.when(i == 0)
    def _():
        zbuf[...] = jnp.zeros_like(zbuf)
        for e in range(N_EXPERTS):
            _chunk_loop(tn_s[e], lambda off, e=e: zero_copy(pl.multiple_of(tst_s[e] + off, ROW_CHUNK)).start())
        lax.fori_loop(nu_s[0], nblocks, lambda blk, c: (zero_block(blk).start(), c)[1], 0)
        for e in range(N_EXPERTS):
            _chunk_loop(tn_s[e], lambda off: zero_copy(0).wait())
        lax.fori_loop(nu_s[0], nblocks, lambda blk, c: (zero_block(0).wait(), c)[1], 0)

    e0, e1 = e_ref[0:1, :], e_ref[1:2, :]
    eio = lax.broadcasted_iota(I32, (N_EXPERTS, tm), 0)
    oh0, oh1 = eio == e0, eio == e1
    hit = jnp.where(oh0 | oh1, 1.0, 0.0).astype(BF16)
    posm = base_ref[0] + _dot(hit, tri_ref[...])
    pos0 = jnp.sum(jnp.where(oh0, posm, 0.0), axis=0, keepdims=True).astype(I32)
    pos1 = jnp.sum(jnp.where(oh1, posm, 0.0), axis=0, keepdims=True).astype(I32)
    pos_ref[...] = jnp.concatenate([pos0, pos1], axis=0)
    rio = lax.broadcasted_iota(I32, (LP, tm), 0)
    perm = jnp.where((rio == pos0) | (rio == pos1), 1.0, 0.0).astype(BF16)
    xs_ref[slot] = _dot(perm, h_ref[...].astype(BF16)).astype(BF16)

    @pl.when(i > 0)
    def _():
        wait_tile(i - 1, 1 - slot)

    for e in range(N_EXPERTS):
        k = i * N_EXPERTS + e
        src0, dst0 = offs_s[k], gst_s[k]
        _segment_copies(nch_s, k, lambda rows, off, src0=src0, dst0=dst0: seg_copy(
            slot, rows, pl.multiple_of(src0 + off, ROW_CHUNK), pl.multiple_of(dst0 + off, ROW_CHUNK)).start())

    @pl.when(i == nt - 1)
    def _():
        wait_tile(i, slot)


def _moe_dispatch(h1, e_kt, tables, nblocks, bm, tm):
    T, D = h1.shape
    NT = T // tm
    nrows = nblocks * bm
    LP = TOP_K * tm + N_EXPERTS * ROW_CHUNK
    r = np.arange(tm)
    tri = jnp.asarray(r[:, None] < r[None, :], F32).astype(BF16)
    base = tables["offs"].astype(F32).reshape(NT, N_EXPERTS, 1)
    flat = lambda a: a.reshape(-1).astype(I32)
    pos, xg = pl.pallas_call(
        functools.partial(_dispatch_kernel, tm=tm, LP=LP, bm=bm, nblocks=nblocks),
        grid_spec=pltpu.PrefetchScalarGridSpec(
            num_scalar_prefetch=6,
            grid=(NT,),
            in_specs=[pl.BlockSpec((tm, D), lambda i, *_: (i, 0)),
                      pl.BlockSpec((TOP_K, tm), lambda i, *_: (0, i)),
                      pl.BlockSpec((1, N_EXPERTS, 1), lambda i, *_: (i, 0, 0)),
                      pl.BlockSpec((tm, tm), lambda i, *_: (0, 0))],
            out_specs=[pl.BlockSpec((TOP_K, tm), lambda i, *_: (0, i)),
                       pl.BlockSpec(memory_space=pl.ANY)],
            scratch_shapes=[pltpu.VMEM((2, LP, D), BF16), pltpu.VMEM((bm, D), BF16),
                            pltpu.SemaphoreType.DMA((2,)), pltpu.SemaphoreType.DMA((1,))]),
        out_shape=[jax.ShapeDtypeStruct((TOP_K, T), I32), jax.ShapeDtypeStruct((nrows, D), BF16)],
        compiler_params=_cparams(("arbitrary",)),
        name="moe_dispatch",
    )(flat(tables["offs"]), flat(tables["gstart"]), flat(tables["nch"]), flat(tables["tail_start"]),
      flat(tables["tail_n"]), tables["nused"], h1, e_kt, base, tri)
    return pos, xg


def _ffn_kernel(be_ref, nu_ref, x_ref, wg_ref, wu_ref, wd_ref, y_ref, wgb, wub, wdb):
    j = pl.program_id(0)
    used = j < nu_ref[0]

    @pl.when((j == 0) | (be_ref[j] != be_ref[jnp.maximum(j - 1, 0)]))
    def _():
        wgb[...] = wg_ref[0].astype(BF16)
        wub[...] = wu_ref[0].astype(BF16)
        wdb[...] = wd_ref[0].astype(BF16)

    @pl.when(used)
    def _():
        x = x_ref[...]
        a = _dot(x, wgb[...])
        hid = (a * _sigmoid(a)) * _dot(x, wub[...])
        y_ref[...] = _dot(hid.astype(BF16), wdb[...]).astype(BF16)

    @pl.when(jnp.logical_not(used))
    def _():
        y_ref[...] = jnp.zeros_like(y_ref)


def _moe_ffn(xg, block_e, nused, wg, wu, wd, bm, first_expert=0):
    nrows, D = xg.shape
    NB = nrows // bm
    DE = wg.shape[-1]
    row = lambda j, be, nu: (jnp.minimum(j, nu[0] - 1), 0)
    wsel = lambda j, be, nu: (first_expert + be[j], 0, 0)
    return pl.pallas_call(
        _ffn_kernel,
        grid_spec=pltpu.PrefetchScalarGridSpec(
            num_scalar_prefetch=2,
            grid=(NB,),
            in_specs=[pl.BlockSpec((bm, D), row),
                      pl.BlockSpec((1, D, DE), wsel),
                      pl.BlockSpec((1, D, DE), wsel),
                      pl.BlockSpec((1, DE, D), wsel)],
            out_specs=pl.BlockSpec((bm, D), lambda j, be, nu: (j, 0)),
            scratch_shapes=[pltpu.VMEM((D, DE), BF16), pltpu.VMEM((D, DE), BF16), pltpu.VMEM((DE, D), BF16)]),
        out_shape=jax.ShapeDtypeStruct((nrows, D), BF16),
        compiler_params=_cparams(("arbitrary",)),
        name="moe_ffn",
    )(block_e, nused, xg, wg, wu, wd)


def _combine_kernel(offs_s, gst_s, nch_s, yb_hbm, pos_ref, w_ref, h_ref, g_ref, b_ref, o_ref, ybl, sem,
                    *, tm, LP, alpha):
    i = pl.program_id(0)
    nt = pl.num_programs(0)
    slot = i % 2

    def seg_copy(s, rows, src_row, dst_row):
        return pltpu.make_async_copy(yb_hbm.at[pl.ds(src_row, rows)],
                                     ybl.at[s, pl.ds(dst_row, rows)], sem.at[s])

    def issue(tile, s):
        for e in range(N_EXPERTS):
            k = tile * N_EXPERTS + e
            src0, dst0 = gst_s[k], offs_s[k]
            _segment_copies(nch_s, k, lambda rows, off, src0=src0, dst0=dst0: seg_copy(
                s, rows, pl.multiple_of(src0 + off, ROW_CHUNK), pl.multiple_of(dst0 + off, ROW_CHUNK)).start())

    @pl.when(i == 0)
    def _():
        ybl[...] = jnp.zeros_like(ybl)
        issue(0, 0)

    @pl.when(i + 1 < nt)
    def _():
        issue(i + 1, 1 - slot)

    _wait_tile_copies(nch_s, i, lambda rows: seg_copy(slot, rows, 0, 0).wait())
    pos0, pos1 = pos_ref[0:1, :], pos_ref[1:2, :]
    w0, w1 = w_ref[0:1, :], w_ref[1:2, :]
    rio = lax.broadcasted_iota(I32, (LP, tm), 0)
    pw = (jnp.where(rio == pos0, w0, 0.0) + jnp.where(rio == pos1, w1, 0.0)).astype(BF16)
    ffn = _dot(pw, ybl[slot], _TN)
    o_ref[...] = _layer_norm(alpha * h_ref[...] + ffn, g_ref[...], b_ref[...])


def _moe_combine(yb, h1, pos, w_kt, tables, g, b, alpha, tm):
    T, D = h1.shape
    LP = TOP_K * tm + N_EXPERTS * ROW_CHUNK
    flat = lambda a: a.reshape(-1).astype(I32)
    return pl.pallas_call(
        functools.partial(_combine_kernel, tm=tm, LP=LP, alpha=alpha),
        grid_spec=pltpu.PrefetchScalarGridSpec(
            num_scalar_prefetch=3,
            grid=(T // tm,),
            in_specs=[pl.BlockSpec(memory_space=pl.ANY),
                      pl.BlockSpec((TOP_K, tm), lambda i, *_: (0, i)),
                      pl.BlockSpec((TOP_K, tm), lambda i, *_: (0, i)),
                      pl.BlockSpec((tm, D), lambda i, *_: (i, 0)),
                      pl.BlockSpec((1, D), lambda i, *_: (0, 0)),
                      pl.BlockSpec((1, D), lambda i, *_: (0, 0))],
            out_specs=pl.BlockSpec((tm, D), lambda i, *_: (i, 0)),
            scratch_shapes=[pltpu.VMEM((2, LP, D), BF16), pltpu.SemaphoreType.DMA((2,))]),
        out_shape=jax.ShapeDtypeStruct((T, D), F32),
        compiler_params=_cparams(("arbitrary",)),
        name="moe_combine",
    )(flat(tables["offs"]), flat(tables["gstart"]), flat(tables["nch"]), yb, pos, w_kt, h1,
      g.reshape(1, D), b.reshape(1, D))


def _dispatch_tables(cnt, bm, nblocks):
    padlen = ((cnt + ROW_CHUNK - 1) // ROW_CHUNK) * ROW_CHUNK
    offs = jnp.cumsum(padlen, axis=1) - padlen
    tot = jnp.sum(padlen, axis=0)
    region = ((tot + bm - 1) // bm) * bm
    rend = jnp.cumsum(region)
    rstart = rend - region
    gstart = rstart[None, :] + jnp.cumsum(padlen, axis=0) - padlen
    blk_row = jnp.arange(nblocks, dtype=I32)[:, None] * bm
    block_e = jnp.minimum(jnp.sum((blk_row >= rend[None, :]).astype(I32), axis=1), N_EXPERTS - 1)
    nch, rest = [], padlen
    for rows in SEG_SIZES:
        nch.append(rest // rows)
        rest = rest % rows
    return dict(offs=offs, gstart=gstart, nch=jnp.stack(nch, axis=-1), tail_start=rstart + tot,
                tail_n=(region - tot) // ROW_CHUNK, block_e=block_e.astype(I32),
                nused=(rend[-1:] // bm).astype(I32))


def _permute_in_columns(w):
    splits = (192, 192, 384, 384, 32, 384, 768, 384, 384, 768)
    offs = np.concatenate([[0], np.cumsum(splits)])
    gq, gk, gv, gg, ga, hq, hf, hi, hgt, hyu = [(int(offs[i]), int(offs[i + 1])) for i in range(10)]
    order = [hq, hi, hgt, hf, gv, gg, gq, gk, hyu, ga]
    assert sum(b - a for a, b in order) == D_IN
    return jnp.concatenate([w[:, a:b] for a, b in order], axis=1)


def kernel(x, ln_in_g, ln_in_b, w_in, gla_wa2, gla_ba, gla_norm_g, hg_lb_logits, hg_norm_g, hy_conv_w, hy_conv_b, hy_w1, hy_b1, hy_freq, hy_w2, hy_b2, hy_w3, hy_bias, w_out, ln1_g, ln1_b, moe_wr_g, moe_br_g, moe_wr_e, moe_br_e, moe_w_gate, moe_w_up, moe_w_down, ln2_g, ln2_b):
    B, L, D = x.shape
    T = B * L
    depth = w_in.shape[0]
    alpha = (2 * depth) ** 0.25
    bm = 512
    tmr = 512
    nblocks = -(-(T * TOP_K + (T // tmr) * N_EXPERTS * (ROW_CHUNK - 1) + N_EXPERTS * (bm - 1)) // bm)
    consts = _dft_consts(L)

    p = jax.nn.softmax(hg_lb_logits.astype(F32), axis=0)
    lbs = jnp.cumsum(p, axis=0) - p[0:1]
    lbc = jnp.concatenate([jnp.broadcast_to((1.0 - lbs)[..., None], lbs.shape + (LANES,)),
                           jnp.broadcast_to(jnp.maximum(lbs, LB_FLOOR)[..., None], lbs.shape + (LANES,))], axis=-1)

    wg_all = moe_w_gate.reshape((depth * N_EXPERTS,) + moe_w_gate.shape[2:])
    wu_all = moe_w_up.reshape((depth * N_EXPERTS,) + moe_w_up.shape[2:])
    wd_all = moe_w_down.reshape((depth * N_EXPERTS,) + moe_w_down.shape[2:])
    h = x.reshape(T, D)
    for l in range(depth):
        w_t = _permute_in_columns(w_in[l]).T.astype(BF16)
        proj_t, h = _inproj(h, ln_in_g, ln_in_b, w_t, apply_ln=(l == 0))
        wa_t = jnp.swapaxes(gla_wa2[l], 1, 2)
        y_gla = _scan(proj_t, "gla", B, L, (wa_t, gla_ba[l].reshape(2, GLA_K, 1)), gla_norm_g[l].reshape(GLA_W, 1))
        y_hg = _scan(proj_t, "hg", B, L, (lbc[l],), hg_norm_g[l].reshape(HG_W, 1))
        kt = _hyena_filters(L, hy_w1[l], hy_b1[l], hy_freq[l], hy_w2[l], hy_b2[l], hy_w3[l])
        kspec = _hyena_spectrum(kt, consts, L)
        y_hy = _hyena(proj_t, kspec, consts, hy_conv_w[l], hy_conv_b[l], hy_bias[l], B, L)
        nr = N_GROUPS + N_EXPERTS
        nrp = -(-nr // 8) * 8
        wr_t = jnp.zeros((nrp, D), F32).at[:nr].set(jnp.concatenate([moe_wr_g[l], moe_wr_e[l]], axis=1).T)
        br = jnp.zeros((nrp, 1), F32).at[:nr, 0].set(jnp.concatenate([moe_br_g[l], moe_br_e[l]]))
        h1, e_kt, w_kt, cnt = _outproj(y_gla, y_hg, y_hy, h, w_out[l], ln1_g[l], ln1_b[l], wr_t, br, alpha, tm=tmr)
        tables = _dispatch_tables(cnt.reshape(T // tmr, N_EXPERTS).astype(I32), bm, nblocks)
        pos, xg = _moe_dispatch(h1, e_kt, tables, nblocks, bm, tmr)
        yb = _moe_ffn(xg, tables["block_e"], tables["nused"], wg_all, wu_all, wd_all, bm, first_expert=l * N_EXPERTS)
        h = _moe_combine(yb, h1, pos, w_kt, tables, ln2_g[l], ln2_b[l], alpha, tmr)
    return h.reshape(B, L, D)
```

```python
import functools
import math

import numpy as np
import jax
import jax.numpy as jnp
from jax import lax
from jax.experimental import pallas as pl
from jax.experimental.pallas import tpu as pltpu

F32 = jnp.float32
BF16 = jnp.bfloat16
I32 = jnp.int32

GLA_HEADS, GLA_DK, GLA_DV, GLA_LOWRANK, GLA_TAU = 6, 32, 64, 16, 16.0
HG_HEADS, HG_DK, HG_DV = 6, 64, 64
HY_WIDTH, HY_ORDER, HY_EMB, HY_FFN, HY_INNER = 256, 2, 33, 64, 2
HY_FAST_DECAY, HY_SLOW_DECAY, HY_TARGET = 0.3, 1.5, 1e-2
N_GROUPS, EXPERTS_PER_GROUP = 4, 4
N_EXPERTS = N_GROUPS * EXPERTS_PER_GROUP
TOP_K = 2
LN_EPS, RMS_EPS, LB_FLOOR = 1e-5, 1e-6, 1e-30

LANES = 128
SCAN_CHUNK = LANES
ROW_CHUNK = 16
SEG_SIZES = (64, 32, ROW_CHUNK)
VMEM_LIMIT = 56 * 1024 * 1024

GLA_W = GLA_HEADS * GLA_DV
GLA_K = GLA_HEADS * GLA_DK
HG_W = HG_HEADS * HG_DV
HG_K = HG_HEADS * HG_DK
OFF_HQ, OFF_HI, OFF_HGT, OFF_HF = 0, 384, 768, 1152
OFF_GV, OFF_GG, OFF_GQ, OFF_GK, OFF_HY, OFF_GA = 1920, 2304, 2688, 2880, 3072, 3840
D_IN = 3872


def _dot(a, b, dims=(((1,), (0,)), ((), ())), precision=None):
    return lax.dot_general(a, b, dims, preferred_element_type=F32, precision=precision)


_NT = (((1,), (1,)), ((), ()))
_TN = (((0,), (0,)), ((), ()))


def _layer_norm(x, g, b):
    mu = jnp.mean(x, axis=-1, keepdims=True)
    xc = x - mu
    var = jnp.mean(xc * xc, axis=-1, keepdims=True)
    return xc * lax.rsqrt(var + LN_EPS) * g + b


def _log_sigmoid(x):
    return jnp.minimum(x, 0.0) - jnp.log(1.0 + jnp.exp(-jnp.abs(x)))


def _sigmoid(x):
    return 0.5 + 0.5 * jnp.tanh(0.5 * x)


def _cparams(sem):
    return pltpu.CompilerParams(dimension_semantics=sem, vmem_limit_bytes=VMEM_LIMIT)


def _inproj_kernel(x_ref, g_ref, b_ref, w_ref, *outs, apply_ln):
    x = x_ref[...]
    if apply_ln:
        x = _layer_norm(x, g_ref[...], b_ref[...])
        outs[1][...] = x
    outs[0][...] = _dot(w_ref[...], x.astype(BF16), _NT)


def _inproj(x, g, b, w_t, apply_ln, tm=512):
    T, D = x.shape
    n_out = w_t.shape[0]
    out_shape = [jax.ShapeDtypeStruct((n_out, T), F32)]
    out_specs = [pl.BlockSpec((n_out, tm), lambda i: (0, i))]
    if apply_ln:
        out_shape.append(jax.ShapeDtypeStruct((T, D), F32))
        out_specs.append(pl.BlockSpec((tm, D), lambda i: (i, 0)))
    res = pl.pallas_call(
        functools.partial(_inproj_kernel, apply_ln=apply_ln),
        grid=(T // tm,),
        in_specs=[pl.BlockSpec((tm, D), lambda i: (i, 0)),
                  pl.BlockSpec((1, D), lambda i: (0, 0)),
                  pl.BlockSpec((1, D), lambda i: (0, 0)),
                  pl.BlockSpec((n_out, D), lambda i: (0, 0))],
        out_specs=out_specs,
        out_shape=out_shape,
        compiler_params=_cparams(("arbitrary",)),
        name="inproj",
    )(x, g.reshape(1, D), b.reshape(1, D), w_t)
    return res if apply_ln else (res[0], x)


def _scan_kernel(*refs, mode, H, K, V, TB, NCB):
    if mode == "gla":
        (qf_ref, qb_ref, kf_ref, kb_ref, vf_ref, vb_ref, gaf_ref, gab_ref, wa_ref, ba_ref,
         gtf_ref, gtb_ref, ng_ref, y_ref, s_ref, oacc_ref, oi_ref, qg_ref, ut_ref, dr_ref) = refs
        q_refs, k_refs, ga_refs = (qf_ref, qb_ref), (kf_ref, kb_ref), (gaf_ref, gab_ref)
    else:
        (qf_ref, qb_ref, zf_ref, zb_ref, vf_ref, vb_ref, lbc_ref,
         gtf_ref, gtb_ref, ng_ref, y_ref, s_ref, oacc_ref, oi_ref, qg_ref, ut_ref, dr_ref) = refs
        q_refs, z_refs = (qf_ref, qb_ref), (zf_ref, zb_ref)
    v_refs, gate_refs = (vf_ref, vb_ref), (gtf_ref, gtb_ref)
    C = SCAN_CHUNK
    half = C // 2
    nchunks = TB // C
    n = pl.program_id(1)
    blocks = (n, NCB - 1 - n)

    @pl.when(n == 0)
    def _():
        s_ref[...] = jnp.zeros_like(s_ref)

    def gates(d, sl):
        if mode == "gla":
            a = _dot(wa_ref[d], ga_refs[d][:, sl], precision=lax.Precision.HIGHEST) + ba_ref[d]
            g = _log_sigmoid(a) * (1.0 / GLA_TAU)
            q = q_refs[d][:, sl] * (K ** -0.5)
            k = k_refs[d][:, sl]
        else:
            one_m_lb, lb_floor = lbc_ref[d, :, 0:C], lbc_ref[d, :, C:2 * C]
            s = one_m_lb * _sigmoid(z_refs[d][:, sl])
            g = jnp.log(s + lb_floor)
            k = one_m_lb - s
            hq = q_refs[d][:, sl]
            q = hq * _sigmoid(hq)
        return q, k, g

    r = lax.broadcasted_iota(I32, (C, C), 0)
    c = lax.broadcasted_iota(I32, (C, C), 1)
    same = (r >= half) == (c >= half)
    lane_lo = lax.broadcasted_iota(I32, (1, C), 1) < half
    sign_lo = jnp.where(lane_lo, 1.0, -1.0)
    consts = (((r <= c).astype(BF16), (r < half) & (c >= half), same & (r <= c),
               (C - 1, half, half // 2, half + half // 2)),
              ((r >= c).astype(BF16), (r >= half) & (c < half), same & (r >= c),
               (0, half - 1, half // 2 - 1, half + half // 2 - 1)))

    heads = [(slice(h * K, (h + 1) * K), slice(h * V, (h + 1) * V)) for h in range(H)]

    def cumulate(d, g):
        cum = consts[d][0]
        g1 = g.astype(BF16)
        g2 = (g - g1.astype(F32)).astype(BF16)
        G = _dot(g1, cum) + _dot(g2, cum)
        ge_row = _dot(ones8, g1, _NT) + _dot(ones8, g2, _NT)
        return G, ge_row

    def scale(d, sl, q, k, G, ge_row):
        _, _, _, (c_end, c_mid, c_a, c_b) = consts[d]
        v = v_refs[d][:, sl].astype(BF16)
        g_end = G[:, c_end:c_end + 1]
        g_mid = G[:, c_mid:c_mid + 1]
        e2 = G - jnp.where(lane_lo, G[:, c_a:c_a + 1], G[:, c_b:c_b + 1])
        q2 = (q * jnp.exp(e2)).astype(BF16)
        k2 = (k * jnp.exp(-e2)).astype(BF16)
        dm = G - g_mid
        x1 = jnp.exp(dm * (sign_lo if d else -sign_lo))
        q1 = (q * x1).astype(BF16)
        k1 = (k * x1).astype(BF16)
        qg = (q * jnp.exp(G)).astype(BF16)
        kd = (k * jnp.exp(g_end - G)).astype(BF16)
        return dict(q1=q1, k1=k1, q2=q2, k2=k2, qg=qg, kd=kd, v=v, dec_row=jnp.exp(ge_row))

    ones8 = jnp.ones((8, C), BF16)

    def scores(x):
        return [(_dot(x["k1"][rk], x["q1"][rk], _TN), _dot(x["k2"][rk], x["q2"][rk], _TN)) for rk, _ in heads]

    def masked(d, ps):
        _, off1, diag, _ = consts[d]
        return [jnp.where(off1, p1, jnp.where(diag, p2, 0.0)).astype(BF16) for p1, p2 in ps]

    def park(d, ci, x, pm):
        oi_ref[d, ci] = jnp.concatenate([_dot(x["v"][rv], pm[h]) for h, (_, rv) in enumerate(heads)], axis=0)
        qg_ref[d, ci] = x["qg"]
        dr_ref[d, ci] = x["dec_row"]
        for h, (rk, rv) in enumerate(heads):
            ut_ref[d, ci, h] = _dot(x["v"][rv], x["kd"][rk], _NT)

    def chunk_slices(ci):
        offs = (pl.multiple_of(ci * C, C), pl.multiple_of((nchunks - 1 - ci) * C, C))
        return offs, [pl.ds(offs[d], C) for d in range(2)]

    def independent(it, carry):
        jobs = [(d, U * it + u, chunk_slices(U * it + u)[1][d]) for u in range(U) for d in range(2)]
        qkg = [gates(d, sl) for d, _, sl in jobs]
        cums = [cumulate(d, qkg[i][2]) for i, (d, _, _) in enumerate(jobs)]
        xs = [scale(d, sl, qkg[i][0], qkg[i][1], *cums[i]) for i, (d, _, sl) in enumerate(jobs)]
        scs = [scores(x) for x in xs]
        pms = [masked(d, scs[i]) for i, (d, _, _) in enumerate(jobs)]
        for i, (d, ci, _) in enumerate(jobs):
            park(d, ci, xs[i], pms[i])
        return carry

    U = 4
    assert nchunks % U == 0
    lax.fori_loop(0, nchunks // U, independent, 0)

    def carried(ci):
        st = [[s_ref[d, h] for h in range(H)] for d in range(2)]
        inter = [[_dot(st[d][h].astype(BF16), qg_ref[d, ci, rk, :]) for h, (rk, _) in enumerate(heads)]
                 for d in range(2)]
        for d in range(2):
            for h, (rk, _) in enumerate(heads):
                s_ref[d, h] = st[d][h] * dr_ref[d, ci, 0:1, rk] + ut_ref[d, ci, h]
        return [oi_ref[d, ci] + jnp.concatenate(inter[d], axis=0) for d in range(2)]

    def finish(o, gate):
        act = gate * _sigmoid(gate) if mode == "gla" else _sigmoid(gate)
        ys = []
        for h in range(H):
            oh = o[h * V:(h + 1) * V]
            ms = jnp.mean(oh * oh, axis=0, keepdims=True)
            ys.append(oh * lax.rsqrt(ms + RMS_EPS))
        return (jnp.concatenate(ys, axis=0) * ng_ref[...] * act).astype(y_ref.dtype)

    def chunk(ci, carry):
        offs, sls = chunk_slices(ci)
        tsls = [pl.ds(pl.multiple_of(blocks[d] * TB + offs[d], C), C) for d in range(2)]
        o = carried(ci)

        @pl.when(2 * n < NCB)
        def _():
            for d in range(2):
                oacc_ref[:, tsls[d]] = o[d]

        @pl.when(2 * n >= NCB)
        def _():
            tot = [o[d] + oacc_ref[:, tsls[d]] for d in range(2)]
            for d in range(2):
                y_ref[:, tsls[d]] = finish(tot[d], gate_refs[d][:, sls[d]])
        return carry

    lax.fori_loop(0, nchunks, chunk, 0)


def _scan(proj_t, mode, B, L, extra, norm_g, TB=512):
    NCB = L // TB
    nch = TB // SCAN_CHUNK
    assert NCB % 2 == 0
    if mode == "gla":
        H, K, V = GLA_HEADS, GLA_DK, GLA_DV
    else:
        H, K, V = HG_HEADS, HG_DK, HG_DV
    HK, HV = H * K, H * V
    cf = lambda b, n: b * NCB + n
    cb = lambda b, n: b * NCB + NCB - 1 - n
    gf = lambda b, n: b * NCB + jnp.maximum(n, NCB // 2)
    gb = lambda b, n: b * NCB + jnp.minimum(NCB - 1 - n, NCB // 2 - 1)

    def pair(rows, off, fwd=cf, bwd=cb, dir_step=0):
        return [pl.BlockSpec((rows, TB), lambda b, n: (off // rows, fwd(b, n))),
                pl.BlockSpec((rows, TB), lambda b, n: (off // rows + dir_step, bwd(b, n)))]

    whole = lambda shp: pl.BlockSpec(shp, lambda b, n: (0,) * len(shp))
    if mode == "gla":
        wa_t, ba = extra
        in_specs = (pair(HK, OFF_GQ) + pair(HK, OFF_GK) + pair(HV, OFF_GV)
                    + pair(GLA_LOWRANK, OFF_GA, dir_step=1)
                    + [whole((2, HK, GLA_LOWRANK)), whole((2, HK, 1))]
                    + pair(HV, OFF_GG, gf, gb) + [whole((HV, 1))])
        args = (proj_t,) * 8 + (wa_t, ba, proj_t, proj_t, norm_g)
    else:
        (lbc,) = extra
        in_specs = (pair(HK, OFF_HQ) + pair(HK, OFF_HF, dir_step=1) + pair(HV, OFF_HI)
                    + [whole((2, HK, 2 * LANES))] + pair(HV, OFF_HGT, gf, gb) + [whole((HV, 1))])
        args = (proj_t,) * 6 + (lbc, proj_t, proj_t, norm_g)
    return pl.pallas_call(
        functools.partial(_scan_kernel, mode=mode, H=H, K=K, V=V, TB=TB, NCB=NCB),
        grid=(B, NCB),
        in_specs=in_specs,
        out_specs=pl.BlockSpec((HV, L), lambda b, n: (0, b)),
        out_shape=jax.ShapeDtypeStruct((HV, B * L), BF16),
        scratch_shapes=[pltpu.VMEM((2, H, V, K), F32), pltpu.VMEM((HV, L), F32),
                        pltpu.VMEM((2, nch, HV, SCAN_CHUNK), F32), pltpu.VMEM((2, nch, HK, SCAN_CHUNK), BF16),
                        pltpu.VMEM((2, nch, H, V, K), F32), pltpu.VMEM((2, nch, 8, HK), F32)],
        compiler_params=_cparams(("arbitrary", "arbitrary")),
        name="scan_" + mode,
    )(*args)


def _dft_consts(L):
    N = 2 * L
    NA = N // LANES
    a = np.arange(NA)[:, None] * np.arange(NA)[None, :]
    ca, sa = np.cos(2 * np.pi * a / NA), np.sin(2 * np.pi * a / NA)
    hh = NA // 2
    w1d = np.block([[ca[:, :hh], sa[:, :hh]], [-sa[:, :hh], ca[:, :hh]]])
    w1f = np.concatenate([ca, -sa], axis=0)
    w1i = np.block([[ca[:hh, :], -sa[:hh, :]], [sa[:hh, :], ca[:hh, :]]])
    bb = np.arange(LANES)[:, None] * np.arange(LANES)[None, :]
    cb, sb = np.cos(2 * np.pi * bb / LANES), np.sin(2 * np.pi * bb / LANES)
    w2 = np.block([[cb, -sb], [sb, cb]])
    w2i = np.block([[cb, sb], [-sb, cb]])
    tw = np.arange(NA)[:, None] * np.arange(LANES)[None, :]
    tc, ts = np.cos(2 * np.pi * tw / N), np.sin(2 * np.pi * tw / N)
    bf = lambda m: jnp.asarray(m, dtype=F32).astype(BF16)
    return dict(w1d=bf(w1d), w1f=bf(w1f), w1i=bf(w1i), w2=bf(w2), w2i=bf(w2i),
                tc=jnp.asarray(tc, F32), ts=jnp.asarray(ts, F32))


def _pos_features(L):
    t = np.linspace(0.0, 1.0, L)
    w = 2.0 * np.pi * np.arange(L) / L
    bands = np.linspace(1e-4, (HY_EMB - 1) // 2 - 1, (HY_EMB - 1) // 2)
    z = np.concatenate([t[None, :], np.cos(bands[:, None] * w[None, :]), -np.sin(bands[:, None] * w[None, :])], axis=0)
    kp = -(-HY_EMB // 8) * 8
    z = np.concatenate([z, np.zeros((kp - HY_EMB, L))], axis=0)
    idx = (L - np.arange(L)) % L
    z_rev, t_rev = z[:, idx], t[idx]
    mask = (np.arange(L) >= 1).astype(np.float64)
    return (jnp.asarray(z, F32), jnp.asarray(z_rev, F32), jnp.asarray(t[None, :], F32),
            jnp.asarray(t_rev[None, :], F32), jnp.asarray(mask[None, :], F32))


def _filter_kernel(z_ref, zr_ref, t_ref, tr_ref, m_ref, w1_ref, b1_ref, fr_ref, w2_ref, b2_ref,
                   w3f_ref, w3b_ref, dl_ref, out_ref, h_ref, *, L):
    first = (pl.program_id(0) == 0) & (pl.program_id(1) == 0)
    hi = lax.Precision.HIGHEST

    @pl.when(first)
    def _():
        fr = fr_ref[...]
        for idx, zz in enumerate((z_ref, zr_ref)):
            h = jnp.sin(fr * (_dot(w1_ref[...], zz[...], precision=hi) + b1_ref[...]))
            for i in range(HY_INNER):
                h = jnp.sin(fr * (_dot(w2_ref[i], h, precision=hi) + b2_ref[i]))
            h_ref[idx] = h.astype(BF16)

    ad = jnp.abs(dl_ref[...])
    kf = _dot(w3f_ref[...].astype(BF16), h_ref[0]) * jnp.exp(-t_ref[...] * ad)
    kb = _dot(w3b_ref[...].astype(BF16), h_ref[1]) * jnp.exp(-tr_ref[...] * ad) * m_ref[...]
    den = jnp.sum(jnp.abs(kf), axis=1, keepdims=True) + jnp.sum(jnp.abs(kb), axis=1, keepdims=True)
    scale = 1.0 / (jnp.maximum(den, 1e-12) * (2.0 * L))
    out_ref[0, :, 0:L] = kf * scale
    out_ref[0, :, L:2 * L] = kb * scale


def _hyena_filters(L, w1, b1, freq, w2, b2, w3, cg=64):
    z, z_rev, t, t_rev, mask = _pos_features(L)
    kp = z.shape[0]
    w1_t = jnp.zeros((HY_FFN, kp), F32).at[:, :HY_EMB].set(w1.T)
    w2_t = jnp.swapaxes(w2, 1, 2)
    w3_t = w3.T
    max_decay = math.log(HY_TARGET) / HY_FAST_DECAY
    min_decay = math.log(HY_TARGET) / HY_SLOW_DECAY
    deltas = jnp.asarray(np.linspace(min_decay, max_decay, HY_WIDTH).reshape(HY_WIDTH, 1), F32)
    ncg = HY_WIDTH // cg
    full = lambda shp: pl.BlockSpec(shp, lambda o, j: (0,) * len(shp))
    return pl.pallas_call(
        functools.partial(_filter_kernel, L=L),
        grid=(HY_ORDER, ncg),
        in_specs=[full((kp, L)), full((kp, L)), full((1, L)), full((1, L)), full((1, L)),
                  full((HY_FFN, kp)), full((HY_FFN, 1)), full((HY_FFN, 1)),
                  full((HY_INNER, HY_FFN, HY_FFN)), full((HY_INNER, HY_FFN, 1)),
                  pl.BlockSpec((cg, HY_FFN), lambda o, j: (o * 2 * ncg + j, 0)),
                  pl.BlockSpec((cg, HY_FFN), lambda o, j: (o * 2 * ncg + ncg + j, 0)),
                  pl.BlockSpec((cg, 1), lambda o, j: (j, 0))],
        out_specs=pl.BlockSpec((1, cg, 2 * L), lambda o, j: (o, j, 0)),
        out_shape=jax.ShapeDtypeStruct((HY_ORDER, HY_WIDTH, 2 * L), F32),
        scratch_shapes=[pltpu.VMEM((2, HY_FFN, L), BF16)],
        compiler_params=_cparams(("arbitrary", "arbitrary")),
        name="hyena_filter",
    )(z, z_rev, t, t_rev, mask, w1_t, b1.reshape(HY_FFN, 1), freq.reshape(HY_FFN, 1), w2_t,
      b2.reshape(HY_INNER, HY_FFN, 1), w3_t, w3_t, deltas)


def _spectrum_kernel(k2_ref, w1_ref, tc_ref, ts_ref, w2_ref, out_ref, k_ref, *, NA, cg):
    k_ref[0] = k2_ref[0].reshape(cg, NA, LANES)
    tc2 = jnp.concatenate([tc_ref[...]] * 2, axis=1)
    ts2 = jnp.concatenate([ts_ref[...]] * 2, axis=1)
    G = 4

    def body(it, carry):
        lhs = []
        for pr in range(G // 2):
            rhs = jnp.concatenate([k_ref[0, G * it + 2 * pr + cc] for cc in range(2)], axis=1)
            a = _dot(w1_ref[...], rhs.astype(BF16))
            a_re, a_im = a[:NA], a[NA:]
            b_re = a_re * tc2 + a_im * ts2
            b_im = a_im * tc2 - a_re * ts2
            lhs += [jnp.concatenate([b_re[:, cc * LANES:(cc + 1) * LANES], b_im[:, cc * LANES:(cc + 1) * LANES]], axis=1)
                    for cc in range(2)]
        x = _dot(jnp.concatenate(lhs, axis=0).astype(BF16), w2_ref[...])
        for i in range(G):
            out_ref[0, G * it + i] = x[i * NA:(i + 1) * NA]
        return carry
    lax.fori_loop(0, cg // G, body, 0)


def _hyena_spectrum(kt, consts, L, cg=32):
    NA = 2 * L // LANES
    full = lambda shp: pl.BlockSpec(shp, lambda o, j: (0,) * len(shp))
    return pl.pallas_call(
        functools.partial(_spectrum_kernel, NA=NA, cg=cg),
        grid=(HY_ORDER, HY_WIDTH // cg),
        in_specs=[pl.BlockSpec((1, cg, 2 * L), lambda o, j: (o, j, 0)),
                  full((2 * NA, NA)), full((NA, LANES)), full((NA, LANES)), full((2 * LANES, 2 * LANES))],
        out_specs=pl.BlockSpec((1, cg, NA, 2 * LANES), lambda o, j: (o, j, 0, 0)),
        out_shape=jax.ShapeDtypeStruct((HY_ORDER, HY_WIDTH, NA, 2 * LANES), F32),
        scratch_shapes=[pltpu.VMEM((1, cg, NA, LANES), F32)],
        compiler_params=_cparams(("arbitrary", "arbitrary")),
        name="hyena_spectrum",
    )(kt, consts["w1f"], consts["tc"], consts["ts"], consts["w2"])


def _hyena_kernel(cw_ref, cb_ref, hb_ref, v2_ref, x12_ref, x22_ref, ks_ref, w1d_ref, w1i_ref, tc_ref, ts_ref,
                  w2_ref, w2i_ref, y2_ref, v_ref, x1_ref, x2_ref, y_ref, *, NA, cg, B):
    hh = NA // 2
    L = hh * LANES
    j = pl.program_id(0)
    for src, dst in ((v2_ref, v_ref), (x12_ref, x1_ref), (x22_ref, x2_ref)):
        for b in range(B):
            dst[:, b] = src[:, b * L:(b + 1) * L].reshape(cg, hh, LANES)
    row = lax.broadcasted_iota(I32, (hh, LANES), 0)
    lane = lax.broadcasted_iota(I32, (hh, LANES), 1)
    first = (row == 0) & (lane == 0)
    last = (row == hh - 1) & (lane == LANES - 1)

    def short_conv(x, ch):
        r1 = pltpu.roll(x, 1, 1)
        prev = jnp.where(lane == 0, pltpu.roll(r1, 1, 0), r1)
        prev = jnp.where(first, 0.0, prev)
        r2 = pltpu.roll(x, LANES - 1, 1)
        nxt = jnp.where(lane == LANES - 1, pltpu.roll(r2, hh - 1, 0), r2)
        nxt = jnp.where(last, 0.0, nxt)
        return cw_ref[0, ch] * prev + cw_ref[1, ch] * x + cw_ref[2, ch] * nxt + cb_ref[ch]

    P = B // 2
    tc, ts = tc_ref[...], ts_ref[...]
    tc2 = jnp.concatenate([tc, tc], axis=1)
    ts2 = jnp.concatenate([ts, ts], axis=1)
    lane2 = lambda x, cc: x[:, cc * LANES:(cc + 1) * LANES]

    def body(it, carry):
        cis = [2 * it + cc for cc in range(2)]
        chs = [j * cg + ci for ci in cis]
        seqs = [(p, cc) for p in range(P) for cc in range(2)]
        z = {(p, cc): [short_conv(v_ref[cis[cc], 2 * p + r], chs[cc]) for r in range(2)] for p, cc in seqs}
        gate_refs = (x1_ref, x2_ref)
        gates = [{(p, cc): [short_conv(gate_refs[o][cis[cc], 2 * p + r], (o + 1) * HY_WIDTH + chs[cc])
                            for r in range(2)] for p, cc in seqs} for o in range(HY_ORDER)]
        for o in range(HY_ORDER):
            lhs = []
            for p in range(P):
                rhs = jnp.concatenate([jnp.concatenate(z[(p, cc)], axis=0) for cc in range(2)], axis=1)
                a = _dot(w1d_ref[...], rhs.astype(BF16))
                a_re, a_im = a[:NA], a[NA:]
                b_re = a_re * tc2 + a_im * ts2
                b_im = a_im * tc2 - a_re * ts2
                lhs += [jnp.concatenate([lane2(b_re, cc), lane2(b_im, cc)], axis=1) for cc in range(2)]
            x = _dot(jnp.concatenate(lhs, axis=0).astype(BF16), w2_ref[...])
            ys = []
            for idx, (p, cc) in enumerate(seqs):
                xb = x[idx * NA:(idx + 1) * NA]
                ks = ks_ref[o, cis[cc]]
                x_re, x_im = lane2(xb, 0), lane2(xb, 1)
                k_re, k_im = lane2(ks, 0), lane2(ks, 1)
                ys.append(jnp.concatenate([x_re * k_re - x_im * k_im, x_re * k_im + x_im * k_re], axis=1))
            bq = _dot(jnp.concatenate(ys, axis=0).astype(BF16), w2i_ref[...])
            for p in range(P):
                cr, cim = [], []
                for cc in range(2):
                    blk = bq[(2 * p + cc) * NA:(2 * p + cc + 1) * NA]
                    b_re, b_im = lane2(blk, 0), lane2(blk, 1)
                    cr.append(b_re * tc - b_im * ts)
                    cim.append(b_re * ts + b_im * tc)
                rhs = jnp.concatenate([jnp.concatenate(cr, axis=1), jnp.concatenate(cim, axis=1)], axis=0)
                conv = _dot(w1i_ref[...], rhs.astype(BF16))
                for cc in range(2):
                    bias = hb_ref[o, chs[cc]]
                    z[(p, cc)] = [gates[o][(p, cc)][r] * (lane2(conv, cc)[r * hh:(r + 1) * hh] + z[(p, cc)][r] * bias)
                                  for r in range(2)]
        for p, cc in seqs:
            for r in range(2):
                y_ref[cis[cc], 2 * p + r] = z[(p, cc)][r]
        return carry

    lax.fori_loop(0, cg // 2, body, 0)
    for b in range(B):
        y2_ref[:, b * L:(b + 1) * L] = y_ref[:, b].reshape(cg, L)


def _hyena(proj_t, kspec, consts, conv_w, conv_b, bias, B, L, cg=8):
    NA = 2 * L // LANES
    hh = NA // 2
    ncg = HY_WIDTH // cg
    base = OFF_HY // cg
    smem = pl.BlockSpec(memory_space=pltpu.SMEM)
    full = lambda shp: pl.BlockSpec(shp, lambda j: (0,) * len(shp))
    blk = lambda off: pl.BlockSpec((cg, B * L), lambda j: (off + j, 0))
    tiles = pltpu.VMEM((cg, B, hh, LANES), F32)
    return pl.pallas_call(
        functools.partial(_hyena_kernel, NA=NA, cg=cg, B=B),
        grid=(ncg,),
        in_specs=[smem, smem, smem, blk(base), blk(base + ncg), blk(base + 2 * ncg),
                  pl.BlockSpec((HY_ORDER, cg, NA, 2 * LANES), lambda j: (0, j, 0, 0)),
                  full((2 * NA, NA)), full((NA, 2 * NA)), full((NA, LANES)), full((NA, LANES)),
                  full((2 * LANES, 2 * LANES)), full((2 * LANES, 2 * LANES))],
        out_specs=pl.BlockSpec((cg, B * L), lambda j: (j, 0)),
        out_shape=jax.ShapeDtypeStruct((HY_WIDTH, B * L), F32),
        scratch_shapes=[tiles, tiles, tiles, tiles],
        compiler_params=_cparams(("arbitrary",)),
        name="hyena_conv",
    )(conv_w, conv_b, bias, proj_t, proj_t, proj_t, kspec, consts["w1d"], consts["w1i"], consts["tc"], consts["ts"],
      consts["w2"], consts["w2i"])


def _outproj_kernel(yg_ref, yh_ref, yy_ref, h_ref, wo_ref, g_ref, b_ref, wrh_ref, wrl_ref, br_ref,
                    h1_ref, e_ref, w_ref, cnt_ref, *, alpha):
    mix = _dot(yg_ref[...].astype(BF16), wo_ref[0:GLA_W], _TN)
    mix += _dot(yh_ref[...].astype(BF16), wo_ref[GLA_W:GLA_W + HG_W], _TN)
    mix += _dot(yy_ref[...].astype(BF16), wo_ref[GLA_W + HG_W:], _TN)
    h1 = _layer_norm(alpha * h_ref[...] + mix, g_ref[...], b_ref[...])
    h1_ref[...] = h1
    hi = h1.astype(BF16)
    lo = (h1 - hi.astype(F32)).astype(BF16)
    lg = _dot(wrh_ref[...], hi, _NT) + _dot(wrh_ref[...], lo, _NT) + _dot(wrl_ref[...], hi, _NT) + br_ref[...]
    tm = lg.shape[1]
    gl = [lg[g:g + 1] for g in range(N_GROUPS)]
    gmax = functools.reduce(jnp.maximum, gl)
    gidx = jnp.full((1, tm), N_GROUPS - 1, I32)
    for g in range(N_GROUPS - 2, -1, -1):
        gidx = jnp.where(gl[g] == gmax, g, gidx)
    gsum = functools.reduce(jnp.add, [jnp.exp(x - gmax) for x in gl])
    g_val = 1.0 / gsum
    el = []
    for r in range(EXPERTS_PER_GROUP):
        acc = jnp.zeros((1, tm), F32)
        for g in range(N_GROUPS):
            row = N_GROUPS + g * EXPERTS_PER_GROUP + r
            acc = jnp.where(gidx == g, lg[row:row + 1], acc)
        el.append(acc)
    emax = functools.reduce(jnp.maximum, el)
    pe = [jnp.exp(x - emax) for x in el]
    esum = functools.reduce(jnp.add, pe)
    pe = [x / esum for x in pe]
    v1 = functools.reduce(jnp.maximum, pe)
    i1 = jnp.full((1, tm), EXPERTS_PER_GROUP - 1, I32)
    for r in range(EXPERTS_PER_GROUP - 2, -1, -1):
        i1 = jnp.where(pe[r] == v1, r, i1)
    pe2 = [jnp.where(i1 == r, -1.0, pe[r]) for r in range(EXPERTS_PER_GROUP)]
    v2 = functools.reduce(jnp.maximum, pe2)
    i2 = jnp.full((1, tm), EXPERTS_PER_GROUP - 1, I32)
    for r in range(EXPERTS_PER_GROUP - 2, -1, -1):
        i2 = jnp.where(pe2[r] == v2, r, i2)
    den = v1 + v2
    e0 = gidx * EXPERTS_PER_GROUP + i1
    e1 = gidx * EXPERTS_PER_GROUP + i2
    e_ref[...] = jnp.concatenate([e0, e1], axis=0)
    w_ref[...] = jnp.concatenate([g_val * (v1 / den), g_val * (v2 / den)], axis=0)
    eio = lax.broadcasted_iota(I32, (N_EXPERTS, tm), 0)
    hit = jnp.where((eio == e0) | (eio == e1), 1.0, 0.0)
    cnt_ref[0] = jnp.sum(hit, axis=1, keepdims=True)


def _outproj(yg, yh, yy, h, w_out, g, b, wr_t, br, alpha, tm=512):
    T, D = h.shape
    nr = wr_t.shape[0]
    wr_hi = wr_t.astype(BF16)
    wr_lo = (wr_t - wr_hi.astype(F32)).astype(BF16)
    full = lambda shp: pl.BlockSpec(shp, lambda i: (0,) * len(shp))
    return pl.pallas_call(
        functools.partial(_outproj_kernel, alpha=alpha),
        grid=(T // tm,),
        in_specs=[pl.BlockSpec((GLA_W, tm), lambda i: (0, i)),
                  pl.BlockSpec((HG_W, tm), lambda i: (0, i)),
                  pl.BlockSpec((HY_WIDTH, tm), lambda i: (0, i)),
                  pl.BlockSpec((tm, D), lambda i: (i, 0)),
                  full((D, D)), full((1, D)), full((1, D)), full((nr, D)), full((nr, D)), full((nr, 1))],
        out_specs=[pl.BlockSpec((tm, D), lambda i: (i, 0)),
                   pl.BlockSpec((TOP_K, tm), lambda i: (0, i)),
                   pl.BlockSpec((TOP_K, tm), lambda i: (0, i)),
                   pl.BlockSpec((1, N_EXPERTS, 1), lambda i: (i, 0, 0))],
        out_shape=[jax.ShapeDtypeStruct((T, D), F32),
                   jax.ShapeDtypeStruct((TOP_K, T), I32),
                   jax.ShapeDtypeStruct((TOP_K, T), F32),
                   jax.ShapeDtypeStruct((T // tm, N_EXPERTS, 1), F32)],
        compiler_params=_cparams(("arbitrary",)),
        name="outproj",
    )(yg, yh, yy, h, w_out.astype(BF16), g.reshape(1, D), b.reshape(1, D), wr_hi, wr_lo, br)


def _chunk_loop(n, fn):
    def body(c, carry):
        fn(pl.multiple_of(c * ROW_CHUNK, ROW_CHUNK))
        return carry
    lax.fori_loop(0, n, body, 0)


def _segment_copies(nch_s, k, fn):
    off = 0
    for si, rows in enumerate(SEG_SIZES):
        n = nch_s[k * len(SEG_SIZES) + si]

        def body(c, carry, off=off, rows=rows):
            fn(rows, pl.multiple_of(off + c * rows, ROW_CHUNK))
            return carry
        lax.fori_loop(0, n, body, 0)
        off = off + n * rows


def _wait_tile_copies(nch_s, tile, wait_one):
    for si, rows in enumerate(SEG_SIZES):
        tot = functools.reduce(lambda a, b: a + b, [nch_s[(tile * N_EXPERTS + e) * len(SEG_SIZES) + si]
                                                    for e in range(N_EXPERTS)])
        lax.fori_loop(0, tot, lambda c, carry, rows=rows: (wait_one(rows), carry)[1], 0)


def _dispatch_kernel(offs_s, gst_s, nch_s, tst_s, tn_s, nu_s, h_ref, e_ref, base_ref, tri_ref,
                     pos_ref, xg_hbm, xs_ref, zbuf, sem, zsem, *, tm, LP, bm, nblocks):
    i = pl.program_id(0)
    nt = pl.num_programs(0)
    slot = i % 2

    def seg_copy(s, rows, src_row, dst_row):
        return pltpu.make_async_copy(xs_ref.at[s, pl.ds(src_row, rows)],
                                     xg_hbm.at[pl.ds(dst_row, rows)], sem.at[s])

    def zero_copy(dst_row):
        return pltpu.make_async_copy(zbuf.at[pl.ds(0, ROW_CHUNK)], xg_hbm.at[pl.ds(dst_row, ROW_CHUNK)], zsem.at[0])

    def zero_block(blk):
        return pltpu.make_async_copy(zbuf, xg_hbm.at[pl.ds(pl.multiple_of(blk * bm, bm), bm)], zsem.at[0])

    def wait_tile(tile, s):
        _wait_tile_copies(nch_s, tile, lambda rows: seg_copy(s, rows, 0, 0).wait())

    @pl.when(i == 0)
    def _():
        zbuf[...] = jnp.zeros_like(zbuf)
        for e in range(N_EXPERTS):
            _chunk_loop(tn_s[e], lambda off, e=e: zero_copy(pl.multiple_of(tst_s[e] + off, ROW_CHUNK)).start())
        lax.fori_loop(nu_s[0], nblocks, lambda blk, c: (zero_block(blk).start(), c)[1], 0)
        for e in range(N_EXPERTS):
            _chunk_loop(tn_s[e], lambda off: zero_copy(0).wait())
        lax.fori_loop(nu_s[0], nblocks, lambda blk, c: (zero_block(0).wait(), c)[1], 0)

    e0, e1 = e_ref[0:1, :], e_ref[1:2, :]
    eio = lax.broadcasted_iota(I32, (N_EXPERTS, tm), 0)
    oh0, oh1 = eio == e0, eio == e1
    hit = jnp.where(oh0 | oh1, 1.0, 0.0).astype(BF16)
    posm = base_ref[0] + _dot(hit, tri_ref[...])
    pos0 = jnp.sum(jnp.where(oh0, posm, 0.0), axis=0, keepdims=True).astype(I32)
    pos1 = jnp.sum(jnp.where(oh1, posm, 0.0), axis=0, keepdims=True).astype(I32)
    pos_ref[...] = jnp.concatenate([pos0, pos1], axis=0)
    rio = lax.broadcasted_iota(I32, (LP, tm), 0)
    perm = jnp.where((rio == pos0) | (rio == pos1), 1.0, 0.0).astype(BF16)
    xs_ref[slot] = _dot(perm, h_ref[...].astype(BF16)).astype(BF16)

    @pl.when(i > 0)
    def _():
        wait_tile(i - 1, 1 - slot)

    for e in range(N_EXPERTS):
        k = i * N_EXPERTS + e
        src0, dst0 = offs_s[k], gst_s[k]
        _segment_copies(nch_s, k, lambda rows, off, src0=src0, dst0=dst0: seg_copy(
            slot, rows, pl.multiple_of(src0 + off, ROW_CHUNK), pl.multiple_of(dst0 + off, ROW_CHUNK)).start())

    @pl.when(i == nt - 1)
    def _():
        wait_tile(i, slot)


def _moe_dispatch(h1, e_kt, tables, nblocks, bm, tm):
    T, D = h1.shape
    NT = T // tm
    nrows = nblocks * bm
    LP = TOP_K * tm + N_EXPERTS * ROW_CHUNK
    r = np.arange(tm)
    tri = jnp.asarray(r[:, None] < r[None, :], F32).astype(BF16)
    base = tables["offs"].astype(F32).reshape(NT, N_EXPERTS, 1)
    flat = lambda a: a.reshape(-1).astype(I32)
    pos, xg = pl.pallas_call(
        functools.partial(_dispatch_kernel, tm=tm, LP=LP, bm=bm, nblocks=nblocks),
        grid_spec=pltpu.PrefetchScalarGridSpec(
            num_scalar_prefetch=6,
            grid=(NT,),
            in_specs=[pl.BlockSpec((tm, D), lambda i, *_: (i, 0)),
                      pl.BlockSpec((TOP_K, tm), lambda i, *_: (0, i)),
                      pl.BlockSpec((1, N_EXPERTS, 1), lambda i, *_: (i, 0, 0)),
                      pl.BlockSpec((tm, tm), lambda i, *_: (0, 0))],
            out_specs=[pl.BlockSpec((TOP_K, tm), lambda i, *_: (0, i)),
                       pl.BlockSpec(memory_space=pl.ANY)],
            scratch_shapes=[pltpu.VMEM((2, LP, D), BF16), pltpu.VMEM((bm, D), BF16),
                            pltpu.SemaphoreType.DMA((2,)), pltpu.SemaphoreType.DMA((1,))]),
        out_shape=[jax.ShapeDtypeStruct((TOP_K, T), I32), jax.ShapeDtypeStruct((nrows, D), BF16)],
        compiler_params=_cparams(("arbitrary",)),
        name="moe_dispatch",
    )(flat(tables["offs"]), flat(tables["gstart"]), flat(tables["nch"]), flat(tables["tail_start"]),
      flat(tables["tail_n"]), tables["nused"], h1, e_kt, base, tri)
    return pos, xg


def _ffn_kernel(be_ref, nu_ref, x_ref, wg_ref, wu_ref, wd_ref, y_ref, wgb, wub, wdb):
    j = pl.program_id(0)
    used = j < nu_ref[0]

    @pl.when((j == 0) | (be_ref[j] != be_ref[jnp.maximum(j - 1, 0)]))
    def _():
        wgb[...] = wg_ref[0].astype(BF16)
        wub[...] = wu_ref[0].astype(BF16)
        wdb[...] = wd_ref[0].astype(BF16)

    @pl.when(used)
    def _():
        x = x_ref[...]
        a = _dot(x, wgb[...])
        hid = (a * _sigmoid(a)) * _dot(x, wub[...])
        y_ref[...] = _dot(hid.astype(BF16), wdb[...]).astype(BF16)

    @pl.when(jnp.logical_not(used))
    def _():
        y_ref[...] = jnp.zeros_like(y_ref)


def _moe_ffn(xg, block_e, nused, wg, wu, wd, bm, first_expert=0):
    nrows, D = xg.shape
    NB = nrows // bm
    DE = wg.shape[-1]
    row = lambda j, be, nu: (jnp.minimum(j, nu[0] - 1), 0)
    wsel = lambda j, be, nu: (first_expert + be[j], 0, 0)
    return pl.pallas_call(
        _ffn_kernel,
        grid_spec=pltpu.PrefetchScalarGridSpec(
            num_scalar_prefetch=2,
            grid=(NB,),
            in_specs=[pl.BlockSpec((bm, D), row),
                      pl.BlockSpec((1, D, DE), wsel),
                      pl.BlockSpec((1, D, DE), wsel),
                      pl.BlockSpec((1, DE, D), wsel)],
            out_specs=pl.BlockSpec((bm, D), lambda j, be, nu: (j, 0)),
            scratch_shapes=[pltpu.VMEM((D, DE), BF16), pltpu.VMEM((D, DE), BF16), pltpu.VMEM((DE, D), BF16)]),
        out_shape=jax.ShapeDtypeStruct((nrows, D), BF16),
        compiler_params=_cparams(("arbitrary",)),
        name="moe_ffn",
    )(block_e, nused, xg, wg, wu, wd)


def _combine_kernel(offs_s, gst_s, nch_s, yb_hbm, pos_ref, w_ref, h_ref, g_ref, b_ref, o_ref, ybl, sem,
                    *, tm, LP, alpha):
    i = pl.program_id(0)
    nt = pl.num_programs(0)
    slot = i % 2

    def seg_copy(s, rows, src_row, dst_row):
        return pltpu.make_async_copy(yb_hbm.at[pl.ds(src_row, rows)],
                                     ybl.at[s, pl.ds(dst_row, rows)], sem.at[s])

    def issue(tile, s):
        for e in range(N_EXPERTS):
            k = tile * N_EXPERTS + e
            src0, dst0 = gst_s[k], offs_s[k]
            _segment_copies(nch_s, k, lambda rows, off, src0=src0, dst0=dst0: seg_copy(
                s, rows, pl.multiple_of(src0 + off, ROW_CHUNK), pl.multiple_of(dst0 + off, ROW_CHUNK)).start())

    @pl.when(i == 0)
    def _():
        ybl[...] = jnp.zeros_like(ybl)
        issue(0, 0)

    @pl.when(i + 1 < nt)
    def _():
        issue(i + 1, 1 - slot)

    _wait_tile_copies(nch_s, i, lambda rows: seg_copy(slot, rows, 0, 0).wait())
    pos0, pos1 = pos_ref[0:1, :], pos_ref[1:2, :]
    w0, w1 = w_ref[0:1, :], w_ref[1:2, :]
    rio = lax.broadcasted_iota(I32, (LP, tm), 0)
    pw = (jnp.where(rio == pos0, w0, 0.0) + jnp.where(rio == pos1, w1, 0.0)).astype(BF16)
    ffn = _dot(pw, ybl[slot], _TN)
    o_ref[...] = _layer_norm(alpha * h_ref[...] + ffn, g_ref[...], b_ref[...])


def _moe_combine(yb, h1, pos, w_kt, tables, g, b, alpha, tm):
    T, D = h1.shape
    LP = TOP_K * tm + N_EXPERTS * ROW_CHUNK
    flat = lambda a: a.reshape(-1).astype(I32)
    return pl.pallas_call(
        functools.partial(_combine_kernel, tm=tm, LP=LP, alpha=alpha),
        grid_spec=pltpu.PrefetchScalarGridSpec(
            num_scalar_prefetch=3,
            grid=(T // tm,),
            in_specs=[pl.BlockSpec(memory_space=pl.ANY),
                      pl.BlockSpec((TOP_K, tm), lambda i, *_: (0, i)),
                      pl.BlockSpec((TOP_K, tm), lambda i, *_: (0, i)),
                      pl.BlockSpec((tm, D), lambda i, *_: (i, 0)),
                      pl.BlockSpec((1, D), lambda i, *_: (0, 0)),
                      pl.BlockSpec((1, D), lambda i, *_: (0, 0))],
            out_specs=pl.BlockSpec((tm, D), lambda i, *_: (i, 0)),
            scratch_shapes=[pltpu.VMEM((2, LP, D), BF16), pltpu.SemaphoreType.DMA((2,))]),
        out_shape=jax.ShapeDtypeStruct((T, D), F32),
        compiler_params=_cparams(("arbitrary",)),
        name="moe_combine",
    )(flat(tables["offs"]), flat(tables["gstart"]), flat(tables["nch"]), yb, pos, w_kt, h1,
      g.reshape(1, D), b.reshape(1, D))


def _dispatch_tables(cnt, bm, nblocks):
    padlen = ((cnt + ROW_CHUNK - 1) // ROW_CHUNK) * ROW_CHUNK
    offs = jnp.cumsum(padlen, axis=1) - padlen
    tot = jnp.sum(padlen, axis=0)
    region = ((tot + bm - 1) // bm) * bm
    rend = jnp.cumsum(region)
    rstart = rend - region
    gstart = rstart[None, :] + jnp.cumsum(padlen, axis=0) - padlen
    blk_row = jnp.arange(nblocks, dtype=I32)[:, None] * bm
    block_e = jnp.minimum(jnp.sum((blk_row >= rend[None, :]).astype(I32), axis=1), N_EXPERTS - 1)
    nch, rest = [], padlen
    for rows in SEG_SIZES:
        nch.append(rest // rows)
        rest = rest % rows
    return dict(offs=offs, gstart=gstart, nch=jnp.stack(nch, axis=-1), tail_start=rstart + tot,
                tail_n=(region - tot) // ROW_CHUNK, block_e=block_e.astype(I32),
                nused=(rend[-1:] // bm).astype(I32))


def _permute_in_columns(w):
    splits = (192, 192, 384, 384, 32, 384, 768, 384, 384, 768)
    offs = np.concatenate([[0], np.cumsum(splits)])
    gq, gk, gv, gg, ga, hq, hf, hi, hgt, hyu = [(int(offs[i]), int(offs[i + 1])) for i in range(10)]
    order = [hq, hi, hgt, hf, gv, gg, gq, gk, hyu, ga]
    assert sum(b - a for a, b in order) == D_IN
    return jnp.concatenate([w[:, a:b] for a, b in order], axis=1)


def kernel(x, ln_in_g, ln_in_b, w_in, gla_wa2, gla_ba, gla_norm_g, hg_lb_logits, hg_norm_g, hy_conv_w, hy_conv_b, hy_w1, hy_b1, hy_freq, hy_w2, hy_b2, hy_w3, hy_bias, w_out, ln1_g, ln1_b, moe_wr_g, moe_br_g, moe_wr_e, moe_br_e, moe_w_gate, moe_w_up, moe_w_down, ln2_g, ln2_b):
    B, L, D = x.shape
    T = B * L
    depth = w_in.shape[0]
    alpha = (2 * depth) ** 0.25
    bm = 512
    tmr = 512
    nblocks = -(-(T * TOP_K + (T // tmr) * N_EXPERTS * (ROW_CHUNK - 1) + N_EXPERTS * (bm - 1)) // bm)
    consts = _dft_consts(L)

    p = jax.nn.softmax(hg_lb_logits.astype(F32), axis=0)
    lbs = jnp.cumsum(p, axis=0) - p[0:1]
    lbc = jnp.concatenate([jnp.broadcast_to((1.0 - lbs)[..., None], lbs.shape + (LANES,)),
                           jnp.broadcast_to(jnp.maximum(lbs, LB_FLOOR)[..., None], lbs.shape + (LANES,))], axis=-1)

    wg_all = moe_w_gate.reshape((depth * N_EXPERTS,) + moe_w_gate.shape[2:])
    wu_all = moe_w_up.reshape((depth * N_EXPERTS,) + moe_w_up.shape[2:])
    wd_all = moe_w_down.reshape((depth * N_EXPERTS,) + moe_w_down.shape[2:])
    h = x.reshape(T, D)
    for l in range(depth):
        w_t = _permute_in_columns(w_in[l]).T.astype(BF16)
        proj_t, h = _inproj(h, ln_in_g, ln_in_b, w_t, apply_ln=(l == 0))
        wa_t = jnp.swapaxes(gla_wa2[l], 1, 2)
        y_gla = _scan(proj_t, "gla", B, L, (wa_t, gla_ba[l].reshape(2, GLA_K, 1)), gla_norm_g[l].reshape(GLA_W, 1))
        y_hg = _scan(proj_t, "hg", B, L, (lbc[l],), hg_norm_g[l].reshape(HG_W, 1))
        kt = _hyena_filters(L, hy_w1[l], hy_b1[l], hy_freq[l], hy_w2[l], hy_b2[l], hy_w3[l])
        kspec = _hyena_spectrum(kt, consts, L)
        y_hy = _hyena(proj_t, kspec, consts, hy_conv_w[l], hy_conv_b[l], hy_bias[l], B, L)
        nr = N_GROUPS + N_EXPERTS
        nrp = -(-nr // 8) * 8
        wr_t = jnp.zeros((nrp, D), F32).at[:nr].set(jnp.concatenate([moe_wr_g[l], moe_wr_e[l]], axis=1).T)
        br = jnp.zeros((nrp, 1), F32).at[:nr, 0].set(jnp.concatenate([moe_br_g[l], moe_br_e[l]]))
        h1, e_kt, w_kt, cnt = _outproj(y_gla, y_hg, y_hy, h, w_out[l], ln1_g[l], ln1_b[l], wr_t, br, alpha, tm=tmr)
        tables = _dispatch_tables(cnt.reshape(T // tmr, N_EXPERTS).astype(I32), bm, nblocks)
        pos, xg = _moe_dispatch(h1, e_kt, tables, nblocks, bm, tmr)
        yb = _moe_ffn(xg, tables["block_e"], tables["nused"], wg_all, wu_all, wd_all, bm, first_expert=l * N_EXPERTS)
        h = _moe_combine(yb, h1, pos, w_kt, tables, ln2_g[l], ln2_b[l], alpha, tmr)
    return h.reshape(B, L, D)
```

```python
import functools
import math

import numpy as np
import jax
import jax.numpy as jnp
from jax import lax
from jax.experimental import pallas as pl
from jax.experimental.pallas import tpu as pltpu

F32 = jnp.float32
BF16 = jnp.bfloat16
I32 = jnp.int32

GLA_HEADS, GLA_DK, GLA_DV, GLA_LOWRANK, GLA_TAU = 6, 32, 64, 16, 16.0
HG_HEADS, HG_DK, HG_DV = 6, 64, 64
HY_WIDTH, HY_ORDER, HY_EMB, HY_FFN, HY_INNER = 256, 2, 33, 64, 2
HY_FAST_DECAY, HY_SLOW_DECAY, HY_TARGET = 0.3, 1.5, 1e-2
N_GROUPS, EXPERTS_PER_GROUP = 4, 4
N_EXPERTS = N_GROUPS * EXPERTS_PER_GROUP
TOP_K = 2
LN_EPS, RMS_EPS, LB_FLOOR = 1e-5, 1e-6, 1e-30

LANES = 128
SCAN_CHUNK = LANES
ROW_CHUNK = 16
SEG_SIZES = (64, 32, ROW_CHUNK)
VMEM_LIMIT = 56 * 1024 * 1024

GLA_W = GLA_HEADS * GLA_DV
GLA_K = GLA_HEADS * GLA_DK
HG_W = HG_HEADS * HG_DV
HG_K = HG_HEADS * HG_DK
OFF_HQ, OFF_HI, OFF_HGT, OFF_HF = 0, 384, 768, 1152
OFF_GV, OFF_GG, OFF_GQ, OFF_GK, OFF_HY, OFF_GA = 1920, 2304, 2688, 2880, 3072, 3840
D_IN = 3872


def _dot(a, b, dims=(((1,), (0,)), ((), ())), precision=None):
    return lax.dot_general(a, b, dims, preferred_element_type=F32, precision=precision)


_NT = (((1,), (1,)), ((), ()))
_TN = (((0,), (0,)), ((), ()))


def _layer_norm(x, g, b):
    mu = jnp.mean(x, axis=-1, keepdims=True)
    xc = x - mu
    var = jnp.mean(xc * xc, axis=-1, keepdims=True)
    return xc * lax.rsqrt(var + LN_EPS) * g + b


def _log_sigmoid(x):
    return jnp.minimum(x, 0.0) - jnp.log(1.0 + jnp.exp(-jnp.abs(x)))


def _sigmoid(x):
    return 0.5 + 0.5 * jnp.tanh(0.5 * x)


def _cparams(sem):
    return pltpu.CompilerParams(dimension_semantics=sem, vmem_limit_bytes=VMEM_LIMIT)


def _inproj_kernel(x_ref, g_ref, b_ref, w_ref, *outs, apply_ln):
    x = x_ref[...]
    if apply_ln:
        x = _layer_norm(x, g_ref[...], b_ref[...])
        outs[1][...] = x
    outs[0][...] = _dot(w_ref[...], x.astype(BF16), _NT)


def _inproj(x, g, b, w_t, apply_ln, tm=512):
    T, D = x.shape
    n_out = w_t.shape[0]
    out_shape = [jax.ShapeDtypeStruct((n_out, T), F32)]
    out_specs = [pl.BlockSpec((n_out, tm), lambda i: (0, i))]
    if apply_ln:
        out_shape.append(jax.ShapeDtypeStruct((T, D), F32))
        out_specs.append(pl.BlockSpec((tm, D), lambda i: (i, 0)))
    res = pl.pallas_call(
        functools.partial(_inproj_kernel, apply_ln=apply_ln),
        grid=(T // tm,),
        in_specs=[pl.BlockSpec((tm, D), lambda i: (i, 0)),
                  pl.BlockSpec((1, D), lambda i: (0, 0)),
                  pl.BlockSpec((1, D), lambda i: (0, 0)),
                  pl.BlockSpec((n_out, D), lambda i: (0, 0))],
        out_specs=out_specs,
        out_shape=out_shape,
        compiler_params=_cparams(("arbitrary",)),
        name="inproj",
    )(x, g.reshape(1, D), b.reshape(1, D), w_t)
    return res if apply_ln else (res[0], x)


def _scan_kernel(*refs, mode, H, K, V, TB, NCB):
    if mode == "gla":
        (qf_ref, qb_ref, kf_ref, kb_ref, vf_ref, vb_ref, gaf_ref, gab_ref, wa_ref, ba_ref,
         gtf_ref, gtb_ref, ng_ref, y_ref, s_ref, oacc_ref, oi_ref, qg_ref, ut_ref, dr_ref) = refs
        q_refs, k_refs, ga_refs = (qf_ref, qb_ref), (kf_ref, kb_ref), (gaf_ref, gab_ref)
    else:
        (qf_ref, qb_ref, zf_ref, zb_ref, vf_ref, vb_ref, lbc_ref,
         gtf_ref, gtb_ref, ng_ref, y_ref, s_ref, oacc_ref, oi_ref, qg_ref, ut_ref, dr_ref) = refs
        q_refs, z_refs = (qf_ref, qb_ref), (zf_ref, zb_ref)
    v_refs, gate_refs = (vf_ref, vb_ref), (gtf_ref, gtb_ref)
    C = SCAN_CHUNK
    half = C // 2
    nchunks = TB // C
    n = pl.program_id(1)
    blocks = (n, NCB - 1 - n)

    @pl.when(n == 0)
    def _():
        s_ref[...] = jnp.zeros_like(s_ref)

    def gates(d, sl):
        if mode == "gla":
            a = _dot(wa_ref[d], ga_refs[d][:, sl], precision=lax.Precision.HIGHEST) + ba_ref[d]
            g = _log_sigmoid(a) * (1.0 / GLA_TAU)
            q = q_refs[d][:, sl] * (K ** -0.5)
            k = k_refs[d][:, sl]
        else:
            one_m_lb, lb_floor = lbc_ref[d, :, 0:C], lbc_ref[d, :, C:2 * C]
            s = one_m_lb * _sigmoid(z_refs[d][:, sl])
            g = jnp.log(s + lb_floor)
            k = one_m_lb - s
            hq = q_refs[d][:, sl]
            q = hq * _sigmoid(hq)
        return q, k, g

    r = lax.broadcasted_iota(I32, (C, C), 0)
    c = lax.broadcasted_iota(I32, (C, C), 1)
    same = (r >= half) == (c >= half)
    lane_lo = lax.broadcasted_iota(I32, (1, C), 1) < half
    sign_lo = jnp.where(lane_lo, 1.0, -1.0)
    consts = (((r <= c).astype(BF16), (r < half) & (c >= half), same & (r <= c),
               (C - 1, half, half // 2, half + half // 2)),
              ((r >= c).astype(BF16), (r >= half) & (c < half), same & (r >= c),
               (0, half - 1, half // 2 - 1, half + half // 2 - 1)))

    heads = [(slice(h * K, (h + 1) * K), slice(h * V, (h + 1) * V)) for h in range(H)]

    def cumulate(d, g):
        cum = consts[d][0]
        g1 = g.astype(BF16)
        g2 = (g - g1.astype(F32)).astype(BF16)
        G = _dot(g1, cum) + _dot(g2, cum)
        ge_row = _dot(ones8, g1, _NT) + _dot(ones8, g2, _NT)
        return G, ge_row

    def scale(d, sl, q, k, G, ge_row):
        _, _, _, (c_end, c_mid, c_a, c_b) = consts[d]
        v = v_refs[d][:, sl].astype(BF16)
        g_end = G[:, c_end:c_end + 1]
        g_mid = G[:, c_mid:c_mid + 1]
        e2 = G - jnp.where(lane_lo, G[:, c_a:c_a + 1], G[:, c_b:c_b + 1])
        q2 = (q * jnp.exp(e2)).astype(BF16)
        k2 = (k * jnp.exp(-e2)).astype(BF16)
        dm = G - g_mid
        x1 = jnp.exp(dm * (sign_lo if d else -sign_lo))
        q1 = (q * x1).astype(BF16)
        k1 = (k * x1).astype(BF16)
        qg = (q * jnp.exp(G)).astype(BF16)
        kd = (k * jnp.exp(g_end - G)).astype(BF16)
        return dict(q1=q1, k1=k1, q2=q2, k2=k2, qg=qg, kd=kd, v=v, dec_row=jnp.exp(ge_row))

    ones8 = jnp.ones((8, C), BF16)

    def scores(x):
        return [(_dot(x["k1"][rk], x["q1"][rk], _TN), _dot(x["k2"][rk], x["q2"][rk], _TN)) for rk, _ in heads]

    def masked(d, ps):
        _, off1, diag, _ = consts[d]
        return [jnp.where(off1, p1, jnp.where(diag, p2, 0.0)).astype(BF16) for p1, p2 in ps]

    def park(d, ci, x, pm):
        oi_ref[d, ci] = jnp.concatenate([_dot(x["v"][rv], pm[h]) for h, (_, rv) in enumerate(heads)], axis=0)
        qg_ref[d, ci] = x["qg"]
        dr_ref[d, ci] = x["dec_row"]
        for h, (rk, rv) in enumerate(heads):
            ut_ref[d, ci, h] = _dot(x["v"][rv], x["kd"][rk], _NT)

    def chunk_slices(ci):
        offs = (pl.multiple_of(ci * C, C), pl.multiple_of((nchunks - 1 - ci) * C, C))
        return offs, [pl.ds(offs[d], C) for d in range(2)]

    def independent(it, carry):
        jobs = [(d, U * it + u, chunk_slices(U * it + u)[1][d]) for u in range(U) for d in range(2)]
        qkg = [gates(d, sl) for d, _, sl in jobs]
        cums = [cumulate(d, qkg[i][2]) for i, (d, _, _) in enumerate(jobs)]
        xs = [scale(d, sl, qkg[i][0], qkg[i][1], *cums[i]) for i, (d, _, sl) in enumerate(jobs)]
        scs = [scores(x) for x in xs]
        pms = [masked(d, scs[i]) for i, (d, _, _) in enumerate(jobs)]
        for i, (d, ci, _) in enumerate(jobs):
            park(d, ci, xs[i], pms[i])
        return carry

    U = 4
    assert nchunks % U == 0
    lax.fori_loop(0, nchunks // U, independent, 0)

    def carried(ci):
        st = [[s_ref[d, h] for h in range(H)] for d in range(2)]
        inter = [[_dot(st[d][h].astype(BF16), qg_ref[d, ci, rk, :]) for h, (rk, _) in enumerate(heads)]
                 for d in range(2)]
        for d in range(2):
            for h, (rk, _) in enumerate(heads):
                s_ref[d, h] = st[d][h] * dr_ref[d, ci, 0:1, rk] + ut_ref[d, ci, h]
        return [oi_ref[d, ci] + jnp.concatenate(inter[d], axis=0) for d in range(2)]

    def finish(o, gate):
        act = gate * _sigmoid(gate) if mode == "gla" else _sigmoid(gate)
        ys = []
        for h in range(H):
            oh = o[h * V:(h + 1) * V]
            ms = jnp.mean(oh * oh, axis=0, keepdims=True)
            ys.append(oh * lax.rsqrt(ms + RMS_EPS))
        return (jnp.concatenate(ys, axis=0) * ng_ref[...] * act).astype(y_ref.dtype)

    def chunk(ci, carry):
        offs, sls = chunk_slices(ci)
        tsls = [pl.ds(pl.multiple_of(blocks[d] * TB + offs[d], C), C) for d in range(2)]
        o = carried(ci)

        @pl.when(2 * n < NCB)
        def _():
            for d in range(2):
                oacc_ref[:, tsls[d]] = o[d]

        @pl.when(2 * n >= NCB)
        def _():
            tot = [o[d] + oacc_ref[:, tsls[d]] for d in range(2)]
            for d in range(2):
                y_ref[:, tsls[d]] = finish(tot[d], gate_refs[d][:, sls[d]])
        return carry

    lax.fori_loop(0, nchunks, chunk, 0)


def _scan(proj_t, mode, B, L, extra, norm_g, TB=512):
    NCB = L // TB
    nch = TB // SCAN_CHUNK
    assert NCB % 2 == 0
    if mode == "gla":
        H, K, V = GLA_HEADS, GLA_DK, GLA_DV
    else:
        H, K, V = HG_HEADS, HG_DK, HG_DV
    HK, HV = H * K, H * V
    cf = lambda b, n: b * NCB + n
    cb = lambda b, n: b * NCB + NCB - 1 - n
    gf = lambda b, n: b * NCB + jnp.maximum(n, NCB // 2)
    gb = lambda b, n: b * NCB + jnp.minimum(NCB - 1 - n, NCB // 2 - 1)

    def pair(rows, off, fwd=cf, bwd=cb, dir_step=0):
        return [pl.BlockSpec((rows, TB), lambda b, n: (off // rows, fwd(b, n))),
                pl.BlockSpec((rows, TB), lambda b, n: (off // rows + dir_step, bwd(b, n)))]

    whole = lambda shp: pl.BlockSpec(shp, lambda b, n: (0,) * len(shp))
    if mode == "gla":
        wa_t, ba = extra
        in_specs = (pair(HK, OFF_GQ) + pair(HK, OFF_GK) + pair(HV, OFF_GV)
                    + pair(GLA_LOWRANK, OFF_GA, dir_step=1)
                    + [whole((2, HK, GLA_LOWRANK)), whole((2, HK, 1))]
                    + pair(HV, OFF_GG, gf, gb) + [whole((HV, 1))])
        args = (proj_t,) * 8 + (wa_t, ba, proj_t, proj_t, norm_g)
    else:
        (lbc,) = extra
        in_specs = (pair(HK, OFF_HQ) + pair(HK, OFF_HF, dir_step=1) + pair(HV, OFF_HI)
                    + [whole((2, HK, 2 * LANES))] + pair(HV, OFF_HGT, gf, gb) + [whole((HV, 1))])
        args = (proj_t,) * 6 + (lbc, proj_t, proj_t, norm_g)
    return pl.pallas_call(
        functools.partial(_scan_kernel, mode=mode, H=H, K=K, V=V, TB=TB, NCB=NCB),
        grid=(B, NCB),
        in_specs=in_specs,
        out_specs=pl.BlockSpec((HV, L), lambda b, n: (0, b)),
        out_shape=jax.ShapeDtypeStruct((HV, B * L), BF16),
        scratch_shapes=[pltpu.VMEM((2, H, V, K), F32), pltpu.VMEM((HV, L), F32),
                        pltpu.VMEM((2, nch, HV, SCAN_CHUNK), F32), pltpu.VMEM((2, nch, HK, SCAN_CHUNK), BF16),
                        pltpu.VMEM((2, nch, H, V, K), F32), pltpu.VMEM((2, nch, 8, HK), F32)],
        compiler_params=_cparams(("arbitrary", "arbitrary")),
        name="scan_" + mode,
    )(*args)


def _dft_consts(L):
    N = 2 * L
    NA = N // LANES
    a = np.arange(NA)[:, None] * np.arange(NA)[None, :]
    ca, sa = np.cos(2 * np.pi * a / NA), np.sin(2 * np.pi * a / NA)
    hh = NA // 2
    w1d = np.block([[ca[:, :hh], sa[:, :hh]], [-sa[:, :hh], ca[:, :hh]]])
    w1f = np.concatenate([ca, -sa], axis=0)
    w1i = np.block([[ca[:hh, :], -sa[:hh, :]], [sa[:hh, :], ca[:hh, :]]])
    bb = np.arange(LANES)[:, None] * np.arange(LANES)[None, :]
    cb, sb = np.cos(2 * np.pi * bb / LANES), np.sin(2 * np.pi * bb / LANES)
    w2 = np.block([[cb, -sb], [sb, cb]])
    w2i = np.block([[cb, sb], [-sb, cb]])
    tw = np.arange(NA)[:, None] * np.arange(LANES)[None, :]
    tc, ts = np.cos(2 * np.pi * tw / N), np.sin(2 * np.pi * tw / N)
    bf = lambda m: jnp.asarray(m, dtype=F32).astype(BF16)
    return dict(w1d=bf(w1d), w1f=bf(w1f), w1i=bf(w1i), w2=bf(w2), w2i=bf(w2i),
                tc=jnp.asarray(tc, F32), ts=jnp.asarray(ts, F32))


def _pos_features(L):
    t = np.linspace(0.0, 1.0, L)
    w = 2.0 * np.pi * np.arange(L) / L
    bands = np.linspace(1e-4, (HY_EMB - 1) // 2 - 1, (HY_EMB - 1) // 2)
    z = np.concatenate([t[None, :], np.cos(bands[:, None] * w[None, :]), -np.sin(bands[:, None] * w[None, :])], axis=0)
    kp = -(-HY_EMB // 8) * 8
    z = np.concatenate([z, np.zeros((kp - HY_EMB, L))], axis=0)
    idx = (L - np.arange(L)) % L
    z_rev, t_rev = z[:, idx], t[idx]
    mask = (np.arange(L) >= 1).astype(np.float64)
    return (jnp.asarray(z, F32), jnp.asarray(z_rev, F32), jnp.asarray(t[None, :], F32),
            jnp.asarray(t_rev[None, :], F32), jnp.asarray(mask[None, :], F32))


def _filter_kernel(z_ref, zr_ref, t_ref, tr_ref, m_ref, w1_ref, b1_ref, fr_ref, w2_ref, b2_ref,
                   w3f_ref, w3b_ref, dl_ref, out_ref, h_ref, *, L):
    first = (pl.program_id(0) == 0) & (pl.program_id(1) == 0)
    hi = lax.Precision.HIGHEST

    @pl.when(first)
    def _():
        fr = fr_ref[...]
        for idx, zz in enumerate((z_ref, zr_ref)):
            h = jnp.sin(fr * (_dot(w1_ref[...], zz[...], precision=hi) + b1_ref[...]))
            for i in range(HY_INNER):
                h = jnp.sin(fr * (_dot(w2_ref[i], h, precision=hi) + b2_ref[i]))
            h_ref[idx] = h.astype(BF16)

    ad = jnp.abs(dl_ref[...])
    kf = _dot(w3f_ref[...].astype(BF16), h_ref[0]) * jnp.exp(-t_ref[...] * ad)
    kb = _dot(w3b_ref[...].astype(BF16), h_ref[1]) * jnp.exp(-tr_ref[...] * ad) * m_ref[...]
    den = jnp.sum(jnp.abs(kf), axis=1, keepdims=True) + jnp.sum(jnp.abs(kb), axis=1, keepdims=True)
    scale = 1.0 / (jnp.maximum(den, 1e-12) * (2.0 * L))
    out_ref[0, :, 0:L] = kf * scale
    out_ref[0, :, L:2 * L] = kb * scale


def _hyena_filters(L, w1, b1, freq, w2, b2, w3, cg=64):
    z, z_rev, t, t_rev, mask = _pos_features(L)
    kp = z.shape[0]
    w1_t = jnp.zeros((HY_FFN, kp), F32).at[:, :HY_EMB].set(w1.T)
    w2_t = jnp.swapaxes(w2, 1, 2)
    w3_t = w3.T
    max_decay = math.log(HY_TARGET) / HY_FAST_DECAY
    min_decay = math.log(HY_TARGET) / HY_SLOW_DECAY
    deltas = jnp.asarray(np.linspace(min_decay, max_decay, HY_WIDTH).reshape(HY_WIDTH, 1), F32)
    ncg = HY_WIDTH // cg
    full = lambda shp: pl.BlockSpec(shp, lambda o, j: (0,) * len(shp))
    return pl.pallas_call(
        functools.partial(_filter_kernel, L=L),
        grid=(HY_ORDER, ncg),
        in_specs=[full((kp, L)), full((kp, L)), full((1, L)), full((1, L)), full((1, L)),
                  full((HY_FFN, kp)), full((HY_FFN, 1)), full((HY_FFN, 1)),
                  full((HY_INNER, HY_FFN, HY_FFN)), full((HY_INNER, HY_FFN, 1)),
                  pl.BlockSpec((cg, HY_FFN), lambda o, j: (o * 2 * ncg + j, 0)),
                  pl.BlockSpec((cg, HY_FFN), lambda o, j: (o * 2 * ncg + ncg + j, 0)),
                  pl.BlockSpec((cg, 1), lambda o, j: (j, 0))],
        out_specs=pl.BlockSpec((1, cg, 2 * L), lambda o, j: (o, j, 0)),
        out_shape=jax.ShapeDtypeStruct((HY_ORDER, HY_WIDTH, 2 * L), F32),
        scratch_shapes=[pltpu.VMEM((2, HY_FFN, L), BF16)],
        compiler_params=_cparams(("arbitrary", "arbitrary")),
        name="hyena_filter",
    )(z, z_rev, t, t_rev, mask, w1_t, b1.reshape(HY_FFN, 1), freq.reshape(HY_FFN, 1), w2_t,
      b2.reshape(HY_INNER, HY_FFN, 1), w3_t, w3_t, deltas)


def _spectrum_kernel(k2_ref, w1_ref, tc_ref, ts_ref, w2_ref, out_ref, k_ref, *, NA, cg):
    k_ref[0] = k2_ref[0].reshape(cg, NA, LANES)
    tc2 = jnp.concatenate([tc_ref[...]] * 2, axis=1)
    ts2 = jnp.concatenate([ts_ref[...]] * 2, axis=1)
    G = 4

    def body(it, carry):
        lhs = []
        for pr in range(G // 2):
            rhs = jnp.concatenate([k_ref[0, G * it + 2 * pr + cc] for cc in range(2)], axis=1)
            a = _dot(w1_ref[...], rhs.astype(BF16))
            a_re, a_im = a[:NA], a[NA:]
            b_re = a_re * tc2 + a_im * ts2
            b_im = a_im * tc2 - a_re * ts2
            lhs += [jnp.concatenate([b_re[:, cc * LANES:(cc + 1) * LANES], b_im[:, cc * LANES:(cc + 1) * LANES]], axis=1)
                    for cc in range(2)]
        x = _dot(jnp.concatenate(lhs, axis=0).astype(BF16), w2_ref[...])
        for i in range(G):
            out_ref[0, G * it + i] = x[i * NA:(i + 1) * NA]
        return carry
    lax.fori_loop(0, cg // G, body, 0)


def _hyena_spectrum(kt, consts, L, cg=32):
    NA = 2 * L // LANES
    full = lambda shp: pl.BlockSpec(shp, lambda o, j: (0,) * len(shp))
    return pl.pallas_call(
        functools.partial(_spectrum_kernel, NA=NA, cg=cg),
        grid=(HY_ORDER, HY_WIDTH // cg),
        in_specs=[pl.BlockSpec((1, cg, 2 * L), lambda o, j: (o, j, 0)),
                  full((2 * NA, NA)), full((NA, LANES)), full((NA, LANES)), full((2 * LANES, 2 * LANES))],
        out_specs=pl.BlockSpec((1, cg, NA, 2 * LANES), lambda o, j: (o, j, 0, 0)),
        out_shape=jax.ShapeDtypeStruct((HY_ORDER, HY_WIDTH, NA, 2 * LANES), F32),
        scratch_shapes=[pltpu.VMEM((1, cg, NA, LANES), F32)],
        compiler_params=_cparams(("arbitrary", "arbitrary")),
        name="hyena_spectrum",
    )(kt, consts["w1f"], consts["tc"], consts["ts"], consts["w2"])


def _hyena_kernel(cw_ref, cb_ref, hb_ref, v2_ref, x12_ref, x22_ref, ks_ref, w1d_ref, w1i_ref, tc_ref, ts_ref,
                  w2_ref, w2i_ref, y2_ref, v_ref, x1_ref, x2_ref, y_ref, *, NA, cg, B):
    hh = NA // 2
    L = hh * LANES
    j = pl.program_id(0)
    for src, dst in ((v2_ref, v_ref), (x12_ref, x1_ref), (x22_ref, x2_ref)):
        for b in range(B):
            dst[:, b] = src[:, b * L:(b + 1) * L].reshape(cg, hh, LANES)
    row = lax.broadcasted_iota(I32, (hh, LANES), 0)
    lane = lax.broadcasted_iota(I32, (hh, LANES), 1)
    first = (row == 0) & (lane == 0)
    last = (row == hh - 1) & (lane == LANES - 1)

    def short_conv(x, ch):
        r1 = pltpu.roll(x, 1, 1)
        prev = jnp.where(lane == 0, pltpu.roll(r1, 1, 0), r1)
        prev = jnp.where(first, 0.0, prev)
        r2 = pltpu.roll(x, LANES - 1, 1)
        nxt = jnp.where(lane == LANES - 1, pltpu.roll(r2, hh - 1, 0), r2)
        nxt = jnp.where(last, 0.0, nxt)
        return cw_ref[0, ch] * prev + cw_ref[1, ch] * x + cw_ref[2, ch] * nxt + cb_ref[ch]

    P = B // 2
    tc, ts = tc_ref[...], ts_ref[...]
    tc2 = jnp.concatenate([tc, tc], axis=1)
    ts2 = jnp.concatenate([ts, ts], axis=1)
    lane2 = lambda x, cc: x[:, cc * LANES:(cc + 1) * LANES]

    seqs = [(p, cc) for p in range(P) for cc in range(2)]
    gate_refs = (x1_ref, x2_ref)

    def load(cis, chs):
        z = {(p, cc): [short_conv(v_ref[cis[cc], 2 * p + r], chs[cc]) for r in range(2)] for p, cc in seqs}
        gates = [{(p, cc): [short_conv(gate_refs[o][cis[cc], 2 * p + r], (o + 1) * HY_WIDTH + chs[cc])
                            for r in range(2)] for p, cc in seqs} for o in range(HY_ORDER)]
        return z, gates

    def dft_rows(z):
        lhs = []
        for p in range(P):
            rhs = jnp.concatenate([jnp.concatenate(z[(p, cc)], axis=0) for cc in range(2)], axis=1)
            a = _dot(w1d_ref[...], rhs.astype(BF16))
            a_re, a_im = a[:NA], a[NA:]
            b_re = a_re * tc2 + a_im * ts2
            b_im = a_im * tc2 - a_re * ts2
            lhs += [jnp.concatenate([lane2(b_re, cc), lane2(b_im, cc)], axis=1) for cc in range(2)]
        return jnp.concatenate(lhs, axis=0).astype(BF16)

    def dft_lanes_times_filter(lhs, o, cis):
        x = _dot(lhs, w2_ref[...])
        ys = []
        for idx, (p, cc) in enumerate(seqs):
            xb = x[idx * NA:(idx + 1) * NA]
            ks = ks_ref[o, cis[cc]]
            x_re, x_im = lane2(xb, 0), lane2(xb, 1)
            k_re, k_im = lane2(ks, 0), lane2(ks, 1)
            ys.append(jnp.concatenate([x_re * k_re - x_im * k_im, x_re * k_im + x_im * k_re], axis=1))
        return jnp.concatenate(ys, axis=0).astype(BF16)

    def idft_lanes(ys):
        bq = _dot(ys, w2i_ref[...])
        out = []
        for p in range(P):
            cr, cim = [], []
            for cc in range(2):
                blk = bq[(2 * p + cc) * NA:(2 * p + cc + 1) * NA]
                b_re, b_im = lane2(blk, 0), lane2(blk, 1)
                cr.append(b_re * tc - b_im * ts)
                cim.append(b_re * ts + b_im * tc)
            out.append(jnp.concatenate([jnp.concatenate(cr, axis=1), jnp.concatenate(cim, axis=1)], axis=0).astype(BF16))
        return out

    def idft_rows_and_gate(rhs, z, gates, o, chs):
        znew = {}
        for p in range(P):
            conv = _dot(w1i_ref[...], rhs[p])
            for cc in range(2):
                bias = hb_ref[o, chs[cc]]
                znew[(p, cc)] = [gates[(p, cc)][r] * (lane2(conv, cc)[r * hh:(r + 1) * hh] + z[(p, cc)][r] * bias)
                                 for r in range(2)]
        return znew

    NG = 2

    def body(it, carry):
        groups = range(NG)
        cis = [[2 * NG * it + 2 * g + cc for cc in range(2)] for g in groups]
        chs = [[j * cg + ci for ci in cis[g]] for g in groups]
        loaded = [load(cis[g], chs[g]) for g in groups]
        zs = [loaded[g][0] for g in groups]
        for o in range(HY_ORDER):
            s1 = [dft_rows(zs[g]) for g in groups]
            s2 = [dft_lanes_times_filter(s1[g], o, cis[g]) for g in groups]
            s3 = [idft_lanes(s2[g]) for g in groups]
            zs = [idft_rows_and_gate(s3[g], zs[g], loaded[g][1][o], o, chs[g]) for g in groups]
        for g in groups:
            for p, cc in seqs:
                for r in range(2):
                    y_ref[cis[g][cc], 2 * p + r] = zs[g][(p, cc)][r]
        return carry

    lax.fori_loop(0, cg // (2 * NG), body, 0)
    for b in range(B):
        y2_ref[:, b * L:(b + 1) * L] = y_ref[:, b].reshape(cg, L)


def _hyena(proj_t, kspec, consts, conv_w, conv_b, bias, B, L, cg=8):
    NA = 2 * L // LANES
    hh = NA // 2
    ncg = HY_WIDTH // cg
    base = OFF_HY // cg
    smem = pl.BlockSpec(memory_space=pltpu.SMEM)
    full = lambda shp: pl.BlockSpec(shp, lambda j: (0,) * len(shp))
    blk = lambda off: pl.BlockSpec((cg, B * L), lambda j: (off + j, 0))
    tiles = pltpu.VMEM((cg, B, hh, LANES), F32)
    return pl.pallas_call(
        functools.partial(_hyena_kernel, NA=NA, cg=cg, B=B),
        grid=(ncg,),
        in_specs=[smem, smem, smem, blk(base), blk(base + ncg), blk(base + 2 * ncg),
                  pl.BlockSpec((HY_ORDER, cg, NA, 2 * LANES), lambda j: (0, j, 0, 0)),
                  full((2 * NA, NA)), full((NA, 2 * NA)), full((NA, LANES)), full((NA, LANES)),
                  full((2 * LANES, 2 * LANES)), full((2 * LANES, 2 * LANES))],
        out_specs=pl.BlockSpec((cg, B * L), lambda j: (j, 0)),
        out_shape=jax.ShapeDtypeStruct((HY_WIDTH, B * L), F32),
        scratch_shapes=[tiles, tiles, tiles, tiles],
        compiler_params=_cparams(("arbitrary",)),
        name="hyena_conv",
    )(conv_w, conv_b, bias, proj_t, proj_t, proj_t, kspec, consts["w1d"], consts["w1i"], consts["tc"], consts["ts"],
      consts["w2"], consts["w2i"])


def _outproj_kernel(yg_ref, yh_ref, yy_ref, h_ref, wo_ref, g_ref, b_ref, wrh_ref, wrl_ref, br_ref,
                    h1_ref, e_ref, w_ref, cnt_ref, *, alpha):
    tm = h_ref.shape[0]
    ts = LANES
    ns = tm // ts
    subs = [slice(s * ts, (s + 1) * ts) for s in range(ns)]
    mixes = [_dot(yg_ref[:, sl].astype(BF16), wo_ref[0:GLA_W], _TN)
             + _dot(yh_ref[:, sl].astype(BF16), wo_ref[GLA_W:GLA_W + HG_W], _TN)
             + _dot(yy_ref[:, sl].astype(BF16), wo_ref[GLA_W + HG_W:], _TN) for sl in subs]
    h1s = [_layer_norm(alpha * h_ref[sl, :] + mixes[s], g_ref[...], b_ref[...]) for s, sl in enumerate(subs)]
    for s, sl in enumerate(subs):
        h1_ref[sl, :] = h1s[s]
    his = [h1.astype(BF16) for h1 in h1s]
    los = [(h1s[s] - his[s].astype(F32)).astype(BF16) for s in range(ns)]
    lgs = [_dot(wrh_ref[...], his[s], _NT) + _dot(wrh_ref[...], los[s], _NT) + _dot(wrl_ref[...], his[s], _NT)
           for s in range(ns)]
    lg = jnp.concatenate(lgs, axis=1) + br_ref[...]
    gl = [lg[g:g + 1] for g in range(N_GROUPS)]
    gmax = functools.reduce(jnp.maximum, gl)
    gidx = jnp.full((1, tm), N_GROUPS - 1, I32)
    for g in range(N_GROUPS - 2, -1, -1):
        gidx = jnp.where(gl[g] == gmax, g, gidx)
    gsum = functools.reduce(jnp.add, [jnp.exp(x - gmax) for x in gl])
    g_val = 1.0 / gsum
    el = []
    for r in range(EXPERTS_PER_GROUP):
        acc = jnp.zeros((1, tm), F32)
        for g in range(N_GROUPS):
            row = N_GROUPS + g * EXPERTS_PER_GROUP + r
            acc = jnp.where(gidx == g, lg[row:row + 1], acc)
        el.append(acc)
    emax = functools.reduce(jnp.maximum, el)
    pe = [jnp.exp(x - emax) for x in el]
    esum = functools.reduce(jnp.add, pe)
    pe = [x / esum for x in pe]
    v1 = functools.reduce(jnp.maximum, pe)
    i1 = jnp.full((1, tm), EXPERTS_PER_GROUP - 1, I32)
    for r in range(EXPERTS_PER_GROUP - 2, -1, -1):
        i1 = jnp.where(pe[r] == v1, r, i1)
    pe2 = [jnp.where(i1 == r, -1.0, pe[r]) for r in range(EXPERTS_PER_GROUP)]
    v2 = functools.reduce(jnp.maximum, pe2)
    i2 = jnp.full((1, tm), EXPERTS_PER_GROUP - 1, I32)
    for r in range(EXPERTS_PER_GROUP - 2, -1, -1):
        i2 = jnp.where(pe2[r] == v2, r, i2)
    den = v1 + v2
    e0 = gidx * EXPERTS_PER_GROUP + i1
    e1 = gidx * EXPERTS_PER_GROUP + i2
    e_ref[...] = jnp.concatenate([e0, e1], axis=0)
    w_ref[...] = jnp.concatenate([g_val * (v1 / den), g_val * (v2 / den)], axis=0)
    eio = lax.broadcasted_iota(I32, (N_EXPERTS, tm), 0)
    hit = jnp.where((eio == e0) | (eio == e1), 1.0, 0.0)
    cnt_ref[0] = jnp.sum(hit, axis=1, keepdims=True)


def _outproj(yg, yh, yy, h, w_out, g, b, wr_t, br, alpha, tm=512):
    T, D = h.shape
    nr = wr_t.shape[0]
    wr_hi = wr_t.astype(BF16)
    wr_lo = (wr_t - wr_hi.astype(F32)).astype(BF16)
    full = lambda shp: pl.BlockSpec(shp, lambda i: (0,) * len(shp))
    return pl.pallas_call(
        functools.partial(_outproj_kernel, alpha=alpha),
        grid=(T // tm,),
        in_specs=[pl.BlockSpec((GLA_W, tm), lambda i: (0, i)),
                  pl.BlockSpec((HG_W, tm), lambda i: (0, i)),
                  pl.BlockSpec((HY_WIDTH, tm), lambda i: (0, i)),
                  pl.BlockSpec((tm, D), lambda i: (i, 0)),
                  full((D, D)), full((1, D)), full((1, D)), full((nr, D)), full((nr, D)), full((nr, 1))],
        out_specs=[pl.BlockSpec((tm, D), lambda i: (i, 0)),
                   pl.BlockSpec((TOP_K, tm), lambda i: (0, i)),
                   pl.BlockSpec((TOP_K, tm), lambda i: (0, i)),
                   pl.BlockSpec((1, N_EXPERTS, 1), lambda i: (i, 0, 0))],
        out_shape=[jax.ShapeDtypeStruct((T, D), F32),
                   jax.ShapeDtypeStruct((TOP_K, T), I32),
                   jax.ShapeDtypeStruct((TOP_K, T), F32),
                   jax.ShapeDtypeStruct((T // tm, N_EXPERTS, 1), F32)],
        compiler_params=_cparams(("arbitrary",)),
        name="outproj",
    )(yg, yh, yy, h, w_out.astype(BF16), g.reshape(1, D), b.reshape(1, D), wr_hi, wr_lo, br)


def _chunk_loop(n, fn):
    def body(c, carry):
        fn(pl.multiple_of(c * ROW_CHUNK, ROW_CHUNK))
        return carry
    lax.fori_loop(0, n, body, 0)


def _segment_copies(nch_s, k, fn):
    off = 0
    for si, rows in enumerate(SEG_SIZES):
        n = nch_s[k * len(SEG_SIZES) + si]

        def body(c, carry, off=off, rows=rows):
            fn(rows, pl.multiple_of(off + c * rows, ROW_CHUNK))
            return carry
        lax.fori_loop(0, n, body, 0)
        off = off + n * rows


def _wait_tile_copies(nch_s, tile, wait_one):
    for si, rows in enumerate(SEG_SIZES):
        tot = functools.reduce(lambda a, b: a + b, [nch_s[(tile * N_EXPERTS + e) * len(SEG_SIZES) + si]
                                                    for e in range(N_EXPERTS)])
        lax.fori_loop(0, tot, lambda c, carry, rows=rows: (wait_one(rows), carry)[1], 0)


def _dispatch_kernel(offs_s, gst_s, nch_s, tst_s, tn_s, nu_s, h_ref, e_ref, base_ref, tri_ref,
                     pos_ref, xg_hbm, xs_ref, zbuf, sem, zsem, *, tm, LP, bm, nblocks):
    i = pl.program_id(0)
    nt = pl.num_programs(0)
    slot = i % 2

    def seg_copy(s, rows, src_row, dst_row):
        return pltpu.make_async_copy(xs_ref.at[s, pl.ds(src_row, rows)],
                                     xg_hbm.at[pl.ds(dst_row, rows)], sem.at[s])

    def zero_copy(dst_row):
        return pltpu.make_async_copy(zbuf.at[pl.ds(0, ROW_CHUNK)], xg_hbm.at[pl.ds(dst_row, ROW_CHUNK)], zsem.at[0])

    def zero_block(blk):
        return pltpu.make_async_copy(zbuf, xg_hbm.at[pl.ds(pl.multiple_of(blk * bm, bm), bm)], zsem.at[0])

    def wait_tile(tile, s):
        _wait_tile_copies(nch_s, tile, lambda rows: seg_copy(s, rows, 0, 0).wait())

    @pl.when(i == 0)
    def _():
        zbuf[...] = jnp.zeros_like(zbuf)
        for e in range(N_EXPERTS):
            _chunk_loop(tn_s[e], lambda off, e=e: zero_copy(pl.multiple_of(tst_s[e] + off, ROW_CHUNK)).start())
        lax.fori_loop(nu_s[0], nblocks, lambda blk, c: (zero_block(blk).start(), c)[1], 0)
        for e in range(N_EXPERTS):
            _chunk_loop(tn_s[e], lambda off: zero_copy(0).wait())
        lax.fori_loop(nu_s[0], nblocks, lambda blk, c: (zero_block(0).wait(), c)[1], 0)

    e0, e1 = e_ref[0:1, :], e_ref[1:2, :]
    eio = lax.broadcasted_iota(I32, (N_EXPERTS, tm), 0)
    oh0, oh1 = eio == e0, eio == e1
    hit = jnp.where(oh0 | oh1, 1.0, 0.0).astype(BF16)
    posm = base_ref[0] + _dot(hit, tri_ref[...])
    pos0 = jnp.sum(jnp.where(oh0, posm, 0.0), axis=0, keepdims=True).astype(I32)
    pos1 = jnp.sum(jnp.where(oh1, posm, 0.0), axis=0, keepdims=True).astype(I32)
    pos_ref[...] = jnp.concatenate([pos0, pos1], axis=0)
    hb = h_ref[...].astype(BF16)
    rb = 256
    assert LP % rb == 0
    rio = lax.broadcasted_iota(I32, (rb, tm), 0)
    for r0 in range(0, LP, rb):
        perm = jnp.where((rio == pos0 - r0) | (rio == pos1 - r0), 1.0, 0.0).astype(BF16)
        xs_ref[slot, r0:r0 + rb] = _dot(perm, hb).astype(BF16)

    @pl.when(i > 0)
    def _():
        wait_tile(i - 1, 1 - slot)

    for e in range(N_EXPERTS):
        k = i * N_EXPERTS + e
        src0, dst0 = offs_s[k], gst_s[k]
        _segment_copies(nch_s, k, lambda rows, off, src0=src0, dst0=dst0: seg_copy(
            slot, rows, pl.multiple_of(src0 + off, ROW_CHUNK), pl.multiple_of(dst0 + off, ROW_CHUNK)).start())

    @pl.when(i == nt - 1)
    def _():
        wait_tile(i, slot)


def _moe_dispatch(h1, e_kt, tables, nblocks, bm, tm):
    T, D = h1.shape
    NT = T // tm
    nrows = nblocks * bm
    LP = TOP_K * tm + N_EXPERTS * ROW_CHUNK
    r = np.arange(tm)
    tri = jnp.asarray(r[:, None] < r[None, :], F32).astype(BF16)
    base = tables["offs"].astype(F32).reshape(NT, N_EXPERTS, 1)
    flat = lambda a: a.reshape(-1).astype(I32)
    pos, xg = pl.pallas_call(
        functools.partial(_dispatch_kernel, tm=tm, LP=LP, bm=bm, nblocks=nblocks),
        grid_spec=pltpu.PrefetchScalarGridSpec(
            num_scalar_prefetch=6,
            grid=(NT,),
            in_specs=[pl.BlockSpec((tm, D), lambda i, *_: (i, 0)),
                      pl.BlockSpec((TOP_K, tm), lambda i, *_: (0, i)),
                      pl.BlockSpec((1, N_EXPERTS, 1), lambda i, *_: (i, 0, 0)),
                      pl.BlockSpec((tm, tm), lambda i, *_: (0, 0))],
            out_specs=[pl.BlockSpec((TOP_K, tm), lambda i, *_: (0, i)),
                       pl.BlockSpec(memory_space=pl.ANY)],
            scratch_shapes=[pltpu.VMEM((2, LP, D), BF16), pltpu.VMEM((bm, D), BF16),
                            pltpu.SemaphoreType.DMA((2,)), pltpu.SemaphoreType.DMA((1,))]),
        out_shape=[jax.ShapeDtypeStruct((TOP_K, T), I32), jax.ShapeDtypeStruct((nrows, D), BF16)],
        compiler_params=_cparams(("arbitrary",)),
        name="moe_dispatch",
    )(flat(tables["offs"]), flat(tables["gstart"]), flat(tables["nch"]), flat(tables["tail_start"]),
      flat(tables["tail_n"]), tables["nused"], h1, e_kt, base, tri)
    return pos, xg


def _ffn_kernel(be_ref, nu_ref, x_ref, wg_ref, wu_ref, wd_ref, y_ref, wgb, wub, wdb):
    j = pl.program_id(0)
    used = j < nu_ref[0]

    @pl.when((j == 0) | (be_ref[j] != be_ref[jnp.maximum(j - 1, 0)]))
    def _():
        wgb[...] = wg_ref[0].astype(BF16)
        wub[...] = wu_ref[0].astype(BF16)
        wdb[...] = wd_ref[0].astype(BF16)

    @pl.when(used)
    def _():
        x = x_ref[...]
        a = _dot(x, wgb[...])
        hid = (a * _sigmoid(a)) * _dot(x, wub[...])
        y_ref[...] = _dot(hid.astype(BF16), wdb[...]).astype(BF16)

    @pl.when(jnp.logical_not(used))
    def _():
        y_ref[...] = jnp.zeros_like(y_ref)


def _moe_ffn(xg, block_e, nused, wg, wu, wd, bm, first_expert=0):
    nrows, D = xg.shape
    NB = nrows // bm
    DE = wg.shape[-1]
    row = lambda j, be, nu: (jnp.minimum(j, nu[0] - 1), 0)
    wsel = lambda j, be, nu: (first_expert + be[j], 0, 0)
    return pl.pallas_call(
        _ffn_kernel,
        grid_spec=pltpu.PrefetchScalarGridSpec(
            num_scalar_prefetch=2,
            grid=(NB,),
            in_specs=[pl.BlockSpec((bm, D), row),
                      pl.BlockSpec((1, D, DE), wsel),
                      pl.BlockSpec((1, D, DE), wsel),
                      pl.BlockSpec((1, DE, D), wsel)],
            out_specs=pl.BlockSpec((bm, D), lambda j, be, nu: (j, 0)),
            scratch_shapes=[pltpu.VMEM((D, DE), BF16), pltpu.VMEM((D, DE), BF16), pltpu.VMEM((DE, D), BF16)]),
        out_shape=jax.ShapeDtypeStruct((nrows, D), BF16),
        compiler_params=_cparams(("arbitrary",)),
        name="moe_ffn",
    )(block_e, nused, xg, wg, wu, wd)


def _combine_kernel(offs_s, gst_s, nch_s, yb_hbm, pos_ref, w_ref, h_ref, g_ref, b_ref, o_ref, ybl, sem,
                    *, tm, LP, alpha):
    i = pl.program_id(0)
    nt = pl.num_programs(0)
    slot = i % 2

    def seg_copy(s, rows, src_row, dst_row):
        return pltpu.make_async_copy(yb_hbm.at[pl.ds(src_row, rows)],
                                     ybl.at[s, pl.ds(dst_row, rows)], sem.at[s])

    def issue(tile, s):
        for e in range(N_EXPERTS):
            k = tile * N_EXPERTS + e
            src0, dst0 = gst_s[k], offs_s[k]
            _segment_copies(nch_s, k, lambda rows, off, src0=src0, dst0=dst0: seg_copy(
                s, rows, pl.multiple_of(src0 + off, ROW_CHUNK), pl.multiple_of(dst0 + off, ROW_CHUNK)).start())

    @pl.when(i == 0)
    def _():
        ybl[...] = jnp.zeros_like(ybl)
        issue(0, 0)

    @pl.when(i + 1 < nt)
    def _():
        issue(i + 1, 1 - slot)

    _wait_tile_copies(nch_s, i, lambda rows: seg_copy(slot, rows, 0, 0).wait())
    pos0, pos1 = pos_ref[0:1, :], pos_ref[1:2, :]
    w0, w1 = w_ref[0:1, :], w_ref[1:2, :]
    rio = lax.broadcasted_iota(I32, (LP, LANES), 0)
    yl = ybl[slot]
    for t0 in range(0, tm, LANES):
        tl = slice(t0, t0 + LANES)
        pw = (jnp.where(rio == pos0[:, tl], w0[:, tl], 0.0) + jnp.where(rio == pos1[:, tl], w1[:, tl], 0.0)).astype(BF16)
        ffn = _dot(pw, yl, _TN)
        o_ref[tl, :] = _layer_norm(alpha * h_ref[tl, :] + ffn, g_ref[...], b_ref[...])


def _moe_combine(yb, h1, pos, w_kt, tables, g, b, alpha, tm):
    T, D = h1.shape
    LP = TOP_K * tm + N_EXPERTS * ROW_CHUNK
    flat = lambda a: a.reshape(-1).astype(I32)
    return pl.pallas_call(
        functools.partial(_combine_kernel, tm=tm, LP=LP, alpha=alpha),
        grid_spec=pltpu.PrefetchScalarGridSpec(
            num_scalar_prefetch=3,
            grid=(T // tm,),
            in_specs=[pl.BlockSpec(memory_space=pl.ANY),
                      pl.BlockSpec((TOP_K, tm), lambda i, *_: (0, i)),
                      pl.BlockSpec((TOP_K, tm), lambda i, *_: (0, i)),
                      pl.BlockSpec((tm, D), lambda i, *_: (i, 0)),
                      pl.BlockSpec((1, D), lambda i, *_: (0, 0)),
                      pl.BlockSpec((1, D), lambda i, *_: (0, 0))],
            out_specs=pl.BlockSpec((tm, D), lambda i, *_: (i, 0)),
            scratch_shapes=[pltpu.VMEM((2, LP, D), BF16), pltpu.SemaphoreType.DMA((2,))]),
        out_shape=jax.ShapeDtypeStruct((T, D), F32),
        compiler_params=_cparams(("arbitrary",)),
        name="moe_combine",
    )(flat(tables["offs"]), flat(tables["gstart"]), flat(tables["nch"]), yb, pos, w_kt, h1,
      g.reshape(1, D), b.reshape(1, D))


def _dispatch_tables(cnt, bm, nblocks):
    padlen = ((cnt + ROW_CHUNK - 1) // ROW_CHUNK) * ROW_CHUNK
    offs = jnp.cumsum(padlen, axis=1) - padlen
    tot = jnp.sum(padlen, axis=0)
    region = ((tot + bm - 1) // bm) * bm
    rend = jnp.cumsum(region)
    rstart = rend - region
    gstart = rstart[None, :] + jnp.cumsum(padlen, axis=0) - padlen
    blk_row = jnp.arange(nblocks, dtype=I32)[:, None] * bm
    block_e = jnp.minimum(jnp.sum((blk_row >= rend[None, :]).astype(I32), axis=1), N_EXPERTS - 1)
    nch, rest = [], padlen
    for rows in SEG_SIZES:
        nch.append(rest // rows)
        rest = rest % rows
    return dict(offs=offs, gstart=gstart, nch=jnp.stack(nch, axis=-1), tail_start=rstart + tot,
                tail_n=(region - tot) // ROW_CHUNK, block_e=block_e.astype(I32),
                nused=(rend[-1:] // bm).astype(I32))


def _permute_in_columns(w):
    splits = (192, 192, 384, 384, 32, 384, 768, 384, 384, 768)
    offs = np.concatenate([[0], np.cumsum(splits)])
    gq, gk, gv, gg, ga, hq, hf, hi, hgt, hyu = [(int(offs[i]), int(offs[i + 1])) for i in range(10)]
    order = [hq, hi, hgt, hf, gv, gg, gq, gk, hyu, ga]
    assert sum(b - a for a, b in order) == D_IN
    return jnp.concatenate([w[:, a:b] for a, b in order], axis=1)


def kernel(x, ln_in_g, ln_in_b, w_in, gla_wa2, gla_ba, gla_norm_g, hg_lb_logits, hg_norm_g, hy_conv_w, hy_conv_b, hy_w1, hy_b1, hy_freq, hy_w2, hy_b2, hy_w3, hy_bias, w_out, ln1_g, ln1_b, moe_wr_g, moe_br_g, moe_wr_e, moe_br_e, moe_w_gate, moe_w_up, moe_w_down, ln2_g, ln2_b):
    B, L, D = x.shape
    T = B * L
    depth = w_in.shape[0]
    alpha = (2 * depth) ** 0.25
    bm = 512
    tmr = 512
    nblocks = -(-(T * TOP_K + (T // tmr) * N_EXPERTS * (ROW_CHUNK - 1) + N_EXPERTS * (bm - 1)) // bm)
    consts = _dft_consts(L)

    p = jax.nn.softmax(hg_lb_logits.astype(F32), axis=0)
    lbs = jnp.cumsum(p, axis=0) - p[0:1]
    lbc = jnp.concatenate([jnp.broadcast_to((1.0 - lbs)[..., None], lbs.shape + (LANES,)),
                           jnp.broadcast_to(jnp.maximum(lbs, LB_FLOOR)[..., None], lbs.shape + (LANES,))], axis=-1)

    wg_all = moe_w_gate.reshape((depth * N_EXPERTS,) + moe_w_gate.shape[2:])
    wu_all = moe_w_up.reshape((depth * N_EXPERTS,) + moe_w_up.shape[2:])
    wd_all = moe_w_down.reshape((depth * N_EXPERTS,) + moe_w_down.shape[2:])
    h = x.reshape(T, D)
    for l in range(depth):
        w_t = _permute_in_columns(w_in[l]).T.astype(BF16)
        proj_t, h = _inproj(h, ln_in_g, ln_in_b, w_t, apply_ln=(l == 0))
        wa_t = jnp.swapaxes(gla_wa2[l], 1, 2)
        y_gla = _scan(proj_t, "gla", B, L, (wa_t, gla_ba[l].reshape(2, GLA_K, 1)), gla_norm_g[l].reshape(GLA_W, 1))
        y_hg = _scan(proj_t, "hg", B, L, (lbc[l],), hg_norm_g[l].reshape(HG_W, 1))
        kt = _hyena_filters(L, hy_w1[l], hy_b1[l], hy_freq[l], hy_w2[l], hy_b2[l], hy_w3[l])
        kspec = _hyena_spectrum(kt, consts, L)
        y_hy = _hyena(proj_t, kspec, consts, hy_conv_w[l], hy_conv_b[l], hy_bias[l], B, L)
        nr = N_GROUPS + N_EXPERTS
        nrp = -(-nr // 8) * 8
        wr_t = jnp.zeros((nrp, D), F32).at[:nr].set(jnp.concatenate([moe_wr_g[l], moe_wr_e[l]], axis=1).T)
        br = jnp.zeros((nrp, 1), F32).at[:nr, 0].set(jnp.concatenate([moe_br_g[l], moe_br_e[l]]))
        h1, e_kt, w_kt, cnt = _outproj(y_gla, y_hg, y_hy, h, w_out[l], ln1_g[l], ln1_b[l], wr_t, br, alpha, tm=tmr)
        tables = _dispatch_tables(cnt.reshape(T // tmr, N_EXPERTS).astype(I32), bm, nblocks)
        pos, xg = _moe_dispatch(h1, e_kt, tables, nblocks, bm, tmr)
        yb = _moe_ffn(xg, tables["block_e"], tables["nused"], wg_all, wu_all, wd_all, bm, first_expert=l * N_EXPERTS)
        h = _moe_combine(yb, h1, pos, w_kt, tables, ln2_g[l], ln2_b[l], alpha, tmr)
    return h.reshape(B, L, D)
```

```python
import functools
import math

import numpy as np
import jax
import jax.numpy as jnp
from jax import lax
from jax.experimental import pallas as pl
from jax.experimental.pallas import tpu as pltpu

F32 = jnp.float32
BF16 = jnp.bfloat16
I32 = jnp.int32

GLA_HEADS, GLA_DK, GLA_DV, GLA_LOWRANK, GLA_TAU = 6, 32, 64, 16, 16.0
HG_HEADS, HG_DK, HG_DV = 6, 64, 64
HY_WIDTH, HY_ORDER, HY_EMB, HY_FFN, HY_INNER = 256, 2, 33, 64, 2
HY_FAST_DECAY, HY_SLOW_DECAY, HY_TARGET = 0.3, 1.5, 1e-2
N_GROUPS, EXPERTS_PER_GROUP = 4, 4
N_EXPERTS = N_GROUPS * EXPERTS_PER_GROUP
TOP_K = 2
LN_EPS, RMS_EPS, LB_FLOOR = 1e-5, 1e-6, 1e-30
LOG2E = 1.4426950408889634

LANES = 128
SCAN_CHUNK = LANES
ROW_CHUNK = 16
SEG_SIZES = (64, 32, ROW_CHUNK)
VMEM_LIMIT = 56 * 1024 * 1024

GLA_W = GLA_HEADS * GLA_DV
GLA_K = GLA_HEADS * GLA_DK
HG_W = HG_HEADS * HG_DV
HG_K = HG_HEADS * HG_DK
OFF_HQ, OFF_HI, OFF_HGT, OFF_HF = 0, 384, 768, 1152
OFF_GV, OFF_GG, OFF_GQ, OFF_GK, OFF_HY, OFF_GA = 1920, 2304, 2688, 2880, 3072, 3840
D_IN = 3872


def _dot(a, b, dims=(((1,), (0,)), ((), ())), precision=None):
    return lax.dot_general(a, b, dims, preferred_element_type=F32, precision=precision)


_NT = (((1,), (1,)), ((), ()))
_TN = (((0,), (0,)), ((), ()))


def _layer_norm(x, g, b):
    mu = jnp.mean(x, axis=-1, keepdims=True)
    xc = x - mu
    var = jnp.mean(xc * xc, axis=-1, keepdims=True)
    return xc * lax.rsqrt(var + LN_EPS) * g + b


def _log_sigmoid(x):
    return jnp.minimum(x, 0.0) - jnp.log(1.0 + jnp.exp(-jnp.abs(x)))


def _cparams(sem):
    return pltpu.CompilerParams(dimension_semantics=sem, vmem_limit_bytes=VMEM_LIMIT)


def _inproj_kernel(x_ref, g_ref, b_ref, w_ref, *outs, apply_ln):
    x = x_ref[...]
    if apply_ln:
        x = _layer_norm(x, g_ref[...], b_ref[...])
        outs[1][...] = x
    outs[0][...] = _dot(w_ref[...], x.astype(BF16), _NT)


def _inproj(x, g, b, w_t, apply_ln, tm=512):
    T, D = x.shape
    n_out = w_t.shape[0]
    out_shape = [jax.ShapeDtypeStruct((n_out, T), F32)]
    out_specs = [pl.BlockSpec((n_out, tm), lambda i: (0, i))]
    if apply_ln:
        out_shape.append(jax.ShapeDtypeStruct((T, D), F32))
        out_specs.append(pl.BlockSpec((tm, D), lambda i: (i, 0)))
    res = pl.pallas_call(
        functools.partial(_inproj_kernel, apply_ln=apply_ln),
        grid=(T // tm,),
        in_specs=[pl.BlockSpec((tm, D), lambda i: (i, 0)),
                  pl.BlockSpec((1, D), lambda i: (0, 0)),
                  pl.BlockSpec((1, D), lambda i: (0, 0)),
                  pl.BlockSpec((n_out, D), lambda i: (0, 0))],
        out_specs=out_specs,
        out_shape=out_shape,
        compiler_params=_cparams(("arbitrary",)),
        name="inproj",
    )(x, g.reshape(1, D), b.reshape(1, D), w_t)
    return res if apply_ln else (res[0], x)


def _scan_kernel(*refs, mode, H, K, V, TB, NCB):
    if mode == "gla":
        (qf_ref, qb_ref, kf_ref, kb_ref, vf_ref, vb_ref, gaf_ref, gab_ref, wa_ref, ba_ref,
         gtf_ref, gtb_ref, ng_ref, y_ref, s_ref, oacc_ref, oi_ref, qg_ref, ut_ref, dr_ref) = refs
        q_refs, k_refs, ga_refs = (qf_ref, qb_ref), (kf_ref, kb_ref), (gaf_ref, gab_ref)
    else:
        (qf_ref, qb_ref, zf_ref, zb_ref, vf_ref, vb_ref, lbc_ref,
         gtf_ref, gtb_ref, ng_ref, y_ref, s_ref, oacc_ref, oi_ref, qg_ref, ut_ref, dr_ref) = refs
        q_refs, z_refs = (qf_ref, qb_ref), (zf_ref, zb_ref)
    v_refs, gate_refs = (vf_ref, vb_ref), (gtf_ref, gtb_ref)
    C = SCAN_CHUNK
    half = C // 2
    nchunks = TB // C
    n = pl.program_id(1)
    blocks = (n, NCB - 1 - n)

    @pl.when(n == 0)
    def _():
        s_ref[...] = jnp.zeros_like(s_ref)

    def gates(d, sl):
        if mode == "gla":
            a = _dot(wa_ref[d], ga_refs[d][:, sl], precision=lax.Precision.HIGHEST) + ba_ref[d]
            g = _log_sigmoid(a) * (LOG2E / GLA_TAU)
            q = q_refs[d][:, sl] * (K ** -0.5)
            k = k_refs[d][:, sl]
        else:
            half_lb, lb_floor = lbc_ref[d, :, 0:C], lbc_ref[d, :, C:2 * C]
            ht = half_lb * jnp.tanh(z_refs[d][:, sl])
            g = jnp.log2(half_lb + ht + lb_floor)
            k = half_lb - ht
            hq = q_refs[d][:, sl]
            q = hq + hq * jnp.tanh(hq)
        return q, k, g

    r = lax.broadcasted_iota(I32, (C, C), 0)
    c = lax.broadcasted_iota(I32, (C, C), 1)
    same = (r >= half) == (c >= half)
    lane_lo = lax.broadcasted_iota(I32, (1, C), 1) < half
    sign_lo = jnp.where(lane_lo, 1.0, -1.0)
    consts = (((r <= c).astype(BF16), (r < half) & (c >= half), same & (r <= c),
               (C - 1, half, half // 2, half + half // 2)),
              ((r >= c).astype(BF16), (r >= half) & (c < half), same & (r >= c),
               (0, half - 1, half // 2 - 1, half + half // 2 - 1)))

    heads = [(slice(h * K, (h + 1) * K), slice(h * V, (h + 1) * V)) for h in range(H)]

    def cumulate(d, g):
        cum = consts[d][0]
        g1 = g.astype(BF16)
        g2 = (g - g1.astype(F32)).astype(BF16)
        G = _dot(g1, cum) + _dot(g2, cum)
        ge_row = _dot(ones8, g1, _NT) + _dot(ones8, g2, _NT)
        return G, ge_row

    def scale(d, sl, q, k, G, ge_row):
        _, _, _, (c_end, c_mid, c_a, c_b) = consts[d]
        v = v_refs[d][:, sl].astype(BF16)
        g_end = G[:, c_end:c_end + 1]
        g_mid = G[:, c_mid:c_mid + 1]
        e2 = G - jnp.where(lane_lo, G[:, c_a:c_a + 1], G[:, c_b:c_b + 1])
        q2 = (q * jnp.exp2(e2)).astype(BF16)
        k2 = (k * jnp.exp2(-e2)).astype(BF16)
        dm = G - g_mid
        x1 = jnp.exp2(dm * (sign_lo if d else -sign_lo))
        q1 = (q * x1).astype(BF16)
        k1 = (k * x1).astype(BF16)
        qg = (q * jnp.exp2(G)).astype(BF16)
        kd = (k * jnp.exp2(g_end - G)).astype(BF16)
        return dict(q1=q1, k1=k1, q2=q2, k2=k2, qg=qg, kd=kd, v=v, dec_row=jnp.exp2(ge_row))

    ones8 = jnp.ones((8, C), BF16)

    def scores(x):
        return [(_dot(x["k1"][rk], x["q1"][rk], _TN), _dot(x["k2"][rk], x["q2"][rk], _TN)) for rk, _ in heads]

    def masked(d, ps):
        _, off1, diag, _ = consts[d]
        return [jnp.where(off1, p1, jnp.where(diag, p2, 0.0)).astype(BF16) for p1, p2 in ps]

    def park(d, ci, x, pm):
        oi_ref[d, ci] = jnp.concatenate([_dot(x["v"][rv], pm[h]) for h, (_, rv) in enumerate(heads)], axis=0)
        qg_ref[d, ci] = x["qg"]
        dr_ref[d, ci] = x["dec_row"]
        for h, (rk, rv) in enumerate(heads):
            ut_ref[d, ci, h] = _dot(x["v"][rv], x["kd"][rk], _NT)

    def chunk_slices(ci):
        offs = (pl.multiple_of(ci * C, C), pl.multiple_of((nchunks - 1 - ci) * C, C))
        return offs, [pl.ds(offs[d], C) for d in range(2)]

    def independent(it, carry):
        jobs = [(d, U * it + u, chunk_slices(U * it + u)[1][d]) for u in range(U) for d in range(2)]
        qkg = [gates(d, sl) for d, _, sl in jobs]
        cums = [cumulate(d, qkg[i][2]) for i, (d, _, _) in enumerate(jobs)]
        xs = [scale(d, sl, qkg[i][0], qkg[i][1], *cums[i]) for i, (d, _, sl) in enumerate(jobs)]
        scs = [scores(x) for x in xs]
        pms = [masked(d, scs[i]) for i, (d, _, _) in enumerate(jobs)]
        for i, (d, ci, _) in enumerate(jobs):
            park(d, ci, xs[i], pms[i])
        return carry

    U = 4
    assert nchunks % U == 0
    lax.fori_loop(0, nchunks // U, independent, 0)

    def carried(ci):
        st = [[s_ref[d, h] for h in range(H)] for d in range(2)]
        inter = [[_dot(st[d][h].astype(BF16), qg_ref[d, ci, rk, :]) for h, (rk, _) in enumerate(heads)]
                 for d in range(2)]
        for d in range(2):
            for h, (rk, _) in enumerate(heads):
                s_ref[d, h] = st[d][h] * dr_ref[d, ci, 0:1, rk] + ut_ref[d, ci, h]
        return [oi_ref[d, ci] + jnp.concatenate(inter[d], axis=0) for d in range(2)]

    def finish(o, gate):
        t = jnp.tanh(gate)
        act = gate + gate * t if mode == "gla" else 0.5 + 0.5 * t
        ys = []
        for h in range(H):
            oh = o[h * V:(h + 1) * V]
            ms = jnp.mean(oh * oh, axis=0, keepdims=True)
            ys.append(oh * lax.rsqrt(ms + RMS_EPS))
        return (jnp.concatenate(ys, axis=0) * ng_ref[...] * act).astype(y_ref.dtype)

    def chunk(ci, carry):
        offs, sls = chunk_slices(ci)
        tsls = [pl.ds(pl.multiple_of(blocks[d] * TB + offs[d], C), C) for d in range(2)]
        o = carried(ci)

        @pl.when(2 * n < NCB)
        def _():
            for d in range(2):
                oacc_ref[:, tsls[d]] = o[d]

        @pl.when(2 * n >= NCB)
        def _():
            tot = [o[d] + oacc_ref[:, tsls[d]] for d in range(2)]
            for d in range(2):
                y_ref[:, tsls[d]] = finish(tot[d], gate_refs[d][:, sls[d]])
        return carry

    lax.fori_loop(0, nchunks, chunk, 0)


def _scan(proj_t, mode, B, L, extra, norm_g, TB=512):
    NCB = L // TB
    nch = TB // SCAN_CHUNK
    assert NCB % 2 == 0
    if mode == "gla":
        H, K, V = GLA_HEADS, GLA_DK, GLA_DV
    else:
        H, K, V = HG_HEADS, HG_DK, HG_DV
    HK, HV = H * K, H * V
    cf = lambda b, n: b * NCB + n
    cb = lambda b, n: b * NCB + NCB - 1 - n
    gf = lambda b, n: b * NCB + jnp.maximum(n, NCB // 2)
    gb = lambda b, n: b * NCB + jnp.minimum(NCB - 1 - n, NCB // 2 - 1)

    def pair(rows, off, fwd=cf, bwd=cb, dir_step=0):
        return [pl.BlockSpec((rows, TB), lambda b, n: (off // rows, fwd(b, n))),
                pl.BlockSpec((rows, TB), lambda b, n: (off // rows + dir_step, bwd(b, n)))]

    whole = lambda shp: pl.BlockSpec(shp, lambda b, n: (0,) * len(shp))
    if mode == "gla":
        wa_t, ba = extra
        in_specs = (pair(HK, OFF_GQ) + pair(HK, OFF_GK) + pair(HV, OFF_GV)
                    + pair(GLA_LOWRANK, OFF_GA, dir_step=1)
                    + [whole((2, HK, GLA_LOWRANK)), whole((2, HK, 1))]
                    + pair(HV, OFF_GG, gf, gb) + [whole((HV, 1))])
        args = (proj_t,) * 8 + (wa_t, ba, proj_t, proj_t, norm_g)
    else:
        (lbc,) = extra
        in_specs = (pair(HK, OFF_HQ) + pair(HK, OFF_HF, dir_step=1) + pair(HV, OFF_HI)
                    + [whole((2, HK, 2 * LANES))] + pair(HV, OFF_HGT, gf, gb) + [whole((HV, 1))])
        args = (proj_t,) * 6 + (lbc, proj_t, proj_t, norm_g)
    return pl.pallas_call(
        functools.partial(_scan_kernel, mode=mode, H=H, K=K, V=V, TB=TB, NCB=NCB),
        grid=(B, NCB),
        in_specs=in_specs,
        out_specs=pl.BlockSpec((HV, L), lambda b, n: (0, b)),
        out_shape=jax.ShapeDtypeStruct((HV, B * L), BF16),
        scratch_shapes=[pltpu.VMEM((2, H, V, K), F32), pltpu.VMEM((HV, L), F32),
                        pltpu.VMEM((2, nch, HV, SCAN_CHUNK), F32), pltpu.VMEM((2, nch, HK, SCAN_CHUNK), BF16),
                        pltpu.VMEM((2, nch, H, V, K), F32), pltpu.VMEM((2, nch, 8, HK), F32)],
        compiler_params=_cparams(("arbitrary", "arbitrary")),
        name="scan_" + mode,
    )(*args)


def _dft_consts(L):
    N = 2 * L
    NA = N // LANES
    a = np.arange(NA)[:, None] * np.arange(NA)[None, :]
    ca, sa = np.cos(2 * np.pi * a / NA), np.sin(2 * np.pi * a / NA)
    hh = NA // 2
    w1d = np.block([[ca[:, :hh], sa[:, :hh]], [-sa[:, :hh], ca[:, :hh]]])
    w1f = np.concatenate([ca, -sa], axis=0)
    w1i = np.block([[ca[:hh, :], -sa[:hh, :]], [sa[:hh, :], ca[:hh, :]]])
    bb = np.arange(LANES)[:, None] * np.arange(LANES)[None, :]
    cb, sb = np.cos(2 * np.pi * bb / LANES), np.sin(2 * np.pi * bb / LANES)
    w2 = np.block([[cb, -sb], [sb, cb]])
    w2i = np.block([[cb, sb], [-sb, cb]])
    tw = np.arange(NA)[:, None] * np.arange(LANES)[None, :]
    tc, ts = np.cos(2 * np.pi * tw / N), np.sin(2 * np.pi * tw / N)
    bf = lambda m: jnp.asarray(m, dtype=F32).astype(BF16)
    return dict(w1d=bf(w1d), w1f=bf(w1f), w1i=bf(w1i), w2=bf(w2), w2i=bf(w2i),
                tc=jnp.asarray(tc, F32), ts=jnp.asarray(ts, F32))


def _pos_features(L):
    t = np.linspace(0.0, 1.0, L)
    w = 2.0 * np.pi * np.arange(L) / L
    bands = np.linspace(1e-4, (HY_EMB - 1) // 2 - 1, (HY_EMB - 1) // 2)
    z = np.concatenate([t[None, :], np.cos(bands[:, None] * w[None, :]), -np.sin(bands[:, None] * w[None, :])], axis=0)
    kp = -(-HY_EMB // 8) * 8
    z = np.concatenate([z, np.zeros((kp - HY_EMB, L))], axis=0)
    t_rev = t[(L - np.arange(L)) % L]
    mask = (np.arange(L) >= 1).astype(np.float64)
    return (jnp.asarray(z, F32), jnp.asarray(t[None, :], F32),
            jnp.asarray(t_rev[None, :], F32), jnp.asarray(mask[None, :], F32))


def _filter_kernel(z_ref, t_ref, tr_ref, m_ref, w1_ref, b1_ref, fr_ref, w2_ref, b2_ref,
                   w3f_ref, w3b_ref, dl_ref, out_ref, h_ref, *, L):
    first = (pl.program_id(0) == 0) & (pl.program_id(1) == 0)
    hi = lax.Precision.HIGHEST

    @pl.when(first)
    def _():
        fr = fr_ref[...]
        h = jnp.sin(fr * (_dot(w1_ref[...], z_ref[...], precision=hi) + b1_ref[...]))
        for i in range(HY_INNER):
            h = jnp.sin(fr * (_dot(w2_ref[i], h, precision=hi) + b2_ref[i]))
        hb = h.astype(BF16)
        h_ref[0] = hb
        src = lax.broadcasted_iota(I32, (LANES, LANES), 0)
        dst = lax.broadcasted_iota(I32, (LANES, LANES), 1)
        flip_shift = jnp.where((dst >= 1) & (src == LANES - dst), 1.0, 0.0).astype(BF16)
        lane0 = jnp.where((dst == 0) & (src == 0), 1.0, 0.0).astype(BF16)
        nb = L // LANES
        tile = lambda b: hb[:, b * LANES:(b + 1) * LANES]
        for jb in range(nb):
            blk = _dot(tile(nb - 1 - jb), flip_shift) + _dot(tile((nb - jb) % nb), lane0)
            h_ref[1, :, jb * LANES:(jb + 1) * LANES] = blk.astype(BF16)

    ad = jnp.abs(dl_ref[...])
    kf = _dot(w3f_ref[...].astype(BF16), h_ref[0]) * jnp.exp(-t_ref[...] * ad)
    kb = _dot(w3b_ref[...].astype(BF16), h_ref[1]) * jnp.exp(-tr_ref[...] * ad) * m_ref[...]
    den = jnp.sum(jnp.abs(kf), axis=1, keepdims=True) + jnp.sum(jnp.abs(kb), axis=1, keepdims=True)
    scale = 1.0 / (jnp.maximum(den, 1e-12) * (2.0 * L))
    out_ref[0, :, 0:L] = kf * scale
    out_ref[0, :, L:2 * L] = kb * scale


def _hyena_filters(L, w1, b1, freq, w2, b2, w3, cg=64):
    z, t, t_rev, mask = _pos_features(L)
    kp = z.shape[0]
    w1_t = jnp.zeros((HY_FFN, kp), F32).at[:, :HY_EMB].set(w1.T)
    w2_t = jnp.swapaxes(w2, 1, 2)
    w3_t = w3.T
    max_decay = math.log(HY_TARGET) / HY_FAST_DECAY
    min_decay = math.log(HY_TARGET) / HY_SLOW_DECAY
    deltas = jnp.asarray(np.linspace(min_decay, max_decay, HY_WIDTH).reshape(HY_WIDTH, 1), F32)
    ncg = HY_WIDTH // cg
    full = lambda shp: pl.BlockSpec(shp, lambda o, j: (0,) * len(shp))
    return pl.pallas_call(
        functools.partial(_filter_kernel, L=L),
        grid=(HY_ORDER, ncg),
        in_specs=[full((kp, L)), full((1, L)), full((1, L)), full((1, L)),
                  full((HY_FFN, kp)), full((HY_FFN, 1)), full((HY_FFN, 1)),
                  full((HY_INNER, HY_FFN, HY_FFN)), full((HY_INNER, HY_FFN, 1)),
                  pl.BlockSpec((cg, HY_FFN), lambda o, j: (o * 2 * ncg + j, 0)),
                  pl.BlockSpec((cg, HY_FFN), lambda o, j: (o * 2 * ncg + ncg + j, 0)),
                  pl.BlockSpec((cg, 1), lambda o, j: (j, 0))],
        out_specs=pl.BlockSpec((1, cg, 2 * L), lambda o, j: (o, j, 0)),
        out_shape=jax.ShapeDtypeStruct((HY_ORDER, HY_WIDTH, 2 * L), F32),
        scratch_shapes=[pltpu.VMEM((2, HY_FFN, L), BF16)],
        compiler_params=_cparams(("arbitrary", "arbitrary")),
        name="hyena_filter",
    )(z, t, t_rev, mask, w1_t, b1.reshape(HY_FFN, 1), freq.reshape(HY_FFN, 1), w2_t,
      b2.reshape(HY_INNER, HY_FFN, 1), w3_t, w3_t, deltas)


def _spectrum_kernel(k2_ref, w1_ref, tc_ref, ts_ref, w2_ref, out_ref, k_ref, *, NA, cg):
    k_ref[0] = k2_ref[0].reshape(cg, NA, LANES)
    tc2 = jnp.concatenate([tc_ref[...]] * 2, axis=1)
    ts2 = jnp.concatenate([ts_ref[...]] * 2, axis=1)
    G = 4

    def body(it, carry):
        lhs = []
        for pr in range(G // 2):
            rhs = jnp.concatenate([k_ref[0, G * it + 2 * pr + cc] for cc in range(2)], axis=1)
            a = _dot(w1_ref[...], rhs.astype(BF16))
            a_re, a_im = a[:NA], a[NA:]
            b_re = a_re * tc2 + a_im * ts2
            b_im = a_im * tc2 - a_re * ts2
            lhs += [jnp.concatenate([b_re[:, cc * LANES:(cc + 1) * LANES], b_im[:, cc * LANES:(cc + 1) * LANES]], axis=1)
                    for cc in range(2)]
        x = _dot(jnp.concatenate(lhs, axis=0).astype(BF16), w2_ref[...])
        for i in range(G):
            out_ref[0, G * it + i] = x[i * NA:(i + 1) * NA]
        return carry
    lax.fori_loop(0, cg // G, body, 0)


def _hyena_spectrum(kt, consts, L, cg=32):
    NA = 2 * L // LANES
    full = lambda shp: pl.BlockSpec(shp, lambda o, j: (0,) * len(shp))
    return pl.pallas_call(
        functools.partial(_spectrum_kernel, NA=NA, cg=cg),
        grid=(HY_ORDER, HY_WIDTH // cg),
        in_specs=[pl.BlockSpec((1, cg, 2 * L), lambda o, j: (o, j, 0)),
                  full((2 * NA, NA)), full((NA, LANES)), full((NA, LANES)), full((2 * LANES, 2 * LANES))],
        out_specs=pl.BlockSpec((1, cg, NA, 2 * LANES), lambda o, j: (o, j, 0, 0)),
        out_shape=jax.ShapeDtypeStruct((HY_ORDER, HY_WIDTH, NA, 2 * LANES), F32),
        scratch_shapes=[pltpu.VMEM((1, cg, NA, LANES), F32)],
        compiler_params=_cparams(("arbitrary", "arbitrary")),
        name="hyena_spectrum",
    )(kt, consts["w1f"], consts["tc"], consts["ts"], consts["w2"])


def _hyena_kernel(cw_ref, cb_ref, hb_ref, v2_ref, x12_ref, x22_ref, ks_ref, w1d_ref, w1i_ref, tc_ref, ts_ref,
                  w2_ref, w2i_ref, y2_ref, v_ref, x1_ref, x2_ref, y_ref, *, NA, cg, B):
    hh = NA // 2
    L = hh * LANES
    j = pl.program_id(0)
    for src, dst in ((v2_ref, v_ref), (x12_ref, x1_ref), (x22_ref, x2_ref)):
        for b in range(B):
            dst[:, b] = src[:, b * L:(b + 1) * L].reshape(cg, hh, LANES)
    row = lax.broadcasted_iota(I32, (hh, LANES), 0)
    lane = lax.broadcasted_iota(I32, (hh, LANES), 1)
    first = (row == 0) & (lane == 0)
    last = (row == hh - 1) & (lane == LANES - 1)

    def short_conv(x, ch):
        r1 = pltpu.roll(x, 1, 1)
        prev = jnp.where(lane == 0, pltpu.roll(r1, 1, 0), r1)
        prev = jnp.where(first, 0.0, prev)
        r2 = pltpu.roll(x, LANES - 1, 1)
        nxt = jnp.where(lane == LANES - 1, pltpu.roll(r2, hh - 1, 0), r2)
        nxt = jnp.where(last, 0.0, nxt)
        return cw_ref[0, ch] * prev + cw_ref[1, ch] * x + cw_ref[2, ch] * nxt + cb_ref[ch]

    P = B // 2
    tc, ts = tc_ref[...], ts_ref[...]
    tc2 = jnp.concatenate([tc, tc], axis=1)
    ts2 = jnp.concatenate([ts, ts], axis=1)
    lane2 = lambda x, cc: x[:, cc * LANES:(cc + 1) * LANES]

    seqs = [(p, cc) for p in range(P) for cc in range(2)]
    gate_refs = (x1_ref, x2_ref)

    def load(cis, chs):
        z = {(p, cc): [short_conv(v_ref[cis[cc], 2 * p + r], chs[cc]) for r in range(2)] for p, cc in seqs}
        gates = [{(p, cc): [short_conv(gate_refs[o][cis[cc], 2 * p + r], (o + 1) * HY_WIDTH + chs[cc])
                            for r in range(2)] for p, cc in seqs} for o in range(HY_ORDER)]
        return z, gates

    def dft_rows(z):
        lhs = []
        for p in range(P):
            rhs = jnp.concatenate([jnp.concatenate(z[(p, cc)], axis=0) for cc in range(2)], axis=1)
            a = _dot(w1d_ref[...], rhs.astype(BF16))
            a_re, a_im = a[:NA], a[NA:]
            b_re = a_re * tc2 + a_im * ts2
            b_im = a_im * tc2 - a_re * ts2
            lhs += [jnp.concatenate([lane2(b_re, cc), lane2(b_im, cc)], axis=1) for cc in range(2)]
        return jnp.concatenate(lhs, axis=0).astype(BF16)

    def dft_lanes_times_filter(lhs, o, cis):
        x = _dot(lhs, w2_ref[...])
        ys = []
        for idx, (p, cc) in enumerate(seqs):
            xb = x[idx * NA:(idx + 1) * NA]
            ks = ks_ref[o, cis[cc]]
            x_re, x_im = lane2(xb, 0), lane2(xb, 1)
            k_re, k_im = lane2(ks, 0), lane2(ks, 1)
            ys.append(jnp.concatenate([x_re * k_re - x_im * k_im, x_re * k_im + x_im * k_re], axis=1))
        return jnp.concatenate(ys, axis=0).astype(BF16)

    def idft_lanes(ys):
        bq = _dot(ys, w2i_ref[...])
        out = []
        for p in range(P):
            cr, cim = [], []
            for cc in range(2):
                blk = bq[(2 * p + cc) * NA:(2 * p + cc + 1) * NA]
                b_re, b_im = lane2(blk, 0), lane2(blk, 1)
                cr.append(b_re * tc - b_im * ts)
                cim.append(b_re * ts + b_im * tc)
            out.append(jnp.concatenate([jnp.concatenate(cr, axis=1), jnp.concatenate(cim, axis=1)], axis=0).astype(BF16))
        return out

    def idft_rows_and_gate(rhs, z, gates, o, chs):
        znew = {}
        for p in range(P):
            conv = _dot(w1i_ref[...], rhs[p])
            for cc in range(2):
                bias = hb_ref[o, chs[cc]]
                znew[(p, cc)] = [gates[(p, cc)][r] * (lane2(conv, cc)[r * hh:(r + 1) * hh] + z[(p, cc)][r] * bias)
                                 for r in range(2)]
        return znew

    NG = 2

    def body(it, carry):
        groups = range(NG)
        cis = [[2 * NG * it + 2 * g + cc for cc in range(2)] for g in groups]
        chs = [[j * cg + ci for ci in cis[g]] for g in groups]
        loaded = [load(cis[g], chs[g]) for g in groups]
        zs = [loaded[g][0] for g in groups]
        for o in range(HY_ORDER):
            s1 = [dft_rows(zs[g]) for g in groups]
            s2 = [dft_lanes_times_filter(s1[g], o, cis[g]) for g in groups]
            s3 = [idft_lanes(s2[g]) for g in groups]
            zs = [idft_rows_and_gate(s3[g], zs[g], loaded[g][1][o], o, chs[g]) for g in groups]
        for g in groups:
            for p, cc in seqs:
                for r in range(2):
                    y_ref[cis[g][cc], 2 * p + r] = zs[g][(p, cc)][r]
        return carry

    lax.fori_loop(0, cg // (2 * NG), body, 0)
    for b in range(B):
        y2_ref[:, b * L:(b + 1) * L] = y_ref[:, b].reshape(cg, L)


def _hyena(proj_t, kspec, consts, conv_w, conv_b, bias, B, L, cg=8):
    NA = 2 * L // LANES
    hh = NA // 2
    ncg = HY_WIDTH // cg
    base = OFF_HY // cg
    smem = pl.BlockSpec(memory_space=pltpu.SMEM)
    full = lambda shp: pl.BlockSpec(shp, lambda j: (0,) * len(shp))
    blk = lambda off: pl.BlockSpec((cg, B * L), lambda j: (off + j, 0))
    tiles = pltpu.VMEM((cg, B, hh, LANES), F32)
    return pl.pallas_call(
        functools.partial(_hyena_kernel, NA=NA, cg=cg, B=B),
        grid=(ncg,),
        in_specs=[smem, smem, smem, blk(base), blk(base + ncg), blk(base + 2 * ncg),
                  pl.BlockSpec((HY_ORDER, cg, NA, 2 * LANES), lambda j: (0, j, 0, 0)),
                  full((2 * NA, NA)), full((NA, 2 * NA)), full((NA, LANES)), full((NA, LANES)),
                  full((2 * LANES, 2 * LANES)), full((2 * LANES, 2 * LANES))],
        out_specs=pl.BlockSpec((cg, B * L), lambda j: (j, 0)),
        out_shape=jax.ShapeDtypeStruct((HY_WIDTH, B * L), F32),
        scratch_shapes=[tiles, tiles, tiles, tiles],
        compiler_params=_cparams(("arbitrary",)),
        name="hyena_conv",
    )(conv_w, conv_b, bias, proj_t, proj_t, proj_t, kspec, consts["w1d"], consts["w1i"], consts["tc"], consts["ts"],
      consts["w2"], consts["w2i"])


def _outproj_kernel(yg_ref, yh_ref, yy_ref, h_ref, wo_ref, g_ref, b_ref, wrh_ref, wrl_ref, br_ref,
                    h1_ref, e_ref, w_ref, cnt_ref, *, alpha):
    tm = h_ref.shape[0]
    ts = LANES
    ns = tm // ts
    subs = [slice(s * ts, (s + 1) * ts) for s in range(ns)]
    mixes = [_dot(yg_ref[:, sl].astype(BF16), wo_ref[0:GLA_W], _TN)
             + _dot(yh_ref[:, sl].astype(BF16), wo_ref[GLA_W:GLA_W + HG_W], _TN)
             + _dot(yy_ref[:, sl].astype(BF16), wo_ref[GLA_W + HG_W:], _TN) for sl in subs]
    h1s = [_layer_norm(alpha * h_ref[sl, :] + mixes[s], g_ref[...], b_ref[...]) for s, sl in enumerate(subs)]
    for s, sl in enumerate(subs):
        h1_ref[sl, :] = h1s[s]
    his = [h1.astype(BF16) for h1 in h1s]
    los = [(h1s[s] - his[s].astype(F32)).astype(BF16) for s in range(ns)]
    lgs = [_dot(wrh_ref[...], his[s], _NT) + _dot(wrh_ref[...], los[s], _NT) + _dot(wrl_ref[...], his[s], _NT)
           for s in range(ns)]
    lg = jnp.concatenate(lgs, axis=1) + br_ref[...]
    gl = [lg[g:g + 1] for g in range(N_GROUPS)]
    gmax = functools.reduce(jnp.maximum, gl)
    gidx = jnp.full((1, tm), N_GROUPS - 1, I32)
    for g in range(N_GROUPS - 2, -1, -1):
        gidx = jnp.where(gl[g] == gmax, g, gidx)
    gsum = functools.reduce(jnp.add, [jnp.exp(x - gmax) for x in gl])
    g_val = 1.0 / gsum
    el = []
    for r in range(EXPERTS_PER_GROUP):
        acc = jnp.zeros((1, tm), F32)
        for g in range(N_GROUPS):
            row = N_GROUPS + g * EXPERTS_PER_GROUP + r
            acc = jnp.where(gidx == g, lg[row:row + 1], acc)
        el.append(acc)
    emax = functools.reduce(jnp.maximum, el)
    pe = [jnp.exp(x - emax) for x in el]
    esum = functools.reduce(jnp.add, pe)
    pe = [x / esum for x in pe]
    v1 = functools.reduce(jnp.maximum, pe)
    i1 = jnp.full((1, tm), EXPERTS_PER_GROUP - 1, I32)
    for r in range(EXPERTS_PER_GROUP - 2, -1, -1):
        i1 = jnp.where(pe[r] == v1, r, i1)
    pe2 = [jnp.where(i1 == r, -1.0, pe[r]) for r in range(EXPERTS_PER_GROUP)]
    v2 = functools.reduce(jnp.maximum, pe2)
    i2 = jnp.full((1, tm), EXPERTS_PER_GROUP - 1, I32)
    for r in range(EXPERTS_PER_GROUP - 2, -1, -1):
        i2 = jnp.where(pe2[r] == v2, r, i2)
    den = v1 + v2
    e0 = gidx * EXPERTS_PER_GROUP + i1
    e1 = gidx * EXPERTS_PER_GROUP + i2
    e_ref[...] = jnp.concatenate([e0, e1], axis=0)
    w_ref[...] = jnp.concatenate([g_val * (v1 / den), g_val * (v2 / den)], axis=0)
    eio = lax.broadcasted_iota(I32, (N_EXPERTS, tm), 0)
    hit = jnp.where((eio == e0) | (eio == e1), 1.0, 0.0)
    cnt_ref[0] = jnp.sum(hit, axis=1, keepdims=True)


def _outproj(yg, yh, yy, h, w_out, g, b, wr_t, br, alpha, tm=512):
    T, D = h.shape
    nr = wr_t.shape[0]
    wr_hi = wr_t.astype(BF16)
    wr_lo = (wr_t - wr_hi.astype(F32)).astype(BF16)
    full = lambda shp: pl.BlockSpec(shp, lambda i: (0,) * len(shp))
    return pl.pallas_call(
        functools.partial(_outproj_kernel, alpha=alpha),
        grid=(T // tm,),
        in_specs=[pl.BlockSpec((GLA_W, tm), lambda i: (0, i)),
                  pl.BlockSpec((HG_W, tm), lambda i: (0, i)),
                  pl.BlockSpec((HY_WIDTH, tm), lambda i: (0, i)),
                  pl.BlockSpec((tm, D), lambda i: (i, 0)),
                  full((D, D)), full((1, D)), full((1, D)), full((nr, D)), full((nr, D)), full((nr, 1))],
        out_specs=[pl.BlockSpec((tm, D), lambda i: (i, 0)),
                   pl.BlockSpec((TOP_K, tm), lambda i: (0, i)),
                   pl.BlockSpec((TOP_K, tm), lambda i: (0, i)),
                   pl.BlockSpec((1, N_EXPERTS, 1), lambda i: (i, 0, 0))],
        out_shape=[jax.ShapeDtypeStruct((T, D), F32),
                   jax.ShapeDtypeStruct((TOP_K, T), I32),
                   jax.ShapeDtypeStruct((TOP_K, T), F32),
                   jax.ShapeDtypeStruct((T // tm, N_EXPERTS, 1), F32)],
        compiler_params=_cparams(("arbitrary",)),
        name="outproj",
    )(yg, yh, yy, h, w_out.astype(BF16), g.reshape(1, D), b.reshape(1, D), wr_hi, wr_lo, br)


def _chunk_loop(n, fn):
    def body(c, carry):
        fn(pl.multiple_of(c * ROW_CHUNK, ROW_CHUNK))
        return carry
    lax.fori_loop(0, n, body, 0)


def _segment_copies(nch_s, k, fn):
    off = 0
    for si, rows in enumerate(SEG_SIZES):
        n = nch_s[k * len(SEG_SIZES) + si]

        def body(c, carry, off=off, rows=rows):
            fn(rows, pl.multiple_of(off + c * rows, ROW_CHUNK))
            return carry
        lax.fori_loop(0, n, body, 0)
        off = off + n * rows


def _wait_tile_copies(nch_s, tile, wait_one):
    for si, rows in enumerate(SEG_SIZES):
        tot = functools.reduce(lambda a, b: a + b, [nch_s[(tile * N_EXPERTS + e) * len(SEG_SIZES) + si]
                                                    for e in range(N_EXPERTS)])
        lax.fori_loop(0, tot, lambda c, carry, rows=rows: (wait_one(rows), carry)[1], 0)


def _dispatch_kernel(offs_s, gst_s, nch_s, tst_s, tn_s, nu_s, h_ref, e_ref, base_ref, tri_ref,
                     pos_ref, xg_hbm, xs_ref, zbuf, sem, zsem, *, tm, LP, bm, nblocks):
    i = pl.program_id(0)
    nt = pl.num_programs(0)
    slot = i % 2

    def seg_copy(s, rows, src_row, dst_row):
        return pltpu.make_async_copy(xs_ref.at[s, pl.ds(src_row, rows)],
                                     xg_hbm.at[pl.ds(dst_row, rows)], sem.at[s])

    def zero_copy(dst_row):
        return pltpu.make_async_copy(zbuf.at[pl.ds(0, ROW_CHUNK)], xg_hbm.at[pl.ds(dst_row, ROW_CHUNK)], zsem.at[0])

    def zero_block(blk):
        return pltpu.make_async_copy(zbuf, xg_hbm.at[pl.ds(pl.multiple_of(blk * bm, bm), bm)], zsem.at[0])

    def wait_tile(tile, s):
        _wait_tile_copies(nch_s, tile, lambda rows: seg_copy(s, rows, 0, 0).wait())

    @pl.when(i == 0)
    def _():
        zbuf[...] = jnp.zeros_like(zbuf)
        for e in range(N_EXPERTS):
            _chunk_loop(tn_s[e], lambda off, e=e: zero_copy(pl.multiple_of(tst_s[e] + off, ROW_CHUNK)).start())
        lax.fori_loop(nu_s[0], nblocks, lambda blk, c: (zero_block(blk).start(), c)[1], 0)
        for e in range(N_EXPERTS):
            _chunk_loop(tn_s[e], lambda off: zero_copy(0).wait())
        lax.fori_loop(nu_s[0], nblocks, lambda blk, c: (zero_block(0).wait(), c)[1], 0)

    e0, e1 = e_ref[0:1, :], e_ref[1:2, :]
    eio = lax.broadcasted_iota(I32, (N_EXPERTS, tm), 0)
    oh0, oh1 = eio == e0, eio == e1
    hit = jnp.where(oh0 | oh1, 1.0, 0.0).astype(BF16)
    posm = base_ref[0] + _dot(hit, tri_ref[...])
    pos0 = jnp.sum(jnp.where(oh0, posm, 0.0), axis=0, keepdims=True).astype(I32)
    pos1 = jnp.sum(jnp.where(oh1, posm, 0.0), axis=0, keepdims=True).astype(I32)
    pos_ref[...] = jnp.concatenate([pos0, pos1], axis=0)
    hb = h_ref[...].astype(BF16)
    rb = 256
    assert LP % rb == 0
    rio = lax.broadcasted_iota(I32, (rb, tm), 0)
    for r0 in range(0, LP, rb):
        perm = jnp.where((rio == pos0 - r0) | (rio == pos1 - r0), 1.0, 0.0).astype(BF16)
        xs_ref[slot, r0:r0 + rb] = _dot(perm, hb).astype(BF16)

    @pl.when(i > 0)
    def _():
        wait_tile(i - 1, 1 - slot)

    for e in range(N_EXPERTS):
        k = i * N_EXPERTS + e
        src0, dst0 = offs_s[k], gst_s[k]
        _segment_copies(nch_s, k, lambda rows, off, src0=src0, dst0=dst0: seg_copy(
            slot, rows, pl.multiple_of(src0 + off, ROW_CHUNK), pl.multiple_of(dst0 + off, ROW_CHUNK)).start())

    @pl.when(i == nt - 1)
    def _():
        wait_tile(i, slot)


def _moe_dispatch(h1, e_kt, tables, nblocks, bm, tm):
    T, D = h1.shape
    NT = T // tm
    nrows = nblocks * bm
    LP = TOP_K * tm + N_EXPERTS * ROW_CHUNK
    r = np.arange(tm)
    tri = jnp.asarray(r[:, None] < r[None, :], F32).astype(BF16)
    base = tables["offs"].astype(F32).reshape(NT, N_EXPERTS, 1)
    flat = lambda a: a.reshape(-1).astype(I32)
    pos, xg = pl.pallas_call(
        functools.partial(_dispatch_kernel, tm=tm, LP=LP, bm=bm, nblocks=nblocks),
        grid_spec=pltpu.PrefetchScalarGridSpec(
            num_scalar_prefetch=6,
            grid=(NT,),
            in_specs=[pl.BlockSpec((tm, D), lambda i, *_: (i, 0)),
                      pl.BlockSpec((TOP_K, tm), lambda i, *_: (0, i)),
                      pl.BlockSpec((1, N_EXPERTS, 1), lambda i, *_: (i, 0, 0)),
                      pl.BlockSpec((tm, tm), lambda i, *_: (0, 0))],
            out_specs=[pl.BlockSpec((TOP_K, tm), lambda i, *_: (0, i)),
                       pl.BlockSpec(memory_space=pl.ANY)],
            scratch_shapes=[pltpu.VMEM((2, LP, D), BF16), pltpu.VMEM((bm, D), BF16),
                            pltpu.SemaphoreType.DMA((2,)), pltpu.SemaphoreType.DMA((1,))]),
        out_shape=[jax.ShapeDtypeStruct((TOP_K, T), I32), jax.ShapeDtypeStruct((nrows, D), BF16)],
        compiler_params=_cparams(("arbitrary",)),
        name="moe_dispatch",
    )(flat(tables["offs"]), flat(tables["gstart"]), flat(tables["nch"]), flat(tables["tail_start"]),
      flat(tables["tail_n"]), tables["nused"], h1, e_kt, base, tri)
    return pos, xg


def _ffn_kernel(be_ref, nu_ref, x_ref, wg_ref, wu_ref, wd_ref, y_ref, wgb, wub, wdb):
    j = pl.program_id(0)
    used = j < nu_ref[0]

    @pl.when((j == 0) | (be_ref[j] != be_ref[jnp.maximum(j - 1, 0)]))
    def _():
        wgb[...] = wg_ref[0].astype(BF16)
        wub[...] = wu_ref[0].astype(BF16)
        wdb[...] = wd_ref[0].astype(BF16)

    @pl.when(used)
    def _():
        x = x_ref[...]
        a = _dot(x, wgb[...])
        ah = 0.5 * a
        hid = (ah + ah * jnp.tanh(ah)) * _dot(x, wub[...])
        y_ref[...] = _dot(hid.astype(BF16), wdb[...]).astype(BF16)

    @pl.when(jnp.logical_not(used))
    def _():
        y_ref[...] = jnp.zeros_like(y_ref)


def _moe_ffn(xg, block_e, nused, wg, wu, wd, bm, first_expert=0):
    nrows, D = xg.shape
    NB = nrows // bm
    DE = wg.shape[-1]
    row = lambda j, be, nu: (jnp.minimum(j, nu[0] - 1), 0)
    wsel = lambda j, be, nu: (first_expert + be[j], 0, 0)
    return pl.pallas_call(
        _ffn_kernel,
        grid_spec=pltpu.PrefetchScalarGridSpec(
            num_scalar_prefetch=2,
            grid=(NB,),
            in_specs=[pl.BlockSpec((bm, D), row),
                      pl.BlockSpec((1, D, DE), wsel),
                      pl.BlockSpec((1, D, DE), wsel),
                      pl.BlockSpec((1, DE, D), wsel)],
            out_specs=pl.BlockSpec((bm, D), lambda j, be, nu: (j, 0)),
            scratch_shapes=[pltpu.VMEM((D, DE), BF16), pltpu.VMEM((D, DE), BF16), pltpu.VMEM((DE, D), BF16)]),
        out_shape=jax.ShapeDtypeStruct((nrows, D), BF16),
        compiler_params=_cparams(("arbitrary",)),
        name="moe_ffn",
    )(block_e, nused, xg, wg, wu, wd)


def _combine_kernel(offs_s, gst_s, nch_s, yb_hbm, pos_ref, w_ref, h_ref, g_ref, b_ref, o_ref, ybl, sem,
                    *, tm, LP, alpha):
    i = pl.program_id(0)
    nt = pl.num_programs(0)
    slot = i % 2

    def seg_copy(s, rows, src_row, dst_row):
        return pltpu.make_async_copy(yb_hbm.at[pl.ds(src_row, rows)],
                                     ybl.at[s, pl.ds(dst_row, rows)], sem.at[s])

    def issue(tile, s):
        for e in range(N_EXPERTS):
            k = tile * N_EXPERTS + e
            src0, dst0 = gst_s[k], offs_s[k]
            _segment_copies(nch_s, k, lambda rows, off, src0=src0, dst0=dst0: seg_copy(
                s, rows, pl.multiple_of(src0 + off, ROW_CHUNK), pl.multiple_of(dst0 + off, ROW_CHUNK)).start())

    @pl.when(i == 0)
    def _():
        ybl[...] = jnp.zeros_like(ybl)
        issue(0, 0)

    @pl.when(i + 1 < nt)
    def _():
        issue(i + 1, 1 - slot)

    _wait_tile_copies(nch_s, i, lambda rows: seg_copy(slot, rows, 0, 0).wait())
    pos0, pos1 = pos_ref[0:1, :], pos_ref[1:2, :]
    w0, w1 = w_ref[0:1, :], w_ref[1:2, :]
    rio = lax.broadcasted_iota(I32, (LP, LANES), 0)
    yl = ybl[slot]
    for t0 in range(0, tm, LANES):
        tl = slice(t0, t0 + LANES)
        pw = (jnp.where(rio == pos0[:, tl], w0[:, tl], 0.0) + jnp.where(rio == pos1[:, tl], w1[:, tl], 0.0)).astype(BF16)
        ffn = _dot(pw, yl, _TN)
        o_ref[tl, :] = _layer_norm(alpha * h_ref[tl, :] + ffn, g_ref[...], b_ref[...])


def _moe_combine(yb, h1, pos, w_kt, tables, g, b, alpha, tm):
    T, D = h1.shape
    LP = TOP_K * tm + N_EXPERTS * ROW_CHUNK
    flat = lambda a: a.reshape(-1).astype(I32)
    return pl.pallas_call(
        functools.partial(_combine_kernel, tm=tm, LP=LP, alpha=alpha),
        grid_spec=pltpu.PrefetchScalarGridSpec(
            num_scalar_prefetch=3,
            grid=(T // tm,),
            in_specs=[pl.BlockSpec(memory_space=pl.ANY),
                      pl.BlockSpec((TOP_K, tm), lambda i, *_: (0, i)),
                      pl.BlockSpec((TOP_K, tm), lambda i, *_: (0, i)),
                      pl.BlockSpec((tm, D), lambda i, *_: (i, 0)),
                      pl.BlockSpec((1, D), lambda i, *_: (0, 0)),
                      pl.BlockSpec((1, D), lambda i, *_: (0, 0))],
            out_specs=pl.BlockSpec((tm, D), lambda i, *_: (i, 0)),
            scratch_shapes=[pltpu.VMEM((2, LP, D), BF16), pltpu.SemaphoreType.DMA((2,))]),
        out_shape=jax.ShapeDtypeStruct((T, D), F32),
        compiler_params=_cparams(("arbitrary",)),
        name="moe_combine",
    )(flat(tables["offs"]), flat(tables["gstart"]), flat(tables["nch"]), yb, pos, w_kt, h1,
      g.reshape(1, D), b.reshape(1, D))


def _dispatch_tables(cnt, bm, nblocks):
    padlen = ((cnt + ROW_CHUNK - 1) // ROW_CHUNK) * ROW_CHUNK
    offs = jnp.cumsum(padlen, axis=1) - padlen
    tot = jnp.sum(padlen, axis=0)
    region = ((tot + bm - 1) // bm) * bm
    rend = jnp.cumsum(region)
    rstart = rend - region
    gstart = rstart[None, :] + jnp.cumsum(padlen, axis=0) - padlen
    blk_row = jnp.arange(nblocks, dtype=I32)[:, None] * bm
    block_e = jnp.minimum(jnp.sum((blk_row >= rend[None, :]).astype(I32), axis=1), N_EXPERTS - 1)
    nch, rest = [], padlen
    for rows in SEG_SIZES:
        nch.append(rest // rows)
        rest = rest % rows
    return dict(offs=offs, gstart=gstart, nch=jnp.stack(nch, axis=-1), tail_start=rstart + tot,
                tail_n=(region - tot) // ROW_CHUNK, block_e=block_e.astype(I32),
                nused=(rend[-1:] // bm).astype(I32))


def _permute_in_columns(w):
    splits = (192, 192, 384, 384, 32, 384, 768, 384, 384, 768)
    offs = np.concatenate([[0], np.cumsum(splits)])
    gq, gk, gv, gg, ga, hq, hf, hi, hgt, hyu = [(int(offs[i]), int(offs[i + 1])) for i in range(10)]
    order = [hq, hi, hgt, hf, gv, gg, gq, gk, hyu, ga]
    halved = (hq, hgt, hf, gg)
    assert sum(b - a for a, b in order) == D_IN
    return jnp.concatenate([w[:, a:b] * 0.5 if (a, b) in halved else w[:, a:b] for a, b in order], axis=1)


def kernel(x, ln_in_g, ln_in_b, w_in, gla_wa2, gla_ba, gla_norm_g, hg_lb_logits, hg_norm_g, hy_conv_w, hy_conv_b, hy_w1, hy_b1, hy_freq, hy_w2, hy_b2, hy_w3, hy_bias, w_out, ln1_g, ln1_b, moe_wr_g, moe_br_g, moe_wr_e, moe_br_e, moe_w_gate, moe_w_up, moe_w_down, ln2_g, ln2_b):
    B, L, D = x.shape
    T = B * L
    depth = w_in.shape[0]
    alpha = (2 * depth) ** 0.25
    bm = 512
    tmr = 512
    nblocks = -(-(T * TOP_K + (T // tmr) * N_EXPERTS * (ROW_CHUNK - 1) + N_EXPERTS * (bm - 1)) // bm)
    consts = _dft_consts(L)

    p = jax.nn.softmax(hg_lb_logits.astype(F32), axis=0)
    lbs = jnp.cumsum(p, axis=0) - p[0:1]
    lbc = jnp.concatenate([jnp.broadcast_to((0.5 * (1.0 - lbs))[..., None], lbs.shape + (LANES,)),
                           jnp.broadcast_to(jnp.maximum(lbs, LB_FLOOR)[..., None], lbs.shape + (LANES,))], axis=-1)

    wg_all = moe_w_gate.reshape((depth * N_EXPERTS,) + moe_w_gate.shape[2:])
    wu_all = moe_w_up.reshape((depth * N_EXPERTS,) + moe_w_up.shape[2:])
    wd_all = moe_w_down.reshape((depth * N_EXPERTS,) + moe_w_down.shape[2:])
    h = x.reshape(T, D)
    for l in range(depth):
        w_t = _permute_in_columns(w_in[l]).T.astype(BF16)
        proj_t, h = _inproj(h, ln_in_g, ln_in_b, w_t, apply_ln=(l == 0))
        wa_t = jnp.swapaxes(gla_wa2[l], 1, 2)
        y_gla = _scan(proj_t, "gla", B, L, (wa_t, gla_ba[l].reshape(2, GLA_K, 1)), gla_norm_g[l].reshape(GLA_W, 1))
        y_hg = _scan(proj_t, "hg", B, L, (lbc[l],), hg_norm_g[l].reshape(HG_W, 1))
        kt = _hyena_filters(L, hy_w1[l], hy_b1[l], hy_freq[l], hy_w2[l], hy_b2[l], hy_w3[l])
        kspec = _hyena_spectrum(kt, consts, L)
        y_hy = _hyena(proj_t, kspec, consts, hy_conv_w[l], hy_conv_b[l], hy_bias[l], B, L)
        nr = N_GROUPS + N_EXPERTS
        nrp = -(-nr // 8) * 8
        wr_t = jnp.zeros((nrp, D), F32).at[:nr].set(jnp.concatenate([moe_wr_g[l], moe_wr_e[l]], axis=1).T)
        br = jnp.zeros((nrp, 1), F32).at[:nr, 0].set(jnp.concatenate([moe_br_g[l], moe_br_e[l]]))
        h1, e_kt, w_kt, cnt = _outproj(y_gla, y_hg, y_hy, h, w_out[l], ln1_g[l], ln1_b[l], wr_t, br, alpha, tm=tmr)
        tables = _dispatch_tables(cnt.reshape(T // tmr, N_EXPERTS).astype(I32), bm, nblocks)
        pos, xg = _moe_dispatch(h1, e_kt, tables, nblocks, bm, tmr)
        yb = _moe_ffn(xg, tables["block_e"], tables["nused"], wg_all, wu_all, wd_all, bm, first_expert=l * N_EXPERTS)
        h = _moe_combine(yb, h1, pos, w_kt, tables, ln2_g[l], ln2_b[l], alpha, tmr)
    return h.reshape(B, L, D)
```

```python
import functools
import math

import numpy as np
import jax
import jax.numpy as jnp
from jax import lax
from jax.experimental import pallas as pl
from jax.experimental.pallas import tpu as pltpu

F32 = jnp.float32
BF16 = jnp.bfloat16
I32 = jnp.int32

GLA_HEADS, GLA_DK, GLA_DV, GLA_LOWRANK, GLA_TAU = 6, 32, 64, 16, 16.0
HG_HEADS, HG_DK, HG_DV = 6, 64, 64
HY_WIDTH, HY_ORDER, HY_EMB, HY_FFN, HY_INNER = 256, 2, 33, 64, 2
HY_FAST_DECAY, HY_SLOW_DECAY, HY_TARGET = 0.3, 1.5, 1e-2
N_GROUPS, EXPERTS_PER_GROUP = 4, 4
N_EXPERTS = N_GROUPS * EXPERTS_PER_GROUP
TOP_K = 2
LN_EPS, RMS_EPS, LB_FLOOR = 1e-5, 1e-6, 1e-30
LOG2E = 1.4426950408889634

LANES = 128
SCAN_CHUNK = LANES
ROW_CHUNK = 16
SEG_SIZES = (64, 32, ROW_CHUNK)
VMEM_LIMIT = 56 * 1024 * 1024

GLA_W = GLA_HEADS * GLA_DV
GLA_K = GLA_HEADS * GLA_DK
HG_W = HG_HEADS * HG_DV
HG_K = HG_HEADS * HG_DK
OFF_HQ, OFF_HI, OFF_HGT, OFF_HF = 0, 384, 768, 1152
OFF_GV, OFF_GG, OFF_GQ, OFF_GK, OFF_HY, OFF_GA = 1920, 2304, 2688, 2880, 3072, 3840
D_IN = 3872


def _dot(a, b, dims=(((1,), (0,)), ((), ())), precision=None):
    return lax.dot_general(a, b, dims, preferred_element_type=F32, precision=precision)


_NT = (((1,), (1,)), ((), ()))
_TN = (((0,), (0,)), ((), ()))


def _layer_norm(x, g, b):
    mu = jnp.mean(x, axis=-1, keepdims=True)
    xc = x - mu
    var = jnp.mean(xc * xc, axis=-1, keepdims=True)
    return xc * lax.rsqrt(var + LN_EPS) * g + b


def _log_sigmoid(x):
    return jnp.minimum(x, 0.0) - jnp.log(1.0 + jnp.exp(-jnp.abs(x)))


def _cparams(sem):
    return pltpu.CompilerParams(dimension_semantics=sem, vmem_limit_bytes=VMEM_LIMIT)


def _inproj_kernel(x_ref, g_ref, b_ref, w_ref, *outs, apply_ln):
    x = x_ref[...]
    if apply_ln:
        x = _layer_norm(x, g_ref[...], b_ref[...])
        outs[1][...] = x
    outs[0][...] = _dot(w_ref[...], x.astype(BF16), _NT)


def _inproj(x, g, b, w_t, apply_ln, tm=512):
    T, D = x.shape
    n_out = w_t.shape[0]
    out_shape = [jax.ShapeDtypeStruct((n_out, T), F32)]
    out_specs = [pl.BlockSpec((n_out, tm), lambda i: (0, i))]
    if apply_ln:
        out_shape.append(jax.ShapeDtypeStruct((T, D), F32))
        out_specs.append(pl.BlockSpec((tm, D), lambda i: (i, 0)))
    res = pl.pallas_call(
        functools.partial(_inproj_kernel, apply_ln=apply_ln),
        grid=(T // tm,),
        in_specs=[pl.BlockSpec((tm, D), lambda i: (i, 0)),
                  pl.BlockSpec((1, D), lambda i: (0, 0)),
                  pl.BlockSpec((1, D), lambda i: (0, 0)),
                  pl.BlockSpec((n_out, D), lambda i: (0, 0))],
        out_specs=out_specs,
        out_shape=out_shape,
        compiler_params=_cparams(("arbitrary",)),
        name="inproj",
    )(x, g.reshape(1, D), b.reshape(1, D), w_t)
    return res if apply_ln else (res[0], x)


def _scan_kernel(*refs, mode, H, K, V, TB, NCB):
    if mode == "gla":
        (qf_ref, qb_ref, kf_ref, kb_ref, vf_ref, vb_ref, gaf_ref, gab_ref, wa_ref, ba_ref,
         gtf_ref, gtb_ref, ng_ref, y_ref, s_ref, oacc_ref, oi_ref, qg_ref, ut_ref, dr_ref) = refs
        q_refs, k_refs, ga_refs = (qf_ref, qb_ref), (kf_ref, kb_ref), (gaf_ref, gab_ref)
    else:
        (qf_ref, qb_ref, zf_ref, zb_ref, vf_ref, vb_ref, lbc_ref,
         gtf_ref, gtb_ref, ng_ref, y_ref, s_ref, oacc_ref, oi_ref, qg_ref, ut_ref, dr_ref) = refs
        q_refs, z_refs = (qf_ref, qb_ref), (zf_ref, zb_ref)
    v_refs, gate_refs = (vf_ref, vb_ref), (gtf_ref, gtb_ref)
    C = SCAN_CHUNK
    half = C // 2
    nchunks = TB // C
    n = pl.program_id(1)
    blocks = (n, NCB - 1 - n)

    @pl.when(n == 0)
    def _():
        s_ref[...] = jnp.zeros_like(s_ref)

    def gates(d, sl):
        if mode == "gla":
            wa, ga = wa_ref[d], ga_refs[d][:, sl]
            wa_hi, ga_hi = wa.astype(BF16), ga.astype(BF16)
            wa_lo = (wa - wa_hi.astype(F32)).astype(BF16)
            ga_lo = (ga - ga_hi.astype(F32)).astype(BF16)
            a = _dot(wa_hi, ga_hi) + _dot(wa_hi, ga_lo) + _dot(wa_lo, ga_hi) + ba_ref[d]
            g = _log_sigmoid(a) * (LOG2E / GLA_TAU)
            q = q_refs[d][:, sl] * (K ** -0.5)
            k = k_refs[d][:, sl]
        else:
            half_lb, lb_floor = lbc_ref[d, :, 0:C], lbc_ref[d, :, C:2 * C]
            ht = half_lb * jnp.tanh(z_refs[d][:, sl])
            g = jnp.log2(half_lb + ht + lb_floor)
            k = half_lb - ht
            hq = q_refs[d][:, sl]
            q = hq + hq * jnp.tanh(hq)
        return q, k, g

    r = lax.broadcasted_iota(I32, (C, C), 0)
    c = lax.broadcasted_iota(I32, (C, C), 1)
    lane_lo = lax.broadcasted_iota(I32, (1, C), 1) < half
    sign_lo = jnp.where(lane_lo, 1.0, -1.0)
    consts = (((r <= c).astype(BF16), r <= c, (r < half) & (c >= half), (C - 1, half, half // 2, half + half // 2)),
              ((r >= c).astype(BF16), r >= c, (r >= half) & (c < half),
               (0, half - 1, half // 2 - 1, half + half // 2 - 1)))

    heads = [(slice(h * K, (h + 1) * K), slice(h * V, (h + 1) * V)) for h in range(H)]

    def cumulate(d, g):
        cum = consts[d][0]
        g1 = g.astype(BF16)
        g2 = (g - g1.astype(F32)).astype(BF16)
        G = _dot(g1, cum) + _dot(g2, cum)
        ge_row = _dot(ones8, g1, _NT) + _dot(ones8, g2, _NT)
        return G, ge_row

    def scale(d, sl, q, k, G, ge_row):
        _, _, _, (c_end, c_mid, c_a, c_b) = consts[d]
        v = v_refs[d][:, sl].astype(BF16)
        g_end = G[:, c_end:c_end + 1]
        g_mid = G[:, c_mid:c_mid + 1]
        e2 = G - jnp.where(lane_lo, G[:, c_a:c_a + 1], G[:, c_b:c_b + 1])
        q2 = (q * jnp.exp2(e2)).astype(BF16)
        k2 = (k * jnp.exp2(-e2)).astype(BF16)
        dm = G - g_mid
        x1 = jnp.exp2(dm * (sign_lo if d else -sign_lo))
        q1 = (q * x1).astype(BF16)
        k1 = (k * x1).astype(BF16)
        qg = (q * jnp.exp2(G)).astype(BF16)
        kd = (k * jnp.exp2(g_end - G)).astype(BF16)
        x = dict(qg=qg, kd=kd, v=v, dec_row=jnp.exp2(ge_row))
        if merge_scores:
            first = lambda a: jnp.where(lane_lo, a, jnp.zeros_like(a))
            second = lambda a: jnp.where(lane_lo, jnp.zeros_like(a), a)
            x["kparts"] = ((second if d else first)(k1), first(k2), second(k2))
            x["qparts"] = ((first if d else second)(q1), first(q2), second(q2))
        else:
            x.update(q1=q1, k1=k1, q2=q2, k2=k2)
        return x

    ones8 = jnp.ones((8, C), BF16)
    merge_scores = 3 * K <= LANES

    def scores(x):
        if merge_scores:
            return [_dot(jnp.concatenate([p[rk] for p in x["kparts"]], axis=0),
                         jnp.concatenate([p[rk] for p in x["qparts"]], axis=0), _TN) for rk, _ in heads]
        return [(_dot(x["k1"][rk], x["q1"][rk], _TN), _dot(x["k2"][rk], x["q2"][rk], _TN)) for rk, _ in heads]

    def masked(d, ps):
        _, tri, off1, _ = consts[d]
        if merge_scores:
            return [jnp.where(tri, p, 0.0).astype(BF16) for p in ps]
        return [jnp.where(off1, p1, jnp.where(tri, p2, 0.0)).astype(BF16) for p1, p2 in ps]

    def park(d, ci, x, pm):
        oi_ref[d, ci] = jnp.concatenate([_dot(x["v"][rv], pm[h]) for h, (_, rv) in enumerate(heads)], axis=0)
        qg_ref[d, ci] = x["qg"]
        dr_ref[d, ci] = x["dec_row"]
        for h, (rk, rv) in enumerate(heads):
            ut_ref[d, ci, h] = _dot(x["v"][rv], x["kd"][rk], _NT)

    def chunk_slices(ci):
        offs = (pl.multiple_of(ci * C, C), pl.multiple_of((nchunks - 1 - ci) * C, C))
        return offs, [pl.ds(offs[d], C) for d in range(2)]

    def independent(it, carry):
        jobs = [(d, U * it + u, chunk_slices(U * it + u)[1][d]) for u in range(U) for d in range(2)]
        qkg = [gates(d, sl) for d, _, sl in jobs]
        cums = [cumulate(d, qkg[i][2]) for i, (d, _, _) in enumerate(jobs)]
        xs = [scale(d, sl, qkg[i][0], qkg[i][1], *cums[i]) for i, (d, _, sl) in enumerate(jobs)]
        scs = [scores(x) for x in xs]
        pms = [masked(d, scs[i]) for i, (d, _, _) in enumerate(jobs)]
        for i, (d, ci, _) in enumerate(jobs):
            park(d, ci, xs[i], pms[i])
        return carry

    U = 4
    assert nchunks % U == 0
    lax.fori_loop(0, nchunks // U, independent, 0)

    def carried(ci):
        st = [[s_ref[d, h] for h in range(H)] for d in range(2)]
        inter = [[_dot(st[d][h].astype(BF16), qg_ref[d, ci, rk, :]) for h, (rk, _) in enumerate(heads)]
                 for d in range(2)]
        for d in range(2):
            for h, (rk, _) in enumerate(heads):
                s_ref[d, h] = st[d][h] * dr_ref[d, ci, 0:1, rk] + ut_ref[d, ci, h]
        return [oi_ref[d, ci] + jnp.concatenate(inter[d], axis=0) for d in range(2)]

    def finish(o, gate):
        t = jnp.tanh(gate)
        act = gate + gate * t if mode == "gla" else 0.5 + 0.5 * t
        ys = []
        for h in range(H):
            oh = o[h * V:(h + 1) * V]
            ms = jnp.mean(oh * oh, axis=0, keepdims=True)
            ys.append(oh * lax.rsqrt(ms + RMS_EPS))
        return (jnp.concatenate(ys, axis=0) * ng_ref[...] * act).astype(y_ref.dtype)

    def chunk(ci, carry):
        offs, sls = chunk_slices(ci)
        tsls = [pl.ds(pl.multiple_of(blocks[d] * TB + offs[d], C), C) for d in range(2)]
        o = carried(ci)

        @pl.when(2 * n < NCB)
        def _():
            for d in range(2):
                oacc_ref[:, tsls[d]] = o[d]

        @pl.when(2 * n >= NCB)
        def _():
            tot = [o[d] + oacc_ref[:, tsls[d]] for d in range(2)]
            for d in range(2):
                y_ref[:, tsls[d]] = finish(tot[d], gate_refs[d][:, sls[d]])
        return carry

    lax.fori_loop(0, nchunks, chunk, 0)


def _scan(proj_t, mode, B, L, extra, norm_g, TB=512):
    NCB = L // TB
    nch = TB // SCAN_CHUNK
    assert NCB % 2 == 0
    if mode == "gla":
        H, K, V = GLA_HEADS, GLA_DK, GLA_DV
    else:
        H, K, V = HG_HEADS, HG_DK, HG_DV
    HK, HV = H * K, H * V
    cf = lambda b, n: b * NCB + n
    cb = lambda b, n: b * NCB + NCB - 1 - n
    gf = lambda b, n: b * NCB + jnp.maximum(n, NCB // 2)
    gb = lambda b, n: b * NCB + jnp.minimum(NCB - 1 - n, NCB // 2 - 1)

    def pair(rows, off, fwd=cf, bwd=cb, dir_step=0):
        return [pl.BlockSpec((rows, TB), lambda b, n: (off // rows, fwd(b, n))),
                pl.BlockSpec((rows, TB), lambda b, n: (off // rows + dir_step, bwd(b, n)))]

    whole = lambda shp: pl.BlockSpec(shp, lambda b, n: (0,) * len(shp))
    if mode == "gla":
        wa_t, ba = extra
        in_specs = (pair(HK, OFF_GQ) + pair(HK, OFF_GK) + pair(HV, OFF_GV)
                    + pair(GLA_LOWRANK, OFF_GA, dir_step=1)
                    + [whole((2, HK, GLA_LOWRANK)), whole((2, HK, 1))]
                    + pair(HV, OFF_GG, gf, gb) + [whole((HV, 1))])
        args = (proj_t,) * 8 + (wa_t, ba, proj_t, proj_t, norm_g)
    else:
        (lbc,) = extra
        in_specs = (pair(HK, OFF_HQ) + pair(HK, OFF_HF, dir_step=1) + pair(HV, OFF_HI)
                    + [whole((2, HK, 2 * LANES))] + pair(HV, OFF_HGT, gf, gb) + [whole((HV, 1))])
        args = (proj_t,) * 6 + (lbc, proj_t, proj_t, norm_g)
    return pl.pallas_call(
        functools.partial(_scan_kernel, mode=mode, H=H, K=K, V=V, TB=TB, NCB=NCB),
        grid=(B, NCB),
        in_specs=in_specs,
        out_specs=pl.BlockSpec((HV, L), lambda b, n: (0, b)),
        out_shape=jax.ShapeDtypeStruct((HV, B * L), BF16),
        scratch_shapes=[pltpu.VMEM((2, H, V, K), F32), pltpu.VMEM((HV, L), F32),
                        pltpu.VMEM((2, nch, HV, SCAN_CHUNK), F32), pltpu.VMEM((2, nch, HK, SCAN_CHUNK), BF16),
                        pltpu.VMEM((2, nch, H, V, K), F32), pltpu.VMEM((2, nch, 8, HK), F32)],
        compiler_params=_cparams(("arbitrary", "arbitrary")),
        name="scan_" + mode,
    )(*args)


def _dft_consts(L):
    N = 2 * L
    NA = N // LANES
    a = np.arange(NA)[:, None] * np.arange(NA)[None, :]
    ca, sa = np.cos(2 * np.pi * a / NA), np.sin(2 * np.pi * a / NA)
    hh = NA // 2
    w1d = np.block([[ca[:, :hh], sa[:, :hh]], [-sa[:, :hh], ca[:, :hh]]])
    w1f = np.concatenate([ca, -sa], axis=0)
    w1i = np.block([[ca[:hh, :], -sa[:hh, :]], [sa[:hh, :], ca[:hh, :]]])
    bb = np.arange(LANES)[:, None] * np.arange(LANES)[None, :]
    cb, sb = np.cos(2 * np.pi * bb / LANES), np.sin(2 * np.pi * bb / LANES)
    w2 = np.block([[cb, -sb], [sb, cb]])
    w2i = np.block([[cb, sb], [-sb, cb]])
    tw = np.arange(NA)[:, None] * np.arange(LANES)[None, :]
    tc, ts = np.cos(2 * np.pi * tw / N), np.sin(2 * np.pi * tw / N)
    bf = lambda m: jnp.asarray(m, dtype=F32).astype(BF16)
    return dict(w1d=bf(w1d), w1f=bf(w1f), w1i=bf(w1i), w2=bf(w2), w2i=bf(w2i),
                tc=jnp.asarray(tc, F32), ts=jnp.asarray(ts, F32))


def _pos_features(L):
    t = np.linspace(0.0, 1.0, L)
    w = 2.0 * np.pi * np.arange(L) / L
    bands = np.linspace(1e-4, (HY_EMB - 1) // 2 - 1, (HY_EMB - 1) // 2)
    z = np.concatenate([t[None, :], np.cos(bands[:, None] * w[None, :]), -np.sin(bands[:, None] * w[None, :])], axis=0)
    kp = -(-HY_EMB // 8) * 8
    z = np.concatenate([z, np.zeros((kp - HY_EMB, L))], axis=0)
    t_rev = t[(L - np.arange(L)) % L]
    mask = (np.arange(L) >= 1).astype(np.float64)
    return (jnp.asarray(z, F32), jnp.asarray(t[None, :], F32),
            jnp.asarray(t_rev[None, :], F32), jnp.asarray(mask[None, :], F32))


def _filter_kernel(z_ref, t_ref, tr_ref, m_ref, w1_ref, b1_ref, fr_ref, w2_ref, b2_ref,
                   w3f_ref, w3b_ref, dl_ref, out_ref, h_ref, *, L):
    first = (pl.program_id(0) == 0) & (pl.program_id(1) == 0)
    hi = lax.Precision.HIGHEST

    @pl.when(first)
    def _():
        fr = fr_ref[...]
        h = jnp.sin(fr * (_dot(w1_ref[...], z_ref[...], precision=hi) + b1_ref[...]))
        for i in range(HY_INNER):
            h = jnp.sin(fr * (_dot(w2_ref[i], h, precision=hi) + b2_ref[i]))
        hb = h.astype(BF16)
        h_ref[0] = hb
        src = lax.broadcasted_iota(I32, (LANES, LANES), 0)
        dst = lax.broadcasted_iota(I32, (LANES, LANES), 1)
        flip_shift = jnp.where((dst >= 1) & (src == LANES - dst), 1.0, 0.0).astype(BF16)
        lane0 = jnp.where((dst == 0) & (src == 0), 1.0, 0.0).astype(BF16)
        nb = L // LANES
        tile = lambda b: hb[:, b * LANES:(b + 1) * LANES]
        for jb in range(nb):
            blk = _dot(tile(nb - 1 - jb), flip_shift) + _dot(tile((nb - jb) % nb), lane0)
            h_ref[1, :, jb * LANES:(jb + 1) * LANES] = blk.astype(BF16)

    ad = jnp.abs(dl_ref[...])
    kf = _dot(w3f_ref[...].astype(BF16), h_ref[0]) * jnp.exp(-t_ref[...] * ad)
    kb = _dot(w3b_ref[...].astype(BF16), h_ref[1]) * jnp.exp(-tr_ref[...] * ad) * m_ref[...]
    den = jnp.sum(jnp.abs(kf), axis=1, keepdims=True) + jnp.sum(jnp.abs(kb), axis=1, keepdims=True)
    scale = 1.0 / (jnp.maximum(den, 1e-12) * (2.0 * L))
    out_ref[0, :, 0:L] = kf * scale
    out_ref[0, :, L:2 * L] = kb * scale


def _hyena_filters(L, w1, b1, freq, w2, b2, w3, cg=64):
    z, t, t_rev, mask = _pos_features(L)
    kp = z.shape[0]
    w1_t = jnp.zeros((HY_FFN, kp), F32).at[:, :HY_EMB].set(w1.T)
    w2_t = jnp.swapaxes(w2, 1, 2)
    w3_t = w3.T
    max_decay = math.log(HY_TARGET) / HY_FAST_DECAY
    min_decay = math.log(HY_TARGET) / HY_SLOW_DECAY
    deltas = jnp.asarray(np.linspace(min_decay, max_decay, HY_WIDTH).reshape(HY_WIDTH, 1), F32)
    ncg = HY_WIDTH // cg
    full = lambda shp: pl.BlockSpec(shp, lambda o, j: (0,) * len(shp))
    return pl.pallas_call(
        functools.partial(_filter_kernel, L=L),
        grid=(HY_ORDER, ncg),
        in_specs=[full((kp, L)), full((1, L)), full((1, L)), full((1, L)),
                  full((HY_FFN, kp)), full((HY_FFN, 1)), full((HY_FFN, 1)),
                  full((HY_INNER, HY_FFN, HY_FFN)), full((HY_INNER, HY_FFN, 1)),
                  pl.BlockSpec((cg, HY_FFN), lambda o, j: (o * 2 * ncg + j, 0)),
                  pl.BlockSpec((cg, HY_FFN), lambda o, j: (o * 2 * ncg + ncg + j, 0)),
                  pl.BlockSpec((cg, 1), lambda o, j: (j, 0))],
        out_specs=pl.BlockSpec((1, cg, 2 * L), lambda o, j: (o, j, 0)),
        out_shape=jax.ShapeDtypeStruct((HY_ORDER, HY_WIDTH, 2 * L), F32),
        scratch_shapes=[pltpu.VMEM((2, HY_FFN, L), BF16)],
        compiler_params=_cparams(("arbitrary", "arbitrary")),
        name="hyena_filter",
    )(z, t, t_rev, mask, w1_t, b1.reshape(HY_FFN, 1), freq.reshape(HY_FFN, 1), w2_t,
      b2.reshape(HY_INNER, HY_FFN, 1), w3_t, w3_t, deltas)


def _spectrum_kernel(k2_ref, w1_ref, tc_ref, ts_ref, w2_ref, out_ref, k_ref, *, NA, cg):
    k_ref[0] = k2_ref[0].reshape(cg, NA, LANES)
    tc2 = jnp.concatenate([tc_ref[...]] * 2, axis=1)
    ts2 = jnp.concatenate([ts_ref[...]] * 2, axis=1)
    G = 4

    def body(it, carry):
        lhs = []
        for pr in range(G // 2):
            rhs = jnp.concatenate([k_ref[0, G * it + 2 * pr + cc] for cc in range(2)], axis=1)
            a = _dot(w1_ref[...], rhs.astype(BF16))
            a_re, a_im = a[:NA], a[NA:]
            b_re = a_re * tc2 + a_im * ts2
            b_im = a_im * tc2 - a_re * ts2
            lhs += [jnp.concatenate([b_re[:, cc * LANES:(cc + 1) * LANES], b_im[:, cc * LANES:(cc + 1) * LANES]], axis=1)
                    for cc in range(2)]
        x = _dot(jnp.concatenate(lhs, axis=0).astype(BF16), w2_ref[...])
        for i in range(G):
            out_ref[0, G * it + i] = x[i * NA:(i + 1) * NA]
        return carry
    lax.fori_loop(0, cg // G, body, 0)


def _hyena_spectrum(kt, consts, L, cg=32):
    NA = 2 * L // LANES
    full = lambda shp: pl.BlockSpec(shp, lambda o, j: (0,) * len(shp))
    return pl.pallas_call(
        functools.partial(_spectrum_kernel, NA=NA, cg=cg),
        grid=(HY_ORDER, HY_WIDTH // cg),
        in_specs=[pl.BlockSpec((1, cg, 2 * L), lambda o, j: (o, j, 0)),
                  full((2 * NA, NA)), full((NA, LANES)), full((NA, LANES)), full((2 * LANES, 2 * LANES))],
        out_specs=pl.BlockSpec((1, cg, NA, 2 * LANES), lambda o, j: (o, j, 0, 0)),
        out_shape=jax.ShapeDtypeStruct((HY_ORDER, HY_WIDTH, NA, 2 * LANES), F32),
        scratch_shapes=[pltpu.VMEM((1, cg, NA, LANES), F32)],
        compiler_params=_cparams(("arbitrary", "arbitrary")),
        name="hyena_spectrum",
    )(kt, consts["w1f"], consts["tc"], consts["ts"], consts["w2"])


def _hyena_kernel(cw_ref, cb_ref, hb_ref, v2_ref, x12_ref, x22_ref, ks_ref, w1d_ref, w1i_ref, tc_ref, ts_ref,
                  w2_ref, w2i_ref, y2_ref, v_ref, x1_ref, x2_ref, y_ref, *, NA, cg, B):
    hh = NA // 2
    L = hh * LANES
    j = pl.program_id(0)
    for src, dst in ((v2_ref, v_ref), (x12_ref, x1_ref), (x22_ref, x2_ref)):
        for b in range(B):
            dst[:, b] = src[:, b * L:(b + 1) * L].reshape(cg, hh, LANES)
    row = lax.broadcasted_iota(I32, (hh, LANES), 0)
    lane = lax.broadcasted_iota(I32, (hh, LANES), 1)
    first = (row == 0) & (lane == 0)
    last = (row == hh - 1) & (lane == LANES - 1)

    def short_conv(x, ch):
        r1 = pltpu.roll(x, 1, 1)
        prev = jnp.where(lane == 0, pltpu.roll(r1, 1, 0), r1)
        prev = jnp.where(first, 0.0, prev)
        r2 = pltpu.roll(x, LANES - 1, 1)
        nxt = jnp.where(lane == LANES - 1, pltpu.roll(r2, hh - 1, 0), r2)
        nxt = jnp.where(last, 0.0, nxt)
        return cw_ref[0, ch] * prev + cw_ref[1, ch] * x + cw_ref[2, ch] * nxt + cb_ref[ch]

    P = B // 2
    tc, ts = tc_ref[...], ts_ref[...]
    tc2 = jnp.concatenate([tc, tc], axis=1)
    ts2 = jnp.concatenate([ts, ts], axis=1)
    lane2 = lambda x, cc: x[:, cc * LANES:(cc + 1) * LANES]

    seqs = [(p, cc) for p in range(P) for cc in range(2)]
    gate_refs = (x1_ref, x2_ref)

    def load(cis, chs):
        z = {(p, cc): [short_conv(v_ref[cis[cc], 2 * p + r], chs[cc]) for r in range(2)] for p, cc in seqs}
        gates = [{(p, cc): [short_conv(gate_refs[o][cis[cc], 2 * p + r], (o + 1) * HY_WIDTH + chs[cc])
                            for r in range(2)] for p, cc in seqs} for o in range(HY_ORDER)]
        return z, gates

    def dft_rows(z):
        lhs = []
        for p in range(P):
            rhs = jnp.concatenate([jnp.concatenate(z[(p, cc)], axis=0) for cc in range(2)], axis=1)
            a = _dot(w1d_ref[...], rhs.astype(BF16))
            a_re, a_im = a[:NA], a[NA:]
            b_re = a_re * tc2 + a_im * ts2
            b_im = a_im * tc2 - a_re * ts2
            lhs += [jnp.concatenate([lane2(b_re, cc), lane2(b_im, cc)], axis=1) for cc in range(2)]
        return jnp.concatenate(lhs, axis=0).astype(BF16)

    def dft_lanes_times_filter(lhs, o, cis):
        x = _dot(lhs, w2_ref[...])
        ys = []
        for idx, (p, cc) in enumerate(seqs):
            xb = x[idx * NA:(idx + 1) * NA]
            ks = ks_ref[o, cis[cc]]
            x_re, x_im = lane2(xb, 0), lane2(xb, 1)
            k_re, k_im = lane2(ks, 0), lane2(ks, 1)
            ys.append(jnp.concatenate([x_re * k_re - x_im * k_im, x_re * k_im + x_im * k_re], axis=1))
        return jnp.concatenate(ys, axis=0).astype(BF16)

    def idft_lanes(ys):
        bq = _dot(ys, w2i_ref[...])
        out = []
        for p in range(P):
            cr, cim = [], []
            for cc in range(2):
                blk = bq[(2 * p + cc) * NA:(2 * p + cc + 1) * NA]
                b_re, b_im = lane2(blk, 0), lane2(blk, 1)
                cr.append(b_re * tc - b_im * ts)
                cim.append(b_re * ts + b_im * tc)
            out.append(jnp.concatenate([jnp.concatenate(cr, axis=1), jnp.concatenate(cim, axis=1)], axis=0).astype(BF16))
        return out

    def idft_rows_and_gate(rhs, z, gates, o, chs):
        znew = {}
        for p in range(P):
            conv = _dot(w1i_ref[...], rhs[p])
            for cc in range(2):
                bias = hb_ref[o, chs[cc]]
                znew[(p, cc)] = [gates[(p, cc)][r] * (lane2(conv, cc)[r * hh:(r + 1) * hh] + z[(p, cc)][r] * bias)
                                 for r in range(2)]
        return znew

    NG = 2

    def body(it, carry):
        groups = range(NG)
        cis = [[2 * NG * it + 2 * g + cc for cc in range(2)] for g in groups]
        chs = [[j * cg + ci for ci in cis[g]] for g in groups]
        loaded = [load(cis[g], chs[g]) for g in groups]
        zs = [loaded[g][0] for g in groups]
        for o in range(HY_ORDER):
            s1 = [dft_rows(zs[g]) for g in groups]
            s2 = [dft_lanes_times_filter(s1[g], o, cis[g]) for g in groups]
            s3 = [idft_lanes(s2[g]) for g in groups]
            zs = [idft_rows_and_gate(s3[g], zs[g], loaded[g][1][o], o, chs[g]) for g in groups]
        for g in groups:
            for p, cc in seqs:
                for r in range(2):
                    y_ref[cis[g][cc], 2 * p + r] = zs[g][(p, cc)][r]
        return carry

    lax.fori_loop(0, cg // (2 * NG), body, 0)
    for b in range(B):
        y2_ref[:, b * L:(b + 1) * L] = y_ref[:, b].reshape(cg, L)


def _hyena(proj_t, kspec, consts, conv_w, conv_b, bias, B, L, cg=8):
    NA = 2 * L // LANES
    hh = NA // 2
    ncg = HY_WIDTH // cg
    base = OFF_HY // cg
    smem = pl.BlockSpec(memory_space=pltpu.SMEM)
    full = lambda shp: pl.BlockSpec(shp, lambda j: (0,) * len(shp))
    blk = lambda off: pl.BlockSpec((cg, B * L), lambda j: (off + j, 0))
    tiles = pltpu.VMEM((cg, B, hh, LANES), F32)
    return pl.pallas_call(
        functools.partial(_hyena_kernel, NA=NA, cg=cg, B=B),
        grid=(ncg,),
        in_specs=[smem, smem, smem, blk(base), blk(base + ncg), blk(base + 2 * ncg),
                  pl.BlockSpec((HY_ORDER, cg, NA, 2 * LANES), lambda j: (0, j, 0, 0)),
                  full((2 * NA, NA)), full((NA, 2 * NA)), full((NA, LANES)), full((NA, LANES)),
                  full((2 * LANES, 2 * LANES)), full((2 * LANES, 2 * LANES))],
        out_specs=pl.BlockSpec((cg, B * L), lambda j: (j, 0)),
        out_shape=jax.ShapeDtypeStruct((HY_WIDTH, B * L), F32),
        scratch_shapes=[tiles, tiles, tiles, tiles],
        compiler_params=_cparams(("arbitrary",)),
        name="hyena_conv",
    )(conv_w, conv_b, bias, proj_t, proj_t, proj_t, kspec, consts["w1d"], consts["w1i"], consts["tc"], consts["ts"],
      consts["w2"], consts["w2i"])


def _outproj_kernel(yg_ref, yh_ref, yy_ref, h_ref, wo_ref, g_ref, b_ref, wrh_ref, wrl_ref, br_ref,
                    h1_ref, e_ref, w_ref, cnt_ref, *, alpha):
    tm = h_ref.shape[0]
    ts = LANES
    ns = tm // ts
    subs = [slice(s * ts, (s + 1) * ts) for s in range(ns)]
    mixes = [_dot(yg_ref[:, sl].astype(BF16), wo_ref[0:GLA_W], _TN)
             + _dot(yh_ref[:, sl].astype(BF16), wo_ref[GLA_W:GLA_W + HG_W], _TN)
             + _dot(yy_ref[:, sl].astype(BF16), wo_ref[GLA_W + HG_W:], _TN) for sl in subs]
    h1s = [_layer_norm(alpha * h_ref[sl, :] + mixes[s], g_ref[...], b_ref[...]) for s, sl in enumerate(subs)]
    for s, sl in enumerate(subs):
        h1_ref[sl, :] = h1s[s]
    his = [h1.astype(BF16) for h1 in h1s]
    los = [(h1s[s] - his[s].astype(F32)).astype(BF16) for s in range(ns)]
    lgs = [_dot(wrh_ref[...], his[s], _NT) + _dot(wrh_ref[...], los[s], _NT) + _dot(wrl_ref[...], his[s], _NT)
           for s in range(ns)]
    lg = jnp.concatenate(lgs, axis=1) + br_ref[...]
    gl = [lg[g:g + 1] for g in range(N_GROUPS)]
    gmax = functools.reduce(jnp.maximum, gl)
    gidx = jnp.full((1, tm), N_GROUPS - 1, I32)
    for g in range(N_GROUPS - 2, -1, -1):
        gidx = jnp.where(gl[g] == gmax, g, gidx)
    gsum = functools.reduce(jnp.add, [jnp.exp(x - gmax) for x in gl])
    g_val = 1.0 / gsum
    el = []
    for r in range(EXPERTS_PER_GROUP):
        acc = jnp.zeros((1, tm), F32)
        for g in range(N_GROUPS):
            row = N_GROUPS + g * EXPERTS_PER_GROUP + r
            acc = jnp.where(gidx == g, lg[row:row + 1], acc)
        el.append(acc)
    emax = functools.reduce(jnp.maximum, el)
    pe = [jnp.exp(x - emax) for x in el]
    esum = functools.reduce(jnp.add, pe)
    pe = [x / esum for x in pe]
    v1 = functools.reduce(jnp.maximum, pe)
    i1 = jnp.full((1, tm), EXPERTS_PER_GROUP - 1, I32)
    for r in range(EXPERTS_PER_GROUP - 2, -1, -1):
        i1 = jnp.where(pe[r] == v1, r, i1)
    pe2 = [jnp.where(i1 == r, -1.0, pe[r]) for r in range(EXPERTS_PER_GROUP)]
    v2 = functools.reduce(jnp.maximum, pe2)
    i2 = jnp.full((1, tm), EXPERTS_PER_GROUP - 1, I32)
    for r in range(EXPERTS_PER_GROUP - 2, -1, -1):
        i2 = jnp.where(pe2[r] == v2, r, i2)
    den = v1 + v2
    e0 = gidx * EXPERTS_PER_GROUP + i1
    e1 = gidx * EXPERTS_PER_GROUP + i2
    e_ref[...] = jnp.concatenate([e0, e1], axis=0)
    w_ref[...] = jnp.concatenate([g_val * (v1 / den), g_val * (v2 / den)], axis=0)
    eio = lax.broadcasted_iota(I32, (N_EXPERTS, tm), 0)
    hit = jnp.where((eio == e0) | (eio == e1), 1.0, 0.0)
    cnt_ref[0] = jnp.sum(hit, axis=1, keepdims=True)


def _outproj(yg, yh, yy, h, w_out, g, b, wr_t, br, alpha, tm=512):
    T, D = h.shape
    nr = wr_t.shape[0]
    wr_hi = wr_t.astype(BF16)
    wr_lo = (wr_t - wr_hi.astype(F32)).astype(BF16)
    full = lambda shp: pl.BlockSpec(shp, lambda i: (0,) * len(shp))
    return pl.pallas_call(
        functools.partial(_outproj_kernel, alpha=alpha),
        grid=(T // tm,),
        in_specs=[pl.BlockSpec((GLA_W, tm), lambda i: (0, i)),
                  pl.BlockSpec((HG_W, tm), lambda i: (0, i)),
                  pl.BlockSpec((HY_WIDTH, tm), lambda i: (0, i)),
                  pl.BlockSpec((tm, D), lambda i: (i, 0)),
                  full((D, D)), full((1, D)), full((1, D)), full((nr, D)), full((nr, D)), full((nr, 1))],
        out_specs=[pl.BlockSpec((tm, D), lambda i: (i, 0)),
                   pl.BlockSpec((TOP_K, tm), lambda i: (0, i)),
                   pl.BlockSpec((TOP_K, tm), lambda i: (0, i)),
                   pl.BlockSpec((1, N_EXPERTS, 1), lambda i: (i, 0, 0))],
        out_shape=[jax.ShapeDtypeStruct((T, D), F32),
                   jax.ShapeDtypeStruct((TOP_K, T), I32),
                   jax.ShapeDtypeStruct((TOP_K, T), F32),
                   jax.ShapeDtypeStruct((T // tm, N_EXPERTS, 1), F32)],
        compiler_params=_cparams(("arbitrary",)),
        name="outproj",
    )(yg, yh, yy, h, w_out.astype(BF16), g.reshape(1, D), b.reshape(1, D), wr_hi, wr_lo, br)


def _chunk_loop(n, fn):
    def body(c, carry):
        fn(pl.multiple_of(c * ROW_CHUNK, ROW_CHUNK))
        return carry
    lax.fori_loop(0, n, body, 0)


def _segment_copies(nch_s, k, fn):
    off = 0
    for si, rows in enumerate(SEG_SIZES):
        n = nch_s[k * len(SEG_SIZES) + si]

        def body(c, carry, off=off, rows=rows):
            fn(rows, pl.multiple_of(off + c * rows, ROW_CHUNK))
            return carry
        lax.fori_loop(0, n, body, 0)
        off = off + n * rows


def _wait_tile_copies(nch_s, tile, wait_one):
    for si, rows in enumerate(SEG_SIZES):
        tot = functools.reduce(lambda a, b: a + b, [nch_s[(tile * N_EXPERTS + e) * len(SEG_SIZES) + si]
                                                    for e in range(N_EXPERTS)])
        lax.fori_loop(0, tot, lambda c, carry, rows=rows: (wait_one(rows), carry)[1], 0)


def _dispatch_kernel(offs_s, gst_s, nch_s, tst_s, tn_s, nu_s, h_ref, e_ref, base_ref, tri_ref,
                     pos_ref, xg_hbm, xs_ref, zbuf, sem, zsem, *, tm, LP, bm, nblocks):
    i = pl.program_id(0)
    nt = pl.num_programs(0)
    slot = i % 2

    def seg_copy(s, rows, src_row, dst_row):
        return pltpu.make_async_copy(xs_ref.at[s, pl.ds(src_row, rows)],
                                     xg_hbm.at[pl.ds(dst_row, rows)], sem.at[s])

    def zero_copy(dst_row):
        return pltpu.make_async_copy(zbuf.at[pl.ds(0, ROW_CHUNK)], xg_hbm.at[pl.ds(dst_row, ROW_CHUNK)], zsem.at[0])

    def zero_block(blk):
        return pltpu.make_async_copy(zbuf, xg_hbm.at[pl.ds(pl.multiple_of(blk * bm, bm), bm)], zsem.at[0])

    def wait_tile(tile, s):
        _wait_tile_copies(nch_s, tile, lambda rows: seg_copy(s, rows, 0, 0).wait())

    @pl.when(i == 0)
    def _():
        zbuf[...] = jnp.zeros_like(zbuf)
        for e in range(N_EXPERTS):
            _chunk_loop(tn_s[e], lambda off, e=e: zero_copy(pl.multiple_of(tst_s[e] + off, ROW_CHUNK)).start())
        lax.fori_loop(nu_s[0], nblocks, lambda blk, c: (zero_block(blk).start(), c)[1], 0)
        for e in range(N_EXPERTS):
            _chunk_loop(tn_s[e], lambda off: zero_copy(0).wait())
        lax.fori_loop(nu_s[0], nblocks, lambda blk, c: (zero_block(0).wait(), c)[1], 0)

    e0, e1 = e_ref[0:1, :], e_ref[1:2, :]
    eio = lax.broadcasted_iota(I32, (N_EXPERTS, tm), 0)
    oh0, oh1 = eio == e0, eio == e1
    hit = jnp.where(oh0 | oh1, 1.0, 0.0).astype(BF16)
    posm = base_ref[0] + _dot(hit, tri_ref[...])
    pos0 = jnp.sum(jnp.where(oh0, posm, 0.0), axis=0, keepdims=True).astype(I32)
    pos1 = jnp.sum(jnp.where(oh1, posm, 0.0), axis=0, keepdims=True).astype(I32)
    pos_ref[...] = jnp.concatenate([pos0, pos1], axis=0)
    hb = h_ref[...].astype(BF16)
    rb = 256
    assert LP % rb == 0
    rio = lax.broadcasted_iota(I32, (rb, tm), 0)
    for r0 in range(0, LP, rb):
        perm = jnp.where((rio == pos0 - r0) | (rio == pos1 - r0), 1.0, 0.0).astype(BF16)
        xs_ref[slot, r0:r0 + rb] = _dot(perm, hb).astype(BF16)

    @pl.when(i > 0)
    def _():
        wait_tile(i - 1, 1 - slot)

    for e in range(N_EXPERTS):
        k = i * N_EXPERTS + e
        src0, dst0 = offs_s[k], gst_s[k]
        _segment_copies(nch_s, k, lambda rows, off, src0=src0, dst0=dst0: seg_copy(
            slot, rows, pl.multiple_of(src0 + off, ROW_CHUNK), pl.multiple_of(dst0 + off, ROW_CHUNK)).start())

    @pl.when(i == nt - 1)
    def _():
        wait_tile(i, slot)


def _moe_dispatch(h1, e_kt, tables, nblocks, bm, tm):
    T, D = h1.shape
    NT = T // tm
    nrows = nblocks * bm
    LP = TOP_K * tm + N_EXPERTS * ROW_CHUNK
    r = np.arange(tm)
    tri = jnp.asarray(r[:, None] < r[None, :], F32).astype(BF16)
    base = tables["offs"].astype(F32).reshape(NT, N_EXPERTS, 1)
    flat = lambda a: a.reshape(-1).astype(I32)
    pos, xg = pl.pallas_call(
        functools.partial(_dispatch_kernel, tm=tm, LP=LP, bm=bm, nblocks=nblocks),
        grid_spec=pltpu.PrefetchScalarGridSpec(
            num_scalar_prefetch=6,
            grid=(NT,),
            in_specs=[pl.BlockSpec((tm, D), lambda i, *_: (i, 0)),
                      pl.BlockSpec((TOP_K, tm), lambda i, *_: (0, i)),
                      pl.BlockSpec((1, N_EXPERTS, 1), lambda i, *_: (i, 0, 0)),
                      pl.BlockSpec((tm, tm), lambda i, *_: (0, 0))],
            out_specs=[pl.BlockSpec((TOP_K, tm), lambda i, *_: (0, i)),
                       pl.BlockSpec(memory_space=pl.ANY)],
            scratch_shapes=[pltpu.VMEM((2, LP, D), BF16), pltpu.VMEM((bm, D), BF16),
                            pltpu.SemaphoreType.DMA((2,)), pltpu.SemaphoreType.DMA((1,))]),
        out_shape=[jax.ShapeDtypeStruct((TOP_K, T), I32), jax.ShapeDtypeStruct((nrows, D), BF16)],
        compiler_params=_cparams(("arbitrary",)),
        name="moe_dispatch",
    )(flat(tables["offs"]), flat(tables["gstart"]), flat(tables["nch"]), flat(tables["tail_start"]),
      flat(tables["tail_n"]), tables["nused"], h1, e_kt, base, tri)
    return pos, xg


def _ffn_kernel(be_ref, nu_ref, x_ref, wg_ref, wu_ref, wd_ref, y_ref, wgb, wub, wdb):
    j = pl.program_id(0)
    used = j < nu_ref[0]

    @pl.when((j == 0) | (be_ref[j] != be_ref[jnp.maximum(j - 1, 0)]))
    def _():
        wgb[...] = wg_ref[0].astype(BF16)
        wub[...] = wu_ref[0].astype(BF16)
        wdb[...] = wd_ref[0].astype(BF16)

    @pl.when(used)
    def _():
        x = x_ref[...]
        a = _dot(x, wgb[...])
        ah = 0.5 * a
        hid = (ah + ah * jnp.tanh(ah)) * _dot(x, wub[...])
        y_ref[...] = _dot(hid.astype(BF16), wdb[...]).astype(BF16)

    @pl.when(jnp.logical_not(used))
    def _():
        y_ref[...] = jnp.zeros_like(y_ref)


def _moe_ffn(xg, block_e, nused, wg, wu, wd, bm, first_expert=0):
    nrows, D = xg.shape
    NB = nrows // bm
    DE = wg.shape[-1]
    row = lambda j, be, nu: (jnp.minimum(j, nu[0] - 1), 0)
    wsel = lambda j, be, nu: (first_expert + be[j], 0, 0)
    return pl.pallas_call(
        _ffn_kernel,
        grid_spec=pltpu.PrefetchScalarGridSpec(
            num_scalar_prefetch=2,
            grid=(NB,),
            in_specs=[pl.BlockSpec((bm, D), row),
                      pl.BlockSpec((1, D, DE), wsel),
                      pl.BlockSpec((1, D, DE), wsel),
                      pl.BlockSpec((1, DE, D), wsel)],
            out_specs=pl.BlockSpec((bm, D), lambda j, be, nu: (j, 0)),
            scratch_shapes=[pltpu.VMEM((D, DE), BF16), pltpu.VMEM((D, DE), BF16), pltpu.VMEM((DE, D), BF16)]),
        out_shape=jax.ShapeDtypeStruct((nrows, D), BF16),
        compiler_params=_cparams(("arbitrary",)),
        name="moe_ffn",
    )(block_e, nused, xg, wg, wu, wd)


def _combine_kernel(offs_s, gst_s, nch_s, yb_hbm, pos_ref, w_ref, h_ref, g_ref, b_ref, o_ref, ybl, sem,
                    *, tm, LP, alpha):
    i = pl.program_id(0)
    nt = pl.num_programs(0)
    slot = i % 2

    def seg_copy(s, rows, src_row, dst_row):
        return pltpu.make_async_copy(yb_hbm.at[pl.ds(src_row, rows)],
                                     ybl.at[s, pl.ds(dst_row, rows)], sem.at[s])

    def issue(tile, s):
        for e in range(N_EXPERTS):
            k = tile * N_EXPERTS + e
            src0, dst0 = gst_s[k], offs_s[k]
            _segment_copies(nch_s, k, lambda rows, off, src0=src0, dst0=dst0: seg_copy(
                s, rows, pl.multiple_of(src0 + off, ROW_CHUNK), pl.multiple_of(dst0 + off, ROW_CHUNK)).start())

    @pl.when(i == 0)
    def _():
        ybl[...] = jnp.zeros_like(ybl)
        issue(0, 0)

    @pl.when(i + 1 < nt)
    def _():
        issue(i + 1, 1 - slot)

    _wait_tile_copies(nch_s, i, lambda rows: seg_copy(slot, rows, 0, 0).wait())
    pos0, pos1 = pos_ref[0:1, :], pos_ref[1:2, :]
    w0, w1 = w_ref[0:1, :], w_ref[1:2, :]
    rio = lax.broadcasted_iota(I32, (LP, LANES), 0)
    yl = ybl[slot]
    for t0 in range(0, tm, LANES):
        tl = slice(t0, t0 + LANES)
        pw = (jnp.where(rio == pos0[:, tl], w0[:, tl], 0.0) + jnp.where(rio == pos1[:, tl], w1[:, tl], 0.0)).astype(BF16)
        ffn = _dot(pw, yl, _TN)
        o_ref[tl, :] = _layer_norm(alpha * h_ref[tl, :] + ffn, g_ref[...], b_ref[...])


def _moe_combine(yb, h1, pos, w_kt, tables, g, b, alpha, tm):
    T, D = h1.shape
    LP = TOP_K * tm + N_EXPERTS * ROW_CHUNK
    flat = lambda a: a.reshape(-1).astype(I32)
    return pl.pallas_call(
        functools.partial(_combine_kernel, tm=tm, LP=LP, alpha=alpha),
        grid_spec=pltpu.PrefetchScalarGridSpec(
            num_scalar_prefetch=3,
            grid=(T // tm,),
            in_specs=[pl.BlockSpec(memory_space=pl.ANY),
                      pl.BlockSpec((TOP_K, tm), lambda i, *_: (0, i)),
                      pl.BlockSpec((TOP_K, tm), lambda i, *_: (0, i)),
                      pl.BlockSpec((tm, D), lambda i, *_: (i, 0)),
                      pl.BlockSpec((1, D), lambda i, *_: (0, 0)),
                      pl.BlockSpec((1, D), lambda i, *_: (0, 0))],
            out_specs=pl.BlockSpec((tm, D), lambda i, *_: (i, 0)),
            scratch_shapes=[pltpu.VMEM((2, LP, D), BF16), pltpu.SemaphoreType.DMA((2,))]),
        out_shape=jax.ShapeDtypeStruct((T, D), F32),
        compiler_params=_cparams(("arbitrary",)),
        name="moe_combine",
    )(flat(tables["offs"]), flat(tables["gstart"]), flat(tables["nch"]), yb, pos, w_kt, h1,
      g.reshape(1, D), b.reshape(1, D))


def _dispatch_tables(cnt, bm, nblocks):
    padlen = ((cnt + ROW_CHUNK - 1) // ROW_CHUNK) * ROW_CHUNK
    offs = jnp.cumsum(padlen, axis=1) - padlen
    tot = jnp.sum(padlen, axis=0)
    region = ((tot + bm - 1) // bm) * bm
    rend = jnp.cumsum(region)
    rstart = rend - region
    gstart = rstart[None, :] + jnp.cumsum(padlen, axis=0) - padlen
    blk_row = jnp.arange(nblocks, dtype=I32)[:, None] * bm
    block_e = jnp.minimum(jnp.sum((blk_row >= rend[None, :]).astype(I32), axis=1), N_EXPERTS - 1)
    nch, rest = [], padlen
    for rows in SEG_SIZES:
        nch.append(rest // rows)
        rest = rest % rows
    return dict(offs=offs, gstart=gstart, nch=jnp.stack(nch, axis=-1), tail_start=rstart + tot,
                tail_n=(region - tot) // ROW_CHUNK, block_e=block_e.astype(I32),
                nused=(rend[-1:] // bm).astype(I32))


def _permute_in_columns(w):
    splits = (192, 192, 384, 384, 32, 384, 768, 384, 384, 768)
    offs = np.concatenate([[0], np.cumsum(splits)])
    gq, gk, gv, gg, ga, hq, hf, hi, hgt, hyu = [(int(offs[i]), int(offs[i + 1])) for i in range(10)]
    order = [hq, hi, hgt, hf, gv, gg, gq, gk, hyu, ga]
    halved = (hq, hgt, hf, gg)
    assert sum(b - a for a, b in order) == D_IN
    return jnp.concatenate([w[:, a:b] * 0.5 if (a, b) in halved else w[:, a:b] for a, b in order], axis=1)


def kernel(x, ln_in_g, ln_in_b, w_in, gla_wa2, gla_ba, gla_norm_g, hg_lb_logits, hg_norm_g, hy_conv_w, hy_conv_b, hy_w1, hy_b1, hy_freq, hy_w2, hy_b2, hy_w3, hy_bias, w_out, ln1_g, ln1_b, moe_wr_g, moe_br_g, moe_wr_e, moe_br_e, moe_w_gate, moe_w_up, moe_w_down, ln2_g, ln2_b):
    B, L, D = x.shape
    T = B * L
    depth = w_in.shape[0]
    alpha = (2 * depth) ** 0.25
    bm = 512
    tmr = 512
    nblocks = -(-(T * TOP_K + (T // tmr) * N_EXPERTS * (ROW_CHUNK - 1) + N_EXPERTS * (bm - 1)) // bm)
    consts = _dft_consts(L)

    p = jax.nn.softmax(hg_lb_logits.astype(F32), axis=0)
    lbs = jnp.cumsum(p, axis=0) - p[0:1]
    lbc = jnp.concatenate([jnp.broadcast_to((0.5 * (1.0 - lbs))[..., None], lbs.shape + (LANES,)),
                           jnp.broadcast_to(jnp.maximum(lbs, LB_FLOOR)[..., None], lbs.shape + (LANES,))], axis=-1)

    wg_all = moe_w_gate.reshape((depth * N_EXPERTS,) + moe_w_gate.shape[2:])
    wu_all = moe_w_up.reshape((depth * N_EXPERTS,) + moe_w_up.shape[2:])
    wd_all = moe_w_down.reshape((depth * N_EXPERTS,) + moe_w_down.shape[2:])
    h = x.reshape(T, D)
    for l in range(depth):
        w_t = _permute_in_columns(w_in[l]).T.astype(BF16)
        proj_t, h = _inproj(h, ln_in_g, ln_in_b, w_t, apply_ln=(l == 0))
        wa_t = jnp.swapaxes(gla_wa2[l], 1, 2)
        y_gla = _scan(proj_t, "gla", B, L, (wa_t, gla_ba[l].reshape(2, GLA_K, 1)), gla_norm_g[l].reshape(GLA_W, 1))
        y_hg = _scan(proj_t, "hg", B, L, (lbc[l],), hg_norm_g[l].reshape(HG_W, 1))
        kt = _hyena_filters(L, hy_w1[l], hy_b1[l], hy_freq[l], hy_w2[l], hy_b2[l], hy_w3[l])
        kspec = _hyena_spectrum(kt, consts, L)
        y_hy = _hyena(proj_t, kspec, consts, hy_conv_w[l], hy_conv_b[l], hy_bias[l], B, L)
        nr = N_GROUPS + N_EXPERTS
        nrp = -(-nr // 8) * 8
        wr_t = jnp.zeros((nrp, D), F32).at[:nr].set(jnp.concatenate([moe_wr_g[l], moe_wr_e[l]], axis=1).T)
        br = jnp.zeros((nrp, 1), F32).at[:nr, 0].set(jnp.concatenate([moe_br_g[l], moe_br_e[l]]))
        h1, e_kt, w_kt, cnt = _outproj(y_gla, y_hg, y_hy, h, w_out[l], ln1_g[l], ln1_b[l], wr_t, br, alpha, tm=tmr)
        tables = _dispatch_tables(cnt.reshape(T // tmr, N_EXPERTS).astype(I32), bm, nblocks)
        pos, xg = _moe_dispatch(h1, e_kt, tables, nblocks, bm, tmr)
        yb = _moe_ffn(xg, tables["block_e"], tables["nused"], wg_all, wu_all, wd_all, bm, first_expert=l * N_EXPERTS)
        h = _moe_combine(yb, h1, pos, w_kt, tables, ln2_g[l], ln2_b[l], alpha, tmr)
    return h.reshape(B, L, D)
```

```python
import functools
import math

import numpy as np
import jax
import jax.numpy as jnp
from jax import lax
from jax.experimental import pallas as pl
from jax.experimental.pallas import tpu as pltpu

F32 = jnp.float32
BF16 = jnp.bfloat16
I32 = jnp.int32

GLA_HEADS, GLA_DK, GLA_DV, GLA_LOWRANK, GLA_TAU = 6, 32, 64, 16, 16.0
HG_HEADS, HG_DK, HG_DV = 6, 64, 64
HY_WIDTH, HY_ORDER, HY_EMB, HY_FFN, HY_INNER = 256, 2, 33, 64, 2
HY_FAST_DECAY, HY_SLOW_DECAY, HY_TARGET = 0.3, 1.5, 1e-2
N_GROUPS, EXPERTS_PER_GROUP = 4, 4
N_EXPERTS = N_GROUPS * EXPERTS_PER_GROUP
TOP_K = 2
LN_EPS, RMS_EPS, LB_FLOOR = 1e-5, 1e-6, 1e-30
LOG2E = 1.4426950408889634

LANES = 128
SCAN_CHUNK = LANES
ROW_CHUNK = 16
SEG_SIZES = (64, 32, ROW_CHUNK)
VMEM_LIMIT = 56 * 1024 * 1024

GLA_W = GLA_HEADS * GLA_DV
GLA_K = GLA_HEADS * GLA_DK
HG_W = HG_HEADS * HG_DV
HG_K = HG_HEADS * HG_DK
OFF_HQ, OFF_HI, OFF_HGT, OFF_HF = 0, 384, 768, 1152
OFF_GV, OFF_GG, OFF_GQ, OFF_GK, OFF_HY, OFF_GA = 1920, 2304, 2688, 2880, 3072, 3840
D_IN = 3872


def _dot(a, b, dims=(((1,), (0,)), ((), ())), precision=None):
    return lax.dot_general(a, b, dims, preferred_element_type=F32, precision=precision)


_NT = (((1,), (1,)), ((), ()))
_TN = (((0,), (0,)), ((), ()))


def _layer_norm(x, g, b):
    mu = jnp.mean(x, axis=-1, keepdims=True)
    xc = x - mu
    var = jnp.mean(xc * xc, axis=-1, keepdims=True)
    return xc * lax.rsqrt(var + LN_EPS) * g + b


def _log_sigmoid(x):
    return jnp.minimum(x, 0.0) - jnp.log(1.0 + jnp.exp(-jnp.abs(x)))


def _cparams(sem):
    return pltpu.CompilerParams(dimension_semantics=sem, vmem_limit_bytes=VMEM_LIMIT)


def _inproj_kernel(x_ref, g_ref, b_ref, w_ref, *outs, apply_ln):
    x = x_ref[...]
    if apply_ln:
        x = _layer_norm(x, g_ref[...], b_ref[...])
        outs[1][...] = x
    outs[0][...] = _dot(w_ref[...], x.astype(BF16), _NT)


def _inproj(x, g, b, w_t, apply_ln, tm=512):
    T, D = x.shape
    n_out = w_t.shape[0]
    out_shape = [jax.ShapeDtypeStruct((n_out, T), F32)]
    out_specs = [pl.BlockSpec((n_out, tm), lambda i: (0, i))]
    if apply_ln:
        out_shape.append(jax.ShapeDtypeStruct((T, D), F32))
        out_specs.append(pl.BlockSpec((tm, D), lambda i: (i, 0)))
    res = pl.pallas_call(
        functools.partial(_inproj_kernel, apply_ln=apply_ln),
        grid=(T // tm,),
        in_specs=[pl.BlockSpec((tm, D), lambda i: (i, 0)),
                  pl.BlockSpec((1, D), lambda i: (0, 0)),
                  pl.BlockSpec((1, D), lambda i: (0, 0)),
                  pl.BlockSpec((n_out, D), lambda i: (0, 0))],
        out_specs=out_specs,
        out_shape=out_shape,
        compiler_params=_cparams(("arbitrary",)),
        name="inproj",
    )(x, g.reshape(1, D), b.reshape(1, D), w_t)
    return res if apply_ln else (res[0], x)


def _scan_kernel(*refs, mode, H, K, V, TB, NCB):
    if mode == "gla":
        (qf_ref, qb_ref, kf_ref, kb_ref, vf_ref, vb_ref, gaf_ref, gab_ref, wa_ref, ba_ref,
         gtf_ref, gtb_ref, ng_ref, y_ref, s_ref, oacc_ref, oi_ref, qg_ref, ut_ref, dr_ref) = refs
        q_refs, k_refs, ga_refs = (qf_ref, qb_ref), (kf_ref, kb_ref), (gaf_ref, gab_ref)
    else:
        (qf_ref, qb_ref, zf_ref, zb_ref, vf_ref, vb_ref, lbc_ref,
         gtf_ref, gtb_ref, ng_ref, y_ref, s_ref, oacc_ref, oi_ref, qg_ref, ut_ref, dr_ref) = refs
        q_refs, z_refs = (qf_ref, qb_ref), (zf_ref, zb_ref)
    v_refs, gate_refs = (vf_ref, vb_ref), (gtf_ref, gtb_ref)
    C = SCAN_CHUNK
    half = C // 2
    nchunks = TB // C
    n = pl.program_id(1)
    blocks = (n, NCB - 1 - n)

    @pl.when(n == 0)
    def _():
        s_ref[...] = jnp.zeros_like(s_ref)

    def gates(d, sl):
        if mode == "gla":
            wa, ga = wa_ref[d], ga_refs[d][:, sl]
            wa_hi, ga_hi = wa.astype(BF16), ga.astype(BF16)
            wa_lo = (wa - wa_hi.astype(F32)).astype(BF16)
            ga_lo = (ga - ga_hi.astype(F32)).astype(BF16)
            a = _dot(wa_hi, ga_hi) + _dot(wa_hi, ga_lo) + _dot(wa_lo, ga_hi) + ba_ref[d]
            g = _log_sigmoid(a) * (LOG2E / GLA_TAU)
            q = q_refs[d][:, sl] * (K ** -0.5)
            k = k_refs[d][:, sl]
        else:
            half_lb, lb_floor = lbc_ref[d, :, 0:C], lbc_ref[d, :, C:2 * C]
            ht = half_lb * jnp.tanh(z_refs[d][:, sl])
            g = jnp.log2(half_lb + ht + lb_floor)
            k = half_lb - ht
            hq = q_refs[d][:, sl]
            q = hq + hq * jnp.tanh(hq)
        return q, k, g

    r = lax.broadcasted_iota(I32, (C, C), 0)
    c = lax.broadcasted_iota(I32, (C, C), 1)
    lane_lo = lax.broadcasted_iota(I32, (1, C), 1) < half
    sign_lo = jnp.where(lane_lo, 1.0, -1.0)
    consts = (((r <= c).astype(BF16), r <= c, (r < half) & (c >= half), (C - 1, half, half // 2, half + half // 2)),
              ((r >= c).astype(BF16), r >= c, (r >= half) & (c < half),
               (0, half - 1, half // 2 - 1, half + half // 2 - 1)))

    heads = [(slice(h * K, (h + 1) * K), slice(h * V, (h + 1) * V)) for h in range(H)]

    def cumulate(d, g):
        cum = consts[d][0]
        g1 = g.astype(BF16)
        g2 = (g - g1.astype(F32)).astype(BF16)
        G = _dot(g1, cum) + _dot(g2, cum)
        ge_row = _dot(ones8, g1, _NT) + _dot(ones8, g2, _NT)
        return G, ge_row

    def scale(d, sl, q, k, G, ge_row):
        _, _, _, (c_end, c_mid, c_a, c_b) = consts[d]
        v = v_refs[d][:, sl].astype(BF16)
        g_end = G[:, c_end:c_end + 1]
        g_mid = G[:, c_mid:c_mid + 1]
        e2 = G - jnp.where(lane_lo, G[:, c_a:c_a + 1], G[:, c_b:c_b + 1])
        q2 = (q * jnp.exp2(e2)).astype(BF16)
        k2 = (k * jnp.exp2(-e2)).astype(BF16)
        dm = G - g_mid
        x1 = jnp.exp2(dm * (sign_lo if d else -sign_lo))
        q1 = (q * x1).astype(BF16)
        k1 = (k * x1).astype(BF16)
        qg = (q * jnp.exp2(G)).astype(BF16)
        kd = (k * jnp.exp2(g_end - G)).astype(BF16)
        x = dict(qg=qg, kd=kd, v=v, dec_row=jnp.exp2(ge_row))
        if merge_scores:
            first = lambda a: jnp.where(lane_lo, a, jnp.zeros_like(a))
            second = lambda a: jnp.where(lane_lo, jnp.zeros_like(a), a)
            x["kparts"] = ((second if d else first)(k1), first(k2), second(k2))
            x["qparts"] = ((first if d else second)(q1), first(q2), second(q2))
        else:
            x.update(q1=q1, k1=k1, q2=q2, k2=k2)
        return x

    ones8 = jnp.ones((8, C), BF16)
    merge_scores = 3 * K <= LANES

    def scores(x):
        if merge_scores:
            return [_dot(jnp.concatenate([p[rk] for p in x["kparts"]], axis=0),
                         jnp.concatenate([p[rk] for p in x["qparts"]], axis=0), _TN) for rk, _ in heads]
        return [(_dot(x["k1"][rk], x["q1"][rk], _TN), _dot(x["k2"][rk], x["q2"][rk], _TN)) for rk, _ in heads]

    def masked(d, ps):
        _, tri, off1, _ = consts[d]
        if merge_scores:
            return [jnp.where(tri, p, 0.0).astype(BF16) for p in ps]
        return [jnp.where(off1, p1, jnp.where(tri, p2, 0.0)).astype(BF16) for p1, p2 in ps]

    def park(d, ci, x, pm):
        oi_ref[d, ci] = jnp.concatenate([_dot(x["v"][rv], pm[h]) for h, (_, rv) in enumerate(heads)], axis=0)
        qg_ref[d, ci] = x["qg"]
        dr_ref[d, ci] = x["dec_row"]
        for h, (rk, rv) in enumerate(heads):
            ut_ref[d, ci, h] = _dot(x["v"][rv], x["kd"][rk], _NT)

    def chunk_slices(ci):
        offs = (pl.multiple_of(ci * C, C), pl.multiple_of((nchunks - 1 - ci) * C, C))
        return offs, [pl.ds(offs[d], C) for d in range(2)]

    def independent(it, carry):
        jobs = [(d, U * it + u, chunk_slices(U * it + u)[1][d]) for u in range(U) for d in range(2)]
        qkg = [gates(d, sl) for d, _, sl in jobs]
        cums = [cumulate(d, qkg[i][2]) for i, (d, _, _) in enumerate(jobs)]
        xs = [scale(d, sl, qkg[i][0], qkg[i][1], *cums[i]) for i, (d, _, sl) in enumerate(jobs)]
        scs = [scores(x) for x in xs]
        pms = [masked(d, scs[i]) for i, (d, _, _) in enumerate(jobs)]
        for i, (d, ci, _) in enumerate(jobs):
            park(d, ci, xs[i], pms[i])
        return carry

    U = 4
    assert nchunks % U == 0
    lax.fori_loop(0, nchunks // U, independent, 0)

    def carried(ci):
        st = [[s_ref[d, h] for h in range(H)] for d in range(2)]
        inter = [[_dot(st[d][h].astype(BF16), qg_ref[d, ci, rk, :]) for h, (rk, _) in enumerate(heads)]
                 for d in range(2)]
        for d in range(2):
            for h, (rk, _) in enumerate(heads):
                s_ref[d, h] = st[d][h] * dr_ref[d, ci, 0:1, rk] + ut_ref[d, ci, h]
        return [oi_ref[d, ci] + jnp.concatenate(inter[d], axis=0) for d in range(2)]

    def finish(o, gate):
        t = jnp.tanh(gate)
        act = gate + gate * t if mode == "gla" else 0.5 + 0.5 * t
        ys = []
        for h in range(H):
            oh = o[h * V:(h + 1) * V]
            ms = jnp.mean(oh * oh, axis=0, keepdims=True)
            ys.append(oh * lax.rsqrt(ms + RMS_EPS))
        return (jnp.concatenate(ys, axis=0) * ng_ref[...] * act).astype(y_ref.dtype)

    def chunk(ci, carry):
        offs, sls = chunk_slices(ci)
        tsls = [pl.ds(pl.multiple_of(blocks[d] * TB + offs[d], C), C) for d in range(2)]
        o = carried(ci)

        @pl.when(2 * n < NCB)
        def _():
            for d in range(2):
                oacc_ref[:, tsls[d]] = o[d]

        @pl.when(2 * n >= NCB)
        def _():
            tot = [o[d] + oacc_ref[:, tsls[d]] for d in range(2)]
            for d in range(2):
                y_ref[:, tsls[d]] = finish(tot[d], gate_refs[d][:, sls[d]])
        return carry

    lax.fori_loop(0, nchunks, chunk, 0)


def _scan(proj_t, mode, B, L, extra, norm_g, TB=512):
    NCB = L // TB
    nch = TB // SCAN_CHUNK
    assert NCB % 2 == 0
    if mode == "gla":
        H, K, V = GLA_HEADS, GLA_DK, GLA_DV
    else:
        H, K, V = HG_HEADS, HG_DK, HG_DV
    HK, HV = H * K, H * V
    cf = lambda b, n: b * NCB + n
    cb = lambda b, n: b * NCB + NCB - 1 - n
    gf = lambda b, n: b * NCB + jnp.maximum(n, NCB // 2)
    gb = lambda b, n: b * NCB + jnp.minimum(NCB - 1 - n, NCB // 2 - 1)

    def pair(rows, off, fwd=cf, bwd=cb, dir_step=0):
        return [pl.BlockSpec((rows, TB), lambda b, n: (off // rows, fwd(b, n))),
                pl.BlockSpec((rows, TB), lambda b, n: (off // rows + dir_step, bwd(b, n)))]

    whole = lambda shp: pl.BlockSpec(shp, lambda b, n: (0,) * len(shp))
    if mode == "gla":
        wa_t, ba = extra
        in_specs = (pair(HK, OFF_GQ) + pair(HK, OFF_GK) + pair(HV, OFF_GV)
                    + pair(GLA_LOWRANK, OFF_GA, dir_step=1)
                    + [whole((2, HK, GLA_LOWRANK)), whole((2, HK, 1))]
                    + pair(HV, OFF_GG, gf, gb) + [whole((HV, 1))])
        args = (proj_t,) * 8 + (wa_t, ba, proj_t, proj_t, norm_g)
    else:
        (lbc,) = extra
        in_specs = (pair(HK, OFF_HQ) + pair(HK, OFF_HF, dir_step=1) + pair(HV, OFF_HI)
                    + [whole((2, HK, 2 * LANES))] + pair(HV, OFF_HGT, gf, gb) + [whole((HV, 1))])
        args = (proj_t,) * 6 + (lbc, proj_t, proj_t, norm_g)
    return pl.pallas_call(
        functools.partial(_scan_kernel, mode=mode, H=H, K=K, V=V, TB=TB, NCB=NCB),
        grid=(B, NCB),
        in_specs=in_specs,
        out_specs=pl.BlockSpec((HV, L), lambda b, n: (0, b)),
        out_shape=jax.ShapeDtypeStruct((HV, B * L), BF16),
        scratch_shapes=[pltpu.VMEM((2, H, V, K), F32), pltpu.VMEM((HV, L), F32),
                        pltpu.VMEM((2, nch, HV, SCAN_CHUNK), F32), pltpu.VMEM((2, nch, HK, SCAN_CHUNK), BF16),
                        pltpu.VMEM((2, nch, H, V, K), F32), pltpu.VMEM((2, nch, 8, HK), F32)],
        compiler_params=_cparams(("arbitrary", "arbitrary")),
        name="scan_" + mode,
    )(*args)


def _dft_consts(L):
    N = 2 * L
    NA = N // LANES
    a = np.arange(NA)[:, None] * np.arange(NA)[None, :]
    ca, sa = np.cos(2 * np.pi * a / NA), np.sin(2 * np.pi * a / NA)
    hh = NA // 2
    w1d = np.block([[ca[:, :hh], sa[:, :hh]], [-sa[:, :hh], ca[:, :hh]]])
    w1f = np.concatenate([ca, -sa], axis=0)
    w1i = np.block([[ca[:hh, :], -sa[:hh, :]], [sa[:hh, :], ca[:hh, :]]])
    bb = np.arange(LANES)[:, None] * np.arange(LANES)[None, :]
    cb, sb = np.cos(2 * np.pi * bb / LANES), np.sin(2 * np.pi * bb / LANES)
    w2 = np.block([[cb, -sb], [sb, cb]])
    w2i = np.block([[cb, sb], [-sb, cb]])
    tw = np.arange(NA)[:, None] * np.arange(LANES)[None, :]
    tc, ts = np.cos(2 * np.pi * tw / N), np.sin(2 * np.pi * tw / N)
    bf = lambda m: jnp.asarray(m, dtype=F32).astype(BF16)
    return dict(w1d=bf(w1d), w1f=bf(w1f), w1i=bf(w1i), w2=bf(w2), w2i=bf(w2i),
                tc=jnp.asarray(tc, F32), ts=jnp.asarray(ts, F32))


def _pos_features(L):
    t = np.linspace(0.0, 1.0, L)
    w = 2.0 * np.pi * np.arange(L) / L
    bands = np.linspace(1e-4, (HY_EMB - 1) // 2 - 1, (HY_EMB - 1) // 2)
    z = np.concatenate([t[None, :], np.cos(bands[:, None] * w[None, :]), -np.sin(bands[:, None] * w[None, :])], axis=0)
    kp = -(-HY_EMB // 8) * 8
    z = np.concatenate([z, np.zeros((kp - HY_EMB, L))], axis=0)
    t_rev = t[(L - np.arange(L)) % L]
    mask = (np.arange(L) >= 1).astype(np.float64)
    return (jnp.asarray(z, F32), jnp.asarray(t[None, :], F32),
            jnp.asarray(t_rev[None, :], F32), jnp.asarray(mask[None, :], F32))


def _filter_kernel(z_ref, t_ref, tr_ref, m_ref, w1_ref, b1_ref, fr_ref, w2_ref, b2_ref,
                   w3f_ref, w3b_ref, dl_ref, out_ref, h_ref, *, L):
    first = (pl.program_id(0) == 0) & (pl.program_id(1) == 0)
    hi = lax.Precision.HIGHEST

    @pl.when(first)
    def _():
        fr = fr_ref[...]
        h = jnp.sin(fr * (_dot(w1_ref[...], z_ref[...], precision=hi) + b1_ref[...]))
        for i in range(HY_INNER):
            h = jnp.sin(fr * (_dot(w2_ref[i], h, precision=hi) + b2_ref[i]))
        hb = h.astype(BF16)
        h_ref[0] = hb
        src = lax.broadcasted_iota(I32, (LANES, LANES), 0)
        dst = lax.broadcasted_iota(I32, (LANES, LANES), 1)
        flip_shift = jnp.where((dst >= 1) & (src == LANES - dst), 1.0, 0.0).astype(BF16)
        lane0 = jnp.where((dst == 0) & (src == 0), 1.0, 0.0).astype(BF16)
        nb = L // LANES
        tile = lambda b: hb[:, b * LANES:(b + 1) * LANES]
        for jb in range(nb):
            blk = _dot(tile(nb - 1 - jb), flip_shift) + _dot(tile((nb - jb) % nb), lane0)
            h_ref[1, :, jb * LANES:(jb + 1) * LANES] = blk.astype(BF16)

    ad = jnp.abs(dl_ref[...])
    kf = _dot(w3f_ref[...].astype(BF16), h_ref[0]) * jnp.exp(-t_ref[...] * ad)
    kb = _dot(w3b_ref[...].astype(BF16), h_ref[1]) * jnp.exp(-tr_ref[...] * ad) * m_ref[...]
    den = jnp.sum(jnp.abs(kf), axis=1, keepdims=True) + jnp.sum(jnp.abs(kb), axis=1, keepdims=True)
    scale = 1.0 / (jnp.maximum(den, 1e-12) * (2.0 * L))
    out_ref[0, :, 0:L] = kf * scale
    out_ref[0, :, L:2 * L] = kb * scale


def _hyena_filters(L, w1, b1, freq, w2, b2, w3, cg=64):
    z, t, t_rev, mask = _pos_features(L)
    kp = z.shape[0]
    w1_t = jnp.zeros((HY_FFN, kp), F32).at[:, :HY_EMB].set(w1.T)
    w2_t = jnp.swapaxes(w2, 1, 2)
    w3_t = w3.T
    max_decay = math.log(HY_TARGET) / HY_FAST_DECAY
    min_decay = math.log(HY_TARGET) / HY_SLOW_DECAY
    deltas = jnp.asarray(np.linspace(min_decay, max_decay, HY_WIDTH).reshape(HY_WIDTH, 1), F32)
    ncg = HY_WIDTH // cg
    full = lambda shp: pl.BlockSpec(shp, lambda o, j: (0,) * len(shp))
    return pl.pallas_call(
        functools.partial(_filter_kernel, L=L),
        grid=(HY_ORDER, ncg),
        in_specs=[full((kp, L)), full((1, L)), full((1, L)), full((1, L)),
                  full((HY_FFN, kp)), full((HY_FFN, 1)), full((HY_FFN, 1)),
                  full((HY_INNER, HY_FFN, HY_FFN)), full((HY_INNER, HY_FFN, 1)),
                  pl.BlockSpec((cg, HY_FFN), lambda o, j: (o * 2 * ncg + j, 0)),
                  pl.BlockSpec((cg, HY_FFN), lambda o, j: (o * 2 * ncg + ncg + j, 0)),
                  pl.BlockSpec((cg, 1), lambda o, j: (j, 0))],
        out_specs=pl.BlockSpec((1, cg, 2 * L), lambda o, j: (o, j, 0)),
        out_shape=jax.ShapeDtypeStruct((HY_ORDER, HY_WIDTH, 2 * L), F32),
        scratch_shapes=[pltpu.VMEM((2, HY_FFN, L), BF16)],
        compiler_params=_cparams(("arbitrary", "arbitrary")),
        name="hyena_filter",
    )(z, t, t_rev, mask, w1_t, b1.reshape(HY_FFN, 1), freq.reshape(HY_FFN, 1), w2_t,
      b2.reshape(HY_INNER, HY_FFN, 1), w3_t, w3_t, deltas)


def _spectrum_kernel(k2_ref, w1_ref, tc_ref, ts_ref, w2_ref, out_ref, k_ref, *, NA, cg):
    k_ref[0] = k2_ref[0].reshape(cg, NA, LANES)
    tc2 = jnp.concatenate([tc_ref[...]] * 2, axis=1)
    ts2 = jnp.concatenate([ts_ref[...]] * 2, axis=1)
    G = 4

    def body(it, carry):
        lhs = []
        for pr in range(G // 2):
            rhs = jnp.concatenate([k_ref[0, G * it + 2 * pr + cc] for cc in range(2)], axis=1)
            a = _dot(w1_ref[...], rhs.astype(BF16))
            a_re, a_im = a[:NA], a[NA:]
            b_re = a_re * tc2 + a_im * ts2
            b_im = a_im * tc2 - a_re * ts2
            lhs += [jnp.concatenate([b_re[:, cc * LANES:(cc + 1) * LANES], b_im[:, cc * LANES:(cc + 1) * LANES]], axis=1)
                    for cc in range(2)]
        x = _dot(jnp.concatenate(lhs, axis=0).astype(BF16), w2_ref[...])
        for i in range(G):
            out_ref[0, G * it + i] = x[i * NA:(i + 1) * NA]
        return carry
    lax.fori_loop(0, cg // G, body, 0)


def _hyena_spectrum(kt, consts, L, cg=32):
    NA = 2 * L // LANES
    full = lambda shp: pl.BlockSpec(shp, lambda o, j: (0,) * len(shp))
    return pl.pallas_call(
        functools.partial(_spectrum_kernel, NA=NA, cg=cg),
        grid=(HY_ORDER, HY_WIDTH // cg),
        in_specs=[pl.BlockSpec((1, cg, 2 * L), lambda o, j: (o, j, 0)),
                  full((2 * NA, NA)), full((NA, LANES)), full((NA, LANES)), full((2 * LANES, 2 * LANES))],
        out_specs=pl.BlockSpec((1, cg, NA, 2 * LANES), lambda o, j: (o, j, 0, 0)),
        out_shape=jax.ShapeDtypeStruct((HY_ORDER, HY_WIDTH, NA, 2 * LANES), F32),
        scratch_shapes=[pltpu.VMEM((1, cg, NA, LANES), F32)],
        compiler_params=_cparams(("arbitrary", "arbitrary")),
        name="hyena_spectrum",
    )(kt, consts["w1f"], consts["tc"], consts["ts"], consts["w2"])


def _hyena_kernel(cw_ref, cb_ref, hb_ref, v2_ref, x12_ref, x22_ref, ks_ref, w1d_ref, w1i_ref, tc_ref, ts_ref,
                  w2_ref, w2i_ref, y2_ref, v_ref, x1_ref, x2_ref, y_ref, *, NA, cg, B):
    hh = NA // 2
    L = hh * LANES
    j = pl.program_id(0)
    for src, dst in ((v2_ref, v_ref), (x12_ref, x1_ref), (x22_ref, x2_ref)):
        for b in range(B):
            dst[:, b] = src[:, b * L:(b + 1) * L].reshape(cg, hh, LANES)
    row = lax.broadcasted_iota(I32, (hh, LANES), 0)
    lane = lax.broadcasted_iota(I32, (hh, LANES), 1)
    first = (row == 0) & (lane == 0)
    last = (row == hh - 1) & (lane == LANES - 1)

    def short_conv(x, ch):
        r1 = pltpu.roll(x, 1, 1)
        prev = jnp.where(lane == 0, pltpu.roll(r1, 1, 0), r1)
        prev = jnp.where(first, 0.0, prev)
        r2 = pltpu.roll(x, LANES - 1, 1)
        nxt = jnp.where(lane == LANES - 1, pltpu.roll(r2, hh - 1, 0), r2)
        nxt = jnp.where(last, 0.0, nxt)
        return cw_ref[0, ch] * prev + cw_ref[1, ch] * x + cw_ref[2, ch] * nxt + cb_ref[ch]

    P = B // 2
    tc, ts = tc_ref[...], ts_ref[...]
    tc2 = jnp.concatenate([tc, tc], axis=1)
    ts2 = jnp.concatenate([ts, ts], axis=1)
    lane2 = lambda x, cc: x[:, cc * LANES:(cc + 1) * LANES]

    seqs = [(p, cc) for p in range(P) for cc in range(2)]
    gate_refs = (x1_ref, x2_ref)

    def load(cis, chs):
        z = {(p, cc): [short_conv(v_ref[cis[cc], 2 * p + r], chs[cc]) for r in range(2)] for p, cc in seqs}
        gates = [{(p, cc): [short_conv(gate_refs[o][cis[cc], 2 * p + r], (o + 1) * HY_WIDTH + chs[cc])
                            for r in range(2)] for p, cc in seqs} for o in range(HY_ORDER)]
        return z, gates

    def dft_rows(z):
        lhs = []
        for p in range(P):
            rhs = jnp.concatenate([jnp.concatenate(z[(p, cc)], axis=0) for cc in range(2)], axis=1)
            a = _dot(w1d_ref[...], rhs.astype(BF16))
            a_re, a_im = a[:NA], a[NA:]
            b_re = a_re * tc2 + a_im * ts2
            b_im = a_im * tc2 - a_re * ts2
            lhs += [jnp.concatenate([lane2(b_re, cc), lane2(b_im, cc)], axis=1) for cc in range(2)]
        return jnp.concatenate(lhs, axis=0).astype(BF16)

    def dft_lanes_times_filter(lhs, o, cis):
        x = _dot(lhs, w2_ref[...])
        ys = []
        for idx, (p, cc) in enumerate(seqs):
            xb = x[idx * NA:(idx + 1) * NA]
            ks = ks_ref[o, cis[cc]]
            x_re, x_im = lane2(xb, 0), lane2(xb, 1)
            k_re, k_im = lane2(ks, 0), lane2(ks, 1)
            ys.append(jnp.concatenate([x_re * k_re - x_im * k_im, x_re * k_im + x_im * k_re], axis=1))
        return jnp.concatenate(ys, axis=0).astype(BF16)

    def idft_lanes(ys):
        bq = _dot(ys, w2i_ref[...])
        out = []
        for p in range(P):
            cr, cim = [], []
            for cc in range(2):
                blk = bq[(2 * p + cc) * NA:(2 * p + cc + 1) * NA]
                b_re, b_im = lane2(blk, 0), lane2(blk, 1)
                cr.append(b_re * tc - b_im * ts)
                cim.append(b_re * ts + b_im * tc)
            out.append(jnp.concatenate([jnp.concatenate(cr, axis=1), jnp.concatenate(cim, axis=1)], axis=0).astype(BF16))
        return out

    def idft_rows_and_gate(rhs, z, gates, o, chs):
        znew = {}
        for p in range(P):
            conv = _dot(w1i_ref[...], rhs[p])
            for cc in range(2):
                bias = hb_ref[o, chs[cc]]
                znew[(p, cc)] = [gates[(p, cc)][r] * (lane2(conv, cc)[r * hh:(r + 1) * hh] + z[(p, cc)][r] * bias)
                                 for r in range(2)]
        return znew

    NG = 2

    def body(it, carry):
        groups = range(NG)
        cis = [[2 * NG * it + 2 * g + cc for cc in range(2)] for g in groups]
        chs = [[j * cg + ci for ci in cis[g]] for g in groups]
        loaded = [load(cis[g], chs[g]) for g in groups]
        zs = [loaded[g][0] for g in groups]
        for o in range(HY_ORDER):
            s1 = [dft_rows(zs[g]) for g in groups]
            s2 = [dft_lanes_times_filter(s1[g], o, cis[g]) for g in groups]
            s3 = [idft_lanes(s2[g]) for g in groups]
            zs = [idft_rows_and_gate(s3[g], zs[g], loaded[g][1][o], o, chs[g]) for g in groups]
        for g in groups:
            for p, cc in seqs:
                for r in range(2):
                    y_ref[cis[g][cc], 2 * p + r] = zs[g][(p, cc)][r]
        return carry

    lax.fori_loop(0, cg // (2 * NG), body, 0)
    for b in range(B):
        y2_ref[:, b * L:(b + 1) * L] = y_ref[:, b].reshape(cg, L)


def _hyena(proj_t, kspec, consts, conv_w, conv_b, bias, B, L, cg=8):
    NA = 2 * L // LANES
    hh = NA // 2
    ncg = HY_WIDTH // cg
    base = OFF_HY // cg
    smem = pl.BlockSpec(memory_space=pltpu.SMEM)
    full = lambda shp: pl.BlockSpec(shp, lambda j: (0,) * len(shp))
    blk = lambda off: pl.BlockSpec((cg, B * L), lambda j: (off + j, 0))
    tiles = pltpu.VMEM((cg, B, hh, LANES), F32)
    return pl.pallas_call(
        functools.partial(_hyena_kernel, NA=NA, cg=cg, B=B),
        grid=(ncg,),
        in_specs=[smem, smem, smem, blk(base), blk(base + ncg), blk(base + 2 * ncg),
                  pl.BlockSpec((HY_ORDER, cg, NA, 2 * LANES), lambda j: (0, j, 0, 0)),
                  full((2 * NA, NA)), full((NA, 2 * NA)), full((NA, LANES)), full((NA, LANES)),
                  full((2 * LANES, 2 * LANES)), full((2 * LANES, 2 * LANES))],
        out_specs=pl.BlockSpec((cg, B * L), lambda j: (j, 0)),
        out_shape=jax.ShapeDtypeStruct((HY_WIDTH, B * L), F32),
        scratch_shapes=[tiles, tiles, tiles, tiles],
        compiler_params=_cparams(("arbitrary",)),
        name="hyena_conv",
    )(conv_w, conv_b, bias, proj_t, proj_t, proj_t, kspec, consts["w1d"], consts["w1i"], consts["tc"], consts["ts"],
      consts["w2"], consts["w2i"])


def _outproj_kernel(yg_ref, yh_ref, yy_ref, h_ref, wo_ref, g_ref, b_ref, wrh_ref, wrl_ref, br_ref,
                    h1_ref, e_ref, w_ref, cnt_ref, *, alpha):
    tm = h_ref.shape[0]
    ts = LANES
    ns = tm // ts
    subs = [slice(s * ts, (s + 1) * ts) for s in range(ns)]
    mixes = [_dot(yg_ref[:, sl].astype(BF16), wo_ref[0:GLA_W], _TN)
             + _dot(yh_ref[:, sl].astype(BF16), wo_ref[GLA_W:GLA_W + HG_W], _TN)
             + _dot(yy_ref[:, sl].astype(BF16), wo_ref[GLA_W + HG_W:], _TN) for sl in subs]
    h1s = [_layer_norm(alpha * h_ref[sl, :] + mixes[s], g_ref[...], b_ref[...]) for s, sl in enumerate(subs)]
    for s, sl in enumerate(subs):
        h1_ref[sl, :] = h1s[s]
    his = [h1.astype(BF16) for h1 in h1s]
    los = [(h1s[s] - his[s].astype(F32)).astype(BF16) for s in range(ns)]
    lgs = [_dot(wrh_ref[...], his[s], _NT) + _dot(wrh_ref[...], los[s], _NT) + _dot(wrl_ref[...], his[s], _NT)
           for s in range(ns)]
    lg = jnp.concatenate(lgs, axis=1) + br_ref[...]
    gl = [lg[g:g + 1] for g in range(N_GROUPS)]
    gmax = functools.reduce(jnp.maximum, gl)
    gidx = jnp.full((1, tm), N_GROUPS - 1, I32)
    for g in range(N_GROUPS - 2, -1, -1):
        gidx = jnp.where(gl[g] == gmax, g, gidx)
    gsum = functools.reduce(jnp.add, [jnp.exp(x - gmax) for x in gl])
    g_val = 1.0 / gsum
    el = []
    for r in range(EXPERTS_PER_GROUP):
        acc = jnp.zeros((1, tm), F32)
        for g in range(N_GROUPS):
            row = N_GROUPS + g * EXPERTS_PER_GROUP + r
            acc = jnp.where(gidx == g, lg[row:row + 1], acc)
        el.append(acc)
    emax = functools.reduce(jnp.maximum, el)
    pe = [jnp.exp(x - emax) for x in el]
    esum = functools.reduce(jnp.add, pe)
    pe = [x / esum for x in pe]
    v1 = functools.reduce(jnp.maximum, pe)
    i1 = jnp.full((1, tm), EXPERTS_PER_GROUP - 1, I32)
    for r in range(EXPERTS_PER_GROUP - 2, -1, -1):
        i1 = jnp.where(pe[r] == v1, r, i1)
    pe2 = [jnp.where(i1 == r, -1.0, pe[r]) for r in range(EXPERTS_PER_GROUP)]
    v2 = functools.reduce(jnp.maximum, pe2)
    i2 = jnp.full((1, tm), EXPERTS_PER_GROUP - 1, I32)
    for r in range(EXPERTS_PER_GROUP - 2, -1, -1):
        i2 = jnp.where(pe2[r] == v2, r, i2)
    den = v1 + v2
    e0 = gidx * EXPERTS_PER_GROUP + i1
    e1 = gidx * EXPERTS_PER_GROUP + i2
    e_ref[...] = jnp.concatenate([e0, e1], axis=0)
    w_ref[...] = jnp.concatenate([g_val * (v1 / den), g_val * (v2 / den)], axis=0)
    eio = lax.broadcasted_iota(I32, (N_EXPERTS, tm), 0)
    hit = jnp.where((eio == e0) | (eio == e1), 1.0, 0.0)
    cnt_ref[0] = jnp.sum(hit, axis=1, keepdims=True)


def _outproj(yg, yh, yy, h, w_out, g, b, wr_t, br, alpha, tm=512):
    T, D = h.shape
    nr = wr_t.shape[0]
    wr_hi = wr_t.astype(BF16)
    wr_lo = (wr_t - wr_hi.astype(F32)).astype(BF16)
    full = lambda shp: pl.BlockSpec(shp, lambda i: (0,) * len(shp))
    return pl.pallas_call(
        functools.partial(_outproj_kernel, alpha=alpha),
        grid=(T // tm,),
        in_specs=[pl.BlockSpec((GLA_W, tm), lambda i: (0, i)),
                  pl.BlockSpec((HG_W, tm), lambda i: (0, i)),
                  pl.BlockSpec((HY_WIDTH, tm), lambda i: (0, i)),
                  pl.BlockSpec((tm, D), lambda i: (i, 0)),
                  full((D, D)), full((1, D)), full((1, D)), full((nr, D)), full((nr, D)), full((nr, 1))],
        out_specs=[pl.BlockSpec((tm, D), lambda i: (i, 0)),
                   pl.BlockSpec((TOP_K, tm), lambda i: (0, i)),
                   pl.BlockSpec((TOP_K, tm), lambda i: (0, i)),
                   pl.BlockSpec((1, N_EXPERTS, 1), lambda i: (i, 0, 0))],
        out_shape=[jax.ShapeDtypeStruct((T, D), F32),
                   jax.ShapeDtypeStruct((TOP_K, T), I32),
                   jax.ShapeDtypeStruct((TOP_K, T), F32),
                   jax.ShapeDtypeStruct((T // tm, N_EXPERTS, 1), F32)],
        compiler_params=_cparams(("arbitrary",)),
        name="outproj",
    )(yg, yh, yy, h, w_out.astype(BF16), g.reshape(1, D), b.reshape(1, D), wr_hi, wr_lo, br)


def _chunk_loop(n, fn):
    def body(c, carry):
        fn(pl.multiple_of(c * ROW_CHUNK, ROW_CHUNK))
        return carry
    lax.fori_loop(0, n, body, 0)


def _segment_copies(nch_s, k, fn):
    off = 0
    for si, rows in enumerate(SEG_SIZES):
        n = nch_s[k * len(SEG_SIZES) + si]

        def body(c, carry, off=off, rows=rows):
            fn(rows, pl.multiple_of(off + c * rows, ROW_CHUNK))
            return carry
        lax.fori_loop(0, n, body, 0)
        off = off + n * rows


def _wait_tile_copies(nch_s, tile, wait_one):
    for si, rows in enumerate(SEG_SIZES):
        tot = functools.reduce(lambda a, b: a + b, [nch_s[(tile * N_EXPERTS + e) * len(SEG_SIZES) + si]
                                                    for e in range(N_EXPERTS)])
        lax.fori_loop(0, tot, lambda c, carry, rows=rows: (wait_one(rows), carry)[1], 0)


def _dispatch_kernel(offs_s, gst_s, nch_s, tst_s, tn_s, nu_s, h_ref, e_ref, base_ref, tri_ref,
                     pos_ref, xg_hbm, xs_ref, zbuf, sem, zsem, *, tm, LP, bm, nblocks):
    i = pl.program_id(0)
    nt = pl.num_programs(0)
    slot = i % 2

    def seg_copy(s, rows, src_row, dst_row):
        return pltpu.make_async_copy(xs_ref.at[s, pl.ds(src_row, rows)],
                                     xg_hbm.at[pl.ds(dst_row, rows)], sem.at[s])

    def zero_copy(dst_row):
        return pltpu.make_async_copy(zbuf.at[pl.ds(0, ROW_CHUNK)], xg_hbm.at[pl.ds(dst_row, ROW_CHUNK)], zsem.at[0])

    def zero_block(blk):
        return pltpu.make_async_copy(zbuf, xg_hbm.at[pl.ds(pl.multiple_of(blk * bm, bm), bm)], zsem.at[0])

    def wait_tile(tile, s):
        _wait_tile_copies(nch_s, tile, lambda rows: seg_copy(s, rows, 0, 0).wait())

    @pl.when(i == 0)
    def _():
        zbuf[...] = jnp.zeros_like(zbuf)
        for e in range(N_EXPERTS):
            _chunk_loop(tn_s[e], lambda off, e=e: zero_copy(pl.multiple_of(tst_s[e] + off, ROW_CHUNK)).start())
        lax.fori_loop(nu_s[0], nblocks, lambda blk, c: (zero_block(blk).start(), c)[1], 0)
        for e in range(N_EXPERTS):
            _chunk_loop(tn_s[e], lambda off: zero_copy(0).wait())
        lax.fori_loop(nu_s[0], nblocks, lambda blk, c: (zero_block(0).wait(), c)[1], 0)

    e0, e1 = e_ref[0:1, :], e_ref[1:2, :]
    eio = lax.broadcasted_iota(I32, (N_EXPERTS, tm), 0)
    oh0, oh1 = eio == e0, eio == e1
    hit = jnp.where(oh0 | oh1, 1.0, 0.0).astype(BF16)
    posm = base_ref[0] + _dot(hit, tri_ref[...])
    pos0 = jnp.sum(jnp.where(oh0, posm, 0.0), axis=0, keepdims=True).astype(I32)
    pos1 = jnp.sum(jnp.where(oh1, posm, 0.0), axis=0, keepdims=True).astype(I32)
    pos_ref[...] = jnp.concatenate([pos0, pos1], axis=0)
    hb = h_ref[...].astype(BF16)
    rb = 256
    assert LP % rb == 0
    rio = lax.broadcasted_iota(I32, (rb, tm), 0)
    for r0 in range(0, LP, rb):
        perm = jnp.where((rio == pos0 - r0) | (rio == pos1 - r0), 1.0, 0.0).astype(BF16)
        xs_ref[slot, r0:r0 + rb] = _dot(perm, hb).astype(BF16)

    @pl.when(i > 0)
    def _():
        wait_tile(i - 1, 1 - slot)

    for e in range(N_EXPERTS):
        k = i * N_EXPERTS + e
        src0, dst0 = offs_s[k], gst_s[k]
        _segment_copies(nch_s, k, lambda rows, off, src0=src0, dst0=dst0: seg_copy(
            slot, rows, pl.multiple_of(src0 + off, ROW_CHUNK), pl.multiple_of(dst0 + off, ROW_CHUNK)).start())

    @pl.when(i == nt - 1)
    def _():
        wait_tile(i, slot)


def _moe_dispatch(h1, e_kt, tables, nblocks, bm, tm):
    T, D = h1.shape
    NT = T // tm
    nrows = nblocks * bm
    LP = TOP_K * tm + N_EXPERTS * ROW_CHUNK
    r = np.arange(tm)
    tri = jnp.asarray(r[:, None] < r[None, :], F32).astype(BF16)
    base = tables["offs"].astype(F32).reshape(NT, N_EXPERTS, 1)
    flat = lambda a: a.reshape(-1).astype(I32)
    pos, xg = pl.pallas_call(
        functools.partial(_dispatch_kernel, tm=tm, LP=LP, bm=bm, nblocks=nblocks),
        grid_spec=pltpu.PrefetchScalarGridSpec(
            num_scalar_prefetch=6,
            grid=(NT,),
            in_specs=[pl.BlockSpec((tm, D), lambda i, *_: (i, 0)),
                      pl.BlockSpec((TOP_K, tm), lambda i, *_: (0, i)),
                      pl.BlockSpec((1, N_EXPERTS, 1), lambda i, *_: (i, 0, 0)),
                      pl.BlockSpec((tm, tm), lambda i, *_: (0, 0))],
            out_specs=[pl.BlockSpec((TOP_K, tm), lambda i, *_: (0, i)),
                       pl.BlockSpec(memory_space=pl.ANY)],
            scratch_shapes=[pltpu.VMEM((2, LP, D), BF16), pltpu.VMEM((bm, D), BF16),
                            pltpu.SemaphoreType.DMA((2,)), pltpu.SemaphoreType.DMA((1,))]),
        out_shape=[jax.ShapeDtypeStruct((TOP_K, T), I32), jax.ShapeDtypeStruct((nrows, D), BF16)],
        compiler_params=_cparams(("arbitrary",)),
        name="moe_dispatch",
    )(flat(tables["offs"]), flat(tables["gstart"]), flat(tables["nch"]), flat(tables["tail_start"]),
      flat(tables["tail_n"]), tables["nused"], h1, e_kt, base, tri)
    return pos, xg


def _ffn_kernel(be_ref, nu_ref, run_ref, ea_ref, eb_ref, x_ref, wga_ref, wua_ref, wda_ref, wgb_ref, wub_ref, wdb_ref,
                y_ref, wgb, wub, wdb):
    j = pl.program_id(0)
    used = j < nu_ref[0]
    new_run = (j == 0) | (be_ref[j] != be_ref[jnp.maximum(j - 1, 0)])
    even = run_ref[j] % 2 == 0

    def cast_from(wg_ref, wu_ref, wd_ref):
        wgb[...] = wg_ref[0].astype(BF16)
        wub[...] = wu_ref[0].astype(BF16)
        wdb[...] = wd_ref[0].astype(BF16)

    @pl.when(new_run & even)
    def _():
        cast_from(wga_ref, wua_ref, wda_ref)

    @pl.when(new_run & jnp.logical_not(even))
    def _():
        cast_from(wgb_ref, wub_ref, wdb_ref)

    @pl.when(used)
    def _():
        x = x_ref[...]
        a = _dot(x, wgb[...])
        ah = 0.5 * a
        hid = (ah + ah * jnp.tanh(ah)) * _dot(x, wub[...])
        y_ref[...] = _dot(hid.astype(BF16), wdb[...]).astype(BF16)

    @pl.when(jnp.logical_not(used))
    def _():
        y_ref[...] = jnp.zeros_like(y_ref)


def _moe_ffn(xg, block_e, nused, wg, wu, wd, bm, first_expert=0):
    nrows, D = xg.shape
    NB = nrows // bm
    DE = wg.shape[-1]
    change = jnp.concatenate([jnp.zeros((1,), I32), (block_e[1:] != block_e[:-1]).astype(I32)])
    run = jnp.cumsum(change).astype(I32)
    run_e = jnp.full((NB + 1,), block_e[-1], I32).at[run].set(block_e)
    e_even = run_e[jnp.where(run % 2 == 0, run, run + 1)]
    e_odd = run_e[jnp.where(run % 2 == 1, run, run + 1)]
    row = lambda j, be, nu, *_: (jnp.minimum(j, nu[0] - 1), 0)
    sel_a = lambda j, be, nu, rn, ea, eb: (first_expert + ea[j], 0, 0)
    sel_b = lambda j, be, nu, rn, ea, eb: (first_expert + eb[j], 0, 0)
    return pl.pallas_call(
        _ffn_kernel,
        grid_spec=pltpu.PrefetchScalarGridSpec(
            num_scalar_prefetch=5,
            grid=(NB,),
            in_specs=[pl.BlockSpec((bm, D), row),
                      pl.BlockSpec((1, D, DE), sel_a), pl.BlockSpec((1, D, DE), sel_a), pl.BlockSpec((1, DE, D), sel_a),
                      pl.BlockSpec((1, D, DE), sel_b), pl.BlockSpec((1, D, DE), sel_b), pl.BlockSpec((1, DE, D), sel_b)],
            out_specs=pl.BlockSpec((bm, D), lambda j, *_: (j, 0)),
            scratch_shapes=[pltpu.VMEM((D, DE), BF16), pltpu.VMEM((D, DE), BF16), pltpu.VMEM((DE, D), BF16)]),
        out_shape=jax.ShapeDtypeStruct((nrows, D), BF16),
        compiler_params=_cparams(("arbitrary",)),
        name="moe_ffn",
    )(block_e, nused, run, e_even, e_odd, xg, wg, wu, wd, wg, wu, wd)


def _combine_kernel(offs_s, gst_s, nch_s, yb_hbm, pos_ref, w_ref, h_ref, g_ref, b_ref, o_ref, ybl, sem,
                    *, tm, LP, alpha):
    i = pl.program_id(0)
    nt = pl.num_programs(0)
    slot = i % 2

    def seg_copy(s, rows, src_row, dst_row):
        return pltpu.make_async_copy(yb_hbm.at[pl.ds(src_row, rows)],
                                     ybl.at[s, pl.ds(dst_row, rows)], sem.at[s])

    def issue(tile, s):
        for e in range(N_EXPERTS):
            k = tile * N_EXPERTS + e
            src0, dst0 = gst_s[k], offs_s[k]
            _segment_copies(nch_s, k, lambda rows, off, src0=src0, dst0=dst0: seg_copy(
                s, rows, pl.multiple_of(src0 + off, ROW_CHUNK), pl.multiple_of(dst0 + off, ROW_CHUNK)).start())

    @pl.when(i == 0)
    def _():
        ybl[...] = jnp.zeros_like(ybl)
        issue(0, 0)

    @pl.when(i + 1 < nt)
    def _():
        issue(i + 1, 1 - slot)

    _wait_tile_copies(nch_s, i, lambda rows: seg_copy(slot, rows, 0, 0).wait())
    pos0, pos1 = pos_ref[0:1, :], pos_ref[1:2, :]
    w0, w1 = w_ref[0:1, :], w_ref[1:2, :]
    rio = lax.broadcasted_iota(I32, (LP, LANES), 0)
    yl = ybl[slot]
    for t0 in range(0, tm, LANES):
        tl = slice(t0, t0 + LANES)
        pw = (jnp.where(rio == pos0[:, tl], w0[:, tl], 0.0) + jnp.where(rio == pos1[:, tl], w1[:, tl], 0.0)).astype(BF16)
        ffn = _dot(pw, yl, _TN)
        o_ref[tl, :] = _layer_norm(alpha * h_ref[tl, :] + ffn, g_ref[...], b_ref[...])


def _moe_combine(yb, h1, pos, w_kt, tables, g, b, alpha, tm):
    T, D = h1.shape
    LP = TOP_K * tm + N_EXPERTS * ROW_CHUNK
    flat = lambda a: a.reshape(-1).astype(I32)
    return pl.pallas_call(
        functools.partial(_combine_kernel, tm=tm, LP=LP, alpha=alpha),
        grid_spec=pltpu.PrefetchScalarGridSpec(
            num_scalar_prefetch=3,
            grid=(T // tm,),
            in_specs=[pl.BlockSpec(memory_space=pl.ANY),
                      pl.BlockSpec((TOP_K, tm), lambda i, *_: (0, i)),
                      pl.BlockSpec((TOP_K, tm), lambda i, *_: (0, i)),
                      pl.BlockSpec((tm, D), lambda i, *_: (i, 0)),
                      pl.BlockSpec((1, D), lambda i, *_: (0, 0)),
                      pl.BlockSpec((1, D), lambda i, *_: (0, 0))],
            out_specs=pl.BlockSpec((tm, D), lambda i, *_: (i, 0)),
            scratch_shapes=[pltpu.VMEM((2, LP, D), BF16), pltpu.SemaphoreType.DMA((2,))]),
        out_shape=jax.ShapeDtypeStruct((T, D), F32),
        compiler_params=_cparams(("arbitrary",)),
        name="moe_combine",
    )(flat(tables["offs"]), flat(tables["gstart"]), flat(tables["nch"]), yb, pos, w_kt, h1,
      g.reshape(1, D), b.reshape(1, D))


def _dispatch_tables(cnt, bm, nblocks):
    padlen = ((cnt + ROW_CHUNK - 1) // ROW_CHUNK) * ROW_CHUNK
    offs = jnp.cumsum(padlen, axis=1) - padlen
    tot = jnp.sum(padlen, axis=0)
    region = ((tot + bm - 1) // bm) * bm
    rend = jnp.cumsum(region)
    rstart = rend - region
    gstart = rstart[None, :] + jnp.cumsum(padlen, axis=0) - padlen
    blk_row = jnp.arange(nblocks, dtype=I32)[:, None] * bm
    block_e = jnp.minimum(jnp.sum((blk_row >= rend[None, :]).astype(I32), axis=1), N_EXPERTS - 1)
    nch, rest = [], padlen
    for rows in SEG_SIZES:
        nch.append(rest // rows)
        rest = rest % rows
    return dict(offs=offs, gstart=gstart, nch=jnp.stack(nch, axis=-1), tail_start=rstart + tot,
                tail_n=(region - tot) // ROW_CHUNK, block_e=block_e.astype(I32),
                nused=(rend[-1:] // bm).astype(I32))


def _permute_in_columns(w):
    splits = (192, 192, 384, 384, 32, 384, 768, 384, 384, 768)
    offs = np.concatenate([[0], np.cumsum(splits)])
    gq, gk, gv, gg, ga, hq, hf, hi, hgt, hyu = [(int(offs[i]), int(offs[i + 1])) for i in range(10)]
    order = [hq, hi, hgt, hf, gv, gg, gq, gk, hyu, ga]
    halved = (hq, hgt, hf, gg)
    assert sum(b - a for a, b in order) == D_IN
    return jnp.concatenate([w[:, a:b] * 0.5 if (a, b) in halved else w[:, a:b] for a, b in order], axis=1)


def kernel(x, ln_in_g, ln_in_b, w_in, gla_wa2, gla_ba, gla_norm_g, hg_lb_logits, hg_norm_g, hy_conv_w, hy_conv_b, hy_w1, hy_b1, hy_freq, hy_w2, hy_b2, hy_w3, hy_bias, w_out, ln1_g, ln1_b, moe_wr_g, moe_br_g, moe_wr_e, moe_br_e, moe_w_gate, moe_w_up, moe_w_down, ln2_g, ln2_b):
    B, L, D = x.shape
    T = B * L
    depth = w_in.shape[0]
    alpha = (2 * depth) ** 0.25
    bm = 512
    tmr = 512
    nblocks = -(-(T * TOP_K + (T // tmr) * N_EXPERTS * (ROW_CHUNK - 1) + N_EXPERTS * (bm - 1)) // bm)
    consts = _dft_consts(L)

    p = jax.nn.softmax(hg_lb_logits.astype(F32), axis=0)
    lbs = jnp.cumsum(p, axis=0) - p[0:1]
    lbc = jnp.concatenate([jnp.broadcast_to((0.5 * (1.0 - lbs))[..., None], lbs.shape + (LANES,)),
                           jnp.broadcast_to(jnp.maximum(lbs, LB_FLOOR)[..., None], lbs.shape + (LANES,))], axis=-1)

    wg_all = moe_w_gate.reshape((depth * N_EXPERTS,) + moe_w_gate.shape[2:])
    wu_all = moe_w_up.reshape((depth * N_EXPERTS,) + moe_w_up.shape[2:])
    wd_all = moe_w_down.reshape((depth * N_EXPERTS,) + moe_w_down.shape[2:])
    h = x.reshape(T, D)
    for l in range(depth):
        w_t = _permute_in_columns(w_in[l]).T.astype(BF16)
        proj_t, h = _inproj(h, ln_in_g, ln_in_b, w_t, apply_ln=(l == 0))
        wa_t = jnp.swapaxes(gla_wa2[l], 1, 2)
        y_gla = _scan(proj_t, "gla", B, L, (wa_t, gla_ba[l].reshape(2, GLA_K, 1)), gla_norm_g[l].reshape(GLA_W, 1))
        y_hg = _scan(proj_t, "hg", B, L, (lbc[l],), hg_norm_g[l].reshape(HG_W, 1))
        kt = _hyena_filters(L, hy_w1[l], hy_b1[l], hy_freq[l], hy_w2[l], hy_b2[l], hy_w3[l])
        kspec = _hyena_spectrum(kt, consts, L)
        y_hy = _hyena(proj_t, kspec, consts, hy_conv_w[l], hy_conv_b[l], hy_bias[l], B, L)
        nr = N_GROUPS + N_EXPERTS
        nrp = -(-nr // 8) * 8
        wr_t = jnp.zeros((nrp, D), F32).at[:nr].set(jnp.concatenate([moe_wr_g[l], moe_wr_e[l]], axis=1).T)
        br = jnp.zeros((nrp, 1), F32).at[:nr, 0].set(jnp.concatenate([moe_br_g[l], moe_br_e[l]]))
        h1, e_kt, w_kt, cnt = _outproj(y_gla, y_hg, y_hy, h, w_out[l], ln1_g[l], ln1_b[l], wr_t, br, alpha, tm=tmr)
        tables = _dispatch_tables(cnt.reshape(T // tmr, N_EXPERTS).astype(I32), bm, nblocks)
        pos, xg = _moe_dispatch(h1, e_kt, tables, nblocks, bm, tmr)
        yb = _moe_ffn(xg, tables["block_e"], tables["nused"], wg_all, wu_all, wd_all, bm, first_expert=l * N_EXPERTS)
        h = _moe_combine(yb, h1, pos, w_kt, tables, ln2_g[l], ln2_b[l], alpha, tmr)
    return h.reshape(B, L, D)
```

```python
import functools
import math

import numpy as np
import jax
import jax.numpy as jnp
from jax import lax
from jax.experimental import pallas as pl
from jax.experimental.pallas import tpu as pltpu

F32 = jnp.float32
BF16 = jnp.bfloat16
I32 = jnp.int32

GLA_HEADS, GLA_DK, GLA_DV, GLA_LOWRANK, GLA_TAU = 6, 32, 64, 16, 16.0
HG_HEADS, HG_DK, HG_DV = 6, 64, 64
HY_WIDTH, HY_ORDER, HY_EMB, HY_FFN, HY_INNER = 256, 2, 33, 64, 2
HY_FAST_DECAY, HY_SLOW_DECAY, HY_TARGET = 0.3, 1.5, 1e-2
N_GROUPS, EXPERTS_PER_GROUP = 4, 4
N_EXPERTS = N_GROUPS * EXPERTS_PER_GROUP
TOP_K = 2
LN_EPS, RMS_EPS, LB_FLOOR = 1e-5, 1e-6, 1e-30
LOG2E = 1.4426950408889634

LANES = 128
SCAN_CHUNK = LANES
ROW_CHUNK = 16
SEG_SIZES = (64, 32, ROW_CHUNK)
VMEM_LIMIT = 56 * 1024 * 1024

GLA_W = GLA_HEADS * GLA_DV
GLA_K = GLA_HEADS * GLA_DK
HG_W = HG_HEADS * HG_DV
HG_K = HG_HEADS * HG_DK
OFF_HQ, OFF_HI, OFF_HGT, OFF_HF = 0, 384, 768, 1152
OFF_GV, OFF_GG, OFF_GQ, OFF_GK, OFF_HY, OFF_GA = 1920, 2304, 2688, 2880, 3072, 3840
D_IN = 3872


def _dot(a, b, dims=(((1,), (0,)), ((), ())), precision=None):
    return lax.dot_general(a, b, dims, preferred_element_type=F32, precision=precision)


_NT = (((1,), (1,)), ((), ()))
_TN = (((0,), (0,)), ((), ()))


def _layer_norm(x, g, b):
    mu = jnp.mean(x, axis=-1, keepdims=True)
    xc = x - mu
    var = jnp.mean(xc * xc, axis=-1, keepdims=True)
    return xc * lax.rsqrt(var + LN_EPS) * g + b


def _log_sigmoid(x):
    return jnp.minimum(x, 0.0) - jnp.log(1.0 + jnp.exp(-jnp.abs(x)))


def _cparams(sem):
    return pltpu.CompilerParams(dimension_semantics=sem, vmem_limit_bytes=VMEM_LIMIT)


def _inproj_kernel(x_ref, g_ref, b_ref, w_ref, *outs, apply_ln):
    x = x_ref[...]
    if apply_ln:
        x = _layer_norm(x, g_ref[...], b_ref[...])
        outs[1][...] = x
    outs[0][...] = _dot(w_ref[...], x.astype(BF16), _NT)


def _inproj(x, g, b, w_t, apply_ln, tm=512):
    T, D = x.shape
    n_out = w_t.shape[0]
    out_shape = [jax.ShapeDtypeStruct((n_out, T), F32)]
    out_specs = [pl.BlockSpec((n_out, tm), lambda i: (0, i))]
    if apply_ln:
        out_shape.append(jax.ShapeDtypeStruct((T, D), F32))
        out_specs.append(pl.BlockSpec((tm, D), lambda i: (i, 0)))
    res = pl.pallas_call(
        functools.partial(_inproj_kernel, apply_ln=apply_ln),
        grid=(T // tm,),
        in_specs=[pl.BlockSpec((tm, D), lambda i: (i, 0)),
                  pl.BlockSpec((1, D), lambda i: (0, 0)),
                  pl.BlockSpec((1, D), lambda i: (0, 0)),
                  pl.BlockSpec((n_out, D), lambda i: (0, 0))],
        out_specs=out_specs,
        out_shape=out_shape,
        compiler_params=_cparams(("arbitrary",)),
        name="inproj",
    )(x, g.reshape(1, D), b.reshape(1, D), w_t)
    return res if apply_ln else (res[0], x)


def _scan_kernel(*refs, mode, H, K, V, TB, NCB):
    if mode == "gla":
        (qf_ref, qb_ref, kf_ref, kb_ref, vf_ref, vb_ref, gaf_ref, gab_ref, wa_ref, ba_ref,
         gtf_ref, gtb_ref, ng_ref, y_ref, s_ref, oacc_ref, oi_ref, qg_ref, ut_ref, dr_ref) = refs
        q_refs, k_refs, ga_refs = (qf_ref, qb_ref), (kf_ref, kb_ref), (gaf_ref, gab_ref)
    else:
        (qf_ref, qb_ref, zf_ref, zb_ref, vf_ref, vb_ref, lbc_ref,
         gtf_ref, gtb_ref, ng_ref, y_ref, s_ref, oacc_ref, oi_ref, qg_ref, ut_ref, dr_ref) = refs
        q_refs, z_refs = (qf_ref, qb_ref), (zf_ref, zb_ref)
    v_refs, gate_refs = (vf_ref, vb_ref), (gtf_ref, gtb_ref)
    C = SCAN_CHUNK
    half = C // 2
    nchunks = TB // C
    n = pl.program_id(1)
    blocks = (n, NCB - 1 - n)

    @pl.when(n == 0)
    def _():
        s_ref[...] = jnp.zeros_like(s_ref)

    def gates(d, sl):
        if mode == "gla":
            wa, ga = wa_ref[d], ga_refs[d][:, sl]
            wa_hi, ga_hi = wa.astype(BF16), ga.astype(BF16)
            wa_lo = (wa - wa_hi.astype(F32)).astype(BF16)
            ga_lo = (ga - ga_hi.astype(F32)).astype(BF16)
            a = _dot(wa_hi, ga_hi) + _dot(wa_hi, ga_lo) + _dot(wa_lo, ga_hi) + ba_ref[d]
            g = _log_sigmoid(a) * (LOG2E / GLA_TAU)
            q = q_refs[d][:, sl] * (K ** -0.5)
            k = k_refs[d][:, sl]
        else:
            half_lb, lb_floor = lbc_ref[d, :, 0:C], lbc_ref[d, :, C:2 * C]
            ht = half_lb * jnp.tanh(z_refs[d][:, sl])
            g = jnp.log2(half_lb + ht + lb_floor)
            k = half_lb - ht
            hq = q_refs[d][:, sl]
            q = hq + hq * jnp.tanh(hq)
        return q, k, g

    r = lax.broadcasted_iota(I32, (C, C), 0)
    c = lax.broadcasted_iota(I32, (C, C), 1)
    lane_lo = lax.broadcasted_iota(I32, (1, C), 1) < half
    sign_lo = jnp.where(lane_lo, 1.0, -1.0)
    consts = (((r <= c).astype(BF16), r <= c, (r < half) & (c >= half), (C - 1, half, half // 2, half + half // 2)),
              ((r >= c).astype(BF16), r >= c, (r >= half) & (c < half),
               (0, half - 1, half // 2 - 1, half + half // 2 - 1)))

    heads = [(slice(h * K, (h + 1) * K), slice(h * V, (h + 1) * V)) for h in range(H)]

    def cumulate(d, g):
        cum = consts[d][0]
        g1 = g.astype(BF16)
        g2 = (g - g1.astype(F32)).astype(BF16)
        G = _dot(g1, cum) + _dot(g2, cum)
        ge_row = _dot(ones8, g1, _NT) + _dot(ones8, g2, _NT)
        return G, ge_row

    def scale(d, sl, q, k, G, ge_row):
        _, _, _, (c_end, c_mid, c_a, c_b) = consts[d]
        v = v_refs[d][:, sl].astype(BF16)
        g_end = G[:, c_end:c_end + 1]
        g_mid = G[:, c_mid:c_mid + 1]
        e2 = G - jnp.where(lane_lo, G[:, c_a:c_a + 1], G[:, c_b:c_b + 1])
        q2 = (q * jnp.exp2(e2)).astype(BF16)
        k2 = (k * jnp.exp2(-e2)).astype(BF16)
        dm = G - g_mid
        x1 = jnp.exp2(dm * (sign_lo if d else -sign_lo))
        q1 = (q * x1).astype(BF16)
        k1 = (k * x1).astype(BF16)
        qg = (q * jnp.exp2(G)).astype(BF16)
        kd = (k * jnp.exp2(g_end - G)).astype(BF16)
        x = dict(qg=qg, kd=kd, v=v, dec_row=jnp.exp2(ge_row))
        if merge_scores:
            first = lambda a: jnp.where(lane_lo, a, jnp.zeros_like(a))
            second = lambda a: jnp.where(lane_lo, jnp.zeros_like(a), a)
            x["kparts"] = ((second if d else first)(k1), first(k2), second(k2))
            x["qparts"] = ((first if d else second)(q1), first(q2), second(q2))
        else:
            x.update(q1=q1, k1=k1, q2=q2, k2=k2)
        return x

    ones8 = jnp.ones((8, C), BF16)
    merge_scores = 3 * K <= LANES

    def scores(x):
        if merge_scores:
            return [_dot(jnp.concatenate([p[rk] for p in x["kparts"]], axis=0),
                         jnp.concatenate([p[rk] for p in x["qparts"]], axis=0), _TN) for rk, _ in heads]
        return [(_dot(x["k1"][rk], x["q1"][rk], _TN), _dot(x["k2"][rk], x["q2"][rk], _TN)) for rk, _ in heads]

    def masked(d, ps):
        _, tri, off1, _ = consts[d]
        if merge_scores:
            return [jnp.where(tri, p, 0.0).astype(BF16) for p in ps]
        return [jnp.where(off1, p1, jnp.where(tri, p2, 0.0)).astype(BF16) for p1, p2 in ps]

    def park(d, ci, x, pm):
        oi_ref[d, ci] = jnp.concatenate([_dot(x["v"][rv], pm[h]) for h, (_, rv) in enumerate(heads)], axis=0)
        qg_ref[d, ci] = x["qg"]
        dr_ref[d, ci] = x["dec_row"]
        for h, (rk, rv) in enumerate(heads):
            ut_ref[d, ci, h] = _dot(x["v"][rv], x["kd"][rk], _NT)

    def chunk_slices(ci):
        offs = (pl.multiple_of(ci * C, C), pl.multiple_of((nchunks - 1 - ci) * C, C))
        return offs, [pl.ds(offs[d], C) for d in range(2)]

    def independent(it, carry):
        jobs = [(d, U * it + u, chunk_slices(U * it + u)[1][d]) for u in range(U) for d in range(2)]
        qkg = [gates(d, sl) for d, _, sl in jobs]
        cums = [cumulate(d, qkg[i][2]) for i, (d, _, _) in enumerate(jobs)]
        xs = [scale(d, sl, qkg[i][0], qkg[i][1], *cums[i]) for i, (d, _, sl) in enumerate(jobs)]
        scs = [scores(x) for x in xs]
        pms = [masked(d, scs[i]) for i, (d, _, _) in enumerate(jobs)]
        for i, (d, ci, _) in enumerate(jobs):
            park(d, ci, xs[i], pms[i])
        return carry

    U = 4
    assert nchunks % U == 0
    lax.fori_loop(0, nchunks // U, independent, 0)

    def carried(ci):
        st = [[s_ref[d, h] for h in range(H)] for d in range(2)]
        inter = [[_dot(st[d][h].astype(BF16), qg_ref[d, ci, rk, :]) for h, (rk, _) in enumerate(heads)]
                 for d in range(2)]
        for d in range(2):
            for h, (rk, _) in enumerate(heads):
                s_ref[d, h] = st[d][h] * dr_ref[d, ci, 0:1, rk] + ut_ref[d, ci, h]
        return [oi_ref[d, ci] + jnp.concatenate(inter[d], axis=0) for d in range(2)]

    def finish(o, gate):
        t = jnp.tanh(gate)
        act = gate + gate * t if mode == "gla" else 0.5 + 0.5 * t
        ys = []
        for h in range(H):
            oh = o[h * V:(h + 1) * V]
            ms = jnp.mean(oh * oh, axis=0, keepdims=True)
            ys.append(oh * lax.rsqrt(ms + RMS_EPS))
        return (jnp.concatenate(ys, axis=0) * ng_ref[...] * act).astype(y_ref.dtype)

    def chunk(ci, carry):
        offs, sls = chunk_slices(ci)
        tsls = [pl.ds(pl.multiple_of(blocks[d] * TB + offs[d], C), C) for d in range(2)]
        o = carried(ci)

        @pl.when(2 * n < NCB)
        def _():
            for d in range(2):
                oacc_ref[:, tsls[d]] = o[d]

        @pl.when(2 * n >= NCB)
        def _():
            tot = [o[d] + oacc_ref[:, tsls[d]] for d in range(2)]
            for d in range(2):
                y_ref[:, tsls[d]] = finish(tot[d], gate_refs[d][:, sls[d]])
        return carry

    lax.fori_loop(0, nchunks, chunk, 0)


def _scan(proj_t, mode, B, L, extra, norm_g, TB=512):
    NCB = L // TB
    nch = TB // SCAN_CHUNK
    assert NCB % 2 == 0
    if mode == "gla":
        H, K, V = GLA_HEADS, GLA_DK, GLA_DV
    else:
        H, K, V = HG_HEADS, HG_DK, HG_DV
    HK, HV = H * K, H * V
    cf = lambda b, n: b * NCB + n
    cb = lambda b, n: b * NCB + NCB - 1 - n
    gf = lambda b, n: b * NCB + jnp.maximum(n, NCB // 2)
    gb = lambda b, n: b * NCB + jnp.minimum(NCB - 1 - n, NCB // 2 - 1)

    def pair(rows, off, fwd=cf, bwd=cb, dir_step=0):
        return [pl.BlockSpec((rows, TB), lambda b, n: (off // rows, fwd(b, n))),
                pl.BlockSpec((rows, TB), lambda b, n: (off // rows + dir_step, bwd(b, n)))]

    whole = lambda shp: pl.BlockSpec(shp, lambda b, n: (0,) * len(shp))
    if mode == "gla":
        wa_t, ba = extra
        in_specs = (pair(HK, OFF_GQ) + pair(HK, OFF_GK) + pair(HV, OFF_GV)
                    + pair(GLA_LOWRANK, OFF_GA, dir_step=1)
                    + [whole((2, HK, GLA_LOWRANK)), whole((2, HK, 1))]
                    + pair(HV, OFF_GG, gf, gb) + [whole((HV, 1))])
        args = (proj_t,) * 8 + (wa_t, ba, proj_t, proj_t, norm_g)
    else:
        (lbc,) = extra
        in_specs = (pair(HK, OFF_HQ) + pair(HK, OFF_HF, dir_step=1) + pair(HV, OFF_HI)
                    + [whole((2, HK, 2 * LANES))] + pair(HV, OFF_HGT, gf, gb) + [whole((HV, 1))])
        args = (proj_t,) * 6 + (lbc, proj_t, proj_t, norm_g)
    return pl.pallas_call(
        functools.partial(_scan_kernel, mode=mode, H=H, K=K, V=V, TB=TB, NCB=NCB),
        grid=(B, NCB),
        in_specs=in_specs,
        out_specs=pl.BlockSpec((HV, L), lambda b, n: (0, b)),
        out_shape=jax.ShapeDtypeStruct((HV, B * L), BF16),
        scratch_shapes=[pltpu.VMEM((2, H, V, K), F32), pltpu.VMEM((HV, L), F32),
                        pltpu.VMEM((2, nch, HV, SCAN_CHUNK), F32), pltpu.VMEM((2, nch, HK, SCAN_CHUNK), BF16),
                        pltpu.VMEM((2, nch, H, V, K), F32), pltpu.VMEM((2, nch, 8, HK), F32)],
        compiler_params=_cparams(("arbitrary", "arbitrary")),
        name="scan_" + mode,
    )(*args)


def _dft_consts(L):
    N = 2 * L
    NA = N // LANES
    a = np.arange(NA)[:, None] * np.arange(NA)[None, :]
    ca, sa = np.cos(2 * np.pi * a / NA), np.sin(2 * np.pi * a / NA)
    hh = NA // 2
    w1d = np.block([[ca[:, :hh], sa[:, :hh]], [-sa[:, :hh], ca[:, :hh]]])
    w1f = np.concatenate([ca, -sa], axis=0)
    w1i = np.block([[ca[:hh, :], -sa[:hh, :]], [sa[:hh, :], ca[:hh, :]]])
    bb = np.arange(LANES)[:, None] * np.arange(LANES)[None, :]
    cb, sb = np.cos(2 * np.pi * bb / LANES), np.sin(2 * np.pi * bb / LANES)
    w2 = np.block([[cb, -sb], [sb, cb]])
    w2i = np.block([[cb, sb], [-sb, cb]])
    tw = np.arange(NA)[:, None] * np.arange(LANES)[None, :]
    tc, ts = np.cos(2 * np.pi * tw / N), np.sin(2 * np.pi * tw / N)
    bf = lambda m: jnp.asarray(m, dtype=F32).astype(BF16)
    return dict(w1d=bf(w1d), w1f=bf(w1f), w1i=bf(w1i), w2=bf(w2), w2i=bf(w2i),
                tc=jnp.asarray(tc, F32), ts=jnp.asarray(ts, F32))


def _pos_features(L):
    t = np.linspace(0.0, 1.0, L)
    w = 2.0 * np.pi * np.arange(L) / L
    bands = np.linspace(1e-4, (HY_EMB - 1) // 2 - 1, (HY_EMB - 1) // 2)
    z = np.concatenate([t[None, :], np.cos(bands[:, None] * w[None, :]), -np.sin(bands[:, None] * w[None, :])], axis=0)
    kp = -(-HY_EMB // 8) * 8
    z = np.concatenate([z, np.zeros((kp - HY_EMB, L))], axis=0)
    t_rev = t[(L - np.arange(L)) % L]
    mask = (np.arange(L) >= 1).astype(np.float64)
    return (jnp.asarray(z, F32), jnp.asarray(t[None, :], F32),
            jnp.asarray(t_rev[None, :], F32), jnp.asarray(mask[None, :], F32))


def _filter_kernel(z_ref, t_ref, tr_ref, m_ref, w1_ref, b1_ref, fr_ref, w2_ref, b2_ref,
                   w3f_ref, w3b_ref, dl_ref, out_ref, h_ref, *, L):
    first = (pl.program_id(0) == 0) & (pl.program_id(1) == 0)
    hi = lax.Precision.HIGHEST

    @pl.when(first)
    def _():
        fr = fr_ref[...]
        h = jnp.sin(fr * (_dot(w1_ref[...], z_ref[...], precision=hi) + b1_ref[...]))
        for i in range(HY_INNER):
            h = jnp.sin(fr * (_dot(w2_ref[i], h, precision=hi) + b2_ref[i]))
        hb = h.astype(BF16)
        h_ref[0] = hb
        src = lax.broadcasted_iota(I32, (LANES, LANES), 0)
        dst = lax.broadcasted_iota(I32, (LANES, LANES), 1)
        flip_shift = jnp.where((dst >= 1) & (src == LANES - dst), 1.0, 0.0).astype(BF16)
        lane0 = jnp.where((dst == 0) & (src == 0), 1.0, 0.0).astype(BF16)
        nb = L // LANES
        tile = lambda b: hb[:, b * LANES:(b + 1) * LANES]
        for jb in range(nb):
            blk = _dot(tile(nb - 1 - jb), flip_shift) + _dot(tile((nb - jb) % nb), lane0)
            h_ref[1, :, jb * LANES:(jb + 1) * LANES] = blk.astype(BF16)

    ad = jnp.abs(dl_ref[...])
    kf = _dot(w3f_ref[...].astype(BF16), h_ref[0]) * jnp.exp(-t_ref[...] * ad)
    kb = _dot(w3b_ref[...].astype(BF16), h_ref[1]) * jnp.exp(-tr_ref[...] * ad) * m_ref[...]
    den = jnp.sum(jnp.abs(kf), axis=1, keepdims=True) + jnp.sum(jnp.abs(kb), axis=1, keepdims=True)
    scale = 1.0 / (jnp.maximum(den, 1e-12) * (2.0 * L))
    out_ref[0, :, 0:L] = kf * scale
    out_ref[0, :, L:2 * L] = kb * scale


def _hyena_filters(L, w1, b1, freq, w2, b2, w3, cg=64):
    z, t, t_rev, mask = _pos_features(L)
    kp = z.shape[0]
    w1_t = jnp.zeros((HY_FFN, kp), F32).at[:, :HY_EMB].set(w1.T)
    w2_t = jnp.swapaxes(w2, 1, 2)
    w3_t = w3.T
    max_decay = math.log(HY_TARGET) / HY_FAST_DECAY
    min_decay = math.log(HY_TARGET) / HY_SLOW_DECAY
    deltas = jnp.asarray(np.linspace(min_decay, max_decay, HY_WIDTH).reshape(HY_WIDTH, 1), F32)
    ncg = HY_WIDTH // cg
    full = lambda shp: pl.BlockSpec(shp, lambda o, j: (0,) * len(shp))
    return pl.pallas_call(
        functools.partial(_filter_kernel, L=L),
        grid=(HY_ORDER, ncg),
        in_specs=[full((kp, L)), full((1, L)), full((1, L)), full((1, L)),
                  full((HY_FFN, kp)), full((HY_FFN, 1)), full((HY_FFN, 1)),
                  full((HY_INNER, HY_FFN, HY_FFN)), full((HY_INNER, HY_FFN, 1)),
                  pl.BlockSpec((cg, HY_FFN), lambda o, j: (o * 2 * ncg + j, 0)),
                  pl.BlockSpec((cg, HY_FFN), lambda o, j: (o * 2 * ncg + ncg + j, 0)),
                  pl.BlockSpec((cg, 1), lambda o, j: (j, 0))],
        out_specs=pl.BlockSpec((1, cg, 2 * L), lambda o, j: (o, j, 0)),
        out_shape=jax.ShapeDtypeStruct((HY_ORDER, HY_WIDTH, 2 * L), F32),
        scratch_shapes=[pltpu.VMEM((2, HY_FFN, L), BF16)],
        compiler_params=_cparams(("arbitrary", "arbitrary")),
        name="hyena_filter",
    )(z, t, t_rev, mask, w1_t, b1.reshape(HY_FFN, 1), freq.reshape(HY_FFN, 1), w2_t,
      b2.reshape(HY_INNER, HY_FFN, 1), w3_t, w3_t, deltas)


def _spectrum_kernel(k2_ref, w1_ref, tc_ref, ts_ref, w2_ref, out_ref, k_ref, *, NA, cg):
    k_ref[0] = k2_ref[0].reshape(cg, NA, LANES)
    tc2 = jnp.concatenate([tc_ref[...]] * 2, axis=1)
    ts2 = jnp.concatenate([ts_ref[...]] * 2, axis=1)
    G = 8

    def body(it, carry):
        lhs = []
        for pr in range(G // 2):
            rhs = jnp.concatenate([k_ref[0, G * it + 2 * pr + cc] for cc in range(2)], axis=1)
            a = _dot(w1_ref[...], rhs.astype(BF16))
            a_re, a_im = a[:NA], a[NA:]
            b_re = a_re * tc2 + a_im * ts2
            b_im = a_im * tc2 - a_re * ts2
            lhs += [jnp.concatenate([b_re[:, cc * LANES:(cc + 1) * LANES], b_im[:, cc * LANES:(cc + 1) * LANES]], axis=1)
                    for cc in range(2)]
        x = _dot(jnp.concatenate(lhs, axis=0).astype(BF16), w2_ref[...])
        for i in range(G):
            out_ref[0, G * it + i] = x[i * NA:(i + 1) * NA]
        return carry
    lax.fori_loop(0, cg // G, body, 0)


def _hyena_spectrum(kt, consts, L, cg=32):
    NA = 2 * L // LANES
    full = lambda shp: pl.BlockSpec(shp, lambda o, j: (0,) * len(shp))
    return pl.pallas_call(
        functools.partial(_spectrum_kernel, NA=NA, cg=cg),
        grid=(HY_ORDER, HY_WIDTH // cg),
        in_specs=[pl.BlockSpec((1, cg, 2 * L), lambda o, j: (o, j, 0)),
                  full((2 * NA, NA)), full((NA, LANES)), full((NA, LANES)), full((2 * LANES, 2 * LANES))],
        out_specs=pl.BlockSpec((1, cg, NA, 2 * LANES), lambda o, j: (o, j, 0, 0)),
        out_shape=jax.ShapeDtypeStruct((HY_ORDER, HY_WIDTH, NA, 2 * LANES), F32),
        scratch_shapes=[pltpu.VMEM((1, cg, NA, LANES), F32)],
        compiler_params=_cparams(("arbitrary", "arbitrary")),
        name="hyena_spectrum",
    )(kt, consts["w1f"], consts["tc"], consts["ts"], consts["w2"])


def _hyena_kernel(cw_ref, cb_ref, hb_ref, v2_ref, x12_ref, x22_ref, ks_ref, w1d_ref, w1i_ref, tc_ref, ts_ref,
                  w2_ref, w2i_ref, y2_ref, v_ref, x1_ref, x2_ref, y_ref, *, NA, cg, B):
    hh = NA // 2
    L = hh * LANES
    j = pl.program_id(0)
    for src, dst in ((v2_ref, v_ref), (x12_ref, x1_ref), (x22_ref, x2_ref)):
        for b in range(B):
            dst[:, b] = src[:, b * L:(b + 1) * L].reshape(cg, hh, LANES)
    row = lax.broadcasted_iota(I32, (hh, LANES), 0)
    lane = lax.broadcasted_iota(I32, (hh, LANES), 1)
    first = (row == 0) & (lane == 0)
    last = (row == hh - 1) & (lane == LANES - 1)

    def short_conv(x, ch):
        r1 = pltpu.roll(x, 1, 1)
        prev = jnp.where(lane == 0, pltpu.roll(r1, 1, 0), r1)
        prev = jnp.where(first, 0.0, prev)
        r2 = pltpu.roll(x, LANES - 1, 1)
        nxt = jnp.where(lane == LANES - 1, pltpu.roll(r2, hh - 1, 0), r2)
        nxt = jnp.where(last, 0.0, nxt)
        return cw_ref[0, ch] * prev + cw_ref[1, ch] * x + cw_ref[2, ch] * nxt + cb_ref[ch]

    P = B // 2
    tc, ts = tc_ref[...], ts_ref[...]
    tc2 = jnp.concatenate([tc, tc], axis=1)
    ts2 = jnp.concatenate([ts, ts], axis=1)
    lane2 = lambda x, cc: x[:, cc * LANES:(cc + 1) * LANES]

    seqs = [(p, cc) for p in range(P) for cc in range(2)]
    gate_refs = (x1_ref, x2_ref)

    def load(cis, chs):
        z = {(p, cc): [short_conv(v_ref[cis[cc], 2 * p + r], chs[cc]) for r in range(2)] for p, cc in seqs}
        gates = [{(p, cc): [short_conv(gate_refs[o][cis[cc], 2 * p + r], (o + 1) * HY_WIDTH + chs[cc])
                            for r in range(2)] for p, cc in seqs} for o in range(HY_ORDER)]
        return z, gates

    def dft_rows(z):
        lhs = []
        for p in range(P):
            rhs = jnp.concatenate([jnp.concatenate(z[(p, cc)], axis=0) for cc in range(2)], axis=1)
            a = _dot(w1d_ref[...], rhs.astype(BF16))
            a_re, a_im = a[:NA], a[NA:]
            b_re = a_re * tc2 + a_im * ts2
            b_im = a_im * tc2 - a_re * ts2
            lhs += [jnp.concatenate([lane2(b_re, cc), lane2(b_im, cc)], axis=1) for cc in range(2)]
        return jnp.concatenate(lhs, axis=0).astype(BF16)

    def dft_lanes_times_filter(lhs, o, cis):
        x = _dot(lhs, w2_ref[...])
        ys = []
        for idx, (p, cc) in enumerate(seqs):
            xb = x[idx * NA:(idx + 1) * NA]
            ks = ks_ref[o, cis[cc]]
            x_re, x_im = lane2(xb, 0), lane2(xb, 1)
            k_re, k_im = lane2(ks, 0), lane2(ks, 1)
            ys.append(jnp.concatenate([x_re * k_re - x_im * k_im, x_re * k_im + x_im * k_re], axis=1))
        return jnp.concatenate(ys, axis=0).astype(BF16)

    def idft_lanes(ys):
        bq = _dot(ys, w2i_ref[...])
        out = []
        for p in range(P):
            cr, cim = [], []
            for cc in range(2):
                blk = bq[(2 * p + cc) * NA:(2 * p + cc + 1) * NA]
                b_re, b_im = lane2(blk, 0), lane2(blk, 1)
                cr.append(b_re * tc - b_im * ts)
                cim.append(b_re * ts + b_im * tc)
            out.append(jnp.concatenate([jnp.concatenate(cr, axis=1), jnp.concatenate(cim, axis=1)], axis=0).astype(BF16))
        return out

    def idft_rows_and_gate(rhs, z, gates, o, chs):
        znew = {}
        for p in range(P):
            conv = _dot(w1i_ref[...], rhs[p])
            for cc in range(2):
                bias = hb_ref[o, chs[cc]]
                znew[(p, cc)] = [gates[(p, cc)][r] * (lane2(conv, cc)[r * hh:(r + 1) * hh] + z[(p, cc)][r] * bias)
                                 for r in range(2)]
        return znew

    NG = 2

    def body(it, carry):
        groups = range(NG)
        cis = [[2 * NG * it + 2 * g + cc for cc in range(2)] for g in groups]
        chs = [[j * cg + ci for ci in cis[g]] for g in groups]
        loaded = [load(cis[g], chs[g]) for g in groups]
        zs = [loaded[g][0] for g in groups]
        for o in range(HY_ORDER):
            s1 = [dft_rows(zs[g]) for g in groups]
            s2 = [dft_lanes_times_filter(s1[g], o, cis[g]) for g in groups]
            s3 = [idft_lanes(s2[g]) for g in groups]
            zs = [idft_rows_and_gate(s3[g], zs[g], loaded[g][1][o], o, chs[g]) for g in groups]
        for g in groups:
            for p, cc in seqs:
                for r in range(2):
                    y_ref[cis[g][cc], 2 * p + r] = zs[g][(p, cc)][r]
        return carry

    lax.fori_loop(0, cg // (2 * NG), body, 0)
    for b in range(B):
        y2_ref[:, b * L:(b + 1) * L] = y_ref[:, b].reshape(cg, L)


def _hyena(proj_t, kspec, consts, conv_w, conv_b, bias, B, L, cg=8):
    NA = 2 * L // LANES
    hh = NA // 2
    ncg = HY_WIDTH // cg
    base = OFF_HY // cg
    smem = pl.BlockSpec(memory_space=pltpu.SMEM)
    full = lambda shp: pl.BlockSpec(shp, lambda j: (0,) * len(shp))
    blk = lambda off: pl.BlockSpec((cg, B * L), lambda j: (off + j, 0))
    tiles = pltpu.VMEM((cg, B, hh, LANES), F32)
    return pl.pallas_call(
        functools.partial(_hyena_kernel, NA=NA, cg=cg, B=B),
        grid=(ncg,),
        in_specs=[smem, smem, smem, blk(base), blk(base + ncg), blk(base + 2 * ncg),
                  pl.BlockSpec((HY_ORDER, cg, NA, 2 * LANES), lambda j: (0, j, 0, 0)),
                  full((2 * NA, NA)), full((NA, 2 * NA)), full((NA, LANES)), full((NA, LANES)),
                  full((2 * LANES, 2 * LANES)), full((2 * LANES, 2 * LANES))],
        out_specs=pl.BlockSpec((cg, B * L), lambda j: (j, 0)),
        out_shape=jax.ShapeDtypeStruct((HY_WIDTH, B * L), F32),
        scratch_shapes=[tiles, tiles, tiles, tiles],
        compiler_params=_cparams(("arbitrary",)),
        name="hyena_conv",
    )(conv_w, conv_b, bias, proj_t, proj_t, proj_t, kspec, consts["w1d"], consts["w1i"], consts["tc"], consts["ts"],
      consts["w2"], consts["w2i"])


def _outproj_kernel(yg_ref, yh_ref, yy_ref, h_ref, wo_ref, g_ref, b_ref, wrh_ref, wrl_ref, br_ref,
                    h1_ref, e_ref, w_ref, cnt_ref, *, alpha):
    tm = h_ref.shape[0]
    ts = LANES
    ns = tm // ts
    subs = [slice(s * ts, (s + 1) * ts) for s in range(ns)]
    mixes = [_dot(yg_ref[:, sl].astype(BF16), wo_ref[0:GLA_W], _TN)
             + _dot(yh_ref[:, sl].astype(BF16), wo_ref[GLA_W:GLA_W + HG_W], _TN)
             + _dot(yy_ref[:, sl].astype(BF16), wo_ref[GLA_W + HG_W:], _TN) for sl in subs]
    h1s = [_layer_norm(alpha * h_ref[sl, :] + mixes[s], g_ref[...], b_ref[...]) for s, sl in enumerate(subs)]
    for s, sl in enumerate(subs):
        h1_ref[sl, :] = h1s[s]
    his = [h1.astype(BF16) for h1 in h1s]
    los = [(h1s[s] - his[s].astype(F32)).astype(BF16) for s in range(ns)]
    lgs = [_dot(wrh_ref[...], his[s], _NT) + _dot(wrh_ref[...], los[s], _NT) + _dot(wrl_ref[...], his[s], _NT)
           for s in range(ns)]
    lg = jnp.concatenate(lgs, axis=1) + br_ref[...]
    gl = [lg[g:g + 1] for g in range(N_GROUPS)]
    gmax = functools.reduce(jnp.maximum, gl)
    gidx = jnp.full((1, tm), N_GROUPS - 1, I32)
    for g in range(N_GROUPS - 2, -1, -1):
        gidx = jnp.where(gl[g] == gmax, g, gidx)
    gsum = functools.reduce(jnp.add, [jnp.exp(x - gmax) for x in gl])
    g_val = 1.0 / gsum
    el = []
    for r in range(EXPERTS_PER_GROUP):
        acc = jnp.zeros((1, tm), F32)
        for g in range(N_GROUPS):
            row = N_GROUPS + g * EXPERTS_PER_GROUP + r
            acc = jnp.where(gidx == g, lg[row:row + 1], acc)
        el.append(acc)
    emax = functools.reduce(jnp.maximum, el)
    pe = [jnp.exp(x - emax) for x in el]
    esum = functools.reduce(jnp.add, pe)
    pe = [x / esum for x in pe]
    v1 = functools.reduce(jnp.maximum, pe)
    i1 = jnp.full((1, tm), EXPERTS_PER_GROUP - 1, I32)
    for r in range(EXPERTS_PER_GROUP - 2, -1, -1):
        i1 = jnp.where(pe[r] == v1, r, i1)
    pe2 = [jnp.where(i1 == r, -1.0, pe[r]) for r in range(EXPERTS_PER_GROUP)]
    v2 = functools.reduce(jnp.maximum, pe2)
    i2 = jnp.full((1, tm), EXPERTS_PER_GROUP - 1, I32)
    for r in range(EXPERTS_PER_GROUP - 2, -1, -1):
        i2 = jnp.where(pe2[r] == v2, r, i2)
    den = v1 + v2
    e0 = gidx * EXPERTS_PER_GROUP + i1
    e1 = gidx * EXPERTS_PER_GROUP + i2
    e_ref[...] = jnp.concatenate([e0, e1], axis=0)
    w_ref[...] = jnp.concatenate([g_val * (v1 / den), g_val * (v2 / den)], axis=0)
    eio = lax.broadcasted_iota(I32, (N_EXPERTS, tm), 0)
    hit = jnp.where((eio == e0) | (eio == e1), 1.0, 0.0)
    cnt_ref[0] = jnp.sum(hit, axis=1, keepdims=True)


def _outproj(yg, yh, yy, h, w_out, g, b, wr_t, br, alpha, tm=512):
    T, D = h.shape
    nr = wr_t.shape[0]
    wr_hi = wr_t.astype(BF16)
    wr_lo = (wr_t - wr_hi.astype(F32)).astype(BF16)
    full = lambda shp: pl.BlockSpec(shp, lambda i: (0,) * len(shp))
    return pl.pallas_call(
        functools.partial(_outproj_kernel, alpha=alpha),
        grid=(T // tm,),
        in_specs=[pl.BlockSpec((GLA_W, tm), lambda i: (0, i)),
                  pl.BlockSpec((HG_W, tm), lambda i: (0, i)),
                  pl.BlockSpec((HY_WIDTH, tm), lambda i: (0, i)),
                  pl.BlockSpec((tm, D), lambda i: (i, 0)),
                  full((D, D)), full((1, D)), full((1, D)), full((nr, D)), full((nr, D)), full((nr, 1))],
        out_specs=[pl.BlockSpec((tm, D), lambda i: (i, 0)),
                   pl.BlockSpec((TOP_K, tm), lambda i: (0, i)),
                   pl.BlockSpec((TOP_K, tm), lambda i: (0, i)),
                   pl.BlockSpec((1, N_EXPERTS, 1), lambda i: (i, 0, 0))],
        out_shape=[jax.ShapeDtypeStruct((T, D), F32),
                   jax.ShapeDtypeStruct((TOP_K, T), I32),
                   jax.ShapeDtypeStruct((TOP_K, T), F32),
                   jax.ShapeDtypeStruct((T // tm, N_EXPERTS, 1), F32)],
        compiler_params=_cparams(("arbitrary",)),
        name="outproj",
    )(yg, yh, yy, h, w_out.astype(BF16), g.reshape(1, D), b.reshape(1, D), wr_hi, wr_lo, br)


def _chunk_loop(n, fn):
    def body(c, carry):
        fn(pl.multiple_of(c * ROW_CHUNK, ROW_CHUNK))
        return carry
    lax.fori_loop(0, n, body, 0)


def _segment_copies(nch_s, k, fn):
    off = 0
    for si, rows in enumerate(SEG_SIZES):
        n = nch_s[k * len(SEG_SIZES) + si]

        def body(c, carry, off=off, rows=rows):
            fn(rows, pl.multiple_of(off + c * rows, ROW_CHUNK))
            return carry
        lax.fori_loop(0, n, body, 0)
        off = off + n * rows


def _wait_tile_copies(nch_s, tile, wait_one):
    for si, rows in enumerate(SEG_SIZES):
        tot = functools.reduce(lambda a, b: a + b, [nch_s[(tile * N_EXPERTS + e) * len(SEG_SIZES) + si]
                                                    for e in range(N_EXPERTS)])
        lax.fori_loop(0, tot, lambda c, carry, rows=rows: (wait_one(rows), carry)[1], 0)


def _dispatch_kernel(offs_s, gst_s, nch_s, tst_s, tn_s, nu_s, h_ref, e_ref, base_ref, tri_ref,
                     pos_ref, xg_hbm, xs_ref, zbuf, sem, zsem, *, tm, LP, bm, nblocks):
    i = pl.program_id(0)
    nt = pl.num_programs(0)
    slot = i % 2

    def seg_copy(s, rows, src_row, dst_row):
        return pltpu.make_async_copy(xs_ref.at[s, pl.ds(src_row, rows)],
                                     xg_hbm.at[pl.ds(dst_row, rows)], sem.at[s])

    def zero_copy(dst_row):
        return pltpu.make_async_copy(zbuf.at[pl.ds(0, ROW_CHUNK)], xg_hbm.at[pl.ds(dst_row, ROW_CHUNK)], zsem.at[0])

    def zero_block(blk):
        return pltpu.make_async_copy(zbuf, xg_hbm.at[pl.ds(pl.multiple_of(blk * bm, bm), bm)], zsem.at[0])

    def wait_tile(tile, s):
        _wait_tile_copies(nch_s, tile, lambda rows: seg_copy(s, rows, 0, 0).wait())

    @pl.when(i == 0)
    def _():
        zbuf[...] = jnp.zeros_like(zbuf)
        for e in range(N_EXPERTS):
            _chunk_loop(tn_s[e], lambda off, e=e: zero_copy(pl.multiple_of(tst_s[e] + off, ROW_CHUNK)).start())
        lax.fori_loop(nu_s[0], nblocks, lambda blk, c: (zero_block(blk).start(), c)[1], 0)
        for e in range(N_EXPERTS):
            _chunk_loop(tn_s[e], lambda off: zero_copy(0).wait())
        lax.fori_loop(nu_s[0], nblocks, lambda blk, c: (zero_block(0).wait(), c)[1], 0)

    e0, e1 = e_ref[0:1, :], e_ref[1:2, :]
    eio = lax.broadcasted_iota(I32, (N_EXPERTS, tm), 0)
    oh0, oh1 = eio == e0, eio == e1
    hit = jnp.where(oh0 | oh1, 1.0, 0.0).astype(BF16)
    posm = base_ref[0] + _dot(hit, tri_ref[...])
    pos0 = jnp.sum(jnp.where(oh0, posm, 0.0), axis=0, keepdims=True).astype(I32)
    pos1 = jnp.sum(jnp.where(oh1, posm, 0.0), axis=0, keepdims=True).astype(I32)
    pos_ref[...] = jnp.concatenate([pos0, pos1], axis=0)
    hb = h_ref[...].astype(BF16)
    rb = 256
    assert LP % rb == 0
    rio = lax.broadcasted_iota(I32, (rb, tm), 0)
    for r0 in range(0, LP, rb):
        perm = jnp.where((rio == pos0 - r0) | (rio == pos1 - r0), 1.0, 0.0).astype(BF16)
        xs_ref[slot, r0:r0 + rb] = _dot(perm, hb).astype(BF16)

    @pl.when(i > 0)
    def _():
        wait_tile(i - 1, 1 - slot)

    for e in range(N_EXPERTS):
        k = i * N_EXPERTS + e
        src0, dst0 = offs_s[k], gst_s[k]
        _segment_copies(nch_s, k, lambda rows, off, src0=src0, dst0=dst0: seg_copy(
            slot, rows, pl.multiple_of(src0 + off, ROW_CHUNK), pl.multiple_of(dst0 + off, ROW_CHUNK)).start())

    @pl.when(i == nt - 1)
    def _():
        wait_tile(i, slot)


def _moe_dispatch(h1, e_kt, tables, nblocks, bm, tm):
    T, D = h1.shape
    NT = T // tm
    nrows = nblocks * bm
    LP = TOP_K * tm + N_EXPERTS * ROW_CHUNK
    r = np.arange(tm)
    tri = jnp.asarray(r[:, None] < r[None, :], F32).astype(BF16)
    base = tables["offs"].astype(F32).reshape(NT, N_EXPERTS, 1)
    flat = lambda a: a.reshape(-1).astype(I32)
    pos, xg = pl.pallas_call(
        functools.partial(_dispatch_kernel, tm=tm, LP=LP, bm=bm, nblocks=nblocks),
        grid_spec=pltpu.PrefetchScalarGridSpec(
            num_scalar_prefetch=6,
            grid=(NT,),
            in_specs=[pl.BlockSpec((tm, D), lambda i, *_: (i, 0)),
                      pl.BlockSpec((TOP_K, tm), lambda i, *_: (0, i)),
                      pl.BlockSpec((1, N_EXPERTS, 1), lambda i, *_: (i, 0, 0)),
                      pl.BlockSpec((tm, tm), lambda i, *_: (0, 0))],
            out_specs=[pl.BlockSpec((TOP_K, tm), lambda i, *_: (0, i)),
                       pl.BlockSpec(memory_space=pl.ANY)],
            scratch_shapes=[pltpu.VMEM((2, LP, D), BF16), pltpu.VMEM((bm, D), BF16),
                            pltpu.SemaphoreType.DMA((2,)), pltpu.SemaphoreType.DMA((1,))]),
        out_shape=[jax.ShapeDtypeStruct((TOP_K, T), I32), jax.ShapeDtypeStruct((nrows, D), BF16)],
        compiler_params=_cparams(("arbitrary",)),
        name="moe_dispatch",
    )(flat(tables["offs"]), flat(tables["gstart"]), flat(tables["nch"]), flat(tables["tail_start"]),
      flat(tables["tail_n"]), tables["nused"], h1, e_kt, base, tri)
    return pos, xg


def _ffn_kernel(be_ref, nu_ref, x_ref, wg_ref, wu_ref, wd_ref, y_ref, wgb, wub, wdb):
    j = pl.program_id(0)
    used = j < nu_ref[0]

    @pl.when((j == 0) | (be_ref[j] != be_ref[jnp.maximum(j - 1, 0)]))
    def _():
        wgb[...] = wg_ref[0].astype(BF16)
        wub[...] = wu_ref[0].astype(BF16)
        wdb[...] = wd_ref[0].astype(BF16)

    @pl.when(used)
    def _():
        x = x_ref[...]
        a = _dot(x, wgb[...])
        ah = 0.5 * a
        hid = (ah + ah * jnp.tanh(ah)) * _dot(x, wub[...])
        y_ref[...] = _dot(hid.astype(BF16), wdb[...]).astype(BF16)

    @pl.when(jnp.logical_not(used))
    def _():
        y_ref[...] = jnp.zeros_like(y_ref)


def _moe_ffn(xg, block_e, nused, wg, wu, wd, bm, first_expert=0):
    nrows, D = xg.shape
    NB = nrows // bm
    DE = wg.shape[-1]
    row = lambda j, be, nu: (jnp.minimum(j, nu[0] - 1), 0)
    wsel = lambda j, be, nu: (first_expert + be[j], 0, 0)
    return pl.pallas_call(
        _ffn_kernel,
        grid_spec=pltpu.PrefetchScalarGridSpec(
            num_scalar_prefetch=2,
            grid=(NB,),
            in_specs=[pl.BlockSpec((bm, D), row),
                      pl.BlockSpec((1, D, DE), wsel),
                      pl.BlockSpec((1, D, DE), wsel),
                      pl.BlockSpec((1, DE, D), wsel)],
            out_specs=pl.BlockSpec((bm, D), lambda j, be, nu: (j, 0)),
            scratch_shapes=[pltpu.VMEM((D, DE), BF16), pltpu.VMEM((D, DE), BF16), pltpu.VMEM((DE, D), BF16)]),
        out_shape=jax.ShapeDtypeStruct((nrows, D), BF16),
        compiler_params=_cparams(("arbitrary",)),
        name="moe_ffn",
    )(block_e, nused, xg, wg, wu, wd)


def _combine_kernel(offs_s, gst_s, nch_s, yb_hbm, pos_ref, w_ref, h_ref, g_ref, b_ref, o_ref, ybl, sem,
                    *, tm, LP, alpha):
    i = pl.program_id(0)
    nt = pl.num_programs(0)
    slot = i % 2

    def seg_copy(s, rows, src_row, dst_row):
        return pltpu.make_async_copy(yb_hbm.at[pl.ds(src_row, rows)],
                                     ybl.at[s, pl.ds(dst_row, rows)], sem.at[s])

    def issue(tile, s):
        for e in range(N_EXPERTS):
            k = tile * N_EXPERTS + e
            src0, dst0 = gst_s[k], offs_s[k]
            _segment_copies(nch_s, k, lambda rows, off, src0=src0, dst0=dst0: seg_copy(
                s, rows, pl.multiple_of(src0 + off, ROW_CHUNK), pl.multiple_of(dst0 + off, ROW_CHUNK)).start())

    @pl.when(i == 0)
    def _():
        ybl[...] = jnp.zeros_like(ybl)
        issue(0, 0)

    @pl.when(i + 1 < nt)
    def _():
        issue(i + 1, 1 - slot)

    _wait_tile_copies(nch_s, i, lambda rows: seg_copy(slot, rows, 0, 0).wait())
    pos0, pos1 = pos_ref[0:1, :], pos_ref[1:2, :]
    w0, w1 = w_ref[0:1, :], w_ref[1:2, :]
    rio = lax.broadcasted_iota(I32, (LP, LANES), 0)
    yl = ybl[slot]
    for t0 in range(0, tm, LANES):
        tl = slice(t0, t0 + LANES)
        pw = (jnp.where(rio == pos0[:, tl], w0[:, tl], 0.0) + jnp.where(rio == pos1[:, tl], w1[:, tl], 0.0)).astype(BF16)
        ffn = _dot(pw, yl, _TN)
        o_ref[tl, :] = _layer_norm(alpha * h_ref[tl, :] + ffn, g_ref[...], b_ref[...])


def _moe_combine(yb, h1, pos, w_kt, tables, g, b, alpha, tm):
    T, D = h1.shape
    LP = TOP_K * tm + N_EXPERTS * ROW_CHUNK
    flat = lambda a: a.reshape(-1).astype(I32)
    return pl.pallas_call(
        functools.partial(_combine_kernel, tm=tm, LP=LP, alpha=alpha),
        grid_spec=pltpu.PrefetchScalarGridSpec(
            num_scalar_prefetch=3,
            grid=(T // tm,),
            in_specs=[pl.BlockSpec(memory_space=pl.ANY),
                      pl.BlockSpec((TOP_K, tm), lambda i, *_: (0, i)),
                      pl.BlockSpec((TOP_K, tm), lambda i, *_: (0, i)),
                      pl.BlockSpec((tm, D), lambda i, *_: (i, 0)),
                      pl.BlockSpec((1, D), lambda i, *_: (0, 0)),
                      pl.BlockSpec((1, D), lambda i, *_: (0, 0))],
            out_specs=pl.BlockSpec((tm, D), lambda i, *_: (i, 0)),
            scratch_shapes=[pltpu.VMEM((2, LP, D), BF16), pltpu.SemaphoreType.DMA((2,))]),
        out_shape=jax.ShapeDtypeStruct((T, D), F32),
        compiler_params=_cparams(("arbitrary",)),
        name="moe_combine",
    )(flat(tables["offs"]), flat(tables["gstart"]), flat(tables["nch"]), yb, pos, w_kt, h1,
      g.reshape(1, D), b.reshape(1, D))


def _dispatch_tables(cnt, bm, nblocks):
    padlen = ((cnt + ROW_CHUNK - 1) // ROW_CHUNK) * ROW_CHUNK
    offs = jnp.cumsum(padlen, axis=1) - padlen
    tot = jnp.sum(padlen, axis=0)
    region = ((tot + bm - 1) // bm) * bm
    rend = jnp.cumsum(region)
    rstart = rend - region
    gstart = rstart[None, :] + jnp.cumsum(padlen, axis=0) - padlen
    blk_row = jnp.arange(nblocks, dtype=I32)[:, None] * bm
    block_e = jnp.minimum(jnp.sum((blk_row >= rend[None, :]).astype(I32), axis=1), N_EXPERTS - 1)
    nch, rest = [], padlen
    for rows in SEG_SIZES:
        nch.append(rest // rows)
        rest = rest % rows
    return dict(offs=offs, gstart=gstart, nch=jnp.stack(nch, axis=-1), tail_start=rstart + tot,
                tail_n=(region - tot) // ROW_CHUNK, block_e=block_e.astype(I32),
                nused=(rend[-1:] // bm).astype(I32))


def _permute_in_columns(w):
    splits = (192, 192, 384, 384, 32, 384, 768, 384, 384, 768)
    offs = np.concatenate([[0], np.cumsum(splits)])
    gq, gk, gv, gg, ga, hq, hf, hi, hgt, hyu = [(int(offs[i]), int(offs[i + 1])) for i in range(10)]
    order = [hq, hi, hgt, hf, gv, gg, gq, gk, hyu, ga]
    halved = (hq, hgt, hf, gg)
    assert sum(b - a for a, b in order) == D_IN
    return jnp.concatenate([w[:, a:b] * 0.5 if (a, b) in halved else w[:, a:b] for a, b in order], axis=1)


def kernel(x, ln_in_g, ln_in_b, w_in, gla_wa2, gla_ba, gla_norm_g, hg_lb_logits, hg_norm_g, hy_conv_w, hy_conv_b, hy_w1, hy_b1, hy_freq, hy_w2, hy_b2, hy_w3, hy_bias, w_out, ln1_g, ln1_b, moe_wr_g, moe_br_g, moe_wr_e, moe_br_e, moe_w_gate, moe_w_up, moe_w_down, ln2_g, ln2_b):
    B, L, D = x.shape
    T = B * L
    depth = w_in.shape[0]
    alpha = (2 * depth) ** 0.25
    bm = 512
    tmr = 512
    nblocks = -(-(T * TOP_K + (T // tmr) * N_EXPERTS * (ROW_CHUNK - 1) + N_EXPERTS * (bm - 1)) // bm)
    consts = _dft_consts(L)

    p = jax.nn.softmax(hg_lb_logits.astype(F32), axis=0)
    lbs = jnp.cumsum(p, axis=0) - p[0:1]
    lbc = jnp.concatenate([jnp.broadcast_to((0.5 * (1.0 - lbs))[..., None], lbs.shape + (LANES,)),
                           jnp.broadcast_to(jnp.maximum(lbs, LB_FLOOR)[..., None], lbs.shape + (LANES,))], axis=-1)

    wg_all = moe_w_gate.reshape((depth * N_EXPERTS,) + moe_w_gate.shape[2:])
    wu_all = moe_w_up.reshape((depth * N_EXPERTS,) + moe_w_up.shape[2:])
    wd_all = moe_w_down.reshape((depth * N_EXPERTS,) + moe_w_down.shape[2:])
    h = x.reshape(T, D)
    for l in range(depth):
        w_t = _permute_in_columns(w_in[l]).T.astype(BF16)
        proj_t, h = _inproj(h, ln_in_g, ln_in_b, w_t, apply_ln=(l == 0))
        wa_t = jnp.swapaxes(gla_wa2[l], 1, 2)
        y_gla = _scan(proj_t, "gla", B, L, (wa_t, gla_ba[l].reshape(2, GLA_K, 1)), gla_norm_g[l].reshape(GLA_W, 1))
        y_hg = _scan(proj_t, "hg", B, L, (lbc[l],), hg_norm_g[l].reshape(HG_W, 1))
        kt = _hyena_filters(L, hy_w1[l], hy_b1[l], hy_freq[l], hy_w2[l], hy_b2[l], hy_w3[l])
        kspec = _hyena_spectrum(kt, consts, L)
        y_hy = _hyena(proj_t, kspec, consts, hy_conv_w[l], hy_conv_b[l], hy_bias[l], B, L)
        nr = N_GROUPS + N_EXPERTS
        nrp = -(-nr // 8) * 8
        wr_t = jnp.zeros((nrp, D), F32).at[:nr].set(jnp.concatenate([moe_wr_g[l], moe_wr_e[l]], axis=1).T)
        br = jnp.zeros((nrp, 1), F32).at[:nr, 0].set(jnp.concatenate([moe_br_g[l], moe_br_e[l]]))
        h1, e_kt, w_kt, cnt = _outproj(y_gla, y_hg, y_hy, h, w_out[l], ln1_g[l], ln1_b[l], wr_t, br, alpha, tm=tmr)
        tables = _dispatch_tables(cnt.reshape(T // tmr, N_EXPERTS).astype(I32), bm, nblocks)
        pos, xg = _moe_dispatch(h1, e_kt, tables, nblocks, bm, tmr)
        yb = _moe_ffn(xg, tables["block_e"], tables["nused"], wg_all, wu_all, wd_all, bm, first_expert=l * N_EXPERTS)
        h = _moe_combine(yb, h1, pos, w_kt, tables, ln2_g[l], ln2_b[l], alpha, tmr)
    return h.reshape(B, L, D)
```

```python
import functools
import math

import numpy as np
import jax
import jax.numpy as jnp
from jax import lax
from jax.experimental import pallas as pl
from jax.experimental.pallas import tpu as pltpu

F32 = jnp.float32
BF16 = jnp.bfloat16
I32 = jnp.int32

GLA_HEADS, GLA_DK, GLA_DV, GLA_LOWRANK, GLA_TAU = 6, 32, 64, 16, 16.0
HG_HEADS, HG_DK, HG_DV = 6, 64, 64
HY_WIDTH, HY_ORDER, HY_EMB, HY_FFN, HY_INNER = 256, 2, 33, 64, 2
HY_FAST_DECAY, HY_SLOW_DECAY, HY_TARGET = 0.3, 1.5, 1e-2
N_GROUPS, EXPERTS_PER_GROUP = 4, 4
N_EXPERTS = N_GROUPS * EXPERTS_PER_GROUP
TOP_K = 2
LN_EPS, RMS_EPS, LB_FLOOR = 1e-5, 1e-6, 1e-30
LOG2E = 1.4426950408889634

LANES = 128
SCAN_CHUNK = LANES
ROW_CHUNK = 16
SEG_SIZES = (64, 32, ROW_CHUNK)
VMEM_LIMIT = 56 * 1024 * 1024

GLA_W = GLA_HEADS * GLA_DV
GLA_K = GLA_HEADS * GLA_DK
HG_W = HG_HEADS * HG_DV
HG_K = HG_HEADS * HG_DK
OFF_HQ, OFF_HI, OFF_HGT, OFF_HF = 0, 384, 768, 1152
OFF_GV, OFF_GG, OFF_GQ, OFF_GK, OFF_HY, OFF_GA = 1920, 2304, 2688, 2880, 3072, 3840
D_IN = 3872


def _dot(a, b, dims=(((1,), (0,)), ((), ())), precision=None):
    return lax.dot_general(a, b, dims, preferred_element_type=F32, precision=precision)


_NT = (((1,), (1,)), ((), ()))
_TN = (((0,), (0,)), ((), ()))


def _layer_norm(x, g, b):
    mu = jnp.mean(x, axis=-1, keepdims=True)
    xc = x - mu
    var = jnp.mean(xc * xc, axis=-1, keepdims=True)
    return xc * lax.rsqrt(var + LN_EPS) * g + b


def _log_sigmoid(x):
    return jnp.minimum(x, 0.0) - jnp.log(1.0 + jnp.exp(-jnp.abs(x)))


def _cparams(sem):
    return pltpu.CompilerParams(dimension_semantics=sem, vmem_limit_bytes=VMEM_LIMIT)


def _inproj_kernel(x_ref, g_ref, b_ref, w_ref, *outs, apply_ln):
    x = x_ref[...]
    if apply_ln:
        x = _layer_norm(x, g_ref[...], b_ref[...])
        outs[1][...] = x
    outs[0][...] = _dot(w_ref[...], x.astype(BF16), _NT)


def _inproj(x, g, b, w_t, apply_ln, tm=512):
    T, D = x.shape
    n_out = w_t.shape[0]
    out_shape = [jax.ShapeDtypeStruct((n_out, T), F32)]
    out_specs = [pl.BlockSpec((n_out, tm), lambda i: (0, i))]
    if apply_ln:
        out_shape.append(jax.ShapeDtypeStruct((T, D), F32))
        out_specs.append(pl.BlockSpec((tm, D), lambda i: (i, 0)))
    res = pl.pallas_call(
        functools.partial(_inproj_kernel, apply_ln=apply_ln),
        grid=(T // tm,),
        in_specs=[pl.BlockSpec((tm, D), lambda i: (i, 0)),
                  pl.BlockSpec((1, D), lambda i: (0, 0)),
                  pl.BlockSpec((1, D), lambda i: (0, 0)),
                  pl.BlockSpec((n_out, D), lambda i: (0, 0))],
        out_specs=out_specs,
        out_shape=out_shape,
        compiler_params=_cparams(("arbitrary",)),
        name="inproj",
    )(x, g.reshape(1, D), b.reshape(1, D), w_t)
    return res if apply_ln else (res[0], x)


def _scan_kernel(*refs, mode, H, K, V, TB, NCB):
    if mode == "gla":
        (qf_ref, qb_ref, kf_ref, kb_ref, vf_ref, vb_ref, gaf_ref, gab_ref, wa_ref, ba_ref,
         gtf_ref, gtb_ref, ng_ref, y_ref, s_ref, oacc_ref, oi_ref, qg_ref, ut_ref, dr_ref) = refs
        q_refs, k_refs, ga_refs = (qf_ref, qb_ref), (kf_ref, kb_ref), (gaf_ref, gab_ref)
    else:
        (qf_ref, qb_ref, zf_ref, zb_ref, vf_ref, vb_ref, lbc_ref,
         gtf_ref, gtb_ref, ng_ref, y_ref, s_ref, oacc_ref, oi_ref, qg_ref, ut_ref, dr_ref) = refs
        q_refs, z_refs = (qf_ref, qb_ref), (zf_ref, zb_ref)
    v_refs, gate_refs = (vf_ref, vb_ref), (gtf_ref, gtb_ref)
    C = SCAN_CHUNK
    half = C // 2
    nchunks = TB // C
    n = pl.program_id(1)
    blocks = (n, NCB - 1 - n)

    @pl.when(n == 0)
    def _():
        s_ref[...] = jnp.zeros_like(s_ref)

    def gates(d, sl):
        if mode == "gla":
            wa, ga = wa_ref[d], ga_refs[d][:, sl]
            wa_hi, ga_hi = wa.astype(BF16), ga.astype(BF16)
            wa_lo = (wa - wa_hi.astype(F32)).astype(BF16)
            ga_lo = (ga - ga_hi.astype(F32)).astype(BF16)
            a = _dot(wa_hi, ga_hi) + _dot(wa_hi, ga_lo) + _dot(wa_lo, ga_hi) + ba_ref[d]
            g = _log_sigmoid(a) * (LOG2E / GLA_TAU)
            q = q_refs[d][:, sl] * (K ** -0.5)
            k = k_refs[d][:, sl]
        else:
            half_lb, lb_floor = lbc_ref[d, :, 0:C], lbc_ref[d, :, C:2 * C]
            ht = half_lb * jnp.tanh(z_refs[d][:, sl])
            g = jnp.log2(half_lb + ht + lb_floor)
            k = half_lb - ht
            hq = q_refs[d][:, sl]
            q = hq + hq * jnp.tanh(hq)
        return q, k, g

    r = lax.broadcasted_iota(I32, (C, C), 0)
    c = lax.broadcasted_iota(I32, (C, C), 1)
    lane_lo = lax.broadcasted_iota(I32, (1, C), 1) < half
    sign_lo = jnp.where(lane_lo, 1.0, -1.0)
    consts = (((r <= c).astype(BF16), r <= c, (r < half) & (c >= half), (C - 1, half, half // 2, half + half // 2)),
              ((r >= c).astype(BF16), r >= c, (r >= half) & (c < half),
               (0, half - 1, half // 2 - 1, half + half // 2 - 1)))

    heads = [(slice(h * K, (h + 1) * K), slice(h * V, (h + 1) * V)) for h in range(H)]

    def cumulate(d, g):
        cum = consts[d][0]
        g1 = g.astype(BF16)
        g2 = (g - g1.astype(F32)).astype(BF16)
        G = _dot(g1, cum) + _dot(g2, cum)
        ge_row = _dot(ones8, g1, _NT) + _dot(ones8, g2, _NT)
        return G, ge_row

    def scale(d, sl, q, k, G, ge_row):
        _, _, _, (c_end, c_mid, c_a, c_b) = consts[d]
        v = v_refs[d][:, sl].astype(BF16)
        g_end = G[:, c_end:c_end + 1]
        g_mid = G[:, c_mid:c_mid + 1]
        e2 = G - jnp.where(lane_lo, G[:, c_a:c_a + 1], G[:, c_b:c_b + 1])
        q2 = (q * jnp.exp2(e2)).astype(BF16)
        k2 = (k * jnp.exp2(-e2)).astype(BF16)
        dm = G - g_mid
        x1 = jnp.exp2(dm * (sign_lo if d else -sign_lo))
        q1 = (q * x1).astype(BF16)
        k1 = (k * x1).astype(BF16)
        qg = (q * jnp.exp2(G)).astype(BF16)
        kd = (k * jnp.exp2(g_end - G)).astype(BF16)
        x = dict(qg=qg, kd=kd, v=v, dec_row=jnp.exp2(ge_row))
        if merge_scores:
            first = lambda a: jnp.where(lane_lo, a, jnp.zeros_like(a))
            second = lambda a: jnp.where(lane_lo, jnp.zeros_like(a), a)
            x["kparts"] = ((second if d else first)(k1), first(k2), second(k2))
            x["qparts"] = ((first if d else second)(q1), first(q2), second(q2))
        else:
            x.update(q1=q1, k1=k1, q2=q2, k2=k2)
        return x

    ones8 = jnp.ones((8, C), BF16)
    merge_scores = 3 * K <= LANES

    def scores(x):
        if merge_scores:
            return [_dot(jnp.concatenate([p[rk] for p in x["kparts"]], axis=0),
                         jnp.concatenate([p[rk] for p in x["qparts"]], axis=0), _TN) for rk, _ in heads]
        return [(_dot(x["k1"][rk], x["q1"][rk], _TN), _dot(x["k2"][rk], x["q2"][rk], _TN)) for rk, _ in heads]

    def masked(d, ps):
        _, tri, off1, _ = consts[d]
        if merge_scores:
            return [jnp.where(tri, p, 0.0).astype(BF16) for p in ps]
        return [jnp.where(off1, p1, jnp.where(tri, p2, 0.0)).astype(BF16) for p1, p2 in ps]

    def park(d, ci, x, pm):
        oi_ref[d, ci] = jnp.concatenate([_dot(x["v"][rv], pm[h]) for h, (_, rv) in enumerate(heads)], axis=0)
        qg_ref[d, ci] = x["qg"]
        dr_ref[d, ci] = x["dec_row"]
        for h, (rk, rv) in enumerate(heads):
            ut_ref[d, ci, h] = _dot(x["v"][rv], x["kd"][rk], _NT)

    def chunk_slices(ci):
        offs = (pl.multiple_of(ci * C, C), pl.multiple_of((nchunks - 1 - ci) * C, C))
        return offs, [pl.ds(offs[d], C) for d in range(2)]

    def independent(it, carry):
        jobs = [(d, U * it + u, chunk_slices(U * it + u)[1][d]) for u in range(U) for d in range(2)]
        qkg = [gates(d, sl) for d, _, sl in jobs]
        cums = [cumulate(d, qkg[i][2]) for i, (d, _, _) in enumerate(jobs)]
        xs = [scale(d, sl, qkg[i][0], qkg[i][1], *cums[i]) for i, (d, _, sl) in enumerate(jobs)]
        scs = [scores(x) for x in xs]
        pms = [masked(d, scs[i]) for i, (d, _, _) in enumerate(jobs)]
        for i, (d, ci, _) in enumerate(jobs):
            park(d, ci, xs[i], pms[i])
        return carry

    U = 4
    assert nchunks % U == 0
    lax.fori_loop(0, nchunks // U, independent, 0)

    def carried(ci):
        st = [[s_ref[d, h] for h in range(H)] for d in range(2)]
        inter = [[_dot(st[d][h].astype(BF16), qg_ref[d, ci, rk, :]) for h, (rk, _) in enumerate(heads)]
                 for d in range(2)]
        for d in range(2):
            for h, (rk, _) in enumerate(heads):
                s_ref[d, h] = st[d][h] * dr_ref[d, ci, 0:1, rk] + ut_ref[d, ci, h]
        return [oi_ref[d, ci] + jnp.concatenate(inter[d], axis=0) for d in range(2)]

    def finish(o, gate):
        t = jnp.tanh(gate)
        act = gate + gate * t if mode == "gla" else 0.5 + 0.5 * t
        ys = []
        for h in range(H):
            oh = o[h * V:(h + 1) * V]
            ms = jnp.mean(oh * oh, axis=0, keepdims=True)
            ys.append(oh * lax.rsqrt(ms + RMS_EPS))
        return (jnp.concatenate(ys, axis=0) * ng_ref[...] * act).astype(y_ref.dtype)

    def chunk(ci, carry):
        offs, sls = chunk_slices(ci)
        tsls = [pl.ds(pl.multiple_of(blocks[d] * TB + offs[d], C), C) for d in range(2)]
        o = carried(ci)

        @pl.when(2 * n < NCB)
        def _():
            for d in range(2):
                oacc_ref[:, tsls[d]] = o[d]

        @pl.when(2 * n >= NCB)
        def _():
            tot = [o[d] + oacc_ref[:, tsls[d]] for d in range(2)]
            for d in range(2):
                y_ref[:, tsls[d]] = finish(tot[d], gate_refs[d][:, sls[d]])
        return carry

    lax.fori_loop(0, nchunks, chunk, 0)


def _scan(proj_t, mode, B, L, extra, norm_g, TB=512):
    NCB = L // TB
    nch = TB // SCAN_CHUNK
    assert NCB % 2 == 0
    if mode == "gla":
        H, K, V = GLA_HEADS, GLA_DK, GLA_DV
    else:
        H, K, V = HG_HEADS, HG_DK, HG_DV
    HK, HV = H * K, H * V
    cf = lambda b, n: b * NCB + n
    cb = lambda b, n: b * NCB + NCB - 1 - n
    gf = lambda b, n: b * NCB + jnp.maximum(n, NCB // 2)
    gb = lambda b, n: b * NCB + jnp.minimum(NCB - 1 - n, NCB // 2 - 1)

    def pair(rows, off, fwd=cf, bwd=cb, dir_step=0):
        return [pl.BlockSpec((rows, TB), lambda b, n: (off // rows, fwd(b, n))),
                pl.BlockSpec((rows, TB), lambda b, n: (off // rows + dir_step, bwd(b, n)))]

    whole = lambda shp: pl.BlockSpec(shp, lambda b, n: (0,) * len(shp))
    if mode == "gla":
        wa_t, ba = extra
        in_specs = (pair(HK, OFF_GQ) + pair(HK, OFF_GK) + pair(HV, OFF_GV)
                    + pair(GLA_LOWRANK, OFF_GA, dir_step=1)
                    + [whole((2, HK, GLA_LOWRANK)), whole((2, HK, 1))]
                    + pair(HV, OFF_GG, gf, gb) + [whole((HV, 1))])
        args = (proj_t,) * 8 + (wa_t, ba, proj_t, proj_t, norm_g)
    else:
        (lbc,) = extra
        in_specs = (pair(HK, OFF_HQ) + pair(HK, OFF_HF, dir_step=1) + pair(HV, OFF_HI)
                    + [whole((2, HK, 2 * LANES))] + pair(HV, OFF_HGT, gf, gb) + [whole((HV, 1))])
        args = (proj_t,) * 6 + (lbc, proj_t, proj_t, norm_g)
    return pl.pallas_call(
        functools.partial(_scan_kernel, mode=mode, H=H, K=K, V=V, TB=TB, NCB=NCB),
        grid=(B, NCB),
        in_specs=in_specs,
        out_specs=pl.BlockSpec((HV, L), lambda b, n: (0, b)),
        out_shape=jax.ShapeDtypeStruct((HV, B * L), BF16),
        scratch_shapes=[pltpu.VMEM((2, H, V, K), F32), pltpu.VMEM((HV, L), F32),
                        pltpu.VMEM((2, nch, HV, SCAN_CHUNK), F32), pltpu.VMEM((2, nch, HK, SCAN_CHUNK), BF16),
                        pltpu.VMEM((2, nch, H, V, K), F32), pltpu.VMEM((2, nch, 8, HK), F32)],
        compiler_params=_cparams(("arbitrary", "arbitrary")),
        name="scan_" + mode,
    )(*args)


def _dft_consts(L):
    N = 2 * L
    NA = N // LANES
    a = np.arange(NA)[:, None] * np.arange(NA)[None, :]
    ca, sa = np.cos(2 * np.pi * a / NA), np.sin(2 * np.pi * a / NA)
    hh = NA // 2
    w1d = np.block([[ca[:, :hh], sa[:, :hh]], [-sa[:, :hh], ca[:, :hh]]])
    w1f = np.concatenate([ca, -sa], axis=0)
    w1i = np.block([[ca[:hh, :], -sa[:hh, :]], [sa[:hh, :], ca[:hh, :]]])
    bb = np.arange(LANES)[:, None] * np.arange(LANES)[None, :]
    cb, sb = np.cos(2 * np.pi * bb / LANES), np.sin(2 * np.pi * bb / LANES)
    w2 = np.block([[cb, -sb], [sb, cb]])
    w2i = np.block([[cb, sb], [-sb, cb]])
    tw = np.arange(NA)[:, None] * np.arange(LANES)[None, :]
    tc, ts = np.cos(2 * np.pi * tw / N), np.sin(2 * np.pi * tw / N)
    bf = lambda m: jnp.asarray(m, dtype=F32).astype(BF16)
    return dict(w1d=bf(w1d), w1f=bf(w1f), w1i=bf(w1i), w2=bf(w2), w2i=bf(w2i),
                tc=jnp.asarray(tc, F32), ts=jnp.asarray(ts, F32))


def _pos_features(L):
    t = np.linspace(0.0, 1.0, L)
    w = 2.0 * np.pi * np.arange(L) / L
    bands = np.linspace(1e-4, (HY_EMB - 1) // 2 - 1, (HY_EMB - 1) // 2)
    z = np.concatenate([t[None, :], np.cos(bands[:, None] * w[None, :]), -np.sin(bands[:, None] * w[None, :])], axis=0)
    kp = -(-HY_EMB // 8) * 8
    z = np.concatenate([z, np.zeros((kp - HY_EMB, L))], axis=0)
    t_rev = t[(L - np.arange(L)) % L]
    mask = (np.arange(L) >= 1).astype(np.float64)
    return (jnp.asarray(z, F32), jnp.asarray(t[None, :], F32),
            jnp.asarray(t_rev[None, :], F32), jnp.asarray(mask[None, :], F32))


def _filter_kernel(z_ref, t_ref, tr_ref, m_ref, w1_ref, b1_ref, fr_ref, w2_ref, b2_ref,
                   w3f_ref, w3b_ref, dl_ref, out_ref, h_ref, *, L):
    first = (pl.program_id(0) == 0) & (pl.program_id(1) == 0)
    hi = lax.Precision.HIGHEST

    @pl.when(first)
    def _():
        fr = fr_ref[...]
        h = jnp.sin(fr * (_dot(w1_ref[...], z_ref[...], precision=hi) + b1_ref[...]))
        for i in range(HY_INNER):
            h = jnp.sin(fr * (_dot(w2_ref[i], h, precision=hi) + b2_ref[i]))
        hb = h.astype(BF16)
        h_ref[0] = hb
        src = lax.broadcasted_iota(I32, (LANES, LANES), 0)
        dst = lax.broadcasted_iota(I32, (LANES, LANES), 1)
        flip_shift = jnp.where((dst >= 1) & (src == LANES - dst), 1.0, 0.0).astype(BF16)
        lane0 = jnp.where((dst == 0) & (src == 0), 1.0, 0.0).astype(BF16)
        nb = L // LANES
        tile = lambda b: hb[:, b * LANES:(b + 1) * LANES]
        for jb in range(nb):
            blk = _dot(tile(nb - 1 - jb), flip_shift) + _dot(tile((nb - jb) % nb), lane0)
            h_ref[1, :, jb * LANES:(jb + 1) * LANES] = blk.astype(BF16)

    ad = jnp.abs(dl_ref[...])
    kf = _dot(w3f_ref[...].astype(BF16), h_ref[0]) * jnp.exp(-t_ref[...] * ad)
    kb = _dot(w3b_ref[...].astype(BF16), h_ref[1]) * jnp.exp(-tr_ref[...] * ad) * m_ref[...]
    den = jnp.sum(jnp.abs(kf), axis=1, keepdims=True) + jnp.sum(jnp.abs(kb), axis=1, keepdims=True)
    scale = 1.0 / (jnp.maximum(den, 1e-12) * (2.0 * L))
    out_ref[0, :, 0:L] = kf * scale
    out_ref[0, :, L:2 * L] = kb * scale


def _hyena_filters(L, w1, b1, freq, w2, b2, w3, cg=64):
    z, t, t_rev, mask = _pos_features(L)
    kp = z.shape[0]
    w1_t = jnp.zeros((HY_FFN, kp), F32).at[:, :HY_EMB].set(w1.T)
    w2_t = jnp.swapaxes(w2, 1, 2)
    w3_t = w3.T
    max_decay = math.log(HY_TARGET) / HY_FAST_DECAY
    min_decay = math.log(HY_TARGET) / HY_SLOW_DECAY
    deltas = jnp.asarray(np.linspace(min_decay, max_decay, HY_WIDTH).reshape(HY_WIDTH, 1), F32)
    ncg = HY_WIDTH // cg
    full = lambda shp: pl.BlockSpec(shp, lambda o, j: (0,) * len(shp))
    return pl.pallas_call(
        functools.partial(_filter_kernel, L=L),
        grid=(HY_ORDER, ncg),
        in_specs=[full((kp, L)), full((1, L)), full((1, L)), full((1, L)),
                  full((HY_FFN, kp)), full((HY_FFN, 1)), full((HY_FFN, 1)),
                  full((HY_INNER, HY_FFN, HY_FFN)), full((HY_INNER, HY_FFN, 1)),
                  pl.BlockSpec((cg, HY_FFN), lambda o, j: (o * 2 * ncg + j, 0)),
                  pl.BlockSpec((cg, HY_FFN), lambda o, j: (o * 2 * ncg + ncg + j, 0)),
                  pl.BlockSpec((cg, 1), lambda o, j: (j, 0))],
        out_specs=pl.BlockSpec((1, cg, 2 * L), lambda o, j: (o, j, 0)),
        out_shape=jax.ShapeDtypeStruct((HY_ORDER, HY_WIDTH, 2 * L), F32),
        scratch_shapes=[pltpu.VMEM((2, HY_FFN, L), BF16)],
        compiler_params=_cparams(("arbitrary", "arbitrary")),
        name="hyena_filter",
    )(z, t, t_rev, mask, w1_t, b1.reshape(HY_FFN, 1), freq.reshape(HY_FFN, 1), w2_t,
      b2.reshape(HY_INNER, HY_FFN, 1), w3_t, w3_t, deltas)


def _spectrum_kernel(k2_ref, w1_ref, tc_ref, ts_ref, w2_ref, out_ref, k_ref, *, NA, cg):
    k_ref[0] = k2_ref[0].reshape(cg, NA, LANES)
    tc2 = jnp.concatenate([tc_ref[...]] * 2, axis=1)
    ts2 = jnp.concatenate([ts_ref[...]] * 2, axis=1)
    G = 8

    def body(it, carry):
        lhs = []
        for pr in range(G // 2):
            rhs = jnp.concatenate([k_ref[0, G * it + 2 * pr + cc] for cc in range(2)], axis=1)
            a = _dot(w1_ref[...], rhs.astype(BF16))
            a_re, a_im = a[:NA], a[NA:]
            b_re = a_re * tc2 + a_im * ts2
            b_im = a_im * tc2 - a_re * ts2
            lhs += [jnp.concatenate([b_re[:, cc * LANES:(cc + 1) * LANES], b_im[:, cc * LANES:(cc + 1) * LANES]], axis=1)
                    for cc in range(2)]
        x = _dot(jnp.concatenate(lhs, axis=0).astype(BF16), w2_ref[...])
        for i in range(G):
            out_ref[0, G * it + i] = x[i * NA:(i + 1) * NA]
        return carry
    lax.fori_loop(0, cg // G, body, 0)


def _hyena_spectrum(kt, consts, L, cg=32):
    NA = 2 * L // LANES
    full = lambda shp: pl.BlockSpec(shp, lambda o, j: (0,) * len(shp))
    return pl.pallas_call(
        functools.partial(_spectrum_kernel, NA=NA, cg=cg),
        grid=(HY_ORDER, HY_WIDTH // cg),
        in_specs=[pl.BlockSpec((1, cg, 2 * L), lambda o, j: (o, j, 0)),
                  full((2 * NA, NA)), full((NA, LANES)), full((NA, LANES)), full((2 * LANES, 2 * LANES))],
        out_specs=pl.BlockSpec((1, cg, NA, 2 * LANES), lambda o, j: (o, j, 0, 0)),
        out_shape=jax.ShapeDtypeStruct((HY_ORDER, HY_WIDTH, NA, 2 * LANES), F32),
        scratch_shapes=[pltpu.VMEM((1, cg, NA, LANES), F32)],
        compiler_params=_cparams(("arbitrary", "arbitrary")),
        name="hyena_spectrum",
    )(kt, consts["w1f"], consts["tc"], consts["ts"], consts["w2"])


def _hyena_kernel(cw_ref, cb_ref, hb_ref, v2_ref, x12_ref, x22_ref, ks_ref, w1d_ref, w1i_ref, tc_ref, ts_ref,
                  w2_ref, w2i_ref, y2_ref, v_ref, x1_ref, x2_ref, y_ref, *, NA, cg, B):
    hh = NA // 2
    L = hh * LANES
    j = pl.program_id(0)
    for src, dst in ((v2_ref, v_ref), (x12_ref, x1_ref), (x22_ref, x2_ref)):
        for b in range(B):
            dst[:, b] = src[:, b * L:(b + 1) * L].reshape(cg, hh, LANES)
    row = lax.broadcasted_iota(I32, (hh, LANES), 0)
    lane = lax.broadcasted_iota(I32, (hh, LANES), 1)
    first = (row == 0) & (lane == 0)
    last = (row == hh - 1) & (lane == LANES - 1)

    def short_conv(x, ch):
        r1 = pltpu.roll(x, 1, 1)
        prev = jnp.where(lane == 0, pltpu.roll(r1, 1, 0), r1)
        prev = jnp.where(first, 0.0, prev)
        r2 = pltpu.roll(x, LANES - 1, 1)
        nxt = jnp.where(lane == LANES - 1, pltpu.roll(r2, hh - 1, 0), r2)
        nxt = jnp.where(last, 0.0, nxt)
        return cw_ref[0, ch] * prev + cw_ref[1, ch] * x + cw_ref[2, ch] * nxt + cb_ref[ch]

    P = B // 2
    tc, ts = tc_ref[...], ts_ref[...]
    tc2 = jnp.concatenate([tc, tc], axis=1)
    ts2 = jnp.concatenate([ts, ts], axis=1)
    lane2 = lambda x, cc: x[:, cc * LANES:(cc + 1) * LANES]

    seqs = [(p, cc) for p in range(P) for cc in range(2)]
    gate_refs = (x1_ref, x2_ref)

    def load(cis, chs):
        z = {(p, cc): [short_conv(v_ref[cis[cc], 2 * p + r], chs[cc]) for r in range(2)] for p, cc in seqs}
        gates = [{(p, cc): [short_conv(gate_refs[o][cis[cc], 2 * p + r], (o + 1) * HY_WIDTH + chs[cc])
                            for r in range(2)] for p, cc in seqs} for o in range(HY_ORDER)]
        return z, gates

    def dft_rows(z):
        lhs = []
        for p in range(P):
            rhs = jnp.concatenate([jnp.concatenate(z[(p, cc)], axis=0) for cc in range(2)], axis=1)
            a = _dot(w1d_ref[...], rhs.astype(BF16))
            a_re, a_im = a[:NA], a[NA:]
            b_re = a_re * tc2 + a_im * ts2
            b_im = a_im * tc2 - a_re * ts2
            lhs += [jnp.concatenate([lane2(b_re, cc), lane2(b_im, cc)], axis=1) for cc in range(2)]
        return jnp.concatenate(lhs, axis=0).astype(BF16)

    def dft_lanes_times_filter(lhs, o, cis):
        x = _dot(lhs, w2_ref[...])
        ys = []
        for idx, (p, cc) in enumerate(seqs):
            xb = x[idx * NA:(idx + 1) * NA]
            ks = ks_ref[o, cis[cc]]
            x_re, x_im = lane2(xb, 0), lane2(xb, 1)
            k_re, k_im = lane2(ks, 0), lane2(ks, 1)
            ys.append(jnp.concatenate([x_re * k_re - x_im * k_im, x_re * k_im + x_im * k_re], axis=1))
        return jnp.concatenate(ys, axis=0).astype(BF16)

    def idft_lanes(ys):
        bq = _dot(ys, w2i_ref[...])
        out = []
        for p in range(P):
            cr, cim = [], []
            for cc in range(2):
                blk = bq[(2 * p + cc) * NA:(2 * p + cc + 1) * NA]
                b_re, b_im = lane2(blk, 0), lane2(blk, 1)
                cr.append(b_re * tc - b_im * ts)
                cim.append(b_re * ts + b_im * tc)
            out.append(jnp.concatenate([jnp.concatenate(cr, axis=1), jnp.concatenate(cim, axis=1)], axis=0).astype(BF16))
        return out

    def idft_rows_and_gate(rhs, z, gates, o, chs):
        znew = {}
        for p in range(P):
            conv = _dot(w1i_ref[...], rhs[p])
            for cc in range(2):
                bias = hb_ref[o, chs[cc]]
                znew[(p, cc)] = [gates[(p, cc)][r] * (lane2(conv, cc)[r * hh:(r + 1) * hh] + z[(p, cc)][r] * bias)
                                 for r in range(2)]
        return znew

    NG = 2

    def body(it, carry):
        groups = range(NG)
        cis = [[2 * NG * it + 2 * g + cc for cc in range(2)] for g in groups]
        chs = [[j * cg + ci for ci in cis[g]] for g in groups]
        loaded = [load(cis[g], chs[g]) for g in groups]
        zs = [loaded[g][0] for g in groups]
        for o in range(HY_ORDER):
            s1 = [dft_rows(zs[g]) for g in groups]
            s2 = [dft_lanes_times_filter(s1[g], o, cis[g]) for g in groups]
            s3 = [idft_lanes(s2[g]) for g in groups]
            zs = [idft_rows_and_gate(s3[g], zs[g], loaded[g][1][o], o, chs[g]) for g in groups]
        for g in groups:
            for p, cc in seqs:
                for r in range(2):
                    y_ref[cis[g][cc], 2 * p + r] = zs[g][(p, cc)][r]
        return carry

    lax.fori_loop(0, cg // (2 * NG), body, 0)
    for b in range(B):
        y2_ref[:, b * L:(b + 1) * L] = y_ref[:, b].reshape(cg, L)


def _hyena(proj_t, kspec, consts, conv_w, conv_b, bias, B, L, cg=8):
    NA = 2 * L // LANES
    hh = NA // 2
    ncg = HY_WIDTH // cg
    base = OFF_HY // cg
    smem = pl.BlockSpec(memory_space=pltpu.SMEM)
    full = lambda shp: pl.BlockSpec(shp, lambda j: (0,) * len(shp))
    blk = lambda off: pl.BlockSpec((cg, B * L), lambda j: (off + j, 0))
    tiles = pltpu.VMEM((cg, B, hh, LANES), F32)
    return pl.pallas_call(
        functools.partial(_hyena_kernel, NA=NA, cg=cg, B=B),
        grid=(ncg,),
        in_specs=[smem, smem, smem, blk(base), blk(base + ncg), blk(base + 2 * ncg),
                  pl.BlockSpec((HY_ORDER, cg, NA, 2 * LANES), lambda j: (0, j, 0, 0)),
                  full((2 * NA, NA)), full((NA, 2 * NA)), full((NA, LANES)), full((NA, LANES)),
                  full((2 * LANES, 2 * LANES)), full((2 * LANES, 2 * LANES))],
        out_specs=pl.BlockSpec((cg, B * L), lambda j: (j, 0)),
        out_shape=jax.ShapeDtypeStruct((HY_WIDTH, B * L), F32),
        scratch_shapes=[tiles, tiles, tiles, tiles],
        compiler_params=_cparams(("arbitrary",)),
        name="hyena_conv",
    )(conv_w, conv_b, bias, proj_t, proj_t, proj_t, kspec, consts["w1d"], consts["w1i"], consts["tc"], consts["ts"],
      consts["w2"], consts["w2i"])


def _outproj_kernel(yg_ref, yh_ref, yy_ref, h_ref, wo_ref, g_ref, b_ref, wrh_ref, wrl_ref, br_ref,
                    h1_ref, e_ref, w_ref, cnt_ref, *, alpha):
    tm = h_ref.shape[0]
    ts = LANES
    ns = tm // ts
    subs = [slice(s * ts, (s + 1) * ts) for s in range(ns)]
    mixes = [_dot(yg_ref[:, sl].astype(BF16), wo_ref[0:GLA_W], _TN)
             + _dot(yh_ref[:, sl].astype(BF16), wo_ref[GLA_W:GLA_W + HG_W], _TN)
             + _dot(yy_ref[:, sl].astype(BF16), wo_ref[GLA_W + HG_W:], _TN) for sl in subs]
    h1s = [_layer_norm(alpha * h_ref[sl, :] + mixes[s], g_ref[...], b_ref[...]) for s, sl in enumerate(subs)]
    for s, sl in enumerate(subs):
        h1_ref[sl, :] = h1s[s]
    his = [h1.astype(BF16) for h1 in h1s]
    los = [(h1s[s] - his[s].astype(F32)).astype(BF16) for s in range(ns)]
    lgs = [_dot(wrh_ref[...], his[s], _NT) + _dot(wrh_ref[...], los[s], _NT) + _dot(wrl_ref[...], his[s], _NT)
           for s in range(ns)]
    lg = jnp.concatenate(lgs, axis=1) + br_ref[...]
    gl = [lg[g:g + 1] for g in range(N_GROUPS)]
    gmax = functools.reduce(jnp.maximum, gl)
    gidx = jnp.full((1, tm), N_GROUPS - 1, I32)
    for g in range(N_GROUPS - 2, -1, -1):
        gidx = jnp.where(gl[g] == gmax, g, gidx)
    gsum = functools.reduce(jnp.add, [jnp.exp(x - gmax) for x in gl])
    g_val = 1.0 / gsum
    el = []
    for r in range(EXPERTS_PER_GROUP):
        acc = jnp.zeros((1, tm), F32)
        for g in range(N_GROUPS):
            row = N_GROUPS + g * EXPERTS_PER_GROUP + r
            acc = jnp.where(gidx == g, lg[row:row + 1], acc)
        el.append(acc)
    emax = functools.reduce(jnp.maximum, el)
    pe = [jnp.exp(x - emax) for x in el]
    esum = functools.reduce(jnp.add, pe)
    pe = [x / esum for x in pe]
    v1 = functools.reduce(jnp.maximum, pe)
    i1 = jnp.full((1, tm), EXPERTS_PER_GROUP - 1, I32)
    for r in range(EXPERTS_PER_GROUP - 2, -1, -1):
        i1 = jnp.where(pe[r] == v1, r, i1)
    pe2 = [jnp.where(i1 == r, -1.0, pe[r]) for r in range(EXPERTS_PER_GROUP)]
    v2 = functools.reduce(jnp.maximum, pe2)
    i2 = jnp.full((1, tm), EXPERTS_PER_GROUP - 1, I32)
    for r in range(EXPERTS_PER_GROUP - 2, -1, -1):
        i2 = jnp.where(pe2[r] == v2, r, i2)
    den = v1 + v2
    e0 = gidx * EXPERTS_PER_GROUP + i1
    e1 = gidx * EXPERTS_PER_GROUP + i2
    e_ref[...] = jnp.concatenate([e0, e1], axis=0)
    w_ref[...] = jnp.concatenate([g_val * (v1 / den), g_val * (v2 / den)], axis=0)
    eio = lax.broadcasted_iota(I32, (N_EXPERTS, tm), 0)
    hit = jnp.where((eio == e0) | (eio == e1), 1.0, 0.0)
    cnt_ref[0] = jnp.sum(hit, axis=1, keepdims=True)


def _outproj(yg, yh, yy, h, w_out, g, b, wr_t, br, alpha, tm=512):
    T, D = h.shape
    nr = wr_t.shape[0]
    wr_hi = wr_t.astype(BF16)
    wr_lo = (wr_t - wr_hi.astype(F32)).astype(BF16)
    full = lambda shp: pl.BlockSpec(shp, lambda i: (0,) * len(shp))
    return pl.pallas_call(
        functools.partial(_outproj_kernel, alpha=alpha),
        grid=(T // tm,),
        in_specs=[pl.BlockSpec((GLA_W, tm), lambda i: (0, i)),
                  pl.BlockSpec((HG_W, tm), lambda i: (0, i)),
                  pl.BlockSpec((HY_WIDTH, tm), lambda i: (0, i)),
                  pl.BlockSpec((tm, D), lambda i: (i, 0)),
                  full((D, D)), full((1, D)), full((1, D)), full((nr, D)), full((nr, D)), full((nr, 1))],
        out_specs=[pl.BlockSpec((tm, D), lambda i: (i, 0)),
                   pl.BlockSpec((TOP_K, tm), lambda i: (0, i)),
                   pl.BlockSpec((TOP_K, tm), lambda i: (0, i)),
                   pl.BlockSpec((1, N_EXPERTS, 1), lambda i: (i, 0, 0))],
        out_shape=[jax.ShapeDtypeStruct((T, D), F32),
                   jax.ShapeDtypeStruct((TOP_K, T), I32),
                   jax.ShapeDtypeStruct((TOP_K, T), F32),
                   jax.ShapeDtypeStruct((T // tm, N_EXPERTS, 1), F32)],
        compiler_params=_cparams(("arbitrary",)),
        name="outproj",
    )(yg, yh, yy, h, w_out.astype(BF16), g.reshape(1, D), b.reshape(1, D), wr_hi, wr_lo, br)


def _chunk_loop(n, fn):
    def body(c, carry):
        fn(pl.multiple_of(c * ROW_CHUNK, ROW_CHUNK))
        return carry
    lax.fori_loop(0, n, body, 0)


def _segment_copies(nch_s, k, fn):
    off = 0
    for si, rows in enumerate(SEG_SIZES):
        n = nch_s[k * len(SEG_SIZES) + si]

        def body(c, carry, off=off, rows=rows):
            fn(rows, pl.multiple_of(off + c * rows, ROW_CHUNK))
            return carry
        lax.fori_loop(0, n, body, 0)
        off = off + n * rows


def _wait_tile_copies(nch_s, tile, wait_one):
    for si, rows in enumerate(SEG_SIZES):
        tot = functools.reduce(lambda a, b: a + b, [nch_s[(tile * N_EXPERTS + e) * len(SEG_SIZES) + si]
                                                    for e in range(N_EXPERTS)])
        lax.fori_loop(0, tot, lambda c, carry, rows=rows: (wait_one(rows), carry)[1], 0)


def _dispatch_kernel(offs_s, gst_s, nch_s, tst_s, tn_s, nu_s, h_ref, e_ref, base_ref, tri_ref,
                     pos_ref, xg_hbm, xs_ref, zbuf, sem, zsem, *, tm, LP, bm, nblocks):
    i = pl.program_id(0)
    nt = pl.num_programs(0)
    slot = i % 2

    def seg_copy(s, rows, src_row, dst_row):
        return pltpu.make_async_copy(xs_ref.at[s, pl.ds(src_row, rows)],
                                     xg_hbm.at[pl.ds(dst_row, rows)], sem.at[s])

    def zero_copy(dst_row):
        return pltpu.make_async_copy(zbuf.at[pl.ds(0, ROW_CHUNK)], xg_hbm.at[pl.ds(dst_row, ROW_CHUNK)], zsem.at[0])

    def zero_block(blk):
        return pltpu.make_async_copy(zbuf, xg_hbm.at[pl.ds(pl.multiple_of(blk * bm, bm), bm)], zsem.at[0])

    def wait_tile(tile, s):
        _wait_tile_copies(nch_s, tile, lambda rows: seg_copy(s, rows, 0, 0).wait())

    @pl.when(i == 0)
    def _():
        zbuf[...] = jnp.zeros_like(zbuf)
        for e in range(N_EXPERTS):
            _chunk_loop(tn_s[e], lambda off, e=e: zero_copy(pl.multiple_of(tst_s[e] + off, ROW_CHUNK)).start())
        lax.fori_loop(nu_s[0], nblocks, lambda blk, c: (zero_block(blk).start(), c)[1], 0)
        for e in range(N_EXPERTS):
            _chunk_loop(tn_s[e], lambda off: zero_copy(0).wait())
        lax.fori_loop(nu_s[0], nblocks, lambda blk, c: (zero_block(0).wait(), c)[1], 0)

    e0, e1 = e_ref[0:1, :], e_ref[1:2, :]
    eio = lax.broadcasted_iota(I32, (N_EXPERTS, tm), 0)
    oh0, oh1 = eio == e0, eio == e1
    hit = jnp.where(oh0 | oh1, 1.0, 0.0).astype(BF16)
    posm = base_ref[0] + _dot(hit, tri_ref[...])
    pos0 = jnp.sum(jnp.where(oh0, posm, 0.0), axis=0, keepdims=True).astype(I32)
    pos1 = jnp.sum(jnp.where(oh1, posm, 0.0), axis=0, keepdims=True).astype(I32)
    pos_ref[...] = jnp.concatenate([pos0, pos1], axis=0)
    hb = h_ref[...].astype(BF16)
    rb = 256
    assert LP % rb == 0
    rio = lax.broadcasted_iota(I32, (rb, tm), 0)
    for r0 in range(0, LP, rb):
        perm = jnp.where((rio == pos0 - r0) | (rio == pos1 - r0), 1.0, 0.0).astype(BF16)
        xs_ref[slot, r0:r0 + rb] = _dot(perm, hb).astype(BF16)

    @pl.when(i > 0)
    def _():
        wait_tile(i - 1, 1 - slot)

    for e in range(N_EXPERTS):
        k = i * N_EXPERTS + e
        src0, dst0 = offs_s[k], gst_s[k]
        _segment_copies(nch_s, k, lambda rows, off, src0=src0, dst0=dst0: seg_copy(
            slot, rows, pl.multiple_of(src0 + off, ROW_CHUNK), pl.multiple_of(dst0 + off, ROW_CHUNK)).start())

    @pl.when(i == nt - 1)
    def _():
        wait_tile(i, slot)


def _moe_dispatch(h1, e_kt, tables, nblocks, bm, tm):
    T, D = h1.shape
    NT = T // tm
    nrows = nblocks * bm
    LP = TOP_K * tm + N_EXPERTS * ROW_CHUNK
    r = np.arange(tm)
    tri = jnp.asarray(r[:, None] < r[None, :], F32).astype(BF16)
    base = tables["offs"].astype(F32).reshape(NT, N_EXPERTS, 1)
    flat = lambda a: a.reshape(-1).astype(I32)
    pos, xg = pl.pallas_call(
        functools.partial(_dispatch_kernel, tm=tm, LP=LP, bm=bm, nblocks=nblocks),
        grid_spec=pltpu.PrefetchScalarGridSpec(
            num_scalar_prefetch=6,
            grid=(NT,),
            in_specs=[pl.BlockSpec((tm, D), lambda i, *_: (i, 0)),
                      pl.BlockSpec((TOP_K, tm), lambda i, *_: (0, i)),
                      pl.BlockSpec((1, N_EXPERTS, 1), lambda i, *_: (i, 0, 0)),
                      pl.BlockSpec((tm, tm), lambda i, *_: (0, 0))],
            out_specs=[pl.BlockSpec((TOP_K, tm), lambda i, *_: (0, i)),
                       pl.BlockSpec(memory_space=pl.ANY)],
            scratch_shapes=[pltpu.VMEM((2, LP, D), BF16), pltpu.VMEM((bm, D), BF16),
                            pltpu.SemaphoreType.DMA((2,)), pltpu.SemaphoreType.DMA((1,))]),
        out_shape=[jax.ShapeDtypeStruct((TOP_K, T), I32), jax.ShapeDtypeStruct((nrows, D), BF16)],
        compiler_params=_cparams(("arbitrary",)),
        name="moe_dispatch",
    )(flat(tables["offs"]), flat(tables["gstart"]), flat(tables["nch"]), flat(tables["tail_start"]),
      flat(tables["tail_n"]), tables["nused"], h1, e_kt, base, tri)
    return pos, xg


def _ffn_kernel(be_ref, nu_ref, x_ref, wg_ref, wu_ref, wd_ref, y_ref, wgb, wub, wdb):
    j = pl.program_id(0)
    used = j < nu_ref[0]

    @pl.when((j == 0) | (be_ref[j] != be_ref[jnp.maximum(j - 1, 0)]))
    def _():
        wgb[...] = wg_ref[0].astype(BF16)
        wub[...] = wu_ref[0].astype(BF16)
        wdb[...] = wd_ref[0].astype(BF16)

    @pl.when(used)
    def _():
        x = x_ref[...]
        a = _dot(x, wgb[...])
        ah = 0.5 * a
        hid = (ah + ah * jnp.tanh(ah)) * _dot(x, wub[...])
        y_ref[...] = _dot(hid.astype(BF16), wdb[...]).astype(BF16)

    @pl.when(jnp.logical_not(used))
    def _():
        y_ref[...] = jnp.zeros_like(y_ref)


def _moe_ffn(xg, block_e, nused, wg, wu, wd, bm, first_expert=0):
    nrows, D = xg.shape
    NB = nrows // bm
    DE = wg.shape[-1]
    row = lambda j, be, nu: (jnp.minimum(j, nu[0] - 1), 0)
    wsel = lambda j, be, nu: (first_expert + be[j], 0, 0)
    return pl.pallas_call(
        _ffn_kernel,
        grid_spec=pltpu.PrefetchScalarGridSpec(
            num_scalar_prefetch=2,
            grid=(NB,),
            in_specs=[pl.BlockSpec((bm, D), row),
                      pl.BlockSpec((1, D, DE), wsel),
                      pl.BlockSpec((1, D, DE), wsel),
                      pl.BlockSpec((1, DE, D), wsel)],
            out_specs=pl.BlockSpec((bm, D), lambda j, be, nu: (j, 0)),
            scratch_shapes=[pltpu.VMEM((D, DE), BF16), pltpu.VMEM((D, DE), BF16), pltpu.VMEM((DE, D), BF16)]),
        out_shape=jax.ShapeDtypeStruct((nrows, D), BF16),
        compiler_params=_cparams(("arbitrary",)),
        name="moe_ffn",
    )(block_e, nused, xg, wg, wu, wd)


def _combine_kernel(offs_s, gst_s, nch_s, yb_hbm, pos_ref, w_ref, h_ref, g_ref, b_ref, o_ref, ybl, sem,
                    *, tm, LP, alpha):
    i = pl.program_id(0)
    nt = pl.num_programs(0)
    slot = i % 2

    def seg_copy(s, rows, src_row, dst_row):
        return pltpu.make_async_copy(yb_hbm.at[pl.ds(src_row, rows)],
                                     ybl.at[s, pl.ds(dst_row, rows)], sem.at[s])

    def issue(tile, s):
        for e in range(N_EXPERTS):
            k = tile * N_EXPERTS + e
            src0, dst0 = gst_s[k], offs_s[k]
            _segment_copies(nch_s, k, lambda rows, off, src0=src0, dst0=dst0: seg_copy(
                s, rows, pl.multiple_of(src0 + off, ROW_CHUNK), pl.multiple_of(dst0 + off, ROW_CHUNK)).start())

    @pl.when(i == 0)
    def _():
        ybl[...] = jnp.zeros_like(ybl)
        issue(0, 0)

    @pl.when(i + 1 < nt)
    def _():
        issue(i + 1, 1 - slot)

    _wait_tile_copies(nch_s, i, lambda rows: seg_copy(slot, rows, 0, 0).wait())
    pos0, pos1 = pos_ref[0:1, :], pos_ref[1:2, :]
    w0, w1 = w_ref[0:1, :], w_ref[1:2, :]
    rio = lax.broadcasted_iota(I32, (LP, LANES), 0)
    yl = ybl[slot]
    for t0 in range(0, tm, LANES):
        tl = slice(t0, t0 + LANES)
        pw = (jnp.where(rio == pos0[:, tl], w0[:, tl], 0.0) + jnp.where(rio == pos1[:, tl], w1[:, tl], 0.0)).astype(BF16)
        ffn = _dot(pw, yl, _TN)
        o_ref[tl, :] = _layer_norm(alpha * h_ref[tl, :] + ffn, g_ref[...], b_ref[...])


def _moe_combine(yb, h1, pos, w_kt, tables, g, b, alpha, tm):
    T, D = h1.shape
    LP = TOP_K * tm + N_EXPERTS * ROW_CHUNK
    flat = lambda a: a.reshape(-1).astype(I32)
    return pl.pallas_call(
        functools.partial(_combine_kernel, tm=tm, LP=LP, alpha=alpha),
        grid_spec=pltpu.PrefetchScalarGridSpec(
            num_scalar_prefetch=3,
            grid=(T // tm,),
            in_specs=[pl.BlockSpec(memory_space=pl.ANY),
                      pl.BlockSpec((TOP_K, tm), lambda i, *_: (0, i)),
                      pl.BlockSpec((TOP_K, tm), lambda i, *_: (0, i)),
                      pl.BlockSpec((tm, D), lambda i, *_: (i, 0)),
                      pl.BlockSpec((1, D), lambda i, *_: (0, 0)),
                      pl.BlockSpec((1, D), lambda i, *_: (0, 0))],
            out_specs=pl.BlockSpec((tm, D), lambda i, *_: (i, 0)),
            scratch_shapes=[pltpu.VMEM((2, LP, D), BF16), pltpu.SemaphoreType.DMA((2,))]),
        out_shape=jax.ShapeDtypeStruct((T, D), F32),
        compiler_params=_cparams(("arbitrary",)),
        name="moe_combine",
    )(flat(tables["offs"]), flat(tables["gstart"]), flat(tables["nch"]), yb, pos, w_kt, h1,
      g.reshape(1, D), b.reshape(1, D))


def _dispatch_tables(cnt, bm, nblocks):
    padlen = ((cnt + ROW_CHUNK - 1) // ROW_CHUNK) * ROW_CHUNK
    offs = jnp.cumsum(padlen, axis=1) - padlen
    tot = jnp.sum(padlen, axis=0)
    region = ((tot + bm - 1) // bm) * bm
    rend = jnp.cumsum(region)
    rstart = rend - region
    gstart = rstart[None, :] + jnp.cumsum(padlen, axis=0) - padlen
    blk_row = jnp.arange(nblocks, dtype=I32)[:, None] * bm
    block_e = jnp.minimum(jnp.sum((blk_row >= rend[None, :]).astype(I32), axis=1), N_EXPERTS - 1)
    nch, rest = [], padlen
    for rows in SEG_SIZES:
        nch.append(rest // rows)
        rest = rest % rows
    return dict(offs=offs, gstart=gstart, nch=jnp.stack(nch, axis=-1), tail_start=rstart + tot,
                tail_n=(region - tot) // ROW_CHUNK, block_e=block_e.astype(I32),
                nused=(rend[-1:] // bm).astype(I32))


def _permute_in_columns(w):
    splits = (192, 192, 384, 384, 32, 384, 768, 384, 384, 768)
    offs = np.concatenate([[0], np.cumsum(splits)])
    gq, gk, gv, gg, ga, hq, hf, hi, hgt, hyu = [(int(offs[i]), int(offs[i + 1])) for i in range(10)]
    order = [hq, hi, hgt, hf, gv, gg, gq, gk, hyu, ga]
    halved = (hq, hgt, hf, gg)
    assert sum(b - a for a, b in order) == D_IN
    return jnp.concatenate([w[:, a:b] * 0.5 if (a, b) in halved else w[:, a:b] for a, b in order], axis=1)


def kernel(x, ln_in_g, ln_in_b, w_in, gla_wa2, gla_ba, gla_norm_g, hg_lb_logits, hg_norm_g, hy_conv_w, hy_conv_b, hy_w1, hy_b1, hy_freq, hy_w2, hy_b2, hy_w3, hy_bias, w_out, ln1_g, ln1_b, moe_wr_g, moe_br_g, moe_wr_e, moe_br_e, moe_w_gate, moe_w_up, moe_w_down, ln2_g, ln2_b):
    B, L, D = x.shape
    T = B * L
    depth = w_in.shape[0]
    alpha = (2 * depth) ** 0.25
    bm = 512
    tmr = 512
    nblocks = -(-(T * TOP_K + (T // tmr) * N_EXPERTS * (ROW_CHUNK - 1) + N_EXPERTS * (bm - 1)) // bm)
    consts = _dft_consts(L)

    p = jax.nn.softmax(hg_lb_logits.astype(F32), axis=0)
    lbs = jnp.cumsum(p, axis=0) - p[0:1]
    lbc = jnp.concatenate([jnp.broadcast_to((0.5 * (1.0 - lbs))[..., None], lbs.shape + (LANES,)),
                           jnp.broadcast_to(jnp.maximum(lbs, LB_FLOOR)[..., None], lbs.shape + (LANES,))], axis=-1)

    wg_all = moe_w_gate.reshape((depth * N_EXPERTS,) + moe_w_gate.shape[2:])
    wu_all = moe_w_up.reshape((depth * N_EXPERTS,) + moe_w_up.shape[2:])
    wd_all = moe_w_down.reshape((depth * N_EXPERTS,) + moe_w_down.shape[2:])
    h = x.reshape(T, D)
    for l in range(depth):
        w_t = _permute_in_columns(w_in[l]).T.astype(BF16)
        proj_t, h = _inproj(h, ln_in_g, ln_in_b, w_t, apply_ln=(l == 0))
        wa_t = jnp.swapaxes(gla_wa2[l], 1, 2)
        y_gla = _scan(proj_t, "gla", B, L, (wa_t, gla_ba[l].reshape(2, GLA_K, 1)), gla_norm_g[l].reshape(GLA_W, 1),
                      TB=1024)
        y_hg = _scan(proj_t, "hg", B, L, (lbc[l],), hg_norm_g[l].reshape(HG_W, 1))
        kt = _hyena_filters(L, hy_w1[l], hy_b1[l], hy_freq[l], hy_w2[l], hy_b2[l], hy_w3[l])
        kspec = _hyena_spectrum(kt, consts, L)
        y_hy = _hyena(proj_t, kspec, consts, hy_conv_w[l], hy_conv_b[l], hy_bias[l], B, L)
        nr = N_GROUPS + N_EXPERTS
        nrp = -(-nr // 8) * 8
        wr_t = jnp.zeros((nrp, D), F32).at[:nr].set(jnp.concatenate([moe_wr_g[l], moe_wr_e[l]], axis=1).T)
        br = jnp.zeros((nrp, 1), F32).at[:nr, 0].set(jnp.concatenate([moe_br_g[l], moe_br_e[l]]))
        h1, e_kt, w_kt, cnt = _outproj(y_gla, y_hg, y_hy, h, w_out[l], ln1_g[l], ln1_b[l], wr_t, br, alpha, tm=tmr)
        tables = _dispatch_tables(cnt.reshape(T // tmr, N_EXPERTS).astype(I32), bm, nblocks)
        pos, xg = _moe_dispatch(h1, e_kt, tables, nblocks, bm, tmr)
        yb = _moe_ffn(xg, tables["block_e"], tables["nused"], wg_all, wu_all, wd_all, bm, first_expert=l * N_EXPERTS)
        h = _moe_combine(yb, h1, pos, w_kt, tables, ln2_g[l], ln2_b[l], alpha, tmr)
    return h.reshape(B, L, D)
```
